```python
import jax, jax.numpy as jnp
from jax import lax
import numpy as np

D_MODEL = 1024
BATCH = 8
SEQ = 2048
DEPTH = 1

PLE_DIM = 256
ROPE_THETA = 10000.0
EPS = 1e-6
NEG_INF = -1e30

MLA_HEADS = 8
MLA_Q_RANK = 384
MLA_KV_RANK = 256
MLA_NOPE = 64
MLA_ROPE = 32
MLA_QK = MLA_NOPE + MLA_ROPE
MLA_V = 64
Q_BLOCK = 128

SWA_HEADS = 8
SWA_KV_HEADS = 2
SWA_GROUP = SWA_HEADS // SWA_KV_HEADS
SWA_HD = 64
WINDOW = 128
BAND_BLOCK = 128

OFF_CQ = MLA_Q_RANK
OFF_CKV = OFF_CQ + MLA_KV_RANK
OFF_KR = OFF_CKV + MLA_ROPE
OFF_QS = OFF_KR + SWA_HEADS * SWA_HD
OFF_KS = OFF_QS + SWA_KV_HEADS * SWA_HD
OFF_VS = OFF_KS + SWA_KV_HEADS * SWA_HD
OFF_GA = OFF_VS + D_MODEL
IN_WIDTH = OFF_GA + D_MODEL

N_EXPERTS = 64
TOP_K = 8
N_GROUPS = 8
TOPK_GROUPS = 4
EXPERTS_PER_GROUP = N_EXPERTS // N_GROUPS
EXPERT_FF = 256
SHARED_FF = 256
ROUTED_SCALE = 2.5
MOE_CHUNK = 128

kernel_name = "hybrid_mla_swa_moe_ple_block"


def rmsnorm(x, g):
    xf = x.astype(jnp.float32)
    y = xf * lax.rsqrt(jnp.mean(xf * xf, axis=-1, keepdims=True) + EPS)
    return (y * g.astype(jnp.float32)).astype(x.dtype)


def rope_cos_sin(positions, dim):
    inv = 1.0 / (ROPE_THETA ** (jnp.arange(0, dim, 2, dtype=jnp.float32) / dim))
    ang = positions.astype(jnp.float32)[..., None] * inv
    return jnp.cos(ang), jnp.sin(ang)


def apply_rope(x, cos, sin):
    x1, x2 = jnp.split(x, 2, axis=-1)
    c = cos[:, :, None, :].astype(x.dtype)
    s = sin[:, :, None, :].astype(x.dtype)
    return jnp.concatenate([x1 * c - x2 * s, x2 * c + x1 * s], axis=-1)


def dense_attention(q, k, v, scale):
    B, S, H, dk = q.shape
    nb = S // Q_BLOCK
    qb = q.reshape(B, nb, Q_BLOCK, H, dk).transpose(1, 0, 2, 3, 4)

    def one_block(qblk):
        s = jnp.einsum('bqhd,bkhd->bhqk', qblk, k).astype(jnp.float32) * scale
        pr = jax.nn.softmax(s, axis=-1).astype(v.dtype)
        return jnp.einsum('bhqk,bkhd->bqhd', pr, v)

    o = lax.map(one_block, qb)
    return o.transpose(1, 0, 2, 3, 4).reshape(B, S, H, v.shape[-1])


def banded_gqa_attention(q, k, v, sink, scale):
    B, S, Hq, d = q.shape
    nb = S // BAND_BLOCK
    pad = ((0, 0), (BAND_BLOCK, BAND_BLOCK), (0, 0), (0, 0))
    kp = jnp.pad(k, pad).reshape(B, nb + 2, BAND_BLOCK, SWA_KV_HEADS, d)
    vp = jnp.pad(v, pad).reshape(B, nb + 2, BAND_BLOCK, SWA_KV_HEADS, d)
    kband = jnp.concatenate([kp[:, :-2], kp[:, 1:-1], kp[:, 2:]], axis=2)
    vband = jnp.concatenate([vp[:, :-2], vp[:, 1:-1], vp[:, 2:]], axis=2)
    qb = q.reshape(B, nb, BAND_BLOCK, SWA_KV_HEADS, SWA_GROUP, d)
    s = jnp.einsum('bnqhgd,bnkhd->bnhgqk', qb, kband).astype(jnp.float32) * scale
    qi = jnp.arange(BAND_BLOCK)[:, None]
    kj = jnp.arange(3 * BAND_BLOCK)[None, :]
    rel = kj - BAND_BLOCK - qi
    kpos = jnp.arange(nb)[:, None, None] * BAND_BLOCK - BAND_BLOCK + kj[None]
    valid = (jnp.abs(rel)[None] <= WINDOW) & (kpos >= 0) & (kpos < S)
    s = jnp.where(valid[None, :, None, None], s, NEG_INF)
    sk = sink.astype(jnp.float32).reshape(SWA_KV_HEADS, SWA_GROUP)[None, None, :, :, None, None]
    m = jnp.maximum(jnp.max(s, axis=-1, keepdims=True), sk)
    e = jnp.exp(s - m)
    denom = jnp.sum(e, axis=-1, keepdims=True) + jnp.exp(sk - m)
    pr = (e / denom).astype(v.dtype)
    o = jnp.einsum('bnhgqk,bnkhd->bnqhgd', pr, vband)
    return o.reshape(B, S, Hq * d)


def token_mixers(x, cos_s, sin_s, cos_m, sin_m, g_mix, w_in, b_gate, g_cq, w_uq, g_ckv, w_ukv,
                 g_qn_mla, g_kn_mla, g_qn_swa, g_kn_swa, sink, w_br_mla, w_br_swa, w_out):
    B, S, _ = x.shape
    h = rmsnorm(x, g_mix)
    z = h @ w_in
    c_q, c_kv, k_r, q_s, k_s, v_s, a_mla, a_swa = jnp.split(
        z, [OFF_CQ, OFF_CKV, OFF_KR, OFF_QS, OFF_KS, OFF_VS, OFF_GA], axis=-1)

    q_m = (rmsnorm(c_q, g_cq) @ w_uq).reshape(B, S, MLA_HEADS, MLA_QK)
    kv_m = (rmsnorm(c_kv, g_ckv) @ w_ukv).reshape(B, S, MLA_HEADS, MLA_NOPE + MLA_V)
    k_nope, v_m = jnp.split(kv_m, [MLA_NOPE], axis=-1)
    k_rope = jnp.broadcast_to(k_r[:, :, None, :], (B, S, MLA_HEADS, MLA_ROPE))
    q_m = rmsnorm(q_m, g_qn_mla)
    k_m = rmsnorm(jnp.concatenate([k_nope, k_rope], axis=-1), g_kn_mla)
    q_m = jnp.concatenate([q_m[..., :MLA_NOPE], apply_rope(q_m[..., MLA_NOPE:], cos_m, sin_m)], axis=-1)
    k_m = jnp.concatenate([k_m[..., :MLA_NOPE], apply_rope(k_m[..., MLA_NOPE:], cos_m, sin_m)], axis=-1)
    o_m = dense_attention(q_m, k_m, v_m, MLA_QK ** -0.5).reshape(B, S, MLA_HEADS * MLA_V)

    q_w = apply_rope(rmsnorm(q_s.reshape(B, S, SWA_HEADS, SWA_HD), g_qn_swa), cos_s, sin_s)
    k_w = apply_rope(rmsnorm(k_s.reshape(B, S, SWA_KV_HEADS, SWA_HD), g_kn_swa), cos_s, sin_s)
    v_w = v_s.reshape(B, S, SWA_KV_HEADS, SWA_HD)
    o_w = banded_gqa_attention(q_w, k_w, v_w, sink, SWA_HD ** -0.5)

    gate_m = jax.nn.sigmoid(a_mla + b_gate[:D_MODEL])
    gate_w = jax.nn.sigmoid(a_swa + b_gate[D_MODEL:])
    merged = gate_m * (o_m @ w_br_mla) + gate_w * (o_w @ w_br_swa)
    return x + merged @ w_out


def moe_ffn(x, g_moe, w_router, router_bias, w_exp_gu, w_exp_down, w_sh_gu, w_sh_down):
    B, S, D = x.shape
    h = rmsnorm(x, g_moe).reshape(B * S, D)
    T = h.shape[0]
    scores = jax.nn.sigmoid((h @ w_router).astype(jnp.float32))
    sel = scores + router_bias.astype(jnp.float32)
    grp = sel.reshape(T, N_GROUPS, EXPERTS_PER_GROUP)
    gscore = jnp.sum(lax.top_k(grp, 2)[0], axis=-1)
    _, gidx = lax.top_k(gscore, TOPK_GROUPS)
    gmask = jnp.sum(jax.nn.one_hot(gidx, N_GROUPS, dtype=jnp.float32), axis=-2) > 0
    emask = jnp.repeat(gmask, EXPERTS_PER_GROUP, axis=-1)
    _, eidx = lax.top_k(jnp.where(emask, sel, NEG_INF), TOP_K)
    w = jnp.take_along_axis(scores, eidx, axis=-1)
    w = w / jnp.sum(w, axis=-1, keepdims=True) * ROUTED_SCALE
    combine = jnp.einsum('tk,tke->te', w, jax.nn.one_hot(eidx, N_EXPERTS, dtype=jnp.float32)).astype(h.dtype)

    xc = h.reshape(T // MOE_CHUNK, MOE_CHUNK, D)
    cc = combine.reshape(T // MOE_CHUNK, MOE_CHUNK, N_EXPERTS)

    def chunk(args):
        xt, ct = args
        gu = jnp.einsum('td,edf->tef', xt, w_exp_gu)
        g, u = jnp.split(gu, 2, axis=-1)
        hid = jax.nn.silu(g) * u * ct[..., None]
        return jnp.einsum('tef,efd->td', hid, w_exp_down)

    routed = lax.map(chunk, (xc, cc)).reshape(T, D)
    sg, su = jnp.split(h @ w_sh_gu, 2, axis=-1)
    shared = (jax.nn.silu(sg) * su) @ w_sh_down
    return x + (routed + shared).reshape(B, S, D)


def per_layer_embedding(x, p_i, g_ple, w_ple_gate, b_ple, w_ple_proj):
    gate = jax.nn.sigmoid(rmsnorm(x, g_ple) @ w_ple_gate + b_ple)
    return x + gate * (p_i @ w_ple_proj)


def setup_inputs(seed: int = 0) -> dict:
    key = jax.random.key(seed)
    ks = iter(jax.random.split(key, 40))
    L = DEPTH

    def dense(shape, fan_in):
        return jax.random.normal(next(ks), shape, jnp.float32) * (fan_in ** -0.5)

    def gain(n):
        return 1.0 + 0.05 * jax.random.normal(next(ks), (L, n), jnp.float32)

    x = jax.random.normal(next(ks), (BATCH, SEQ, D_MODEL), jnp.float32)
    p = jax.random.normal(next(ks), (DEPTH, BATCH, SEQ, PLE_DIM), jnp.float32)
    offset = jax.random.randint(next(ks), (BATCH, 1), 0, 1024, dtype=jnp.int32)
    positions = offset + jnp.arange(SEQ, dtype=jnp.int32)[None, :]
    return {
        "x": x,
        "p": p,
        "positions": positions,
        "g_mix": gain(D_MODEL),
        "w_in": dense((L, D_MODEL, IN_WIDTH), D_MODEL),
        "b_gate": 0.1 * jax.random.normal(next(ks), (L, 2 * D_MODEL), jnp.float32),
        "g_cq": gain(MLA_Q_RANK),
        "w_uq": dense((L, MLA_Q_RANK, MLA_HEADS * MLA_QK), MLA_Q_RANK),
        "g_ckv": gain(MLA_KV_RANK),
        "w_ukv": dense((L, MLA_KV_RANK, MLA_HEADS * (MLA_NOPE + MLA_V)), MLA_KV_RANK),
        "g_qn_mla": gain(MLA_QK),
        "g_kn_mla": gain(MLA_QK),
        "g_qn_swa": gain(SWA_HD),
        "g_kn_swa": gain(SWA_HD),
        "sink": 0.5 * jax.random.normal(next(ks), (L, SWA_HEADS), jnp.float32),
        "w_br_mla": dense((L, MLA_HEADS * MLA_V, D_MODEL), MLA_HEADS * MLA_V),
        "w_br_swa": dense((L, SWA_HEADS * SWA_HD, D_MODEL), SWA_HEADS * SWA_HD),
        "w_out": dense((L, D_MODEL, D_MODEL), D_MODEL),
        "g_moe": gain(D_MODEL),
        "w_router": dense((L, D_MODEL, N_EXPERTS), D_MODEL),
        "router_bias": 0.01 * jax.random.normal(next(ks), (L, N_EXPERTS), jnp.float32),
        "w_exp_gu": dense((L, N_EXPERTS, D_MODEL, 2 * EXPERT_FF), D_MODEL),
        "w_exp_down": dense((L, N_EXPERTS, EXPERT_FF, D_MODEL), EXPERT_FF),
        "w_sh_gu": dense((L, D_MODEL, 2 * SHARED_FF), D_MODEL),
        "w_sh_down": dense((L, SHARED_FF, D_MODEL), SHARED_FF),
        "g_ple": gain(D_MODEL),
        "w_ple_gate": dense((L, D_MODEL, D_MODEL), D_MODEL),
        "b_ple": 0.1 * jax.random.normal(next(ks), (L, D_MODEL), jnp.float32),
        "w_ple_proj": dense((L, PLE_DIM, D_MODEL), PLE_DIM),
    }


def reference(x, p, positions, g_mix, w_in, b_gate, g_cq, w_uq, g_ckv, w_ukv, g_qn_mla, g_kn_mla,
              g_qn_swa, g_kn_swa, sink, w_br_mla, w_br_swa, w_out, g_moe, w_router, router_bias,
              w_exp_gu, w_exp_down, w_sh_gu, w_sh_down, g_ple, w_ple_gate, b_ple, w_ple_proj):
    cos_s, sin_s = rope_cos_sin(positions, SWA_HD)
    cos_m, sin_m = rope_cos_sin(positions, MLA_ROPE)
    for i in range(DEPTH):
        x = token_mixers(x, cos_s, sin_s, cos_m, sin_m, g_mix[i], w_in[i], b_gate[i], g_cq[i], w_uq[i],
                         g_ckv[i], w_ukv[i], g_qn_mla[i], g_kn_mla[i], g_qn_swa[i], g_kn_swa[i], sink[i],
                         w_br_mla[i], w_br_swa[i], w_out[i])
        x = moe_ffn(x, g_moe[i], w_router[i], router_bias[i], w_exp_gu[i], w_exp_down[i],
                    w_sh_gu[i], w_sh_down[i])
        x = per_layer_embedding(x, p[i], g_ple[i], w_ple_gate[i], b_ple[i], w_ple_proj[i])
    return x
```

```python
import functools

import jax
import jax.numpy as jnp
from jax import lax
from jax.experimental import pallas as pl
from jax.experimental.pallas import tpu as pltpu

D_MODEL = 1024
PLE_DIM = 256
ROPE_THETA = 10000.0
EPS = 1e-6
NEG_INF = -1e30

MLA_HEADS = 8
MLA_Q_RANK = 384
MLA_KV_RANK = 256
MLA_NOPE = 64
MLA_ROPE = 32
MLA_QK = MLA_NOPE + MLA_ROPE
MLA_V = 64

SWA_HEADS = 8
SWA_KV_HEADS = 2
SWA_GROUP = SWA_HEADS // SWA_KV_HEADS
SWA_HD = 64
WINDOW = 128

OFF_CQ = MLA_Q_RANK
OFF_CKV = OFF_CQ + MLA_KV_RANK
OFF_KR = OFF_CKV + MLA_ROPE
OFF_QS = OFF_KR + SWA_HEADS * SWA_HD
OFF_KS = OFF_QS + SWA_KV_HEADS * SWA_HD
OFF_VS = OFF_KS + SWA_KV_HEADS * SWA_HD
OFF_GA = OFF_VS + D_MODEL

N_EXPERTS = 64
TOP_K = 8
N_GROUPS = 8
TOPK_GROUPS = 4
EXPERTS_PER_GROUP = N_EXPERTS // N_GROUPS
EXPERT_FF = 256
SHARED_FF = 256
ROUTED_SCALE = 2.5

LANES = 128
VMEM_LIMIT = 56 * 1024 * 1024

BF16 = jnp.bfloat16
F32 = jnp.float32

C_CQ = 0
C_CKV = C_CQ + MLA_Q_RANK
C_KR = C_CKV + MLA_KV_RANK
C_QS = C_KR + LANES
C_KS = C_QS + SWA_HEADS * LANES
C_VS = C_KS + SWA_KV_HEADS * LANES
C_GA = C_VS + SWA_KV_HEADS * LANES
C_END = C_GA + 2 * D_MODEL


def _full(shape):
    nd = len(shape)
    return pl.BlockSpec(shape, lambda *_: (0,) * nd)


def _dot(a, b):
    return jnp.dot(a, b, preferred_element_type=F32)


def _dot_nt(a, b, precision=None):
    return lax.dot_general(a, b, (((1,), (1,)), ((), ())), precision=precision,
                           preferred_element_type=F32)


def _rms(v, n):
    return v * lax.rsqrt(jnp.sum(v * v, axis=-1, keepdims=True) * (1.0 / n) + EPS)


def _rope(v, cos, sin_a, sin_b, half):
    return v * cos + pltpu.roll(v, LANES - half, 1) * sin_a + pltpu.roll(v, half, 1) * sin_b


def _pre_kernel(x_ref, pos_ref, w_all_ref, w_uq_ref, w_k_ref, w_v_ref, g_mix_ref, g_cq_ref,
                g_ckv_ref, gq_m_ref, gk_m_ref, gq_s_ref, gk_s_ref, b_gate_ref, rope_ref,
                qm_ref, km_ref, vm_ref, qw_ref, kw_ref, vw_ref, gate_ref):
    x = x_ref[...]
    h = (_rms(x, D_MODEL) * g_mix_ref[...]).astype(BF16)
    z = _dot(h, w_all_ref[...])

    pos = pos_ref[...].astype(F32)
    rope = rope_ref[...]
    ang_m = pos * rope[0:1, :]
    cos_m = jnp.cos(ang_m)
    sin_m = jnp.sin(ang_m)
    sin_ma = sin_m * rope[1:2, :]
    sin_mb = sin_m * rope[2:3, :]
    ang_s = pos * rope[3:4, :]
    cos_s = jnp.cos(ang_s)
    sin_s = jnp.sin(ang_s)
    sin_sa = sin_s * rope[4:5, :]
    sin_sb = sin_s * rope[5:6, :]

    cqn = (_rms(z[:, C_CQ:C_CKV], MLA_Q_RANK) * g_cq_ref[...]).astype(BF16)
    q = _dot(cqn, w_uq_ref[...])
    gq_m = gq_m_ref[...]
    for hd in range(MLA_HEADS):
        qh = _rms(q[:, hd * LANES:(hd + 1) * LANES], MLA_QK) * gq_m
        qh = _rope(qh, cos_m, sin_ma, sin_mb, MLA_ROPE // 2) * (MLA_QK ** -0.5)
        qm_ref[:, hd * LANES:(hd + 1) * LANES] = qh.astype(BF16)

    ckvn = (_rms(z[:, C_CKV:C_KR], MLA_KV_RANK) * g_ckv_ref[...]).astype(BF16)
    kn = _dot(ckvn, w_k_ref[...])
    vm_ref[...] = _dot(ckvn, w_v_ref[...]).astype(BF16)
    gk_m = gk_m_ref[...]
    kr = z[:, C_KR:C_QS]
    ss_kr = jnp.sum(kr * kr, axis=-1, keepdims=True)
    kr_rot = _rope(kr * gk_m, cos_m, sin_ma, sin_mb, MLA_ROPE // 2)
    for hd in range(MLA_HEADS):
        kh = kn[:, hd * LANES:(hd + 1) * LANES]
        ss = jnp.sum(kh * kh, axis=-1, keepdims=True) + ss_kr
        sc = lax.rsqrt(ss * (1.0 / MLA_QK) + EPS)
        km_ref[:, hd * LANES:(hd + 1) * LANES] = ((kh * gk_m + kr_rot) * sc).astype(BF16)

    gq_s = gq_s_ref[...]
    for hd in range(SWA_HEADS):
        qh = _rms(z[:, C_QS + hd * LANES:C_QS + (hd + 1) * LANES], SWA_HD) * gq_s
        qh = _rope(qh, cos_s, sin_sa, sin_sb, SWA_HD // 2) * (SWA_HD ** -0.5)
        qw_ref[:, hd * LANES:(hd + 1) * LANES] = qh.astype(BF16)
    gk_s = gk_s_ref[...]
    for hd in range(SWA_KV_HEADS):
        kh = _rms(z[:, C_KS + hd * LANES:C_KS + (hd + 1) * LANES], SWA_HD) * gk_s
        kh = _rope(kh, cos_s, sin_sa, sin_sb, SWA_HD // 2)
        kw_ref[:, hd * LANES:(hd + 1) * LANES] = kh.astype(BF16)
    vw_ref[...] = z[:, C_VS:C_GA].astype(BF16)

    gate_ref[...] = jax.nn.sigmoid(z[:, C_GA:C_END] + b_gate_ref[...]).astype(BF16)


def _pre_attention(x2d, pos2d, w_all, w_uq, w_k, w_v, g_mix, g_cq, g_ckv, gq_m, gk_m, gq_s,
                   gk_s, b_gate, rope_tab, tm):
    T = x2d.shape[0]
    row = lambda n: pl.BlockSpec((tm, n), lambda i: (i, 0))
    outs = [(MLA_HEADS * LANES, BF16), (MLA_HEADS * LANES, BF16), (MLA_HEADS * MLA_V, BF16),
            (SWA_HEADS * LANES, BF16), (SWA_KV_HEADS * LANES, BF16),
            (SWA_KV_HEADS * LANES, BF16), (2 * D_MODEL, BF16)]
    consts = [w_all, w_uq, w_k, w_v, g_mix, g_cq, g_ckv, gq_m, gk_m, gq_s, gk_s, b_gate, rope_tab]
    return pl.pallas_call(
        _pre_kernel,
        grid=(T // tm,),
        in_specs=[row(D_MODEL), row(1)] + [_full(c.shape) for c in consts],
        out_specs=[row(n) for n, _ in outs],
        out_shape=[jax.ShapeDtypeStruct((T, n), dt) for n, dt in outs],
        compiler_params=pltpu.CompilerParams(dimension_semantics=("arbitrary",),
                                             vmem_limit_bytes=VMEM_LIMIT),
        name="pre_attention",
    )(x2d, pos2d, *consts)


def _half_masks(dtype):
    lane = lax.broadcasted_iota(jnp.int32, (1, LANES), 1)
    lo = (lane < LANES // 2).astype(dtype)
    return lo, 1 - lo


def _mla_kernel(q_ref, k_ref, v_ref, o_ref):
    v = v_ref[...]
    masks = _half_masks(v.dtype)
    acc = None
    for hh in range(2):
        q = q_ref[:, hh * LANES:(hh + 1) * LANES]
        k = k_ref[:, hh * LANES:(hh + 1) * LANES]
        s = _dot_nt(q, k)
        m = jnp.max(s, axis=-1, keepdims=True)
        p = jnp.exp(s - m)
        l = jnp.sum(p, axis=-1, keepdims=True)
        o = _dot(p.astype(BF16), v * masks[hh]) / l
        acc = o if acc is None else acc + o
    o_ref[...] = acc.astype(o_ref.dtype)


def _mla_attention(qm, km, vm, B, S, tq):
    pairs = MLA_HEADS // 2
    q3 = qm.reshape(B, S, MLA_HEADS * LANES)
    k3 = km.reshape(B, S, MLA_HEADS * LANES)
    v3 = vm.reshape(B, S, MLA_HEADS * MLA_V)
    return pl.pallas_call(
        _mla_kernel,
        grid=(B, pairs, S // tq),
        in_specs=[pl.BlockSpec((None, tq, 2 * LANES), lambda b, p, i: (b, i, p)),
                  pl.BlockSpec((None, S, 2 * LANES), lambda b, p, i: (b, 0, p)),
                  pl.BlockSpec((None, S, LANES), lambda b, p, i: (b, 0, p))],
        out_specs=pl.BlockSpec((None, tq, LANES), lambda b, p, i: (b, i, p)),
        out_shape=jax.ShapeDtypeStruct((B, S, MLA_HEADS * MLA_V), BF16),
        compiler_params=pltpu.CompilerParams(
            dimension_semantics=("arbitrary", "arbitrary", "arbitrary"),
            vmem_limit_bytes=VMEM_LIMIT),
        name="mla_attention",
    )(q3, k3, v3)


def _swa_kernel(sink_ref, q_ref, k_ref, v_ref, o_ref, *, tq, S):
    pair = pl.program_id(1)
    i = pl.program_id(2)
    tk = tq + 2 * WINDOW
    kstart = pl.multiple_of(jnp.clip(i * tq - WINDOW, 0, S - tk), WINDOW)
    k = k_ref[pl.ds(kstart, tk), :]
    v = v_ref[pl.ds(kstart, tk), :]
    masks = _half_masks(v.dtype)
    qpos = i * tq + lax.broadcasted_iota(jnp.int32, (tq, 1), 0)
    kpos = kstart + lax.broadcasted_iota(jnp.int32, (1, tk), 1)
    valid = jnp.abs(kpos - qpos) <= WINDOW
    acc = None
    for hh in range(2):
        q = q_ref[:, hh * LANES:(hh + 1) * LANES]
        s = jnp.where(valid, _dot_nt(q, k), NEG_INF)
        sk = sink_ref[2 * pair + hh]
        m = jnp.maximum(jnp.max(s, axis=-1, keepdims=True), sk)
        e = jnp.exp(s - m)
        denom = jnp.sum(e, axis=-1, keepdims=True) + jnp.exp(sk - m)
        o = _dot(e.astype(BF16), v * masks[hh]) / denom
        acc = o if acc is None else acc + o
    o_ref[...] = acc.astype(o_ref.dtype)


def _swa_attention(qw, kw, vw, sink, B, S, tq):
    pairs = SWA_HEADS // 2
    q3 = qw.reshape(B, S, SWA_HEADS * LANES)
    k3 = kw.reshape(B, S, SWA_KV_HEADS * LANES)
    v3 = vw.reshape(B, S, SWA_KV_HEADS * LANES)
    pairs_per_kv = SWA_GROUP // 2
    return pl.pallas_call(
        functools.partial(_swa_kernel, tq=tq, S=S),
        grid=(B, pairs, S // tq),
        in_specs=[pl.BlockSpec(memory_space=pltpu.SMEM),
                  pl.BlockSpec((None, tq, 2 * LANES), lambda b, p, i: (b, i, p)),
                  pl.BlockSpec((None, S, LANES), lambda b, p, i: (b, 0, p // pairs_per_kv)),
                  pl.BlockSpec((None, S, LANES), lambda b, p, i: (b, 0, p // pairs_per_kv))],
        out_specs=pl.BlockSpec((None, tq, LANES), lambda b, p, i: (b, i, p)),
        out_shape=jax.ShapeDtypeStruct((B, S, SWA_HEADS * SWA_HD), BF16),
        compiler_params=pltpu.CompilerParams(
            dimension_semantics=("arbitrary", "arbitrary", "arbitrary"),
            vmem_limit_bytes=VMEM_LIMIT),
        name="swa_attention",
    )(sink, q3, k3, v3)


def _beats(vj, vi, j_first):
    return (vj >= vi) if j_first else (vj > vi)


def _route(scores, sel):
    G, P = N_GROUPS, EXPERTS_PER_GROUP
    groups = [sel[g * P:(g + 1) * P, :] for g in range(G)]
    row = lax.broadcasted_iota(jnp.int32, (P, 1), 0)
    gscore = []
    for vg in groups:
        m1 = jnp.max(vg, axis=0, keepdims=True)
        first = jnp.min(jnp.where(vg == m1, row, P), axis=0, keepdims=True)
        m2 = jnp.max(jnp.where(row == first, -jnp.inf, vg), axis=0, keepdims=True)
        gscore.append(m1 + m2)
    masked = []
    for g in range(G):
        rank = jnp.zeros_like(gscore[g], dtype=jnp.int32)
        for g2 in range(G):
            if g2 != g:
                rank = rank + _beats(gscore[g2], gscore[g], g2 < g).astype(jnp.int32)
        masked.append(jnp.where(rank < TOPK_GROUPS, groups[g], NEG_INF))
    ranks = [jnp.zeros((P, masked[0].shape[1]), jnp.int32) for _ in range(G)]
    for gj in range(G):
        for r in range(P):
            vj = masked[gj][r:r + 1, :]
            for gi in range(G):
                if gi < gj:
                    b = vj > masked[gi]
                elif gi > gj:
                    b = vj >= masked[gi]
                else:
                    b = (vj > masked[gi]) | ((vj == masked[gi]) & (row > r))
                ranks[gi] = ranks[gi] + b.astype(jnp.int32)
    picked = [jnp.where(ranks[g] < TOP_K, scores[g * P:(g + 1) * P, :], 0.0) for g in range(G)]
    total = picked[0]
    for g in range(1, G):
        total = total + picked[g]
    denom = jnp.sum(total, axis=0, keepdims=True)
    return [pk / denom * ROUTED_SCALE for pk in picked]


def _post_kernel(om_ref, ow_ref, gate_ref, x_ref, wbm_ref, wbw_ref, wout_ref, g_moe_ref, wr_ref,
                 rb_ref, wsgu_ref, wsd_ref, x1_ref, h2_ref, comb_ref):
    am = _dot(om_ref[...], wbm_ref[...])
    aw = _dot(ow_ref[...], wbw_ref[...])
    gates = gate_ref[...].astype(F32)
    merged = gates[:, :D_MODEL] * am + gates[:, D_MODEL:] * aw
    x1 = x_ref[...] + _dot(merged.astype(BF16), wout_ref[...])

    h2 = _rms(x1, D_MODEL) * g_moe_ref[...]
    h2b = h2.astype(BF16)
    h2_ref[...] = h2b

    logits = _dot_nt(wr_ref[...], h2, precision=lax.Precision.HIGHEST)
    scores = jax.nn.sigmoid(logits)
    comb = _route(scores, scores + rb_ref[...])
    for g in range(N_GROUPS):
        comb_ref[g * EXPERTS_PER_GROUP:(g + 1) * EXPERTS_PER_GROUP, :] = comb[g]

    sgu = _dot(h2b, wsgu_ref[...])
    sh = jax.nn.silu(sgu[:, :SHARED_FF]) * sgu[:, SHARED_FF:]
    x1_ref[...] = x1 + _dot(sh.astype(BF16), wsd_ref[...])


def _post_attention(om, ow, gates, x2d, wbm, wbw, wout, g_moe, wr_t, rbias, wsgu, wsd, tm):
    T = x2d.shape[0]
    row = lambda n: pl.BlockSpec((tm, n), lambda i: (i, 0))
    consts = [wbm, wbw, wout, g_moe, wr_t, rbias, wsgu, wsd]
    return pl.pallas_call(
        _post_kernel,
        grid=(T // tm,),
        in_specs=[row(om.shape[1]), row(ow.shape[1]), row(2 * D_MODEL), row(D_MODEL)]
        + [_full(c.shape) for c in consts],
        out_specs=[row(D_MODEL), row(D_MODEL), pl.BlockSpec((N_EXPERTS, tm), lambda i: (0, i))],
        out_shape=[jax.ShapeDtypeStruct((T, D_MODEL), F32),
                   jax.ShapeDtypeStruct((T, D_MODEL), BF16),
                   jax.ShapeDtypeStruct((N_EXPERTS, T), F32)],
        compiler_params=pltpu.CompilerParams(dimension_semantics=("arbitrary",),
                                             vmem_limit_bytes=VMEM_LIMIT),
        name="post_attention",
    )(om, ow, gates, x2d, *consts)


def _moe_kernel(h_ref, comb_ref, wgu_ref, wd_ref, o_ref):
    e = pl.program_id(1)

    @pl.when(e == 0)
    def _():
        o_ref[...] = jnp.zeros_like(o_ref)

    gu = _dot(h_ref[...], wgu_ref[...].astype(BF16))
    rows = lax.broadcasted_iota(jnp.int32, (comb_ref.shape[1], EXPERT_FF), 0)
    pick = ((rows & (N_EXPERTS - 1)) == e) & (rows < 3 * N_EXPERTS)
    cexp = _dot(comb_ref[...], pick.astype(BF16))
    hid = jax.nn.silu(gu[:, :EXPERT_FF]) * gu[:, EXPERT_FF:] * cexp
    o_ref[...] += _dot(hid.astype(BF16), wd_ref[...].astype(BF16))


def _moe_dense(h2b, comb3, w_gu, w_d, tm):
    T = h2b.shape[0]
    return pl.pallas_call(
        _moe_kernel,
        grid=(T // tm, N_EXPERTS),
        in_specs=[pl.BlockSpec((tm, D_MODEL), lambda i, e: (i, 0)),
                  pl.BlockSpec((tm, comb3.shape[1]), lambda i, e: (i, 0)),
                  pl.BlockSpec((None, D_MODEL, 2 * EXPERT_FF), lambda i, e: (e, 0, 0)),
                  pl.BlockSpec((None, EXPERT_FF, D_MODEL), lambda i, e: (e, 0, 0))],
        out_specs=pl.BlockSpec((tm, D_MODEL), lambda i, e: (i, 0)),
        out_shape=jax.ShapeDtypeStruct((T, D_MODEL), F32),
        compiler_params=pltpu.CompilerParams(dimension_semantics=("arbitrary", "arbitrary"),
                                             vmem_limit_bytes=VMEM_LIMIT),
        name="moe_experts",
    )(h2b, comb3, w_gu, w_d)


def _ple_kernel(x1_ref, r_ref, p_ref, g_ref, wg_ref, b_ref, wp_ref, o_ref):
    x2 = x1_ref[...] + r_ref[...]
    hn = (_rms(x2, D_MODEL) * g_ref[...]).astype(BF16)
    gate = jax.nn.sigmoid(_dot(hn, wg_ref[...]) + b_ref[...])
    o_ref[...] = x2 + gate * _dot(p_ref[...].astype(BF16), wp_ref[...])


def _ple(x1s, routed, p2d, g_ple, wg, b_ple, wp, tm):
    T = x1s.shape[0]
    row = lambda n: pl.BlockSpec((tm, n), lambda i: (i, 0))
    consts = [g_ple, wg, b_ple, wp]
    return pl.pallas_call(
        _ple_kernel,
        grid=(T // tm,),
        in_specs=[row(D_MODEL), row(D_MODEL), row(PLE_DIM)] + [_full(c.shape) for c in consts],
        out_specs=row(D_MODEL),
        out_shape=jax.ShapeDtypeStruct((T, D_MODEL), F32),
        compiler_params=pltpu.CompilerParams(dimension_semantics=("arbitrary",),
                                             vmem_limit_bytes=VMEM_LIMIT),
        name="ple",
    )(x1s, routed, p2d, *consts)


def _pad_heads(w, heads, dim):
    k = w.shape[0]
    w = w.reshape(k, heads, dim)
    return jnp.pad(w, ((0, 0), (0, 0), (0, LANES - dim))).reshape(k, heads * LANES)


def _pad_lanes(g, offset=0):
    out = jnp.zeros((1, LANES), F32)
    return out.at[0, offset:offset + g.shape[0]].set(g)


def _rope_table():
    def block(dim, offset):
        inv = 1.0 / (ROPE_THETA ** (jnp.arange(0, dim, 2, dtype=F32) / dim))
        half = dim // 2
        z = jnp.zeros((LANES,), F32)
        invl = z.at[offset:offset + half].set(inv).at[offset + half:offset + dim].set(inv)
        sa = z.at[offset:offset + half].set(-1.0)
        sb = z.at[offset + half:offset + dim].set(1.0)
        return [invl, sa, sb]
    rows = block(MLA_ROPE, MLA_NOPE) + block(SWA_HD, 0)
    rows += [jnp.zeros((LANES,), F32)] * 2
    return jnp.stack(rows)


def _layer(x2d, p2d, pos2d, B, S, g_mix, w_in, b_gate, g_cq, w_uq, g_ckv, w_ukv, g_qn_mla, g_kn_mla,
           g_qn_swa, g_kn_swa, sink, w_br_mla, w_br_swa, w_out, g_moe, w_router, router_bias,
           w_exp_gu, w_exp_down, w_sh_gu, w_sh_down, g_ple, w_ple_gate, b_ple, w_ple_proj):
    w_kr = jnp.zeros((D_MODEL, LANES), F32).at[:, MLA_NOPE:MLA_QK].set(w_in[:, OFF_CKV:OFF_KR])
    w_vs = w_in[:, OFF_KS:OFF_VS].reshape(D_MODEL, SWA_KV_HEADS, 1, SWA_HD)
    w_vs = jnp.broadcast_to(w_vs, (D_MODEL, SWA_KV_HEADS, 2, SWA_HD)).reshape(D_MODEL, -1)
    w_all = jnp.concatenate([
        w_in[:, :OFF_CKV], w_kr,
        _pad_heads(w_in[:, OFF_KR:OFF_QS], SWA_HEADS, SWA_HD),
        _pad_heads(w_in[:, OFF_QS:OFF_KS], SWA_KV_HEADS, SWA_HD),
        w_vs, w_in[:, OFF_VS:]], axis=1).astype(BF16)
    assert w_all.shape[1] == C_END
    w_uq_p = _pad_heads(w_uq, MLA_HEADS, MLA_QK).astype(BF16)
    w_ukv3 = w_ukv.reshape(MLA_KV_RANK, MLA_HEADS, MLA_NOPE + MLA_V)
    w_k = _pad_heads(w_ukv3[:, :, :MLA_NOPE].reshape(MLA_KV_RANK, -1), MLA_HEADS, MLA_NOPE).astype(BF16)
    w_v = w_ukv3[:, :, MLA_NOPE:].reshape(MLA_KV_RANK, -1).astype(BF16)

    qm, km, vm, qw, kw, vw, gates = _pre_attention(
        x2d, pos2d, w_all, w_uq_p, w_k, w_v, g_mix[None], g_cq[None], g_ckv[None],
        _pad_lanes(g_qn_mla), _pad_lanes(g_kn_mla), _pad_lanes(g_qn_swa), _pad_lanes(g_kn_swa),
        b_gate[None], _rope_table(), tm=256)

    om = _mla_attention(qm, km, vm, B, S, tq=512).reshape(B * S, -1)
    ow = _swa_attention(qw, kw, vw, sink, B, S, tq=256).reshape(B * S, -1)

    x1s, h2b, comb_t = _post_attention(
        om, ow, gates, x2d, w_br_mla.astype(BF16), w_br_swa.astype(BF16), w_out.astype(BF16),
        g_moe[None], w_router.T, router_bias[:, None], w_sh_gu.astype(BF16),
        w_sh_down.astype(BF16), tm=256)

    comb = comb_t.T
    c_hi = comb.astype(BF16)
    c_mid = (comb - c_hi.astype(F32)).astype(BF16)
    c_lo = (comb - c_hi.astype(F32) - c_mid.astype(F32)).astype(BF16)
    comb3 = jnp.concatenate([c_hi, c_mid, c_lo, jnp.zeros_like(c_hi)], axis=1)

    routed = _moe_dense(h2b, comb3, w_exp_gu, w_exp_down, tm=2048)

    return _ple(x1s, routed, p2d, g_ple[None], w_ple_gate.astype(BF16), b_ple[None],
                w_ple_proj.astype(BF16), tm=256)


def kernel(x, p, positions, g_mix, w_in, b_gate, g_cq, w_uq, g_ckv, w_ukv, g_qn_mla, g_kn_mla, g_qn_swa, g_kn_swa, sink, w_br_mla, w_br_swa, w_out, g_moe, w_router, router_bias, w_exp_gu, w_exp_down, w_sh_gu, w_sh_down, g_ple, w_ple_gate, b_ple, w_ple_proj):
    B, S, D = x.shape
    x2d = x.reshape(B * S, D)
    pos2d = positions.reshape(B * S, 1)
    for i in range(p.shape[0]):
        x2d = _layer(x2d, p[i].reshape(B * S, -1), pos2d, B, S, g_mix[i], w_in[i], b_gate[i],
                     g_cq[i], w_uq[i], g_ckv[i], w_ukv[i], g_qn_mla[i], g_kn_mla[i], g_qn_swa[i],
                     g_kn_swa[i], sink[i], w_br_mla[i], w_br_swa[i], w_out[i], g_moe[i],
                     w_router[i], router_bias[i], w_exp_gu[i], w_exp_down[i], w_sh_gu[i],
                     w_sh_down[i], g_ple[i], w_ple_gate[i], b_ple[i], w_ple_proj[i])
    return x2d.reshape(B, S, D)
```

```python
import functools

import jax
import jax.numpy as jnp
from jax import lax
from jax.experimental import pallas as pl
from jax.experimental.pallas import tpu as pltpu

D_MODEL = 1024
PLE_DIM = 256
ROPE_THETA = 10000.0
EPS = 1e-6
NEG_INF = -1e30

MLA_HEADS = 8
MLA_Q_RANK = 384
MLA_KV_RANK = 256
MLA_NOPE = 64
MLA_ROPE = 32
MLA_QK = MLA_NOPE + MLA_ROPE
MLA_V = 64

SWA_HEADS = 8
SWA_KV_HEADS = 2
SWA_GROUP = SWA_HEADS // SWA_KV_HEADS
SWA_HD = 64
WINDOW = 128

OFF_CQ = MLA_Q_RANK
OFF_CKV = OFF_CQ + MLA_KV_RANK
OFF_KR = OFF_CKV + MLA_ROPE
OFF_QS = OFF_KR + SWA_HEADS * SWA_HD
OFF_KS = OFF_QS + SWA_KV_HEADS * SWA_HD
OFF_VS = OFF_KS + SWA_KV_HEADS * SWA_HD
OFF_GA = OFF_VS + D_MODEL

N_EXPERTS = 64
TOP_K = 8
N_GROUPS = 8
TOPK_GROUPS = 4
EXPERTS_PER_GROUP = N_EXPERTS // N_GROUPS
EXPERT_FF = 256
SHARED_FF = 256
ROUTED_SCALE = 2.5

LANES = 128
ROW_WORDS = D_MODEL // 2
ROW_SUB = ROW_WORDS // LANES
MOE_CHUNK = 2048
MOE_TILE = 128
SLOT_ALIGN = 16
COMBINE_BLOCK = 256
VMEM_LIMIT = 56 * 1024 * 1024

BF16 = jnp.bfloat16
F32 = jnp.float32

C_CQ = 0
C_CKV = C_CQ + MLA_Q_RANK
C_KR = C_CKV + MLA_KV_RANK
C_QS = C_KR + LANES
C_KS = C_QS + SWA_HEADS * LANES
C_VS = C_KS + SWA_KV_HEADS * LANES
C_GA = C_VS + SWA_KV_HEADS * LANES
C_END = C_GA + 2 * D_MODEL


def _full(shape):
    nd = len(shape)
    return pl.BlockSpec(shape, lambda *_: (0,) * nd)


def _dot(a, b):
    return jnp.dot(a, b, preferred_element_type=F32)


def _dot_nt(a, b, precision=None):
    return lax.dot_general(a, b, (((1,), (1,)), ((), ())), precision=precision,
                           preferred_element_type=F32)


def _rms(v, n):
    return v * lax.rsqrt(jnp.sum(v * v, axis=-1, keepdims=True) * (1.0 / n) + EPS)


def _rope(v, cos, sin_a, sin_b, half):
    return v * cos + pltpu.roll(v, LANES - half, 1) * sin_a + pltpu.roll(v, half, 1) * sin_b


def _pre_kernel(x_ref, pos_ref, w_all_ref, w_uq_ref, w_k_ref, w_v_ref, g_mix_ref, g_cq_ref,
                g_ckv_ref, gq_m_ref, gk_m_ref, gq_s_ref, gk_s_ref, b_gate_ref, rope_ref,
                qm_ref, km_ref, vm_ref, qw_ref, kw_ref, vw_ref, gate_ref):
    x = x_ref[...]
    h = (_rms(x, D_MODEL) * g_mix_ref[...]).astype(BF16)
    z = _dot(h, w_all_ref[...])

    pos = pos_ref[...].astype(F32)
    rope = rope_ref[...]
    ang_m = pos * rope[0:1, :]
    cos_m = jnp.cos(ang_m)
    sin_m = jnp.sin(ang_m)
    sin_ma = sin_m * rope[1:2, :]
    sin_mb = sin_m * rope[2:3, :]
    ang_s = pos * rope[3:4, :]
    cos_s = jnp.cos(ang_s)
    sin_s = jnp.sin(ang_s)
    sin_sa = sin_s * rope[4:5, :]
    sin_sb = sin_s * rope[5:6, :]

    cqn = (_rms(z[:, C_CQ:C_CKV], MLA_Q_RANK) * g_cq_ref[...]).astype(BF16)
    q = _dot(cqn, w_uq_ref[...])
    gq_m = gq_m_ref[...]
    for hd in range(MLA_HEADS):
        qh = _rms(q[:, hd * LANES:(hd + 1) * LANES], MLA_QK) * gq_m
        qh = _rope(qh, cos_m, sin_ma, sin_mb, MLA_ROPE // 2) * (MLA_QK ** -0.5)
        qm_ref[:, hd * LANES:(hd + 1) * LANES] = qh.astype(BF16)

    ckvn = (_rms(z[:, C_CKV:C_KR], MLA_KV_RANK) * g_ckv_ref[...]).astype(BF16)
    kn = _dot(ckvn, w_k_ref[...])
    vm_ref[...] = _dot(ckvn, w_v_ref[...]).astype(BF16)
    gk_m = gk_m_ref[...]
    kr = z[:, C_KR:C_QS]
    ss_kr = jnp.sum(kr * kr, axis=-1, keepdims=True)
    kr_rot = _rope(kr * gk_m, cos_m, sin_ma, sin_mb, MLA_ROPE // 2)
    for hd in range(MLA_HEADS):
        kh = kn[:, hd * LANES:(hd + 1) * LANES]
        ss = jnp.sum(kh * kh, axis=-1, keepdims=True) + ss_kr
        sc = lax.rsqrt(ss * (1.0 / MLA_QK) + EPS)
        km_ref[:, hd * LANES:(hd + 1) * LANES] = ((kh * gk_m + kr_rot) * sc).astype(BF16)

    gq_s = gq_s_ref[...]
    for hd in range(SWA_HEADS):
        qh = _rms(z[:, C_QS + hd * LANES:C_QS + (hd + 1) * LANES], SWA_HD) * gq_s
        qh = _rope(qh, cos_s, sin_sa, sin_sb, SWA_HD // 2) * (SWA_HD ** -0.5)
        qw_ref[:, hd * LANES:(hd + 1) * LANES] = qh.astype(BF16)
    gk_s = gk_s_ref[...]
    for hd in range(SWA_KV_HEADS):
        kh = _rms(z[:, C_KS + hd * LANES:C_KS + (hd + 1) * LANES], SWA_HD) * gk_s
        kh = _rope(kh, cos_s, sin_sa, sin_sb, SWA_HD // 2)
        kw_ref[:, hd * LANES:(hd + 1) * LANES] = kh.astype(BF16)
    vw_ref[...] = z[:, C_VS:C_GA].astype(BF16)

    gate_ref[...] = jax.nn.sigmoid(z[:, C_GA:C_END] + b_gate_ref[...]).astype(BF16)


def _pre_attention(x2d, pos2d, w_all, w_uq, w_k, w_v, g_mix, g_cq, g_ckv, gq_m, gk_m, gq_s,
                   gk_s, b_gate, rope_tab, tm):
    T = x2d.shape[0]
    row = lambda n: pl.BlockSpec((tm, n), lambda i: (i, 0))
    outs = [(MLA_HEADS * LANES, BF16), (MLA_HEADS * LANES, BF16), (MLA_HEADS * MLA_V, BF16),
            (SWA_HEADS * LANES, BF16), (SWA_KV_HEADS * LANES, BF16),
            (SWA_KV_HEADS * LANES, BF16), (2 * D_MODEL, BF16)]
    consts = [w_all, w_uq, w_k, w_v, g_mix, g_cq, g_ckv, gq_m, gk_m, gq_s, gk_s, b_gate, rope_tab]
    return pl.pallas_call(
        _pre_kernel,
        grid=(T // tm,),
        in_specs=[row(D_MODEL), row(1)] + [_full(c.shape) for c in consts],
        out_specs=[row(n) for n, _ in outs],
        out_shape=[jax.ShapeDtypeStruct((T, n), dt) for n, dt in outs],
        compiler_params=pltpu.CompilerParams(dimension_semantics=("arbitrary",),
                                             vmem_limit_bytes=VMEM_LIMIT),
        name="pre_attention",
    )(x2d, pos2d, *consts)


def _half_masks(dtype):
    lane = lax.broadcasted_iota(jnp.int32, (1, LANES), 1)
    lo = (lane < LANES // 2).astype(dtype)
    return lo, 1 - lo


def _mla_kernel(q_ref, k_ref, v_ref, o_ref):
    v = v_ref[...]
    masks = _half_masks(v.dtype)
    acc = None
    for hh in range(2):
        q = q_ref[:, hh * LANES:(hh + 1) * LANES]
        k = k_ref[:, hh * LANES:(hh + 1) * LANES]
        s = _dot_nt(q, k)
        m = jnp.max(s, axis=-1, keepdims=True)
        p = jnp.exp(s - m)
        l = jnp.sum(p, axis=-1, keepdims=True)
        o = _dot(p.astype(BF16), v * masks[hh]) / l
        acc = o if acc is None else acc + o
    o_ref[...] = acc.astype(o_ref.dtype)


def _mla_attention(qm, km, vm, B, S, tq):
    pairs = MLA_HEADS // 2
    q3 = qm.reshape(B, S, MLA_HEADS * LANES)
    k3 = km.reshape(B, S, MLA_HEADS * LANES)
    v3 = vm.reshape(B, S, MLA_HEADS * MLA_V)
    return pl.pallas_call(
        _mla_kernel,
        grid=(B, pairs, S // tq),
        in_specs=[pl.BlockSpec((None, tq, 2 * LANES), lambda b, p, i: (b, i, p)),
                  pl.BlockSpec((None, S, 2 * LANES), lambda b, p, i: (b, 0, p)),
                  pl.BlockSpec((None, S, LANES), lambda b, p, i: (b, 0, p))],
        out_specs=pl.BlockSpec((None, tq, LANES), lambda b, p, i: (b, i, p)),
        out_shape=jax.ShapeDtypeStruct((B, S, MLA_HEADS * MLA_V), BF16),
        compiler_params=pltpu.CompilerParams(
            dimension_semantics=("arbitrary", "arbitrary", "arbitrary"),
            vmem_limit_bytes=VMEM_LIMIT),
        name="mla_attention",
    )(q3, k3, v3)


def _swa_kernel(sink_ref, q_ref, k_ref, v_ref, o_ref, *, tq, S):
    pair = pl.program_id(1)
    i = pl.program_id(2)
    tk = tq + 2 * WINDOW
    kstart = pl.multiple_of(jnp.clip(i * tq - WINDOW, 0, S - tk), WINDOW)
    k = k_ref[pl.ds(kstart, tk), :]
    v = v_ref[pl.ds(kstart, tk), :]
    masks = _half_masks(v.dtype)
    qpos = i * tq + lax.broadcasted_iota(jnp.int32, (tq, 1), 0)
    kpos = kstart + lax.broadcasted_iota(jnp.int32, (1, tk), 1)
    valid = jnp.abs(kpos - qpos) <= WINDOW
    acc = None
    for hh in range(2):
        q = q_ref[:, hh * LANES:(hh + 1) * LANES]
        s = jnp.where(valid, _dot_nt(q, k), NEG_INF)
        sk = sink_ref[2 * pair + hh]
        m = jnp.maximum(jnp.max(s, axis=-1, keepdims=True), sk)
        e = jnp.exp(s - m)
        denom = jnp.sum(e, axis=-1, keepdims=True) + jnp.exp(sk - m)
        o = _dot(e.astype(BF16), v * masks[hh]) / denom
        acc = o if acc is None else acc + o
    o_ref[...] = acc.astype(o_ref.dtype)


def _swa_attention(qw, kw, vw, sink, B, S, tq):
    pairs = SWA_HEADS // 2
    q3 = qw.reshape(B, S, SWA_HEADS * LANES)
    k3 = kw.reshape(B, S, SWA_KV_HEADS * LANES)
    v3 = vw.reshape(B, S, SWA_KV_HEADS * LANES)
    pairs_per_kv = SWA_GROUP // 2
    return pl.pallas_call(
        functools.partial(_swa_kernel, tq=tq, S=S),
        grid=(B, pairs, S // tq),
        in_specs=[pl.BlockSpec(memory_space=pltpu.SMEM),
                  pl.BlockSpec((None, tq, 2 * LANES), lambda b, p, i: (b, i, p)),
                  pl.BlockSpec((None, S, LANES), lambda b, p, i: (b, 0, p // pairs_per_kv)),
                  pl.BlockSpec((None, S, LANES), lambda b, p, i: (b, 0, p // pairs_per_kv))],
        out_specs=pl.BlockSpec((None, tq, LANES), lambda b, p, i: (b, i, p)),
        out_shape=jax.ShapeDtypeStruct((B, S, SWA_HEADS * SWA_HD), BF16),
        compiler_params=pltpu.CompilerParams(
            dimension_semantics=("arbitrary", "arbitrary", "arbitrary"),
            vmem_limit_bytes=VMEM_LIMIT),
        name="swa_attention",
    )(sink, q3, k3, v3)


def _beats(vj, vi, j_first):
    return (vj >= vi) if j_first else (vj > vi)


def _route(scores, sel):
    G, P = N_GROUPS, EXPERTS_PER_GROUP
    groups = [sel[g * P:(g + 1) * P, :] for g in range(G)]
    row = lax.broadcasted_iota(jnp.int32, (P, 1), 0)
    gscore = []
    for vg in groups:
        m1 = jnp.max(vg, axis=0, keepdims=True)
        first = jnp.min(jnp.where(vg == m1, row, P), axis=0, keepdims=True)
        m2 = jnp.max(jnp.where(row == first, -jnp.inf, vg), axis=0, keepdims=True)
        gscore.append(m1 + m2)
    masked = []
    for g in range(G):
        rank = jnp.zeros_like(gscore[g], dtype=jnp.int32)
        for g2 in range(G):
            if g2 != g:
                rank = rank + _beats(gscore[g2], gscore[g], g2 < g).astype(jnp.int32)
        masked.append(jnp.where(rank < TOPK_GROUPS, groups[g], NEG_INF))
    ranks = [jnp.zeros((P, masked[0].shape[1]), jnp.int32) for _ in range(G)]
    for gj in range(G):
        for r in range(P):
            vj = masked[gj][r:r + 1, :]
            for gi in range(G):
                if gi < gj:
                    b = vj > masked[gi]
                elif gi > gj:
                    b = vj >= masked[gi]
                else:
                    b = (vj > masked[gi]) | ((vj == masked[gi]) & (row > r))
                ranks[gi] = ranks[gi] + b.astype(jnp.int32)
    picked = [jnp.where(ranks[g] < TOP_K, scores[g * P:(g + 1) * P, :], 0.0) for g in range(G)]
    total = picked[0]
    for g in range(1, G):
        total = total + picked[g]
    denom = jnp.sum(total, axis=0, keepdims=True)
    return [pk / denom * ROUTED_SCALE for pk in picked]


def _pack_pair(lo, hi):
    return pltpu.pack_elementwise([lo, hi], packed_dtype=BF16)


def _unpack_pair(word):
    lo = pltpu.unpack_elementwise(word, index=0, packed_dtype=BF16, unpacked_dtype=F32)
    hi = pltpu.unpack_elementwise(word, index=1, packed_dtype=BF16, unpacked_dtype=F32)
    return lo, hi


def _store_rows_dense(ref, words):
    for j in range(ROW_SUB):
        ref[pl.ds(j, words.shape[0], stride=ROW_SUB), :] = words[:, j * LANES:(j + 1) * LANES]


def _load_rows_dense(ref, rows):
    sub = ROW_SUB
    return jnp.concatenate([ref[pl.ds(j, rows, stride=sub), :] for j in range(sub)], axis=1)


def _post_kernel(om_ref, ow_ref, gate_ref, x_ref, wbm_ref, wbw_ref, wout_ref, g_moe_ref, wr_ref,
                 rb_ref, wsgu_ref, wsd_ref, x1_ref, hp_ref, comb_ref):
    am = _dot(om_ref[...], wbm_ref[...])
    aw = _dot(ow_ref[...], wbw_ref[...])
    gates = gate_ref[...].astype(F32)
    merged = gates[:, :D_MODEL] * am + gates[:, D_MODEL:] * aw
    x1 = x_ref[...] + _dot(merged.astype(BF16), wout_ref[...])

    h2 = _rms(x1, D_MODEL) * g_moe_ref[...]
    h2b = h2.astype(BF16)
    _store_rows_dense(hp_ref, _pack_pair(h2[:, :ROW_WORDS], h2[:, ROW_WORDS:]))

    logits = _dot_nt(wr_ref[...], h2, precision=lax.Precision.HIGHEST)
    scores = jax.nn.sigmoid(logits)
    comb = _route(scores, scores + rb_ref[...])
    for g in range(N_GROUPS):
        comb_ref[g * EXPERTS_PER_GROUP:(g + 1) * EXPERTS_PER_GROUP, :] = comb[g]

    sgu = _dot(h2b, wsgu_ref[...])
    sh = jax.nn.silu(sgu[:, :SHARED_FF]) * sgu[:, SHARED_FF:]
    x1_ref[...] = x1 + _dot(sh.astype(BF16), wsd_ref[...])


def _post_attention(om, ow, gates, x2d, wbm, wbw, wout, g_moe, wr_t, rbias, wsgu, wsd, tm):
    T = x2d.shape[0]
    row = lambda n: pl.BlockSpec((tm, n), lambda i: (i, 0))
    consts = [wbm, wbw, wout, g_moe, wr_t, rbias, wsgu, wsd]
    return pl.pallas_call(
        _post_kernel,
        grid=(T // tm,),
        in_specs=[row(om.shape[1]), row(ow.shape[1]), row(2 * D_MODEL), row(D_MODEL)]
        + [_full(c.shape) for c in consts],
        out_specs=[row(D_MODEL), pl.BlockSpec((tm * ROW_SUB, LANES), lambda i: (i, 0)),
                   pl.BlockSpec((N_EXPERTS, tm), lambda i: (0, i))],
        out_shape=[jax.ShapeDtypeStruct((T, D_MODEL), F32),
                   jax.ShapeDtypeStruct((T * ROW_SUB, LANES), jnp.uint32),
                   jax.ShapeDtypeStruct((N_EXPERTS, T), F32)],
        compiler_params=pltpu.CompilerParams(dimension_semantics=("arbitrary",),
                                             vmem_limit_bytes=VMEM_LIMIT),
        name="post_attention",
    )(om, ow, gates, x2d, *consts)


SLOT_ROWS = -(-(MOE_CHUNK * TOP_K + N_EXPERTS * (SLOT_ALIGN - 1) + MOE_TILE) // MOE_TILE) * MOE_TILE
DUMMY_SLOT = SLOT_ROWS - 1
PLAN_BLOCK = 256


def _plan_kernel(comb_ref, slot_ref, w_ref, off_ref, cnt_ref):
    comb = comb_ref[...]
    sel = comb > 0.0
    m = sel.astype(F32)
    mb = m.astype(BF16)
    r_i = lax.broadcasted_iota(jnp.int32, (PLAN_BLOCK, PLAN_BLOCK), 0)
    c_i = lax.broadcasted_iota(jnp.int32, (PLAN_BLOCK, PLAN_BLOCK), 1)
    before = (r_i < c_i).astype(BF16)
    carry = jnp.zeros((N_EXPERTS, 1), F32)
    ranks = []
    for b in range(MOE_CHUNK // PLAN_BLOCK):
        blk = slice(b * PLAN_BLOCK, (b + 1) * PLAN_BLOCK)
        ranks.append(_dot(mb[:, blk], before) + carry)
        carry = carry + jnp.sum(m[:, blk], axis=1, keepdims=True)
    rank = jnp.concatenate(ranks, axis=1)
    cnt = carry
    cnt_pad = jnp.floor((cnt + (SLOT_ALIGN - 1)) * (1.0 / SLOT_ALIGN)) * SLOT_ALIGN
    e_r = lax.broadcasted_iota(jnp.int32, (N_EXPERTS, N_EXPERTS), 0)
    e_c = lax.broadcasted_iota(jnp.int32, (N_EXPERTS, N_EXPERTS), 1)
    below = (e_c < e_r).astype(F32)
    off = jnp.dot(below, jnp.broadcast_to(cnt_pad, (N_EXPERTS, LANES)),
                  precision=lax.Precision.HIGHEST, preferred_element_type=F32)
    slot = off[:, :1] + rank
    kidx = _dot(below.astype(BF16), mb)
    row = lax.broadcasted_iota(jnp.int32, (TOP_K, 1), 0)
    slot_acc = jnp.zeros((TOP_K, MOE_CHUNK), F32)
    w_acc = jnp.zeros((TOP_K, MOE_CHUNK), F32)
    for k in range(TOP_K):
        pick = jnp.where(sel & (kidx == k), 1.0, 0.0)
        found = jnp.sum(pick, axis=0, keepdims=True) > 0.0
        s_k = jnp.where(found, jnp.sum(pick * slot, axis=0, keepdims=True), float(DUMMY_SLOT))
        w_k = jnp.sum(pick * comb, axis=0, keepdims=True)
        slot_acc = jnp.where(row == k, s_k, slot_acc)
        w_acc = jnp.where(row == k, w_k, w_acc)
    slot_ref[...] = slot_acc.astype(jnp.int32)
    w_ref[...] = w_acc
    off_ref[...] = off.astype(jnp.int32)
    cnt_ref[...] = jnp.broadcast_to(cnt, (N_EXPERTS, LANES)).astype(jnp.int32)


def _moe_plan(comb_t):
    T = comb_t.shape[1]
    nch = T // MOE_CHUNK
    per_pair = pl.BlockSpec((None, TOP_K, MOE_CHUNK), lambda c: (c, 0, 0))
    per_expert = pl.BlockSpec((None, N_EXPERTS, LANES), lambda c: (c, 0, 0))
    return pl.pallas_call(
        _plan_kernel,
        grid=(nch,),
        in_specs=[pl.BlockSpec((N_EXPERTS, MOE_CHUNK), lambda c: (0, c))],
        out_specs=[per_pair, per_pair, per_expert, per_expert],
        out_shape=[jax.ShapeDtypeStruct((nch, TOP_K, MOE_CHUNK), jnp.int32),
                   jax.ShapeDtypeStruct((nch, TOP_K, MOE_CHUNK), F32),
                   jax.ShapeDtypeStruct((nch, N_EXPERTS, LANES), jnp.int32),
                   jax.ShapeDtypeStruct((nch, N_EXPERTS, LANES), jnp.int32)],
        compiler_params=pltpu.CompilerParams(dimension_semantics=("arbitrary",),
                                             vmem_limit_bytes=VMEM_LIMIT),
        name="moe_plan",
    )(comb_t)


def _slab(ref, row):
    return ref.at[pl.ds(pl.multiple_of(row * ROW_SUB, ROW_SUB), ROW_SUB), :]


def _moe_kernel(off_ref, cnt_ref, slot_hbm, w_hbm, hp_ref, wgu_ref, wd_ref, o_ref,
                buf, clo, chi, slot_s, w_s, sem):
    c = pl.program_id(0)
    s = pl.program_id(1)

    @pl.when(s == 0)
    def _dispatch():
        slot_cp = pltpu.make_async_copy(slot_hbm.at[c], slot_s, sem.at[0])
        w_cp = pltpu.make_async_copy(w_hbm.at[c], w_s, sem.at[1])
        slot_cp.start()
        w_cp.start()

        @pl.when(c == 0)
        def _():
            buf[...] = jnp.zeros_like(buf)

        slot_cp.wait()
        w_cp.wait()

        def scatter(tb, carry):
            for tt in range(8):
                t = tb * 8 + tt
                slab = _slab(hp_ref, t)[...]
                for k in range(TOP_K):
                    _slab(buf, slot_s[k * MOE_CHUNK + t])[...] = slab
            return carry

        lax.fori_loop(0, MOE_CHUNK // 8, scatter, 0)

    @pl.when(s < N_EXPERTS)
    def _expert():
        n = cnt_ref[c, s]
        off = off_ref[c, s]

        def tile(r, carry):
            row0 = off + r * MOE_TILE
            view = buf.at[pl.ds(pl.multiple_of(row0 * ROW_SUB, SLOT_ALIGN * ROW_SUB),
                                MOE_TILE * ROW_SUB), :]
            xw = _load_rows_dense(view, MOE_TILE)
            x_lo, x_hi = _unpack_pair(xw)
            gu = (_dot(x_lo.astype(BF16), wgu_ref[:ROW_WORDS, :])
                  + _dot(x_hi.astype(BF16), wgu_ref[ROW_WORDS:, :]))
            hid = jax.nn.silu(gu[:, :EXPERT_FF]) * gu[:, EXPERT_FF:]
            y = _dot(hid.astype(BF16), wd_ref[...])
            mine = lax.broadcasted_iota(jnp.int32, (MOE_TILE, 1), 0) < (n - r * MOE_TILE)
            _store_rows_dense(view, _pack_pair(jnp.where(mine, y[:, :ROW_WORDS], x_lo),
                                               jnp.where(mine, y[:, ROW_WORDS:], x_hi)))
            return carry

        lax.fori_loop(0, (n + MOE_TILE - 1) // MOE_TILE, tile, 0)

    @pl.when(s >= N_EXPERTS)
    def _combine():
        t0 = (s - N_EXPERTS) * COMBINE_BLOCK

        def gather(i, carry):
            for tt in range(4):
                tl = i * 4 + tt
                t = t0 + tl
                acc_lo = jnp.zeros((ROW_SUB, LANES), F32)
                acc_hi = jnp.zeros((ROW_SUB, LANES), F32)
                for k in range(TOP_K):
                    lo, hi = _unpack_pair(_slab(buf, slot_s[k * MOE_CHUNK + t])[...])
                    wk = w_s[k * MOE_CHUNK + t]
                    acc_lo = acc_lo + wk * lo
                    acc_hi = acc_hi + wk * hi
                _slab(clo, tl)[...] = acc_lo
                _slab(chi, tl)[...] = acc_hi
            return carry

        lax.fori_loop(0, COMBINE_BLOCK // 4, gather, 0)
        o_ref[:, :ROW_WORDS] = _load_rows_dense(clo, COMBINE_BLOCK)
        o_ref[:, ROW_WORDS:] = _load_rows_dense(chi, COMBINE_BLOCK)


def _moe_sparse(offs, cnts, slots, wts, hp, w_gu, w_d):
    nch = offs.shape[0]
    T = nch * MOE_CHUNK
    blocks = MOE_CHUNK // COMBINE_BLOCK
    expert = lambda c, s, *_: (jnp.minimum(s, N_EXPERTS - 1), 0, 0)
    grid_spec = pltpu.PrefetchScalarGridSpec(
        num_scalar_prefetch=2,
        grid=(nch, N_EXPERTS + blocks),
        in_specs=[pl.BlockSpec(memory_space=pl.ANY),
                  pl.BlockSpec(memory_space=pl.ANY),
                  pl.BlockSpec((MOE_CHUNK * ROW_SUB, LANES), lambda c, s, *_: (c, 0)),
                  pl.BlockSpec((None, D_MODEL, 2 * EXPERT_FF), expert),
                  pl.BlockSpec((None, EXPERT_FF, D_MODEL), expert)],
        out_specs=pl.BlockSpec(
            (COMBINE_BLOCK, D_MODEL),
            lambda c, s, *_: (c * blocks + jnp.maximum(s - N_EXPERTS, 0), 0)),
        scratch_shapes=[pltpu.VMEM((SLOT_ROWS * ROW_SUB, LANES), jnp.uint32),
                        pltpu.VMEM((COMBINE_BLOCK * ROW_SUB, LANES), F32),
                        pltpu.VMEM((COMBINE_BLOCK * ROW_SUB, LANES), F32),
                        pltpu.SMEM((TOP_K * MOE_CHUNK,), jnp.int32),
                        pltpu.SMEM((TOP_K * MOE_CHUNK,), F32),
                        pltpu.SemaphoreType.DMA((2,))])
    return pl.pallas_call(
        _moe_kernel,
        grid_spec=grid_spec,
        out_shape=jax.ShapeDtypeStruct((T, D_MODEL), F32),
        compiler_params=pltpu.CompilerParams(dimension_semantics=("arbitrary", "arbitrary"),
                                             vmem_limit_bytes=VMEM_LIMIT),
        name="moe_experts",
    )(offs, cnts, slots, wts, hp, w_gu, w_d)


def _ple_kernel(x1_ref, r_ref, p_ref, g_ref, wg_ref, b_ref, wp_ref, o_ref):
    x2 = x1_ref[...] + r_ref[...]
    hn = (_rms(x2, D_MODEL) * g_ref[...]).astype(BF16)
    gate = jax.nn.sigmoid(_dot(hn, wg_ref[...]) + b_ref[...])
    o_ref[...] = x2 + gate * _dot(p_ref[...].astype(BF16), wp_ref[...])


def _ple(x1s, routed, p2d, g_ple, wg, b_ple, wp, tm):
    T = x1s.shape[0]
    row = lambda n: pl.BlockSpec((tm, n), lambda i: (i, 0))
    consts = [g_ple, wg, b_ple, wp]
    return pl.pallas_call(
        _ple_kernel,
        grid=(T // tm,),
        in_specs=[row(D_MODEL), row(D_MODEL), row(PLE_DIM)] + [_full(c.shape) for c in consts],
        out_specs=row(D_MODEL),
        out_shape=jax.ShapeDtypeStruct((T, D_MODEL), F32),
        compiler_params=pltpu.CompilerParams(dimension_semantics=("arbitrary",),
                                             vmem_limit_bytes=VMEM_LIMIT),
        name="ple",
    )(x1s, routed, p2d, *consts)


def _pad_heads(w, heads, dim):
    k = w.shape[0]
    w = w.reshape(k, heads, dim)
    return jnp.pad(w, ((0, 0), (0, 0), (0, LANES - dim))).reshape(k, heads * LANES)


def _pad_lanes(g, offset=0):
    out = jnp.zeros((1, LANES), F32)
    return out.at[0, offset:offset + g.shape[0]].set(g)


def _rope_table():
    def block(dim, offset):
        inv = 1.0 / (ROPE_THETA ** (jnp.arange(0, dim, 2, dtype=F32) / dim))
        half = dim // 2
        z = jnp.zeros((LANES,), F32)
        invl = z.at[offset:offset + half].set(inv).at[offset + half:offset + dim].set(inv)
        sa = z.at[offset:offset + half].set(-1.0)
        sb = z.at[offset + half:offset + dim].set(1.0)
        return [invl, sa, sb]
    rows = block(MLA_ROPE, MLA_NOPE) + block(SWA_HD, 0)
    rows += [jnp.zeros((LANES,), F32)] * 2
    return jnp.stack(rows)


def _layer(x2d, p2d, pos2d, B, S, g_mix, w_in, b_gate, g_cq, w_uq, g_ckv, w_ukv, g_qn_mla, g_kn_mla,
           g_qn_swa, g_kn_swa, sink, w_br_mla, w_br_swa, w_out, g_moe, w_router, router_bias,
           w_exp_gu, w_exp_down, w_sh_gu, w_sh_down, g_ple, w_ple_gate, b_ple, w_ple_proj):
    w_kr = jnp.zeros((D_MODEL, LANES), F32).at[:, MLA_NOPE:MLA_QK].set(w_in[:, OFF_CKV:OFF_KR])
    w_vs = w_in[:, OFF_KS:OFF_VS].reshape(D_MODEL, SWA_KV_HEADS, 1, SWA_HD)
    w_vs = jnp.broadcast_to(w_vs, (D_MODEL, SWA_KV_HEADS, 2, SWA_HD)).reshape(D_MODEL, -1)
    w_all = jnp.concatenate([
        w_in[:, :OFF_CKV], w_kr,
        _pad_heads(w_in[:, OFF_KR:OFF_QS], SWA_HEADS, SWA_HD),
        _pad_heads(w_in[:, OFF_QS:OFF_KS], SWA_KV_HEADS, SWA_HD),
        w_vs, w_in[:, OFF_VS:]], axis=1).astype(BF16)
    assert w_all.shape[1] == C_END
    w_uq_p = _pad_heads(w_uq, MLA_HEADS, MLA_QK).astype(BF16)
    w_ukv3 = w_ukv.reshape(MLA_KV_RANK, MLA_HEADS, MLA_NOPE + MLA_V)
    w_k = _pad_heads(w_ukv3[:, :, :MLA_NOPE].reshape(MLA_KV_RANK, -1), MLA_HEADS, MLA_NOPE).astype(BF16)
    w_v = w_ukv3[:, :, MLA_NOPE:].reshape(MLA_KV_RANK, -1).astype(BF16)

    qm, km, vm, qw, kw, vw, gates = _pre_attention(
        x2d, pos2d, w_all, w_uq_p, w_k, w_v, g_mix[None], g_cq[None], g_ckv[None],
        _pad_lanes(g_qn_mla), _pad_lanes(g_kn_mla), _pad_lanes(g_qn_swa), _pad_lanes(g_kn_swa),
        b_gate[None], _rope_table(), tm=256)

    om = _mla_attention(qm, km, vm, B, S, tq=512).reshape(B * S, -1)
    ow = _swa_attention(qw, kw, vw, sink, B, S, tq=256).reshape(B * S, -1)

    x1s, hp, comb_t = _post_attention(
        om, ow, gates, x2d, w_br_mla.astype(BF16), w_br_swa.astype(BF16), w_out.astype(BF16),
        g_moe[None], w_router.T, router_bias[:, None], w_sh_gu.astype(BF16),
        w_sh_down.astype(BF16), tm=256)

    slots, wts, offs, cnts = _moe_plan(comb_t)
    routed = _moe_sparse(offs[:, :, 0], cnts[:, :, 0], slots.reshape(slots.shape[0], -1),
                         wts.reshape(wts.shape[0], -1), hp, w_exp_gu.astype(BF16),
                         w_exp_down.astype(BF16))

    return _ple(x1s, routed, p2d, g_ple[None], w_ple_gate.astype(BF16), b_ple[None],
                w_ple_proj.astype(BF16), tm=256)


def kernel(x, p, positions, g_mix, w_in, b_gate, g_cq, w_uq, g_ckv, w_ukv, g_qn_mla, g_kn_mla, g_qn_swa, g_kn_swa, sink, w_br_mla, w_br_swa, w_out, g_moe, w_router, router_bias, w_exp_gu, w_exp_down, w_sh_gu, w_sh_down, g_ple, w_ple_gate, b_ple, w_ple_proj):
    B, S, D = x.shape
    x2d = x.reshape(B * S, D)
    pos2d = positions.reshape(B * S, 1)
    for i in range(p.shape[0]):
        x2d = _layer(x2d, p[i].reshape(B * S, -1), pos2d, B, S, g_mix[i], w_in[i], b_gate[i],
                     g_cq[i], w_uq[i], g_ckv[i], w_ukv[i], g_qn_mla[i], g_kn_mla[i], g_qn_swa[i],
                     g_kn_swa[i], sink[i], w_br_mla[i], w_br_swa[i], w_out[i], g_moe[i],
                     w_router[i], router_bias[i], w_exp_gu[i], w_exp_down[i], w_sh_gu[i],
                     w_sh_down[i], g_ple[i], w_ple_gate[i], b_ple[i], w_ple_proj[i])
    return x2d.reshape(B, S, D)
```

```python
import functools

import jax
import jax.numpy as jnp
from jax import lax
from jax.experimental import pallas as pl
from jax.experimental.pallas import tpu as pltpu

D_MODEL = 1024
PLE_DIM = 256
ROPE_THETA = 10000.0
EPS = 1e-6
NEG_INF = -1e30

MLA_HEADS = 8
MLA_Q_RANK = 384
MLA_KV_RANK = 256
MLA_NOPE = 64
MLA_ROPE = 32
MLA_QK = MLA_NOPE + MLA_ROPE
MLA_V = 64

SWA_HEADS = 8
SWA_KV_HEADS = 2
SWA_GROUP = SWA_HEADS // SWA_KV_HEADS
SWA_HD = 64
WINDOW = 128

OFF_CQ = MLA_Q_RANK
OFF_CKV = OFF_CQ + MLA_KV_RANK
OFF_KR = OFF_CKV + MLA_ROPE
OFF_QS = OFF_KR + SWA_HEADS * SWA_HD
OFF_KS = OFF_QS + SWA_KV_HEADS * SWA_HD
OFF_VS = OFF_KS + SWA_KV_HEADS * SWA_HD
OFF_GA = OFF_VS + D_MODEL

N_EXPERTS = 64
TOP_K = 8
N_GROUPS = 8
TOPK_GROUPS = 4
EXPERTS_PER_GROUP = N_EXPERTS // N_GROUPS
EXPERT_FF = 256
SHARED_FF = 256
ROUTED_SCALE = 2.5

LANES = 128
ROW_WORDS = D_MODEL // 2
ROW_SUB = ROW_WORDS // LANES
MOE_CHUNK = 2048
MOE_TILE = 128
MOE_TILE_BIG = 256
SLOT_ALIGN = 16
COMBINE_BLOCK = 256
PRE_TILE = 256
ROW_TILE = 512
MLA_Q_TILE = 512
SWA_Q_TILE = 256
VMEM_LIMIT = 56 * 1024 * 1024

BF16 = jnp.bfloat16
F32 = jnp.float32

C_CQ = 0
C_CKV = C_CQ + MLA_Q_RANK
C_KR = C_CKV + MLA_KV_RANK
C_QS = C_KR + LANES
C_KS = C_QS + SWA_HEADS * LANES
C_VS = C_KS + SWA_KV_HEADS * LANES
C_GA = C_VS + SWA_KV_HEADS * LANES
C_END = C_GA + 2 * D_MODEL


def _full(shape):
    nd = len(shape)
    return pl.BlockSpec(shape, lambda *_: (0,) * nd)


def _dot(a, b):
    return jnp.dot(a, b, preferred_element_type=F32)


def _dot_nt(a, b, precision=None):
    return lax.dot_general(a, b, (((1,), (1,)), ((), ())), precision=precision,
                           preferred_element_type=F32)


def _rms(v, n):
    return v * lax.rsqrt(jnp.sum(v * v, axis=-1, keepdims=True) * (1.0 / n) + EPS)


def _rope(v, cos, sin_a, sin_b, half):
    return v * cos + pltpu.roll(v, LANES - half, 1) * sin_a + pltpu.roll(v, half, 1) * sin_b


def _pre_kernel(x_ref, pos_ref, w_all_ref, w_uq_ref, w_k_ref, w_v_ref, g_mix_ref, g_cq_ref,
                g_ckv_ref, gq_m_ref, gk_m_ref, gq_s_ref, gk_s_ref, b_gate_ref, rope_ref,
                qm_ref, km_ref, vm_ref, qw_ref, kw_ref, vw_ref, gate_ref):
    x = x_ref[...]
    h = (_rms(x, D_MODEL) * g_mix_ref[...]).astype(BF16)
    z = _dot(h, w_all_ref[...])

    pos = pos_ref[...].astype(F32)
    rope = rope_ref[...]
    ang_m = pos * rope[0:1, :]
    cos_m = jnp.cos(ang_m)
    sin_m = jnp.sin(ang_m)
    sin_ma = sin_m * rope[1:2, :]
    sin_mb = sin_m * rope[2:3, :]
    ang_s = pos * rope[3:4, :]
    cos_s = jnp.cos(ang_s)
    sin_s = jnp.sin(ang_s)
    sin_sa = sin_s * rope[4:5, :]
    sin_sb = sin_s * rope[5:6, :]

    cqn = (_rms(z[:, C_CQ:C_CKV], MLA_Q_RANK) * g_cq_ref[...]).astype(BF16)
    q = _dot(cqn, w_uq_ref[...])
    gq_m = gq_m_ref[...]
    for hd in range(MLA_HEADS):
        qh = _rms(q[:, hd * LANES:(hd + 1) * LANES], MLA_QK) * gq_m
        qh = _rope(qh, cos_m, sin_ma, sin_mb, MLA_ROPE // 2) * (MLA_QK ** -0.5)
        qm_ref[:, hd * LANES:(hd + 1) * LANES] = qh.astype(BF16)

    ckvn = (_rms(z[:, C_CKV:C_KR], MLA_KV_RANK) * g_ckv_ref[...]).astype(BF16)
    kn = _dot(ckvn, w_k_ref[...])
    vm_ref[...] = _dot(ckvn, w_v_ref[...]).astype(BF16)
    gk_m = gk_m_ref[...]
    kr = z[:, C_KR:C_QS]
    ss_kr = jnp.sum(kr * kr, axis=-1, keepdims=True)
    kr_rot = _rope(kr * gk_m, cos_m, sin_ma, sin_mb, MLA_ROPE // 2)
    for hd in range(MLA_HEADS):
        kh = kn[:, hd * LANES:(hd + 1) * LANES]
        ss = jnp.sum(kh * kh, axis=-1, keepdims=True) + ss_kr
        sc = lax.rsqrt(ss * (1.0 / MLA_QK) + EPS)
        km_ref[:, hd * LANES:(hd + 1) * LANES] = ((kh * gk_m + kr_rot) * sc).astype(BF16)

    gq_s = gq_s_ref[...]
    for hd in range(SWA_HEADS):
        qh = _rms(z[:, C_QS + hd * LANES:C_QS + (hd + 1) * LANES], SWA_HD) * gq_s
        qh = _rope(qh, cos_s, sin_sa, sin_sb, SWA_HD // 2) * (SWA_HD ** -0.5)
        qw_ref[:, hd * LANES:(hd + 1) * LANES] = qh.astype(BF16)
    gk_s = gk_s_ref[...]
    for hd in range(SWA_KV_HEADS):
        kh = _rms(z[:, C_KS + hd * LANES:C_KS + (hd + 1) * LANES], SWA_HD) * gk_s
        kh = _rope(kh, cos_s, sin_sa, sin_sb, SWA_HD // 2)
        kw_ref[:, hd * LANES:(hd + 1) * LANES] = kh.astype(BF16)
    vw_ref[...] = z[:, C_VS:C_GA].astype(BF16)

    gate_ref[...] = jax.nn.sigmoid(z[:, C_GA:C_END] + b_gate_ref[...]).astype(BF16)


def _pre_attention(x2d, pos2d, w_all, w_uq, w_k, w_v, g_mix, g_cq, g_ckv, gq_m, gk_m, gq_s,
                   gk_s, b_gate, rope_tab, tm):
    T = x2d.shape[0]
    row = lambda n: pl.BlockSpec((tm, n), lambda i: (i, 0))
    outs = [(MLA_HEADS * LANES, BF16), (MLA_HEADS * LANES, BF16), (MLA_HEADS * MLA_V, BF16),
            (SWA_HEADS * LANES, BF16), (SWA_KV_HEADS * LANES, BF16),
            (SWA_KV_HEADS * LANES, BF16), (2 * D_MODEL, BF16)]
    consts = [w_all, w_uq, w_k, w_v, g_mix, g_cq, g_ckv, gq_m, gk_m, gq_s, gk_s, b_gate, rope_tab]
    return pl.pallas_call(
        _pre_kernel,
        grid=(T // tm,),
        in_specs=[row(D_MODEL), row(1)] + [_full(c.shape) for c in consts],
        out_specs=[row(n) for n, _ in outs],
        out_shape=[jax.ShapeDtypeStruct((T, n), dt) for n, dt in outs],
        compiler_params=pltpu.CompilerParams(dimension_semantics=("arbitrary",),
                                             vmem_limit_bytes=VMEM_LIMIT),
        name="pre_attention",
    )(x2d, pos2d, *consts)


def _half_masks(dtype):
    lane = lax.broadcasted_iota(jnp.int32, (1, LANES), 1)
    lo = (lane < LANES // 2).astype(dtype)
    return lo, 1 - lo


def _mla_kernel(q_ref, k_ref, v_ref, o_ref):
    v = v_ref[...]
    masks = _half_masks(v.dtype)
    acc = None
    for hh in range(2):
        q = q_ref[:, hh * LANES:(hh + 1) * LANES]
        k = k_ref[:, hh * LANES:(hh + 1) * LANES]
        s = _dot_nt(q, k)
        m = jnp.max(s, axis=-1, keepdims=True)
        p = jnp.exp(s - m)
        l = jnp.sum(p, axis=-1, keepdims=True)
        o = _dot(p.astype(BF16), v * masks[hh]) / l
        acc = o if acc is None else acc + o
    o_ref[...] = acc.astype(o_ref.dtype)


def _mla_attention(qm, km, vm, B, S, tq):
    pairs = MLA_HEADS // 2
    q3 = qm.reshape(B, S, MLA_HEADS * LANES)
    k3 = km.reshape(B, S, MLA_HEADS * LANES)
    v3 = vm.reshape(B, S, MLA_HEADS * MLA_V)
    return pl.pallas_call(
        _mla_kernel,
        grid=(B, pairs, S // tq),
        in_specs=[pl.BlockSpec((None, tq, 2 * LANES), lambda b, p, i: (b, i, p)),
                  pl.BlockSpec((None, S, 2 * LANES), lambda b, p, i: (b, 0, p)),
                  pl.BlockSpec((None, S, LANES), lambda b, p, i: (b, 0, p))],
        out_specs=pl.BlockSpec((None, tq, LANES), lambda b, p, i: (b, i, p)),
        out_shape=jax.ShapeDtypeStruct((B, S, MLA_HEADS * MLA_V), BF16),
        compiler_params=pltpu.CompilerParams(
            dimension_semantics=("arbitrary", "arbitrary", "arbitrary"),
            vmem_limit_bytes=VMEM_LIMIT),
        name="mla_attention",
    )(q3, k3, v3)


def _swa_kernel(sink_ref, q_ref, k_ref, v_ref, o_ref, *, tq, S):
    pair = pl.program_id(1)
    i = pl.program_id(2)
    tk = tq + 2 * WINDOW
    kstart = pl.multiple_of(jnp.clip(i * tq - WINDOW, 0, S - tk), WINDOW)
    k = k_ref[pl.ds(kstart, tk), :]
    v = v_ref[pl.ds(kstart, tk), :]
    masks = _half_masks(v.dtype)
    qpos = i * tq + lax.broadcasted_iota(jnp.int32, (tq, 1), 0)
    kpos = kstart + lax.broadcasted_iota(jnp.int32, (1, tk), 1)
    valid = jnp.abs(kpos - qpos) <= WINDOW
    acc = None
    for hh in range(2):
        q = q_ref[:, hh * LANES:(hh + 1) * LANES]
        s = jnp.where(valid, _dot_nt(q, k), NEG_INF)
        sk = sink_ref[2 * pair + hh]
        m = jnp.maximum(jnp.max(s, axis=-1, keepdims=True), sk)
        e = jnp.exp(s - m)
        denom = jnp.sum(e, axis=-1, keepdims=True) + jnp.exp(sk - m)
        o = _dot(e.astype(BF16), v * masks[hh]) / denom
        acc = o if acc is None else acc + o
    o_ref[...] = acc.astype(o_ref.dtype)


def _swa_attention(qw, kw, vw, sink, B, S, tq):
    pairs = SWA_HEADS // 2
    q3 = qw.reshape(B, S, SWA_HEADS * LANES)
    k3 = kw.reshape(B, S, SWA_KV_HEADS * LANES)
    v3 = vw.reshape(B, S, SWA_KV_HEADS * LANES)
    pairs_per_kv = SWA_GROUP // 2
    return pl.pallas_call(
        functools.partial(_swa_kernel, tq=tq, S=S),
        grid=(B, pairs, S // tq),
        in_specs=[pl.BlockSpec(memory_space=pltpu.SMEM),
                  pl.BlockSpec((None, tq, 2 * LANES), lambda b, p, i: (b, i, p)),
                  pl.BlockSpec((None, S, LANES), lambda b, p, i: (b, 0, p // pairs_per_kv)),
                  pl.BlockSpec((None, S, LANES), lambda b, p, i: (b, 0, p // pairs_per_kv))],
        out_specs=pl.BlockSpec((None, tq, LANES), lambda b, p, i: (b, i, p)),
        out_shape=jax.ShapeDtypeStruct((B, S, SWA_HEADS * SWA_HD), BF16),
        compiler_params=pltpu.CompilerParams(
            dimension_semantics=("arbitrary", "arbitrary", "arbitrary"),
            vmem_limit_bytes=VMEM_LIMIT),
        name="swa_attention",
    )(sink, q3, k3, v3)


def _beats(vj, vi, j_first):
    return (vj >= vi) if j_first else (vj > vi)


def _route(scores, sel):
    G, P = N_GROUPS, EXPERTS_PER_GROUP
    groups = [sel[g * P:(g + 1) * P, :] for g in range(G)]
    row = lax.broadcasted_iota(jnp.int32, (P, 1), 0)
    gscore = []
    for vg in groups:
        m1 = jnp.max(vg, axis=0, keepdims=True)
        first = jnp.min(jnp.where(vg == m1, row, P), axis=0, keepdims=True)
        m2 = jnp.max(jnp.where(row == first, -jnp.inf, vg), axis=0, keepdims=True)
        gscore.append(m1 + m2)
    masked = []
    for g in range(G):
        rank = jnp.zeros_like(gscore[g], dtype=jnp.int32)
        for g2 in range(G):
            if g2 != g:
                rank = rank + _beats(gscore[g2], gscore[g], g2 < g).astype(jnp.int32)
        masked.append(jnp.where(rank < TOPK_GROUPS, groups[g], NEG_INF))
    ranks = [jnp.zeros((P, masked[0].shape[1]), jnp.int32) for _ in range(G)]
    for gj in range(G):
        for r in range(P):
            vj = masked[gj][r:r + 1, :]
            for gi in range(G):
                if gi < gj:
                    b = vj > masked[gi]
                elif gi > gj:
                    b = vj >= masked[gi]
                else:
                    b = (vj > masked[gi]) | ((vj == masked[gi]) & (row > r))
                ranks[gi] = ranks[gi] + b.astype(jnp.int32)
    picked = [jnp.where(ranks[g] < TOP_K, scores[g * P:(g + 1) * P, :], 0.0) for g in range(G)]
    total = picked[0]
    for g in range(1, G):
        total = total + picked[g]
    denom = jnp.sum(total, axis=0, keepdims=True)
    return [pk / denom * ROUTED_SCALE for pk in picked]


def _pack_pair(lo, hi):
    return pltpu.pack_elementwise([lo, hi], packed_dtype=BF16)


def _unpack_pair(word):
    lo = pltpu.unpack_elementwise(word, index=0, packed_dtype=BF16, unpacked_dtype=F32)
    hi = pltpu.unpack_elementwise(word, index=1, packed_dtype=BF16, unpacked_dtype=F32)
    return lo, hi


def _store_rows_dense(ref, words):
    for j in range(ROW_SUB):
        ref[pl.ds(j, words.shape[0], stride=ROW_SUB), :] = words[:, j * LANES:(j + 1) * LANES]


def _load_rows_dense(ref, rows):
    sub = ROW_SUB
    return jnp.concatenate([ref[pl.ds(j, rows, stride=sub), :] for j in range(sub)], axis=1)


def _post_kernel(om_ref, ow_ref, gate_ref, x_ref, wbm_ref, wbw_ref, wout_ref, g_moe_ref, wr_ref,
                 rb_ref, wsgu_ref, wsd_ref, x1_ref, hp_ref, comb_ref):
    am = _dot(om_ref[...], wbm_ref[...])
    aw = _dot(ow_ref[...], wbw_ref[...])
    gates = gate_ref[...].astype(F32)
    merged = gates[:, :D_MODEL] * am + gates[:, D_MODEL:] * aw
    x1 = x_ref[...] + _dot(merged.astype(BF16), wout_ref[...])

    h2 = _rms(x1, D_MODEL) * g_moe_ref[...]
    h2b = h2.astype(BF16)
    _store_rows_dense(hp_ref, _pack_pair(h2[:, :ROW_WORDS], h2[:, ROW_WORDS:]))

    logits = _dot_nt(wr_ref[...], h2, precision=lax.Precision.HIGHEST)
    scores = jax.nn.sigmoid(logits)
    comb = _route(scores, scores + rb_ref[...])
    for g in range(N_GROUPS):
        comb_ref[g * EXPERTS_PER_GROUP:(g + 1) * EXPERTS_PER_GROUP, :] = comb[g]

    sgu = _dot(h2b, wsgu_ref[...])
    sh = jax.nn.silu(sgu[:, :SHARED_FF]) * sgu[:, SHARED_FF:]
    x1_ref[...] = x1 + _dot(sh.astype(BF16), wsd_ref[...])


def _post_attention(om, ow, gates, x2d, wbm, wbw, wout, g_moe, wr_t, rbias, wsgu, wsd, tm):
    T = x2d.shape[0]
    row = lambda n: pl.BlockSpec((tm, n), lambda i: (i, 0))
    consts = [wbm, wbw, wout, g_moe, wr_t, rbias, wsgu, wsd]
    return pl.pallas_call(
        _post_kernel,
        grid=(T // tm,),
        in_specs=[row(om.shape[1]), row(ow.shape[1]), row(2 * D_MODEL), row(D_MODEL)]
        + [_full(c.shape) for c in consts],
        out_specs=[row(D_MODEL), pl.BlockSpec((tm * ROW_SUB, LANES), lambda i: (i, 0)),
                   pl.BlockSpec((N_EXPERTS, tm), lambda i: (0, i))],
        out_shape=[jax.ShapeDtypeStruct((T, D_MODEL), F32),
                   jax.ShapeDtypeStruct((T * ROW_SUB, LANES), jnp.uint32),
                   jax.ShapeDtypeStruct((N_EXPERTS, T), F32)],
        compiler_params=pltpu.CompilerParams(dimension_semantics=("arbitrary",),
                                             vmem_limit_bytes=VMEM_LIMIT),
        name="post_attention",
    )(om, ow, gates, x2d, *consts)


SLOT_ROWS = -(-(MOE_CHUNK * TOP_K + N_EXPERTS * (SLOT_ALIGN - 1) + MOE_TILE_BIG) // MOE_TILE) * MOE_TILE
DUMMY_SLOT = SLOT_ROWS - 1
PLAN_BLOCK = 256


def _plan_kernel(comb_ref, slot_ref, w_ref, off_ref, cnt_ref):
    comb = comb_ref[...]
    sel = comb > 0.0
    m = sel.astype(F32)
    mb = m.astype(BF16)
    r_i = lax.broadcasted_iota(jnp.int32, (PLAN_BLOCK, PLAN_BLOCK), 0)
    c_i = lax.broadcasted_iota(jnp.int32, (PLAN_BLOCK, PLAN_BLOCK), 1)
    before = (r_i < c_i).astype(BF16)
    carry = jnp.zeros((N_EXPERTS, 1), F32)
    ranks = []
    for b in range(MOE_CHUNK // PLAN_BLOCK):
        blk = slice(b * PLAN_BLOCK, (b + 1) * PLAN_BLOCK)
        ranks.append(_dot(mb[:, blk], before) + carry)
        carry = carry + jnp.sum(m[:, blk], axis=1, keepdims=True)
    rank = jnp.concatenate(ranks, axis=1)
    cnt = carry
    cnt_pad = jnp.floor((cnt + (SLOT_ALIGN - 1)) * (1.0 / SLOT_ALIGN)) * SLOT_ALIGN
    e_r = lax.broadcasted_iota(jnp.int32, (N_EXPERTS, N_EXPERTS), 0)
    e_c = lax.broadcasted_iota(jnp.int32, (N_EXPERTS, N_EXPERTS), 1)
    below = (e_c < e_r).astype(F32)
    off = jnp.dot(below, jnp.broadcast_to(cnt_pad, (N_EXPERTS, LANES)),
                  precision=lax.Precision.HIGHEST, preferred_element_type=F32)
    slot = off[:, :1] + rank
    kidx = _dot(below.astype(BF16), mb)
    row = lax.broadcasted_iota(jnp.int32, (TOP_K, 1), 0)
    slot_acc = jnp.zeros((TOP_K, MOE_CHUNK), F32)
    w_acc = jnp.zeros((TOP_K, MOE_CHUNK), F32)
    for k in range(TOP_K):
        pick = jnp.where(sel & (kidx == k), 1.0, 0.0)
        found = jnp.sum(pick, axis=0, keepdims=True) > 0.0
        s_k = jnp.where(found, jnp.sum(pick * slot, axis=0, keepdims=True), float(DUMMY_SLOT))
        w_k = jnp.sum(pick * comb, axis=0, keepdims=True)
        slot_acc = jnp.where(row == k, s_k, slot_acc)
        w_acc = jnp.where(row == k, w_k, w_acc)
    slot_ref[...] = slot_acc.astype(jnp.int32) * ROW_SUB
    w_ref[...] = w_acc
    off_ref[...] = off.astype(jnp.int32)
    cnt_ref[...] = jnp.broadcast_to(cnt, (N_EXPERTS, LANES)).astype(jnp.int32)


def _moe_plan(comb_t):
    T = comb_t.shape[1]
    nch = T // MOE_CHUNK
    per_pair = pl.BlockSpec((None, TOP_K, MOE_CHUNK), lambda c: (c, 0, 0))
    per_expert = pl.BlockSpec((None, N_EXPERTS, LANES), lambda c: (c, 0, 0))
    return pl.pallas_call(
        _plan_kernel,
        grid=(nch,),
        in_specs=[pl.BlockSpec((N_EXPERTS, MOE_CHUNK), lambda c: (0, c))],
        out_specs=[per_pair, per_pair, per_expert, per_expert],
        out_shape=[jax.ShapeDtypeStruct((nch, TOP_K, MOE_CHUNK), jnp.int32),
                   jax.ShapeDtypeStruct((nch, TOP_K, MOE_CHUNK), F32),
                   jax.ShapeDtypeStruct((nch, N_EXPERTS, LANES), jnp.int32),
                   jax.ShapeDtypeStruct((nch, N_EXPERTS, LANES), jnp.int32)],
        compiler_params=pltpu.CompilerParams(dimension_semantics=("arbitrary",),
                                             vmem_limit_bytes=VMEM_LIMIT),
        name="moe_plan",
    )(comb_t)


def _slab_at(ref, first):
    return ref.at[pl.ds(pl.multiple_of(first, ROW_SUB), ROW_SUB), :]


def _slab(ref, row):
    return _slab_at(ref, row * ROW_SUB)


def _moe_kernel(off_ref, cnt_ref, slot_hbm, w_hbm, hp_ref, wgu_ref, wd_ref, o_ref,
                buf, clo, chi, slot_s, w_s, sem):
    c = pl.program_id(0)
    s = pl.program_id(1)

    @pl.when(s == 0)
    def _dispatch():
        slot_cp = pltpu.make_async_copy(slot_hbm.at[c], slot_s, sem.at[0])
        w_cp = pltpu.make_async_copy(w_hbm.at[c], w_s, sem.at[1])
        slot_cp.start()
        w_cp.start()

        @pl.when(c == 0)
        def _():
            buf[...] = jnp.zeros_like(buf)

        slot_cp.wait()
        w_cp.wait()

        def scatter(tb, carry):
            for tt in range(8):
                t = tb * 8 + tt
                slab = _slab(hp_ref, t)[...]
                for k in range(TOP_K):
                    _slab_at(buf, slot_s[k * MOE_CHUNK + t])[...] = slab
            return carry

        lax.fori_loop(0, MOE_CHUNK // 8, scatter, 0)

    @pl.when(s < N_EXPERTS)
    def _expert():
        n = cnt_ref[c, s]
        off = off_ref[c, s]

        def ffn(start, rows):
            view = buf.at[pl.ds(pl.multiple_of((off + start) * ROW_SUB, SLOT_ALIGN * ROW_SUB),
                                rows * ROW_SUB), :]
            x_lo, x_hi = _unpack_pair(_load_rows_dense(view, rows))
            gu = (_dot(x_lo.astype(BF16), wgu_ref[:ROW_WORDS, :])
                  + _dot(x_hi.astype(BF16), wgu_ref[ROW_WORDS:, :]))
            hid = jax.nn.silu(gu[:, :EXPERT_FF]) * gu[:, EXPERT_FF:]
            y = _dot(hid.astype(BF16), wd_ref[...])
            mine = lax.broadcasted_iota(jnp.int32, (rows, 1), 0) < (n - start)
            _store_rows_dense(view, _pack_pair(jnp.where(mine, y[:, :ROW_WORDS], x_lo),
                                               jnp.where(mine, y[:, ROW_WORDS:], x_hi)))

        n_big = (n + MOE_TILE_BIG - MOE_TILE - 1) // MOE_TILE_BIG

        def big(r, carry):
            ffn(r * MOE_TILE_BIG, MOE_TILE_BIG)
            return carry

        lax.fori_loop(0, n_big, big, 0)

        @pl.when(n > n_big * MOE_TILE_BIG)
        def _():
            ffn(n_big * MOE_TILE_BIG, MOE_TILE)

    @pl.when(s >= N_EXPERTS)
    def _combine():
        t0 = (s - N_EXPERTS) * COMBINE_BLOCK

        def gather(i, carry):
            for tt in range(4):
                tl = i * 4 + tt
                t = t0 + tl
                acc_lo = jnp.zeros((ROW_SUB, LANES), F32)
                acc_hi = jnp.zeros((ROW_SUB, LANES), F32)
                for k in range(TOP_K):
                    lo, hi = _unpack_pair(_slab_at(buf, slot_s[k * MOE_CHUNK + t])[...])
                    wk = w_s[k * MOE_CHUNK + t]
                    acc_lo = acc_lo + wk * lo
                    acc_hi = acc_hi + wk * hi
                _slab(clo, tl)[...] = acc_lo
                _slab(chi, tl)[...] = acc_hi
            return carry

        lax.fori_loop(0, COMBINE_BLOCK // 4, gather, 0)
        o_ref[:, :ROW_WORDS] = _load_rows_dense(clo, COMBINE_BLOCK)
        o_ref[:, ROW_WORDS:] = _load_rows_dense(chi, COMBINE_BLOCK)


def _moe_sparse(offs, cnts, slots, wts, hp, w_gu, w_d):
    nch = offs.shape[0]
    T = nch * MOE_CHUNK
    blocks = MOE_CHUNK // COMBINE_BLOCK
    expert = lambda c, s, *_: (jnp.minimum(s, N_EXPERTS - 1), 0, 0)
    grid_spec = pltpu.PrefetchScalarGridSpec(
        num_scalar_prefetch=2,
        grid=(nch, N_EXPERTS + blocks),
        in_specs=[pl.BlockSpec(memory_space=pl.ANY),
                  pl.BlockSpec(memory_space=pl.ANY),
                  pl.BlockSpec((MOE_CHUNK * ROW_SUB, LANES), lambda c, s, *_: (c, 0)),
                  pl.BlockSpec((None, D_MODEL, 2 * EXPERT_FF), expert),
                  pl.BlockSpec((None, EXPERT_FF, D_MODEL), expert)],
        out_specs=pl.BlockSpec(
            (COMBINE_BLOCK, D_MODEL),
            lambda c, s, *_: (c * blocks + jnp.maximum(s - N_EXPERTS, 0), 0)),
        scratch_shapes=[pltpu.VMEM((SLOT_ROWS * ROW_SUB, LANES), jnp.uint32),
                        pltpu.VMEM((COMBINE_BLOCK * ROW_SUB, LANES), F32),
                        pltpu.VMEM((COMBINE_BLOCK * ROW_SUB, LANES), F32),
                        pltpu.SMEM((TOP_K * MOE_CHUNK,), jnp.int32),
                        pltpu.SMEM((TOP_K * MOE_CHUNK,), F32),
                        pltpu.SemaphoreType.DMA((2,))])
    return pl.pallas_call(
        _moe_kernel,
        grid_spec=grid_spec,
        out_shape=jax.ShapeDtypeStruct((T, D_MODEL), F32),
        compiler_params=pltpu.CompilerParams(dimension_semantics=("arbitrary", "arbitrary"),
                                             vmem_limit_bytes=VMEM_LIMIT),
        name="moe_experts",
    )(offs, cnts, slots, wts, hp, w_gu, w_d)


def _ple_kernel(x1_ref, r_ref, p_ref, g_ref, wg_ref, b_ref, wp_ref, o_ref):
    x2 = x1_ref[...] + r_ref[...]
    hn = (_rms(x2, D_MODEL) * g_ref[...]).astype(BF16)
    gate = jax.nn.sigmoid(_dot(hn, wg_ref[...]) + b_ref[...])
    o_ref[...] = x2 + gate * _dot(p_ref[...].astype(BF16), wp_ref[...])


def _ple(x1s, routed, p2d, g_ple, wg, b_ple, wp, tm):
    T = x1s.shape[0]
    row = lambda n: pl.BlockSpec((tm, n), lambda i: (i, 0))
    consts = [g_ple, wg, b_ple, wp]
    return pl.pallas_call(
        _ple_kernel,
        grid=(T // tm,),
        in_specs=[row(D_MODEL), row(D_MODEL), row(PLE_DIM)] + [_full(c.shape) for c in consts],
        out_specs=row(D_MODEL),
        out_shape=jax.ShapeDtypeStruct((T, D_MODEL), F32),
        compiler_params=pltpu.CompilerParams(dimension_semantics=("arbitrary",),
                                             vmem_limit_bytes=VMEM_LIMIT),
        name="ple",
    )(x1s, routed, p2d, *consts)


def _pad_heads(w, heads, dim):
    k = w.shape[0]
    w = w.reshape(k, heads, dim)
    return jnp.pad(w, ((0, 0), (0, 0), (0, LANES - dim))).reshape(k, heads * LANES)


def _pad_lanes(g, offset=0):
    out = jnp.zeros((1, LANES), F32)
    return out.at[0, offset:offset + g.shape[0]].set(g)


def _rope_table():
    def block(dim, offset):
        inv = 1.0 / (ROPE_THETA ** (jnp.arange(0, dim, 2, dtype=F32) / dim))
        half = dim // 2
        z = jnp.zeros((LANES,), F32)
        invl = z.at[offset:offset + half].set(inv).at[offset + half:offset + dim].set(inv)
        sa = z.at[offset:offset + half].set(-1.0)
        sb = z.at[offset + half:offset + dim].set(1.0)
        return [invl, sa, sb]
    rows = block(MLA_ROPE, MLA_NOPE) + block(SWA_HD, 0)
    rows += [jnp.zeros((LANES,), F32)] * 2
    return jnp.stack(rows)


def _layer(x2d, p2d, pos2d, B, S, g_mix, w_in, b_gate, g_cq, w_uq, g_ckv, w_ukv, g_qn_mla, g_kn_mla,
           g_qn_swa, g_kn_swa, sink, w_br_mla, w_br_swa, w_out, g_moe, w_router, router_bias,
           w_exp_gu, w_exp_down, w_sh_gu, w_sh_down, g_ple, w_ple_gate, b_ple, w_ple_proj):
    w_kr = jnp.zeros((D_MODEL, LANES), F32).at[:, MLA_NOPE:MLA_QK].set(w_in[:, OFF_CKV:OFF_KR])
    w_vs = w_in[:, OFF_KS:OFF_VS].reshape(D_MODEL, SWA_KV_HEADS, 1, SWA_HD)
    w_vs = jnp.broadcast_to(w_vs, (D_MODEL, SWA_KV_HEADS, 2, SWA_HD)).reshape(D_MODEL, -1)
    w_all = jnp.concatenate([
        w_in[:, :OFF_CKV], w_kr,
        _pad_heads(w_in[:, OFF_KR:OFF_QS], SWA_HEADS, SWA_HD),
        _pad_heads(w_in[:, OFF_QS:OFF_KS], SWA_KV_HEADS, SWA_HD),
        w_vs, w_in[:, OFF_VS:]], axis=1).astype(BF16)
    assert w_all.shape[1] == C_END
    w_uq_p = _pad_heads(w_uq, MLA_HEADS, MLA_QK).astype(BF16)
    w_ukv3 = w_ukv.reshape(MLA_KV_RANK, MLA_HEADS, MLA_NOPE + MLA_V)
    w_k = _pad_heads(w_ukv3[:, :, :MLA_NOPE].reshape(MLA_KV_RANK, -1), MLA_HEADS, MLA_NOPE).astype(BF16)
    w_v = w_ukv3[:, :, MLA_NOPE:].reshape(MLA_KV_RANK, -1).astype(BF16)

    qm, km, vm, qw, kw, vw, gates = _pre_attention(
        x2d, pos2d, w_all, w_uq_p, w_k, w_v, g_mix[None], g_cq[None], g_ckv[None],
        _pad_lanes(g_qn_mla), _pad_lanes(g_kn_mla), _pad_lanes(g_qn_swa), _pad_lanes(g_kn_swa),
        b_gate[None], _rope_table(), tm=PRE_TILE)

    om = _mla_attention(qm, km, vm, B, S, tq=MLA_Q_TILE).reshape(B * S, -1)
    ow = _swa_attention(qw, kw, vw, sink, B, S, tq=SWA_Q_TILE).reshape(B * S, -1)

    x1s, hp, comb_t = _post_attention(
        om, ow, gates, x2d, w_br_mla.astype(BF16), w_br_swa.astype(BF16), w_out.astype(BF16),
        g_moe[None], w_router.T, router_bias[:, None], w_sh_gu.astype(BF16),
        w_sh_down.astype(BF16), tm=ROW_TILE)

    slots, wts, offs, cnts = _moe_plan(comb_t)
    routed = _moe_sparse(offs[:, :, 0], cnts[:, :, 0], slots.reshape(slots.shape[0], -1),
                         wts.reshape(wts.shape[0], -1), hp, w_exp_gu.astype(BF16),
                         w_exp_down.astype(BF16))

    return _ple(x1s, routed, p2d, g_ple[None], w_ple_gate.astype(BF16), b_ple[None],
                w_ple_proj.astype(BF16), tm=ROW_TILE)


def kernel(x, p, positions, g_mix, w_in, b_gate, g_cq, w_uq, g_ckv, w_ukv, g_qn_mla, g_kn_mla, g_qn_swa, g_kn_swa, sink, w_br_mla, w_br_swa, w_out, g_moe, w_router, router_bias, w_exp_gu, w_exp_down, w_sh_gu, w_sh_down, g_ple, w_ple_gate, b_ple, w_ple_proj):
    B, S, D = x.shape
    x2d = x.reshape(B * S, D)
    pos2d = positions.reshape(B * S, 1)
    for i in range(p.shape[0]):
        x2d = _layer(x2d, p[i].reshape(B * S, -1), pos2d, B, S, g_mix[i], w_in[i], b_gate[i],
                     g_cq[i], w_uq[i], g_ckv[i], w_ukv[i], g_qn_mla[i], g_kn_mla[i], g_qn_swa[i],
                     g_kn_swa[i], sink[i], w_br_mla[i], w_br_swa[i], w_out[i], g_moe[i],
                     w_router[i], router_bias[i], w_exp_gu[i], w_exp_down[i], w_sh_gu[i],
                     w_sh_down[i], g_ple[i], w_ple_gate[i], b_ple[i], w_ple_proj[i])
    return x2d.reshape(B, S, D)
```

```python
import functools

import jax
import jax.numpy as jnp
from jax import lax
from jax.experimental import pallas as pl
from jax.experimental.pallas import tpu as pltpu

D_MODEL = 1024
PLE_DIM = 256
ROPE_THETA = 10000.0
EPS = 1e-6
NEG_INF = -1e30

MLA_HEADS = 8
MLA_Q_RANK = 384
MLA_KV_RANK = 256
MLA_NOPE = 64
MLA_ROPE = 32
MLA_QK = MLA_NOPE + MLA_ROPE
MLA_V = 64

SWA_HEADS = 8
SWA_KV_HEADS = 2
SWA_GROUP = SWA_HEADS // SWA_KV_HEADS
SWA_HD = 64
WINDOW = 128

OFF_CQ = MLA_Q_RANK
OFF_CKV = OFF_CQ + MLA_KV_RANK
OFF_KR = OFF_CKV + MLA_ROPE
OFF_QS = OFF_KR + SWA_HEADS * SWA_HD
OFF_KS = OFF_QS + SWA_KV_HEADS * SWA_HD
OFF_VS = OFF_KS + SWA_KV_HEADS * SWA_HD
OFF_GA = OFF_VS + D_MODEL

N_EXPERTS = 64
TOP_K = 8
N_GROUPS = 8
TOPK_GROUPS = 4
EXPERTS_PER_GROUP = N_EXPERTS // N_GROUPS
EXPERT_FF = 256
SHARED_FF = 256
ROUTED_SCALE = 2.5

LANES = 128
ROW_WORDS = D_MODEL // 2
ROW_SUB = ROW_WORDS // LANES
MOE_CHUNK = 2048
MOE_TILE = 128
MOE_TILE_BIG = 256
SLOT_ALIGN = 16
COMBINE_BLOCK = 256
PRE_TILE = 256
ROW_TILE = 512
MLA_Q_TILE = 512
MLA_KV_TILE = 256
LOG2E = 1.4426950408889634
SOFTMAX_UNDERFLOW_GUARD = 1e-30
SWA_Q_TILE = 512
SWA_BLOCK = 128
VMEM_LIMIT = 56 * 1024 * 1024

BF16 = jnp.bfloat16
F32 = jnp.float32

C_CQ = 0
C_CKV = C_CQ + MLA_Q_RANK
C_KR = C_CKV + MLA_KV_RANK
C_QS = C_KR + LANES
C_KS = C_QS + SWA_HEADS * LANES
C_VS = C_KS + SWA_KV_HEADS * LANES
C_GA = C_VS + SWA_KV_HEADS * LANES
C_END = C_GA + 2 * D_MODEL


def _full(shape):
    nd = len(shape)
    return pl.BlockSpec(shape, lambda *_: (0,) * nd)


def _dot(a, b):
    return jnp.dot(a, b, preferred_element_type=F32)


def _dot_nt(a, b, precision=None):
    return lax.dot_general(a, b, (((1,), (1,)), ((), ())), precision=precision,
                           preferred_element_type=F32)


def _rms(v, n):
    return v * lax.rsqrt(jnp.sum(v * v, axis=-1, keepdims=True) * (1.0 / n) + EPS)


def _rope(v, cos, sin_a, sin_b, half):
    return v * cos + pltpu.roll(v, LANES - half, 1) * sin_a + pltpu.roll(v, half, 1) * sin_b


def _pre_kernel(x_ref, pos_ref, w_all_ref, w_uq_ref, w_k_ref, w_v_ref, g_mix_ref, g_cq_ref,
                g_ckv_ref, gq_m_ref, gk_m_ref, gq_s_ref, gk_s_ref, b_gate_ref, rope_ref,
                qm_ref, km_ref, vm_ref, qw_ref, kw_ref, vw_ref, gate_ref):
    x = x_ref[...]
    h = (_rms(x, D_MODEL) * g_mix_ref[...]).astype(BF16)
    z = _dot(h, w_all_ref[...])

    pos = pos_ref[...].astype(F32)
    rope = rope_ref[...]
    ang_m = pos * rope[0:1, :]
    cos_m = jnp.cos(ang_m)
    sin_m = jnp.sin(ang_m)
    sin_ma = sin_m * rope[1:2, :]
    sin_mb = sin_m * rope[2:3, :]
    ang_s = pos * rope[3:4, :]
    cos_s = jnp.cos(ang_s)
    sin_s = jnp.sin(ang_s)
    sin_sa = sin_s * rope[4:5, :]
    sin_sb = sin_s * rope[5:6, :]

    cqn = (_rms(z[:, C_CQ:C_CKV], MLA_Q_RANK) * g_cq_ref[...]).astype(BF16)
    q = _dot(cqn, w_uq_ref[...])
    gq_m = gq_m_ref[...]
    for hd in range(MLA_HEADS):
        qh = _rms(q[:, hd * LANES:(hd + 1) * LANES], MLA_QK) * gq_m
        qh = _rope(qh, cos_m, sin_ma, sin_mb, MLA_ROPE // 2) * (MLA_QK ** -0.5 * LOG2E)
        qm_ref[:, hd * LANES:(hd + 1) * LANES] = qh.astype(BF16)

    ckvn = (_rms(z[:, C_CKV:C_KR], MLA_KV_RANK) * g_ckv_ref[...]).astype(BF16)
    kn = _dot(ckvn, w_k_ref[...])
    vm_ref[...] = _dot(ckvn, w_v_ref[...]).astype(BF16)
    gk_m = gk_m_ref[...]
    kr = z[:, C_KR:C_QS]
    ss_kr = jnp.sum(kr * kr, axis=-1, keepdims=True)
    kr_rot = _rope(kr * gk_m, cos_m, sin_ma, sin_mb, MLA_ROPE // 2)
    for hd in range(MLA_HEADS):
        kh = kn[:, hd * LANES:(hd + 1) * LANES]
        ss = jnp.sum(kh * kh, axis=-1, keepdims=True) + ss_kr
        sc = lax.rsqrt(ss * (1.0 / MLA_QK) + EPS)
        km_ref[:, hd * LANES:(hd + 1) * LANES] = ((kh * gk_m + kr_rot) * sc).astype(BF16)

    gq_s = gq_s_ref[...]
    for hd in range(SWA_HEADS):
        qh = _rms(z[:, C_QS + hd * LANES:C_QS + (hd + 1) * LANES], SWA_HD) * gq_s
        qh = _rope(qh, cos_s, sin_sa, sin_sb, SWA_HD // 2) * (SWA_HD ** -0.5)
        qw_ref[:, hd * LANES:(hd + 1) * LANES] = qh.astype(BF16)
    gk_s = gk_s_ref[...]
    for hd in range(SWA_KV_HEADS):
        kh = _rms(z[:, C_KS + hd * LANES:C_KS + (hd + 1) * LANES], SWA_HD) * gk_s
        kh = _rope(kh, cos_s, sin_sa, sin_sb, SWA_HD // 2)
        kw_ref[:, hd * LANES:(hd + 1) * LANES] = kh.astype(BF16)
    vw_ref[...] = z[:, C_VS:C_GA].astype(BF16)

    gate_ref[...] = jax.nn.sigmoid(z[:, C_GA:C_END] + b_gate_ref[...]).astype(BF16)


def _pre_attention(x2d, pos2d, w_all, w_uq, w_k, w_v, g_mix, g_cq, g_ckv, gq_m, gk_m, gq_s,
                   gk_s, b_gate, rope_tab, tm):
    T = x2d.shape[0]
    row = lambda n: pl.BlockSpec((tm, n), lambda i: (i, 0))
    outs = [(MLA_HEADS * LANES, BF16), (MLA_HEADS * LANES, BF16), (MLA_HEADS * MLA_V, BF16),
            (SWA_HEADS * LANES, BF16), (SWA_KV_HEADS * LANES, BF16),
            (SWA_KV_HEADS * LANES, BF16), (2 * D_MODEL, BF16)]
    consts = [w_all, w_uq, w_k, w_v, g_mix, g_cq, g_ckv, gq_m, gk_m, gq_s, gk_s, b_gate, rope_tab]
    return pl.pallas_call(
        _pre_kernel,
        grid=(T // tm,),
        in_specs=[row(D_MODEL), row(1)] + [_full(c.shape) for c in consts],
        out_specs=[row(n) for n, _ in outs],
        out_shape=[jax.ShapeDtypeStruct((T, n), dt) for n, dt in outs],
        compiler_params=pltpu.CompilerParams(dimension_semantics=("arbitrary",),
                                             vmem_limit_bytes=VMEM_LIMIT),
        name="pre_attention",
    )(x2d, pos2d, *consts)


def _half_masks(dtype):
    lane = lax.broadcasted_iota(jnp.int32, (1, LANES), 1)
    lo = (lane < LANES // 2).astype(dtype)
    return lo, 1 - lo


def _mla_exact(q_ref, k_ref, v_ref, o_ref):
    v = v_ref[...]
    masks = _half_masks(v.dtype)
    acc = None
    for hh in range(2):
        q = q_ref[:, hh * LANES:(hh + 1) * LANES]
        k = k_ref[:, hh * LANES:(hh + 1) * LANES]
        s = _dot_nt(q, k)
        m = jnp.max(s, axis=-1, keepdims=True)
        p = jnp.exp2(s - m)
        l = jnp.sum(p, axis=-1, keepdims=True)
        o = _dot(p.astype(BF16), v * masks[hh]) / l
        acc = o if acc is None else acc + o
    o_ref[...] = acc.astype(o_ref.dtype)


def _mla_kernel(q_ref, k_ref, v_ref, o_ref, kmax_ref):
    S = k_ref.shape[0]
    tq = q_ref.shape[0]

    @pl.when(pl.program_id(2) == 0)
    def _():
        for hh in range(2):
            k = k_ref[:, hh * LANES:(hh + 1) * LANES].astype(F32)
            kn2 = jnp.max(jnp.sum(k * k, axis=-1, keepdims=True), axis=0, keepdims=True)
            kmax_ref[hh] = jnp.broadcast_to(jnp.sqrt(kn2), kmax_ref.shape[1:])

    masks = _half_masks(BF16)
    acc = None
    lmin = None
    for hh in range(2):
        q = q_ref[:, hh * LANES:(hh + 1) * LANES]
        qf = q.astype(F32)
        bound = jnp.sqrt(jnp.sum(qf * qf, axis=-1, keepdims=True)) * kmax_ref[hh][0:1, 0:1]
        v = v_ref[...] * masks[hh]
        o = jnp.zeros((tq, LANES), F32)
        lsum = jnp.zeros((tq, LANES), F32)
        for j in range(S // MLA_KV_TILE):
            rows = slice(j * MLA_KV_TILE, (j + 1) * MLA_KV_TILE)
            p = jnp.exp2(_dot_nt(q, k_ref[rows, hh * LANES:(hh + 1) * LANES]) - bound)
            for t in range(MLA_KV_TILE // LANES):
                lsum = lsum + p[:, t * LANES:(t + 1) * LANES]
            o = o + _dot(p.astype(BF16), v[rows, :])
        l = jnp.sum(lsum, axis=-1, keepdims=True)
        o = o / l
        acc = o if acc is None else acc + o
        lm = jnp.min(l)
        lmin = lm if lmin is None else jnp.minimum(lmin, lm)
    o_ref[...] = acc.astype(o_ref.dtype)

    @pl.when(jnp.logical_not(lmin > SOFTMAX_UNDERFLOW_GUARD))
    def _():
        _mla_exact(q_ref, k_ref, v_ref, o_ref)


def _mla_attention(qm, km, vm, B, S, tq):
    pairs = MLA_HEADS // 2
    q3 = qm.reshape(B, S, MLA_HEADS * LANES)
    k3 = km.reshape(B, S, MLA_HEADS * LANES)
    v3 = vm.reshape(B, S, MLA_HEADS * MLA_V)
    return pl.pallas_call(
        _mla_kernel,
        grid=(B, pairs, S // tq),
        in_specs=[pl.BlockSpec((None, tq, 2 * LANES), lambda b, p, i: (b, i, p)),
                  pl.BlockSpec((None, S, 2 * LANES), lambda b, p, i: (b, 0, p)),
                  pl.BlockSpec((None, S, LANES), lambda b, p, i: (b, 0, p))],
        out_specs=pl.BlockSpec((None, tq, LANES), lambda b, p, i: (b, i, p)),
        out_shape=jax.ShapeDtypeStruct((B, S, MLA_HEADS * MLA_V), BF16),
        scratch_shapes=[pltpu.VMEM((2, 8, LANES), F32)],
        compiler_params=pltpu.CompilerParams(
            dimension_semantics=("arbitrary", "arbitrary", "arbitrary"),
            vmem_limit_bytes=VMEM_LIMIT),
        name="mla_attention",
    )(q3, k3, v3)


def _swa_kernel(sink_ref, q_ref, k_ref, v_ref, o_ref, *, tq, S):
    hk = pl.program_id(1)
    i = pl.program_id(2)
    tk = SWA_BLOCK + 2 * WINDOW
    row = lax.broadcasted_iota(jnp.int32, (SWA_GROUP * SWA_BLOCK, 1), 0)
    qoff = row & (SWA_BLOCK - 1)
    head = row // SWA_BLOCK
    sk = jnp.zeros((SWA_GROUP * SWA_BLOCK, 1), F32)
    for g in range(SWA_GROUP):
        sk = jnp.where(head == g, sink_ref[SWA_GROUP * hk + g], sk)
    low_half = lax.broadcasted_iota(jnp.int32, (1, LANES), 1) < LANES // 2
    for sub in range(tq // SWA_BLOCK):
        rows = slice(sub * SWA_BLOCK, (sub + 1) * SWA_BLOCK)
        q0 = i * tq + sub * SWA_BLOCK
        kstart = pl.multiple_of(jnp.clip(q0 - WINDOW, 0, S - tk), WINDOW)
        k = k_ref[pl.ds(kstart, tk), :]
        v = v_ref[pl.ds(kstart, tk), :]
        q = jnp.concatenate([q_ref[rows, g * LANES:(g + 1) * LANES] for g in range(SWA_GROUP)], axis=0)
        kpos = kstart + lax.broadcasted_iota(jnp.int32, (1, tk), 1)
        valid = jnp.abs(kpos - (q0 + qoff)) <= WINDOW
        s = jnp.where(valid, _dot_nt(q, k), NEG_INF)
        m = jnp.maximum(jnp.max(s, axis=-1, keepdims=True), sk)
        e = jnp.exp(s - m)
        denom = jnp.sum(e, axis=-1, keepdims=True) + jnp.exp(sk - m)
        o = _dot(e.astype(BF16), v) / denom
        for j in range(SWA_GROUP // 2):
            even = o[(2 * j) * SWA_BLOCK:(2 * j + 1) * SWA_BLOCK, :]
            odd = o[(2 * j + 1) * SWA_BLOCK:(2 * j + 2) * SWA_BLOCK, :]
            o_ref[rows, j * LANES:(j + 1) * LANES] = jnp.where(low_half, even, odd).astype(o_ref.dtype)


def _swa_attention(qw, kw, vw, sink, B, S, tq):
    q3 = qw.reshape(B, S, SWA_HEADS * LANES)
    k3 = kw.reshape(B, S, SWA_KV_HEADS * LANES)
    v3 = vw.reshape(B, S, SWA_KV_HEADS * LANES)
    return pl.pallas_call(
        functools.partial(_swa_kernel, tq=tq, S=S),
        grid=(B, SWA_KV_HEADS, S // tq),
        in_specs=[pl.BlockSpec(memory_space=pltpu.SMEM),
                  pl.BlockSpec((None, tq, SWA_GROUP * LANES), lambda b, h, i: (b, i, h)),
                  pl.BlockSpec((None, S, LANES), lambda b, h, i: (b, 0, h)),
                  pl.BlockSpec((None, S, LANES), lambda b, h, i: (b, 0, h))],
        out_specs=pl.BlockSpec((None, tq, SWA_GROUP * SWA_HD), lambda b, h, i: (b, i, h)),
        out_shape=jax.ShapeDtypeStruct((B, S, SWA_HEADS * SWA_HD), BF16),
        compiler_params=pltpu.CompilerParams(
            dimension_semantics=("arbitrary", "arbitrary", "arbitrary"),
            vmem_limit_bytes=VMEM_LIMIT),
        name="swa_attention",
    )(sink, q3, k3, v3)


def _beats(vj, vi, j_first):
    return (vj >= vi) if j_first else (vj > vi)


def _route(scores, sel):
    G, P = N_GROUPS, EXPERTS_PER_GROUP
    groups = [sel[g * P:(g + 1) * P, :] for g in range(G)]
    row = lax.broadcasted_iota(jnp.int32, (P, 1), 0)
    gscore = []
    for vg in groups:
        m1 = jnp.max(vg, axis=0, keepdims=True)
        first = jnp.min(jnp.where(vg == m1, row, P), axis=0, keepdims=True)
        m2 = jnp.max(jnp.where(row == first, -jnp.inf, vg), axis=0, keepdims=True)
        gscore.append(m1 + m2)
    masked = []
    for g in range(G):
        rank = jnp.zeros_like(gscore[g], dtype=jnp.int32)
        for g2 in range(G):
            if g2 != g:
                rank = rank + _beats(gscore[g2], gscore[g], g2 < g).astype(jnp.int32)
        masked.append(jnp.where(rank < TOPK_GROUPS, groups[g], NEG_INF))
    ranks = [jnp.zeros((P, masked[0].shape[1]), jnp.int32) for _ in range(G)]
    for gj in range(G):
        for r in range(P):
            vj = masked[gj][r:r + 1, :]
            for gi in range(G):
                if gi < gj:
                    b = vj > masked[gi]
                elif gi > gj:
                    b = vj >= masked[gi]
                else:
                    b = (vj > masked[gi]) | ((vj == masked[gi]) & (row > r))
                ranks[gi] = ranks[gi] + b.astype(jnp.int32)
    picked = [jnp.where(ranks[g] < TOP_K, scores[g * P:(g + 1) * P, :], 0.0) for g in range(G)]
    total = picked[0]
    for g in range(1, G):
        total = total + picked[g]
    denom = jnp.sum(total, axis=0, keepdims=True)
    return [pk / denom * ROUTED_SCALE for pk in picked]


def _pack_pair(lo, hi):
    return pltpu.pack_elementwise([lo, hi], packed_dtype=BF16)


def _unpack_pair(word):
    lo = pltpu.unpack_elementwise(word, index=0, packed_dtype=BF16, unpacked_dtype=F32)
    hi = pltpu.unpack_elementwise(word, index=1, packed_dtype=BF16, unpacked_dtype=F32)
    return lo, hi


def _store_rows_dense(ref, words):
    for j in range(ROW_SUB):
        ref[pl.ds(j, words.shape[0], stride=ROW_SUB), :] = words[:, j * LANES:(j + 1) * LANES]


def _load_rows_dense(ref, rows):
    sub = ROW_SUB
    return jnp.concatenate([ref[pl.ds(j, rows, stride=sub), :] for j in range(sub)], axis=1)


def _post_kernel(om_ref, ow_ref, gate_ref, x_ref, wbm_ref, wbw_ref, wout_ref, g_moe_ref, wr_ref,
                 rb_ref, wsgu_ref, wsd_ref, x1_ref, hp_ref, comb_ref):
    am = _dot(om_ref[...], wbm_ref[...])
    aw = _dot(ow_ref[...], wbw_ref[...])
    gates = gate_ref[...].astype(F32)
    merged = gates[:, :D_MODEL] * am + gates[:, D_MODEL:] * aw
    x1 = x_ref[...] + _dot(merged.astype(BF16), wout_ref[...])

    h2 = _rms(x1, D_MODEL) * g_moe_ref[...]
    h2b = h2.astype(BF16)
    _store_rows_dense(hp_ref, _pack_pair(h2[:, :ROW_WORDS], h2[:, ROW_WORDS:]))

    logits = _dot_nt(wr_ref[...], h2, precision=lax.Precision.HIGHEST)
    scores = jax.nn.sigmoid(logits)
    comb = _route(scores, scores + rb_ref[...])
    for g in range(N_GROUPS):
        comb_ref[g * EXPERTS_PER_GROUP:(g + 1) * EXPERTS_PER_GROUP, :] = comb[g]

    sgu = _dot(h2b, wsgu_ref[...])
    sh = jax.nn.silu(sgu[:, :SHARED_FF]) * sgu[:, SHARED_FF:]
    x1_ref[...] = x1 + _dot(sh.astype(BF16), wsd_ref[...])


def _post_attention(om, ow, gates, x2d, wbm, wbw, wout, g_moe, wr_t, rbias, wsgu, wsd, tm):
    T = x2d.shape[0]
    row = lambda n: pl.BlockSpec((tm, n), lambda i: (i, 0))
    consts = [wbm, wbw, wout, g_moe, wr_t, rbias, wsgu, wsd]
    return pl.pallas_call(
        _post_kernel,
        grid=(T // tm,),
        in_specs=[row(om.shape[1]), row(ow.shape[1]), row(2 * D_MODEL), row(D_MODEL)]
        + [_full(c.shape) for c in consts],
        out_specs=[row(D_MODEL), pl.BlockSpec((tm * ROW_SUB, LANES), lambda i: (i, 0)),
                   pl.BlockSpec((N_EXPERTS, tm), lambda i: (0, i))],
        out_shape=[jax.ShapeDtypeStruct((T, D_MODEL), F32),
                   jax.ShapeDtypeStruct((T * ROW_SUB, LANES), jnp.uint32),
                   jax.ShapeDtypeStruct((N_EXPERTS, T), F32)],
        compiler_params=pltpu.CompilerParams(dimension_semantics=("arbitrary",),
                                             vmem_limit_bytes=VMEM_LIMIT),
        name="post_attention",
    )(om, ow, gates, x2d, *consts)


SLOT_ROWS = -(-(MOE_CHUNK * TOP_K + N_EXPERTS * (SLOT_ALIGN - 1) + MOE_TILE_BIG) // MOE_TILE) * MOE_TILE
DUMMY_SLOT = SLOT_ROWS - 1
PLAN_BLOCK = 256


def _plan_kernel(comb_ref, slot_ref, w_ref, off_ref, cnt_ref):
    comb = comb_ref[...]
    sel = comb > 0.0
    m = sel.astype(F32)
    mb = m.astype(BF16)
    r_i = lax.broadcasted_iota(jnp.int32, (PLAN_BLOCK, PLAN_BLOCK), 0)
    c_i = lax.broadcasted_iota(jnp.int32, (PLAN_BLOCK, PLAN_BLOCK), 1)
    before = (r_i < c_i).astype(BF16)
    carry = jnp.zeros((N_EXPERTS, 1), F32)
    ranks = []
    for b in range(MOE_CHUNK // PLAN_BLOCK):
        blk = slice(b * PLAN_BLOCK, (b + 1) * PLAN_BLOCK)
        ranks.append(_dot(mb[:, blk], before) + carry)
        carry = carry + jnp.sum(m[:, blk], axis=1, keepdims=True)
    rank = jnp.concatenate(ranks, axis=1)
    cnt = carry
    cnt_pad = jnp.floor((cnt + (SLOT_ALIGN - 1)) * (1.0 / SLOT_ALIGN)) * SLOT_ALIGN
    e_r = lax.broadcasted_iota(jnp.int32, (N_EXPERTS, N_EXPERTS), 0)
    e_c = lax.broadcasted_iota(jnp.int32, (N_EXPERTS, N_EXPERTS), 1)
    below = (e_c < e_r).astype(F32)
    off = jnp.dot(below, jnp.broadcast_to(cnt_pad, (N_EXPERTS, LANES)),
                  precision=lax.Precision.HIGHEST, preferred_element_type=F32)
    slot = off[:, :1] + rank
    kidx = _dot(below.astype(BF16), mb)
    row = lax.broadcasted_iota(jnp.int32, (TOP_K, 1), 0)
    slot_acc = jnp.zeros((TOP_K, MOE_CHUNK), F32)
    w_acc = jnp.zeros((TOP_K, MOE_CHUNK), F32)
    for k in range(TOP_K):
        pick = jnp.where(sel & (kidx == k), 1.0, 0.0)
        found = jnp.sum(pick, axis=0, keepdims=True) > 0.0
        s_k = jnp.where(found, jnp.sum(pick * slot, axis=0, keepdims=True), float(DUMMY_SLOT))
        w_k = jnp.sum(pick * comb, axis=0, keepdims=True)
        slot_acc = jnp.where(row == k, s_k, slot_acc)
        w_acc = jnp.where(row == k, w_k, w_acc)
    slot_ref[...] = slot_acc.astype(jnp.int32) * ROW_SUB
    w_ref[...] = w_acc
    off_ref[...] = off.astype(jnp.int32)
    cnt_ref[...] = jnp.broadcast_to(cnt, (N_EXPERTS, LANES)).astype(jnp.int32)


def _moe_plan(comb_t):
    T = comb_t.shape[1]
    nch = T // MOE_CHUNK
    per_pair = pl.BlockSpec((None, TOP_K, MOE_CHUNK), lambda c: (c, 0, 0))
    per_expert = pl.BlockSpec((None, N_EXPERTS, LANES), lambda c: (c, 0, 0))
    return pl.pallas_call(
        _plan_kernel,
        grid=(nch,),
        in_specs=[pl.BlockSpec((N_EXPERTS, MOE_CHUNK), lambda c: (0, c))],
        out_specs=[per_pair, per_pair, per_expert, per_expert],
        out_shape=[jax.ShapeDtypeStruct((nch, TOP_K, MOE_CHUNK), jnp.int32),
                   jax.ShapeDtypeStruct((nch, TOP_K, MOE_CHUNK), F32),
                   jax.ShapeDtypeStruct((nch, N_EXPERTS, LANES), jnp.int32),
                   jax.ShapeDtypeStruct((nch, N_EXPERTS, LANES), jnp.int32)],
        compiler_params=pltpu.CompilerParams(dimension_semantics=("arbitrary",),
                                             vmem_limit_bytes=VMEM_LIMIT),
        name="moe_plan",
    )(comb_t)


def _slab_at(ref, first):
    return ref.at[pl.ds(pl.multiple_of(first, ROW_SUB), ROW_SUB), :]


def _slab(ref, row):
    return _slab_at(ref, row * ROW_SUB)


def _moe_kernel(off_ref, cnt_ref, slot_hbm, w_hbm, hp_ref, wgu_ref, wd_ref, o_ref,
                buf, clo, chi, slot_s, w_s, sem):
    c = pl.program_id(0)
    s = pl.program_id(1)

    @pl.when(s == 0)
    def _dispatch():
        slot_cp = pltpu.make_async_copy(slot_hbm.at[c], slot_s, sem.at[0])
        w_cp = pltpu.make_async_copy(w_hbm.at[c], w_s, sem.at[1])
        slot_cp.start()
        w_cp.start()

        @pl.when(c == 0)
        def _():
            buf[...] = jnp.zeros_like(buf)

        slot_cp.wait()
        w_cp.wait()

        def scatter(tb, carry):
            for tt in range(8):
                t = tb * 8 + tt
                slab = _slab(hp_ref, t)[...]
                for k in range(TOP_K):
                    _slab_at(buf, slot_s[k * MOE_CHUNK + t])[...] = slab
            return carry

        lax.fori_loop(0, MOE_CHUNK // 8, scatter, 0)

    @pl.when(s < N_EXPERTS)
    def _expert():
        n = cnt_ref[c, s]
        off = off_ref[c, s]

        def ffn(start, rows):
            view = buf.at[pl.ds(pl.multiple_of((off + start) * ROW_SUB, SLOT_ALIGN * ROW_SUB),
                                rows * ROW_SUB), :]
            x_lo, x_hi = _unpack_pair(_load_rows_dense(view, rows))
            gu = (_dot(x_lo.astype(BF16), wgu_ref[:ROW_WORDS, :])
                  + _dot(x_hi.astype(BF16), wgu_ref[ROW_WORDS:, :]))
            hid = jax.nn.silu(gu[:, :EXPERT_FF]) * gu[:, EXPERT_FF:]
            y = _dot(hid.astype(BF16), wd_ref[...])
            mine = lax.broadcasted_iota(jnp.int32, (rows, 1), 0) < (n - start)
            _store_rows_dense(view, _pack_pair(jnp.where(mine, y[:, :ROW_WORDS], x_lo),
                                               jnp.where(mine, y[:, ROW_WORDS:], x_hi)))

        n_big = (n + MOE_TILE_BIG - MOE_TILE - 1) // MOE_TILE_BIG

        def big(r, carry):
            ffn(r * MOE_TILE_BIG, MOE_TILE_BIG)
            return carry

        lax.fori_loop(0, n_big, big, 0)

        @pl.when(n > n_big * MOE_TILE_BIG)
        def _():
            ffn(n_big * MOE_TILE_BIG, MOE_TILE)

    @pl.when(s >= N_EXPERTS)
    def _combine():
        t0 = (s - N_EXPERTS) * COMBINE_BLOCK

        def gather(i, carry):
            for tt in range(4):
                tl = i * 4 + tt
                t = t0 + tl
                acc_lo = jnp.zeros((ROW_SUB, LANES), F32)
                acc_hi = jnp.zeros((ROW_SUB, LANES), F32)
                for k in range(TOP_K):
                    lo, hi = _unpack_pair(_slab_at(buf, slot_s[k * MOE_CHUNK + t])[...])
                    wk = w_s[k * MOE_CHUNK + t]
                    acc_lo = acc_lo + wk * lo
                    acc_hi = acc_hi + wk * hi
                _slab(clo, tl)[...] = acc_lo
                _slab(chi, tl)[...] = acc_hi
            return carry

        lax.fori_loop(0, COMBINE_BLOCK // 4, gather, 0)
        o_ref[:, :ROW_WORDS] = _load_rows_dense(clo, COMBINE_BLOCK)
        o_ref[:, ROW_WORDS:] = _load_rows_dense(chi, COMBINE_BLOCK)


def _moe_sparse(offs, cnts, slots, wts, hp, w_gu, w_d):
    nch = offs.shape[0]
    T = nch * MOE_CHUNK
    blocks = MOE_CHUNK // COMBINE_BLOCK
    expert = lambda c, s, *_: (jnp.minimum(s, N_EXPERTS - 1), 0, 0)
    grid_spec = pltpu.PrefetchScalarGridSpec(
        num_scalar_prefetch=2,
        grid=(nch, N_EXPERTS + blocks),
        in_specs=[pl.BlockSpec(memory_space=pl.ANY),
                  pl.BlockSpec(memory_space=pl.ANY),
                  pl.BlockSpec((MOE_CHUNK * ROW_SUB, LANES), lambda c, s, *_: (c, 0)),
                  pl.BlockSpec((None, D_MODEL, 2 * EXPERT_FF), expert),
                  pl.BlockSpec((None, EXPERT_FF, D_MODEL), expert)],
        out_specs=pl.BlockSpec(
            (COMBINE_BLOCK, D_MODEL),
            lambda c, s, *_: (c * blocks + jnp.maximum(s - N_EXPERTS, 0), 0)),
        scratch_shapes=[pltpu.VMEM((SLOT_ROWS * ROW_SUB, LANES), jnp.uint32),
                        pltpu.VMEM((COMBINE_BLOCK * ROW_SUB, LANES), F32),
                        pltpu.VMEM((COMBINE_BLOCK * ROW_SUB, LANES), F32),
                        pltpu.SMEM((TOP_K * MOE_CHUNK,), jnp.int32),
                        pltpu.SMEM((TOP_K * MOE_CHUNK,), F32),
                        pltpu.SemaphoreType.DMA((2,))])
    return pl.pallas_call(
        _moe_kernel,
        grid_spec=grid_spec,
        out_shape=jax.ShapeDtypeStruct((T, D_MODEL), F32),
        compiler_params=pltpu.CompilerParams(dimension_semantics=("arbitrary", "arbitrary"),
                                             vmem_limit_bytes=VMEM_LIMIT),
        name="moe_experts",
    )(offs, cnts, slots, wts, hp, w_gu, w_d)


def _ple_kernel(x1_ref, r_ref, p_ref, g_ref, wg_ref, b_ref, wp_ref, o_ref):
    x2 = x1_ref[...] + r_ref[...]
    hn = (_rms(x2, D_MODEL) * g_ref[...]).astype(BF16)
    gate = jax.nn.sigmoid(_dot(hn, wg_ref[...]) + b_ref[...])
    o_ref[...] = x2 + gate * _dot(p_ref[...].astype(BF16), wp_ref[...])


def _ple(x1s, routed, p2d, g_ple, wg, b_ple, wp, tm):
    T = x1s.shape[0]
    row = lambda n: pl.BlockSpec((tm, n), lambda i: (i, 0))
    consts = [g_ple, wg, b_ple, wp]
    return pl.pallas_call(
        _ple_kernel,
        grid=(T // tm,),
        in_specs=[row(D_MODEL), row(D_MODEL), row(PLE_DIM)] + [_full(c.shape) for c in consts],
        out_specs=row(D_MODEL),
        out_shape=jax.ShapeDtypeStruct((T, D_MODEL), F32),
        compiler_params=pltpu.CompilerParams(dimension_semantics=("arbitrary",),
                                             vmem_limit_bytes=VMEM_LIMIT),
        name="ple",
    )(x1s, routed, p2d, *consts)


def _pad_heads(w, heads, dim):
    k = w.shape[0]
    w = w.reshape(k, heads, dim)
    return jnp.pad(w, ((0, 0), (0, 0), (0, LANES - dim))).reshape(k, heads * LANES)


def _pad_lanes(g, offset=0):
    out = jnp.zeros((1, LANES), F32)
    return out.at[0, offset:offset + g.shape[0]].set(g)


def _rope_table():
    def block(dim, offset):
        inv = 1.0 / (ROPE_THETA ** (jnp.arange(0, dim, 2, dtype=F32) / dim))
        half = dim // 2
        z = jnp.zeros((LANES,), F32)
        invl = z.at[offset:offset + half].set(inv).at[offset + half:offset + dim].set(inv)
        sa = z.at[offset:offset + half].set(-1.0)
        sb = z.at[offset + half:offset + dim].set(1.0)
        return [invl, sa, sb]
    rows = block(MLA_ROPE, MLA_NOPE) + block(SWA_HD, 0)
    rows += [jnp.zeros((LANES,), F32)] * 2
    return jnp.stack(rows)


def _layer(x2d, p2d, pos2d, B, S, g_mix, w_in, b_gate, g_cq, w_uq, g_ckv, w_ukv, g_qn_mla, g_kn_mla,
           g_qn_swa, g_kn_swa, sink, w_br_mla, w_br_swa, w_out, g_moe, w_router, router_bias,
           w_exp_gu, w_exp_down, w_sh_gu, w_sh_down, g_ple, w_ple_gate, b_ple, w_ple_proj):
    w_kr = jnp.zeros((D_MODEL, LANES), F32).at[:, MLA_NOPE:MLA_QK].set(w_in[:, OFF_CKV:OFF_KR])
    w_vs = w_in[:, OFF_KS:OFF_VS].reshape(D_MODEL, SWA_KV_HEADS, 1, SWA_HD)
    w_vs = jnp.broadcast_to(w_vs, (D_MODEL, SWA_KV_HEADS, 2, SWA_HD)).reshape(D_MODEL, -1)
    w_all = jnp.concatenate([
        w_in[:, :OFF_CKV], w_kr,
        _pad_heads(w_in[:, OFF_KR:OFF_QS], SWA_HEADS, SWA_HD),
        _pad_heads(w_in[:, OFF_QS:OFF_KS], SWA_KV_HEADS, SWA_HD),
        w_vs, w_in[:, OFF_VS:]], axis=1).astype(BF16)
    assert w_all.shape[1] == C_END
    w_uq_p = _pad_heads(w_uq, MLA_HEADS, MLA_QK).astype(BF16)
    w_ukv3 = w_ukv.reshape(MLA_KV_RANK, MLA_HEADS, MLA_NOPE + MLA_V)
    w_k = _pad_heads(w_ukv3[:, :, :MLA_NOPE].reshape(MLA_KV_RANK, -1), MLA_HEADS, MLA_NOPE).astype(BF16)
    w_v = w_ukv3[:, :, MLA_NOPE:].reshape(MLA_KV_RANK, -1).astype(BF16)

    qm, km, vm, qw, kw, vw, gates = _pre_attention(
        x2d, pos2d, w_all, w_uq_p, w_k, w_v, g_mix[None], g_cq[None], g_ckv[None],
        _pad_lanes(g_qn_mla), _pad_lanes(g_kn_mla), _pad_lanes(g_qn_swa), _pad_lanes(g_kn_swa),
        b_gate[None], _rope_table(), tm=PRE_TILE)

    om = _mla_attention(qm, km, vm, B, S, tq=MLA_Q_TILE).reshape(B * S, -1)
    ow = _swa_attention(qw, kw, vw, sink, B, S, tq=SWA_Q_TILE).reshape(B * S, -1)

    x1s, hp, comb_t = _post_attention(
        om, ow, gates, x2d, w_br_mla.astype(BF16), w_br_swa.astype(BF16), w_out.astype(BF16),
        g_moe[None], w_router.T, router_bias[:, None], w_sh_gu.astype(BF16),
        w_sh_down.astype(BF16), tm=ROW_TILE)

    slots, wts, offs, cnts = _moe_plan(comb_t)
    routed = _moe_sparse(offs[:, :, 0], cnts[:, :, 0], slots.reshape(slots.shape[0], -1),
                         wts.reshape(wts.shape[0], -1), hp, w_exp_gu.astype(BF16),
                         w_exp_down.astype(BF16))

    return _ple(x1s, routed, p2d, g_ple[None], w_ple_gate.astype(BF16), b_ple[None],
                w_ple_proj.astype(BF16), tm=ROW_TILE)


def kernel(x, p, positions, g_mix, w_in, b_gate, g_cq, w_uq, g_ckv, w_ukv, g_qn_mla, g_kn_mla, g_qn_swa, g_kn_swa, sink, w_br_mla, w_br_swa, w_out, g_moe, w_router, router_bias, w_exp_gu, w_exp_down, w_sh_gu, w_sh_down, g_ple, w_ple_gate, b_ple, w_ple_proj):
    B, S, D = x.shape
    x2d = x.reshape(B * S, D)
    pos2d = positions.reshape(B * S, 1)
    for i in range(p.shape[0]):
        x2d = _layer(x2d, p[i].reshape(B * S, -1), pos2d, B, S, g_mix[i], w_in[i], b_gate[i],
                     g_cq[i], w_uq[i], g_ckv[i], w_ukv[i], g_qn_mla[i], g_kn_mla[i], g_qn_swa[i],
                     g_kn_swa[i], sink[i], w_br_mla[i], w_br_swa[i], w_out[i], g_moe[i],
                     w_router[i], router_bias[i], w_exp_gu[i], w_exp_down[i], w_sh_gu[i],
                     w_sh_down[i], g_ple[i], w_ple_gate[i], b_ple[i], w_ple_proj[i])
    return x2d.reshape(B, S, D)
```

```python
import functools

import jax
import jax.numpy as jnp
import numpy as np
from jax import lax
from jax.experimental import pallas as pl
from jax.experimental.pallas import tpu as pltpu

D_MODEL = 1024
PLE_DIM = 256
ROPE_THETA = 10000.0
EPS = 1e-6
NEG_INF = -1e30

MLA_HEADS = 8
MLA_Q_RANK = 384
MLA_KV_RANK = 256
MLA_NOPE = 64
MLA_ROPE = 32
MLA_QK = MLA_NOPE + MLA_ROPE
MLA_V = 64

SWA_HEADS = 8
SWA_KV_HEADS = 2
SWA_GROUP = SWA_HEADS // SWA_KV_HEADS
SWA_HD = 64
WINDOW = 128

OFF_CQ = MLA_Q_RANK
OFF_CKV = OFF_CQ + MLA_KV_RANK
OFF_KR = OFF_CKV + MLA_ROPE
OFF_QS = OFF_KR + SWA_HEADS * SWA_HD
OFF_KS = OFF_QS + SWA_KV_HEADS * SWA_HD
OFF_VS = OFF_KS + SWA_KV_HEADS * SWA_HD
OFF_GA = OFF_VS + D_MODEL

N_EXPERTS = 64
TOP_K = 8
N_GROUPS = 8
TOPK_GROUPS = 4
EXPERTS_PER_GROUP = N_EXPERTS // N_GROUPS
EXPERT_FF = 256
SHARED_FF = 256
ROUTED_SCALE = 2.5

LANES = 128
ROW_WORDS = D_MODEL // 2
ROW_SUB = ROW_WORDS // LANES
MOE_CHUNK = 2048
MOE_TILE = 128
MOE_TILE_BIG = 256
SLOT_ALIGN = 16
COMBINE_BLOCK = 256
PRE_TILE = 256
ROW_TILE = 512
MLA_Q_TILE = 512
MLA_KV_TILE = 256
LOG2E = 1.4426950408889634
SOFTMAX_UNDERFLOW_GUARD = 1e-30
SWA_Q_TILE = 512
SWA_BLOCK = 128
VMEM_LIMIT = 56 * 1024 * 1024

BF16 = jnp.bfloat16
F32 = jnp.float32

C_CQ = 0
C_CKV = C_CQ + MLA_Q_RANK
C_KR = C_CKV + MLA_KV_RANK
C_QS = C_KR + LANES
C_KS = C_QS + SWA_HEADS * LANES
C_VS = C_KS + SWA_KV_HEADS * LANES
C_GA = C_VS + SWA_KV_HEADS * LANES
C_END = C_GA + 2 * D_MODEL


def _full(shape):
    nd = len(shape)
    return pl.BlockSpec(shape, lambda *_: (0,) * nd)


def _dot(a, b):
    return jnp.dot(a, b, preferred_element_type=F32)


def _dot_nt(a, b, precision=None):
    return lax.dot_general(a, b, (((1,), (1,)), ((), ())), precision=precision,
                           preferred_element_type=F32)


def _rms(v, n):
    return v * lax.rsqrt(jnp.sum(v * v, axis=-1, keepdims=True) * (1.0 / n) + EPS)


def _rope(v, cos, sin):
    return v * cos + pltpu.roll(v, LANES // 2, 1) * sin


def _pre_kernel(x_ref, pos_ref, w_all_ref, w_uq_ref, w_k_ref, w_v_ref, g_mix_ref, g_cq_ref,
                g_ckv_ref, gq_m_ref, gk_m_ref, gq_s_ref, gk_s_ref, b_gate_ref, rope_ref,
                qm_ref, km_ref, vm_ref, qw_ref, kw_ref, vw_ref, gate_ref):
    x = x_ref[...]
    h = (_rms(x, D_MODEL) * g_mix_ref[...]).astype(BF16)

    def proj(lo, hi):
        return _dot(h, w_all_ref[:, lo:hi])

    z_lat = proj(C_CQ, C_QS)

    pos = pos_ref[...].astype(F32)
    rope = rope_ref[...]
    ang = pos * rope[0:1, :]
    cos_m1 = jnp.cos(ang) - 1.0
    sin = jnp.sin(ang)
    cos_m, sin_m = 1.0 + cos_m1 * rope[2:3, :], sin * rope[1:2, :]
    cos_s, sin_s = 1.0 + cos_m1 * rope[4:5, :], sin * rope[3:4, :]

    cqn = (_rms(z_lat[:, C_CQ:C_CKV], MLA_Q_RANK) * g_cq_ref[...]).astype(BF16)
    ckvn = (_rms(z_lat[:, C_CKV:C_KR], MLA_KV_RANK) * g_ckv_ref[...]).astype(BF16)
    q = _dot(cqn, w_uq_ref[...])
    z_qs = proj(C_QS, C_KS)
    gq_m = gq_m_ref[...]
    for hd in range(MLA_HEADS):
        qh = _rms(q[:, hd * LANES:(hd + 1) * LANES], MLA_QK) * gq_m
        qh = _rope(qh, cos_m, sin_m) * (MLA_QK ** -0.5 * LOG2E)
        qm_ref[:, hd * LANES:(hd + 1) * LANES] = qh.astype(BF16)

    kn = _dot(ckvn, w_k_ref[...])
    vm_ref[...] = _dot(ckvn, w_v_ref[...]).astype(BF16)
    z_kv = proj(C_KS, C_GA)

    gq_s = gq_s_ref[...]
    for hd in range(SWA_HEADS):
        qh = _rms(z_qs[:, hd * LANES:(hd + 1) * LANES], SWA_HD) * gq_s
        qh = _rope(qh, cos_s, sin_s) * (SWA_HD ** -0.5)
        qw_ref[:, hd * LANES:(hd + 1) * LANES] = qh.astype(BF16)

    half = (C_END - C_GA) // 2
    z_ga = proj(C_GA, C_GA + half)

    gk_m = gk_m_ref[...]
    kr = z_lat[:, C_KR:C_QS]
    ss_kr = jnp.sum(kr * kr, axis=-1, keepdims=True)
    kr_rot = _rope(kr * gk_m, cos_m, sin_m)
    for hd in range(MLA_HEADS):
        kh = kn[:, hd * LANES:(hd + 1) * LANES]
        ss = jnp.sum(kh * kh, axis=-1, keepdims=True) + ss_kr
        sc = lax.rsqrt(ss * (1.0 / MLA_QK) + EPS)
        km_ref[:, hd * LANES:(hd + 1) * LANES] = ((kh * gk_m + kr_rot) * sc).astype(BF16)

    z_gb = proj(C_GA + half, C_END)

    gk_s = gk_s_ref[...]
    for hd in range(SWA_KV_HEADS):
        kh = _rms(z_kv[:, hd * LANES:(hd + 1) * LANES], SWA_HD) * gk_s
        kh = _rope(kh, cos_s, sin_s)
        kw_ref[:, hd * LANES:(hd + 1) * LANES] = kh.astype(BF16)
    vw_ref[...] = z_kv[:, C_VS - C_KS:].astype(BF16)

    gate_ref[:, :half] = jax.nn.sigmoid(z_ga + b_gate_ref[:, :half]).astype(BF16)
    gate_ref[:, half:] = jax.nn.sigmoid(z_gb + b_gate_ref[:, half:]).astype(BF16)


def _pre_attention(x2d, pos2d, w_all, w_uq, w_k, w_v, g_mix, g_cq, g_ckv, gq_m, gk_m, gq_s,
                   gk_s, b_gate, rope_tab, tm):
    T = x2d.shape[0]
    row = lambda n: pl.BlockSpec((tm, n), lambda i: (i, 0))
    outs = [(MLA_HEADS * LANES, BF16), (MLA_HEADS * LANES, BF16), (MLA_HEADS * MLA_V, BF16),
            (SWA_HEADS * LANES, BF16), (SWA_KV_HEADS * LANES, BF16),
            (SWA_KV_HEADS * LANES, BF16), (2 * D_MODEL, BF16)]
    consts = [w_all, w_uq, w_k, w_v, g_mix, g_cq, g_ckv, gq_m, gk_m, gq_s, gk_s, b_gate, rope_tab]
    return pl.pallas_call(
        _pre_kernel,
        grid=(T // tm,),
        in_specs=[row(D_MODEL), row(1)] + [_full(c.shape) for c in consts],
        out_specs=[row(n) for n, _ in outs],
        out_shape=[jax.ShapeDtypeStruct((T, n), dt) for n, dt in outs],
        compiler_params=pltpu.CompilerParams(dimension_semantics=("arbitrary",),
                                             vmem_limit_bytes=VMEM_LIMIT),
        name="pre_attention",
    )(x2d, pos2d, *consts)


def _half_masks(dtype):
    lane = lax.broadcasted_iota(jnp.int32, (1, LANES), 1)
    lo = (lane < LANES // 2).astype(dtype)
    return lo, 1 - lo


def _mla_exact(q_ref, k_ref, v_ref, o_ref):
    v = v_ref[...]
    masks = _half_masks(v.dtype)
    acc = None
    for hh in range(2):
        q = q_ref[:, hh * LANES:(hh + 1) * LANES]
        k = k_ref[:, hh * LANES:(hh + 1) * LANES]
        s = _dot_nt(q, k)
        m = jnp.max(s, axis=-1, keepdims=True)
        p = jnp.exp2(s - m)
        l = jnp.sum(p, axis=-1, keepdims=True)
        o = _dot(p.astype(BF16), v * masks[hh]) / l
        acc = o if acc is None else acc + o
    o_ref[...] = acc.astype(o_ref.dtype)


def _mla_kernel(q_ref, k_ref, v_ref, o_ref, kmax_ref):
    S = k_ref.shape[0]
    tq = q_ref.shape[0]

    @pl.when(pl.program_id(2) == 0)
    def _():
        for hh in range(2):
            k = k_ref[:, hh * LANES:(hh + 1) * LANES].astype(F32)
            kn2 = jnp.max(jnp.sum(k * k, axis=-1, keepdims=True), axis=0, keepdims=True)
            kmax_ref[hh] = jnp.broadcast_to(jnp.sqrt(kn2), kmax_ref.shape[1:])

    masks = _half_masks(BF16)
    acc = None
    lmin = None
    for hh in range(2):
        q = q_ref[:, hh * LANES:(hh + 1) * LANES]
        qf = q.astype(F32)
        bound = jnp.sqrt(jnp.sum(qf * qf, axis=-1, keepdims=True)) * kmax_ref[hh][0:1, 0:1]
        v = v_ref[...] * masks[hh]
        o = jnp.zeros((tq, LANES), F32)
        lsum = jnp.zeros((tq, LANES), F32)
        for j in range(S // MLA_KV_TILE):
            rows = slice(j * MLA_KV_TILE, (j + 1) * MLA_KV_TILE)
            p = jnp.exp2(_dot_nt(q, k_ref[rows, hh * LANES:(hh + 1) * LANES]) - bound)
            for t in range(MLA_KV_TILE // LANES):
                lsum = lsum + p[:, t * LANES:(t + 1) * LANES]
            o = o + _dot(p.astype(BF16), v[rows, :])
        l = jnp.sum(lsum, axis=-1, keepdims=True)
        o = o / l
        acc = o if acc is None else acc + o
        lm = jnp.min(l)
        lmin = lm if lmin is None else jnp.minimum(lmin, lm)
    o_ref[...] = acc.astype(o_ref.dtype)

    @pl.when(jnp.logical_not(lmin > SOFTMAX_UNDERFLOW_GUARD))
    def _():
        _mla_exact(q_ref, k_ref, v_ref, o_ref)


def _mla_attention(qm, km, vm, B, S, tq):
    pairs = MLA_HEADS // 2
    q3 = qm.reshape(B, S, MLA_HEADS * LANES)
    k3 = km.reshape(B, S, MLA_HEADS * LANES)
    v3 = vm.reshape(B, S, MLA_HEADS * MLA_V)
    return pl.pallas_call(
        _mla_kernel,
        grid=(B, pairs, S // tq),
        in_specs=[pl.BlockSpec((None, tq, 2 * LANES), lambda b, p, i: (b, i, p)),
                  pl.BlockSpec((None, S, 2 * LANES), lambda b, p, i: (b, 0, p)),
                  pl.BlockSpec((None, S, LANES), lambda b, p, i: (b, 0, p))],
        out_specs=pl.BlockSpec((None, tq, LANES), lambda b, p, i: (b, i, p)),
        out_shape=jax.ShapeDtypeStruct((B, S, MLA_HEADS * MLA_V), BF16),
        scratch_shapes=[pltpu.VMEM((2, 8, LANES), F32)],
        compiler_params=pltpu.CompilerParams(
            dimension_semantics=("arbitrary", "arbitrary", "arbitrary"),
            vmem_limit_bytes=VMEM_LIMIT),
        name="mla_attention",
    )(q3, k3, v3)


def _swa_kernel(sink_ref, q_ref, k_ref, v_ref, o_ref, *, tq, S):
    hk = pl.program_id(1)
    i = pl.program_id(2)
    tk = SWA_BLOCK + 2 * WINDOW
    row = lax.broadcasted_iota(jnp.int32, (SWA_GROUP * SWA_BLOCK, 1), 0)
    qoff = row & (SWA_BLOCK - 1)
    head = row // SWA_BLOCK
    sk = jnp.zeros((SWA_GROUP * SWA_BLOCK, 1), F32)
    for g in range(SWA_GROUP):
        sk = jnp.where(head == g, sink_ref[SWA_GROUP * hk + g], sk)
    low_half = lax.broadcasted_iota(jnp.int32, (1, LANES), 1) < LANES // 2
    for sub in range(tq // SWA_BLOCK):
        rows = slice(sub * SWA_BLOCK, (sub + 1) * SWA_BLOCK)
        q0 = i * tq + sub * SWA_BLOCK
        kstart = pl.multiple_of(jnp.clip(q0 - WINDOW, 0, S - tk), WINDOW)
        k = k_ref[pl.ds(kstart, tk), :]
        v = v_ref[pl.ds(kstart, tk), :]
        q = jnp.concatenate([q_ref[rows, g * LANES:(g + 1) * LANES] for g in range(SWA_GROUP)], axis=0)
        kpos = kstart + lax.broadcasted_iota(jnp.int32, (1, tk), 1)
        valid = jnp.abs(kpos - (q0 + qoff)) <= WINDOW
        s = jnp.where(valid, _dot_nt(q, k), NEG_INF)
        m = jnp.maximum(jnp.max(s, axis=-1, keepdims=True), sk)
        e = jnp.exp(s - m)
        denom = jnp.sum(e, axis=-1, keepdims=True) + jnp.exp(sk - m)
        o = _dot(e.astype(BF16), v) / denom
        for j in range(SWA_GROUP // 2):
            even = o[(2 * j) * SWA_BLOCK:(2 * j + 1) * SWA_BLOCK, :]
            odd = o[(2 * j + 1) * SWA_BLOCK:(2 * j + 2) * SWA_BLOCK, :]
            o_ref[rows, j * LANES:(j + 1) * LANES] = jnp.where(low_half, even, odd).astype(o_ref.dtype)


def _swa_attention(qw, kw, vw, sink, B, S, tq):
    q3 = qw.reshape(B, S, SWA_HEADS * LANES)
    k3 = kw.reshape(B, S, SWA_KV_HEADS * LANES)
    v3 = vw.reshape(B, S, SWA_KV_HEADS * LANES)
    return pl.pallas_call(
        functools.partial(_swa_kernel, tq=tq, S=S),
        grid=(B, SWA_KV_HEADS, S // tq),
        in_specs=[pl.BlockSpec(memory_space=pltpu.SMEM),
                  pl.BlockSpec((None, tq, SWA_GROUP * LANES), lambda b, h, i: (b, i, h)),
                  pl.BlockSpec((None, S, LANES), lambda b, h, i: (b, 0, h)),
                  pl.BlockSpec((None, S, LANES), lambda b, h, i: (b, 0, h))],
        out_specs=pl.BlockSpec((None, tq, SWA_GROUP * SWA_HD), lambda b, h, i: (b, i, h)),
        out_shape=jax.ShapeDtypeStruct((B, S, SWA_HEADS * SWA_HD), BF16),
        compiler_params=pltpu.CompilerParams(
            dimension_semantics=("arbitrary", "arbitrary", "arbitrary"),
            vmem_limit_bytes=VMEM_LIMIT),
        name="swa_attention",
    )(sink, q3, k3, v3)


def _beats(vj, vi, j_first):
    return (vj >= vi) if j_first else (vj > vi)


def _route(scores, sel):
    G, P = N_GROUPS, EXPERTS_PER_GROUP
    groups = [sel[g * P:(g + 1) * P, :] for g in range(G)]
    row = lax.broadcasted_iota(jnp.int32, (P, 1), 0)
    gscore = []
    for vg in groups:
        m1 = jnp.max(vg, axis=0, keepdims=True)
        first = jnp.min(jnp.where(vg == m1, row, P), axis=0, keepdims=True)
        m2 = jnp.max(jnp.where(row == first, -jnp.inf, vg), axis=0, keepdims=True)
        gscore.append(m1 + m2)
    masked = []
    for g in range(G):
        rank = jnp.zeros_like(gscore[g], dtype=jnp.int32)
        for g2 in range(G):
            if g2 != g:
                rank = rank + _beats(gscore[g2], gscore[g], g2 < g).astype(jnp.int32)
        masked.append(jnp.where(rank < TOPK_GROUPS, groups[g], NEG_INF))
    ranks = [jnp.zeros((P, masked[0].shape[1]), jnp.int32) for _ in range(G)]
    for gj in range(G):
        for r in range(P):
            vj = masked[gj][r:r + 1, :]
            for gi in range(G):
                if gi < gj:
                    b = vj > masked[gi]
                elif gi > gj:
                    b = vj >= masked[gi]
                else:
                    b = (vj > masked[gi]) | ((vj == masked[gi]) & (row > r))
                ranks[gi] = ranks[gi] + b.astype(jnp.int32)
    picked = [jnp.where(ranks[g] < TOP_K, scores[g * P:(g + 1) * P, :], 0.0) for g in range(G)]
    total = picked[0]
    for g in range(1, G):
        total = total + picked[g]
    denom = jnp.sum(total, axis=0, keepdims=True)
    return [pk / denom * ROUTED_SCALE for pk in picked]


def _pack_pair(lo, hi):
    return pltpu.pack_elementwise([lo, hi], packed_dtype=BF16)


def _unpack_pair(word):
    lo = pltpu.unpack_elementwise(word, index=0, packed_dtype=BF16, unpacked_dtype=F32)
    hi = pltpu.unpack_elementwise(word, index=1, packed_dtype=BF16, unpacked_dtype=F32)
    return lo, hi


def _store_rows_dense(ref, words):
    for j in range(ROW_SUB):
        ref[pl.ds(j, words.shape[0], stride=ROW_SUB), :] = words[:, j * LANES:(j + 1) * LANES]


def _load_rows_dense(ref, rows):
    sub = ROW_SUB
    return jnp.concatenate([ref[pl.ds(j, rows, stride=sub), :] for j in range(sub)], axis=1)


def _post_kernel(om_ref, ow_ref, gate_ref, x_ref, wbm_ref, wbw_ref, wout_ref, g_moe_ref, wr_ref,
                 rb_ref, wsgu_ref, wsd_ref, x1_ref, hp_ref, comb_ref):
    am = _dot(om_ref[...], wbm_ref[...])
    aw = _dot(ow_ref[...], wbw_ref[...])
    gates = gate_ref[...].astype(F32)
    merged = gates[:, :D_MODEL] * am + gates[:, D_MODEL:] * aw
    x1 = x_ref[...] + _dot(merged.astype(BF16), wout_ref[...])

    h2 = _rms(x1, D_MODEL) * g_moe_ref[...]
    h2b = h2.astype(BF16)
    _store_rows_dense(hp_ref, _pack_pair(h2[:, :ROW_WORDS], h2[:, ROW_WORDS:]))

    wr = wr_ref[...]
    w_hi = wr.astype(BF16)
    w_lo = (wr - w_hi.astype(F32)).astype(BF16)
    h_lo = (h2 - h2b.astype(F32)).astype(BF16)
    by_hi = _dot_nt(jnp.concatenate([w_hi, w_lo], axis=0), h2b)
    logits = by_hi[:N_EXPERTS] + by_hi[N_EXPERTS:] + _dot_nt(w_hi, h_lo)

    sgu = _dot(h2b, wsgu_ref[...])
    sh = jax.nn.silu(sgu[:, :SHARED_FF]) * sgu[:, SHARED_FF:]
    x1_ref[...] = x1 + _dot(sh.astype(BF16), wsd_ref[...])

    scores = jax.nn.sigmoid(logits)
    comb = _route(scores, scores + rb_ref[...])
    for g in range(N_GROUPS):
        comb_ref[g * EXPERTS_PER_GROUP:(g + 1) * EXPERTS_PER_GROUP, :] = comb[g]


def _post_attention(om, ow, gates, x2d, wbm, wbw, wout, g_moe, wr_t, rbias, wsgu, wsd, tm):
    T = x2d.shape[0]
    row = lambda n: pl.BlockSpec((tm, n), lambda i: (i, 0))
    consts = [wbm, wbw, wout, g_moe, wr_t, rbias, wsgu, wsd]
    return pl.pallas_call(
        _post_kernel,
        grid=(T // tm,),
        in_specs=[row(om.shape[1]), row(ow.shape[1]), row(2 * D_MODEL), row(D_MODEL)]
        + [_full(c.shape) for c in consts],
        out_specs=[row(D_MODEL), pl.BlockSpec((tm * ROW_SUB, LANES), lambda i: (i, 0)),
                   pl.BlockSpec((N_EXPERTS, tm), lambda i: (0, i))],
        out_shape=[jax.ShapeDtypeStruct((T, D_MODEL), F32),
                   jax.ShapeDtypeStruct((T * ROW_SUB, LANES), jnp.uint32),
                   jax.ShapeDtypeStruct((N_EXPERTS, T), F32)],
        compiler_params=pltpu.CompilerParams(dimension_semantics=("arbitrary",),
                                             vmem_limit_bytes=VMEM_LIMIT),
        name="post_attention",
    )(om, ow, gates, x2d, *consts)


SLOT_ROWS = -(-(MOE_CHUNK * TOP_K + N_EXPERTS * (SLOT_ALIGN - 1) + MOE_TILE_BIG) // MOE_TILE) * MOE_TILE
DUMMY_SLOT = SLOT_ROWS - 1
PLAN_BLOCK = 256


def _plan_kernel(comb_ref, slot_ref, w_ref, off_ref, cnt_ref):
    comb = comb_ref[...]
    sel = comb > 0.0
    m = sel.astype(F32)
    mb = m.astype(BF16)
    r_i = lax.broadcasted_iota(jnp.int32, (PLAN_BLOCK, PLAN_BLOCK), 0)
    c_i = lax.broadcasted_iota(jnp.int32, (PLAN_BLOCK, PLAN_BLOCK), 1)
    before = (r_i < c_i).astype(BF16)
    carry = jnp.zeros((N_EXPERTS, 1), F32)
    ranks = []
    for b in range(MOE_CHUNK // PLAN_BLOCK):
        blk = slice(b * PLAN_BLOCK, (b + 1) * PLAN_BLOCK)
        ranks.append(_dot(mb[:, blk], before) + carry)
        carry = carry + jnp.sum(m[:, blk], axis=1, keepdims=True)
    rank = jnp.concatenate(ranks, axis=1)
    cnt = carry
    cnt_pad = jnp.floor((cnt + (SLOT_ALIGN - 1)) * (1.0 / SLOT_ALIGN)) * SLOT_ALIGN
    e_r = lax.broadcasted_iota(jnp.int32, (N_EXPERTS, N_EXPERTS), 0)
    e_c = lax.broadcasted_iota(jnp.int32, (N_EXPERTS, N_EXPERTS), 1)
    below = (e_c < e_r).astype(F32)
    off = jnp.dot(below, jnp.broadcast_to(cnt_pad, (N_EXPERTS, LANES)),
                  precision=lax.Precision.HIGHEST, preferred_element_type=F32)
    slot = off[:, :1] + rank
    kidx = _dot(below.astype(BF16), mb)
    row = lax.broadcasted_iota(jnp.int32, (TOP_K, 1), 0)
    slot_acc = jnp.zeros((TOP_K, MOE_CHUNK), F32)
    w_acc = jnp.zeros((TOP_K, MOE_CHUNK), F32)
    for k in range(TOP_K):
        pick = jnp.where(sel & (kidx == k), 1.0, 0.0)
        found = jnp.sum(pick, axis=0, keepdims=True) > 0.0
        s_k = jnp.where(found, jnp.sum(pick * slot, axis=0, keepdims=True), float(DUMMY_SLOT))
        w_k = jnp.sum(pick * comb, axis=0, keepdims=True)
        slot_acc = jnp.where(row == k, s_k, slot_acc)
        w_acc = jnp.where(row == k, w_k, w_acc)
    slot_ref[...] = slot_acc.astype(jnp.int32) * ROW_SUB
    w_ref[...] = w_acc
    off_ref[...] = off.astype(jnp.int32)
    cnt_ref[...] = jnp.broadcast_to(cnt, (N_EXPERTS, LANES)).astype(jnp.int32)


def _moe_plan(comb_t):
    T = comb_t.shape[1]
    nch = T // MOE_CHUNK
    per_pair = pl.BlockSpec((None, TOP_K, MOE_CHUNK), lambda c: (c, 0, 0))
    per_expert = pl.BlockSpec((None, N_EXPERTS, LANES), lambda c: (c, 0, 0))
    return pl.pallas_call(
        _plan_kernel,
        grid=(nch,),
        in_specs=[pl.BlockSpec((N_EXPERTS, MOE_CHUNK), lambda c: (0, c))],
        out_specs=[per_pair, per_pair, per_expert, per_expert],
        out_shape=[jax.ShapeDtypeStruct((nch, TOP_K, MOE_CHUNK), jnp.int32),
                   jax.ShapeDtypeStruct((nch, TOP_K, MOE_CHUNK), F32),
                   jax.ShapeDtypeStruct((nch, N_EXPERTS, LANES), jnp.int32),
                   jax.ShapeDtypeStruct((nch, N_EXPERTS, LANES), jnp.int32)],
        compiler_params=pltpu.CompilerParams(dimension_semantics=("arbitrary",),
                                             vmem_limit_bytes=VMEM_LIMIT),
        name="moe_plan",
    )(comb_t)


def _slab_at(ref, first):
    return ref.at[pl.ds(pl.multiple_of(first, ROW_SUB), ROW_SUB), :]


def _slab(ref, row):
    return _slab_at(ref, row * ROW_SUB)


def _moe_kernel(off_ref, cnt_ref, slot_hbm, w_hbm, hp_ref, wgu_ref, wd_ref, o_ref,
                buf, clo, chi, slot_s, w_s, sem):
    c = pl.program_id(0)
    s = pl.program_id(1)

    @pl.when(s == 0)
    def _dispatch():
        slot_cp = pltpu.make_async_copy(slot_hbm.at[c], slot_s, sem.at[0])
        w_cp = pltpu.make_async_copy(w_hbm.at[c], w_s, sem.at[1])
        slot_cp.start()
        w_cp.start()

        @pl.when(c == 0)
        def _():
            buf[...] = jnp.zeros_like(buf)

        slot_cp.wait()
        w_cp.wait()

        def scatter(tb, carry):
            for tt in range(8):
                t = tb * 8 + tt
                slab = _slab(hp_ref, t)[...]
                for k in range(TOP_K):
                    _slab_at(buf, slot_s[k * MOE_CHUNK + t])[...] = slab
            return carry

        lax.fori_loop(0, MOE_CHUNK // 8, scatter, 0)

    @pl.when(s < N_EXPERTS)
    def _expert():
        n = cnt_ref[c, s]
        off = off_ref[c, s]

        def ffn(start, rows):
            view = buf.at[pl.ds(pl.multiple_of((off + start) * ROW_SUB, SLOT_ALIGN * ROW_SUB),
                                rows * ROW_SUB), :]
            x_lo, x_hi = _unpack_pair(_load_rows_dense(view, rows))
            gu = (_dot(x_lo.astype(BF16), wgu_ref[:ROW_WORDS, :])
                  + _dot(x_hi.astype(BF16), wgu_ref[ROW_WORDS:, :]))
            hid = jax.nn.silu(gu[:, :EXPERT_FF]) * gu[:, EXPERT_FF:]
            y = _dot(hid.astype(BF16), wd_ref[...])
            mine = lax.broadcasted_iota(jnp.int32, (rows, 1), 0) < (n - start)
            _store_rows_dense(view, _pack_pair(jnp.where(mine, y[:, :ROW_WORDS], x_lo),
                                               jnp.where(mine, y[:, ROW_WORDS:], x_hi)))

        n_big = (n + MOE_TILE_BIG - MOE_TILE - 1) // MOE_TILE_BIG

        def big(r, carry):
            ffn(r * MOE_TILE_BIG, MOE_TILE_BIG)
            return carry

        lax.fori_loop(0, n_big, big, 0)

        @pl.when(n > n_big * MOE_TILE_BIG)
        def _():
            ffn(n_big * MOE_TILE_BIG, MOE_TILE)

    @pl.when(s >= N_EXPERTS)
    def _combine():
        t0 = (s - N_EXPERTS) * COMBINE_BLOCK

        def gather(i, carry):
            for tt in range(4):
                tl = i * 4 + tt
                t = t0 + tl
                acc_lo = jnp.zeros((ROW_SUB, LANES), F32)
                acc_hi = jnp.zeros((ROW_SUB, LANES), F32)
                for k in range(TOP_K):
                    lo, hi = _unpack_pair(_slab_at(buf, slot_s[k * MOE_CHUNK + t])[...])
                    wk = w_s[k * MOE_CHUNK + t]
                    acc_lo = acc_lo + wk * lo
                    acc_hi = acc_hi + wk * hi
                _slab(clo, tl)[...] = acc_lo
                _slab(chi, tl)[...] = acc_hi
            return carry

        lax.fori_loop(0, COMBINE_BLOCK // 4, gather, 0)
        o_ref[:, :ROW_WORDS] = _load_rows_dense(clo, COMBINE_BLOCK)
        o_ref[:, ROW_WORDS:] = _load_rows_dense(chi, COMBINE_BLOCK)


def _moe_sparse(offs, cnts, slots, wts, hp, w_gu, w_d):
    nch = offs.shape[0]
    T = nch * MOE_CHUNK
    blocks = MOE_CHUNK // COMBINE_BLOCK
    expert = lambda c, s, *_: (jnp.minimum(s, N_EXPERTS - 1), 0, 0)
    grid_spec = pltpu.PrefetchScalarGridSpec(
        num_scalar_prefetch=2,
        grid=(nch, N_EXPERTS + blocks),
        in_specs=[pl.BlockSpec(memory_space=pl.ANY),
                  pl.BlockSpec(memory_space=pl.ANY),
                  pl.BlockSpec((MOE_CHUNK * ROW_SUB, LANES), lambda c, s, *_: (c, 0)),
                  pl.BlockSpec((None, D_MODEL, 2 * EXPERT_FF), expert),
                  pl.BlockSpec((None, EXPERT_FF, D_MODEL), expert)],
        out_specs=pl.BlockSpec(
            (COMBINE_BLOCK, D_MODEL),
            lambda c, s, *_: (c * blocks + jnp.maximum(s - N_EXPERTS, 0), 0)),
        scratch_shapes=[pltpu.VMEM((SLOT_ROWS * ROW_SUB, LANES), jnp.uint32),
                        pltpu.VMEM((COMBINE_BLOCK * ROW_SUB, LANES), F32),
                        pltpu.VMEM((COMBINE_BLOCK * ROW_SUB, LANES), F32),
                        pltpu.SMEM((TOP_K * MOE_CHUNK,), jnp.int32),
                        pltpu.SMEM((TOP_K * MOE_CHUNK,), F32),
                        pltpu.SemaphoreType.DMA((2,))])
    return pl.pallas_call(
        _moe_kernel,
        grid_spec=grid_spec,
        out_shape=jax.ShapeDtypeStruct((T, D_MODEL), F32),
        compiler_params=pltpu.CompilerParams(dimension_semantics=("arbitrary", "arbitrary"),
                                             vmem_limit_bytes=VMEM_LIMIT),
        name="moe_experts",
    )(offs, cnts, slots, wts, hp, w_gu, w_d)


def _ple_kernel(x1_ref, r_ref, p_ref, g_ref, wg_ref, b_ref, wp_ref, o_ref):
    x2 = x1_ref[...] + r_ref[...]
    hn = (_rms(x2, D_MODEL) * g_ref[...]).astype(BF16)
    gate = jax.nn.sigmoid(_dot(hn, wg_ref[...]) + b_ref[...])
    o_ref[...] = x2 + gate * _dot(p_ref[...].astype(BF16), wp_ref[...])


def _ple(x1s, routed, p2d, g_ple, wg, b_ple, wp, tm):
    T = x1s.shape[0]
    row = lambda n: pl.BlockSpec((tm, n), lambda i: (i, 0))
    consts = [g_ple, wg, b_ple, wp]
    return pl.pallas_call(
        _ple_kernel,
        grid=(T // tm,),
        in_specs=[row(D_MODEL), row(D_MODEL), row(PLE_DIM)] + [_full(c.shape) for c in consts],
        out_specs=row(D_MODEL),
        out_shape=jax.ShapeDtypeStruct((T, D_MODEL), F32),
        compiler_params=pltpu.CompilerParams(dimension_semantics=("arbitrary",),
                                             vmem_limit_bytes=VMEM_LIMIT),
        name="ple",
    )(x1s, routed, p2d, *consts)


def _lane_map(*runs):
    src = np.full((LANES,), -1)
    for lane, dim, n in runs:
        src[lane:lane + n] = np.arange(dim, dim + n)
    return src


_MLA_HALF = MLA_ROPE // 2
_SWA_HALF = SWA_HD // 2
MLA_LANES = _lane_map((0, MLA_NOPE, _MLA_HALF), (_MLA_HALF, 0, LANES // 2 - _MLA_HALF),
                      (LANES // 2, MLA_NOPE + _MLA_HALF, _MLA_HALF),
                      (LANES // 2 + _MLA_HALF, LANES // 2 - _MLA_HALF, MLA_NOPE - LANES // 2 + _MLA_HALF))
MLA_NOPE_LANES = np.where(MLA_LANES < MLA_NOPE, MLA_LANES, -1)
MLA_ROPE_LANES = np.where(MLA_LANES >= MLA_NOPE, MLA_LANES - MLA_NOPE, -1)
SWA_LANES = _lane_map((LANES // 2 - _SWA_HALF, 0, _SWA_HALF), (LANES - _SWA_HALF, _SWA_HALF, _SWA_HALF))


def _spread(w, heads, lane_src):
    k = w.shape[0]
    dim = w.shape[1] // heads
    w = jnp.pad(w.reshape(k, heads, dim), ((0, 0), (0, 0), (0, 1)))
    return w[:, :, np.where(lane_src < 0, dim, lane_src)].reshape(k, heads * LANES)


def _rope_table():
    def inv_freq(dim):
        return 1.0 / (ROPE_THETA ** (jnp.arange(0, dim, 2, dtype=F32) / dim))

    def selector(lane_src, half):
        sel = np.where(lane_src < 0, 0.0, np.where(lane_src < half, -1.0, 1.0))
        return jnp.asarray(sel, F32)

    sel_m = selector(MLA_ROPE_LANES, _MLA_HALF)
    sel_s = selector(SWA_LANES, _SWA_HALF)
    freq_m = _spread(jnp.tile(inv_freq(MLA_ROPE), 2)[None], 1, MLA_ROPE_LANES)[0]
    freq_s = _spread(jnp.tile(inv_freq(SWA_HD), 2)[None], 1, SWA_LANES)[0]
    zero = jnp.zeros((LANES,), F32)
    rows = [freq_m + freq_s, sel_m, jnp.abs(sel_m), sel_s, jnp.abs(sel_s), zero, zero, zero]
    return jnp.stack(rows)


def _layer(x2d, p2d, pos2d, B, S, g_mix, w_in, b_gate, g_cq, w_uq, g_ckv, w_ukv, g_qn_mla, g_kn_mla,
           g_qn_swa, g_kn_swa, sink, w_br_mla, w_br_swa, w_out, g_moe, w_router, router_bias,
           w_exp_gu, w_exp_down, w_sh_gu, w_sh_down, g_ple, w_ple_gate, b_ple, w_ple_proj):
    w_kr = _spread(w_in[:, OFF_CKV:OFF_KR], 1, MLA_ROPE_LANES)
    w_vs = w_in[:, OFF_KS:OFF_VS].reshape(D_MODEL, SWA_KV_HEADS, 1, SWA_HD)
    w_vs = jnp.broadcast_to(w_vs, (D_MODEL, SWA_KV_HEADS, 2, SWA_HD)).reshape(D_MODEL, -1)
    w_all = jnp.concatenate([
        w_in[:, :OFF_CKV], w_kr,
        _spread(w_in[:, OFF_KR:OFF_QS], SWA_HEADS, SWA_LANES),
        _spread(w_in[:, OFF_QS:OFF_KS], SWA_KV_HEADS, SWA_LANES),
        w_vs, w_in[:, OFF_VS:]], axis=1).astype(BF16)
    assert w_all.shape[1] == C_END
    w_uq_p = _spread(w_uq, MLA_HEADS, MLA_LANES).astype(BF16)
    w_ukv3 = w_ukv.reshape(MLA_KV_RANK, MLA_HEADS, MLA_NOPE + MLA_V)
    w_k = _spread(w_ukv3[:, :, :MLA_NOPE].reshape(MLA_KV_RANK, -1), MLA_HEADS, MLA_NOPE_LANES).astype(BF16)
    w_v = w_ukv3[:, :, MLA_NOPE:].reshape(MLA_KV_RANK, -1).astype(BF16)

    qm, km, vm, qw, kw, vw, gates = _pre_attention(
        x2d, pos2d, w_all, w_uq_p, w_k, w_v, g_mix[None], g_cq[None], g_ckv[None],
        _spread(g_qn_mla[None], 1, MLA_LANES), _spread(g_kn_mla[None], 1, MLA_LANES),
        _spread(g_qn_swa[None], 1, SWA_LANES), _spread(g_kn_swa[None], 1, SWA_LANES),
        b_gate[None], _rope_table(), tm=PRE_TILE)

    om = _mla_attention(qm, km, vm, B, S, tq=MLA_Q_TILE).reshape(B * S, -1)
    ow = _swa_attention(qw, kw, vw, sink, B, S, tq=SWA_Q_TILE).reshape(B * S, -1)

    x1s, hp, comb_t = _post_attention(
        om, ow, gates, x2d, w_br_mla.astype(BF16), w_br_swa.astype(BF16), w_out.astype(BF16),
        g_moe[None], w_router.T, router_bias[:, None], w_sh_gu.astype(BF16),
        w_sh_down.astype(BF16), tm=ROW_TILE)

    slots, wts, offs, cnts = _moe_plan(comb_t)
    routed = _moe_sparse(offs[:, :, 0], cnts[:, :, 0], slots.reshape(slots.shape[0], -1),
                         wts.reshape(wts.shape[0], -1), hp, w_exp_gu.astype(BF16),
                         w_exp_down.astype(BF16))

    return _ple(x1s, routed, p2d, g_ple[None], w_ple_gate.astype(BF16), b_ple[None],
                w_ple_proj.astype(BF16), tm=ROW_TILE)


def kernel(x, p, positions, g_mix, w_in, b_gate, g_cq, w_uq, g_ckv, w_ukv, g_qn_mla, g_kn_mla, g_qn_swa, g_kn_swa, sink, w_br_mla, w_br_swa, w_out, g_moe, w_router, router_bias, w_exp_gu, w_exp_down, w_sh_gu, w_sh_down, g_ple, w_ple_gate, b_ple, w_ple_proj):
    B, S, D = x.shape
    x2d = x.reshape(B * S, D)
    pos2d = positions.reshape(B * S, 1)
    for i in range(p.shape[0]):
        x2d = _layer(x2d, p[i].reshape(B * S, -1), pos2d, B, S, g_mix[i], w_in[i], b_gate[i],
                     g_cq[i], w_uq[i], g_ckv[i], w_ukv[i], g_qn_mla[i], g_kn_mla[i], g_qn_swa[i],
                     g_kn_swa[i], sink[i], w_br_mla[i], w_br_swa[i], w_out[i], g_moe[i],
                     w_router[i], router_bias[i], w_exp_gu[i], w_exp_down[i], w_sh_gu[i],
                     w_sh_down[i], g_ple[i], w_ple_gate[i], b_ple[i], w_ple_proj[i])
    return x2d.reshape(B, S, D)
```

```python
import functools

import jax
import jax.numpy as jnp
import numpy as np
from jax import lax
from jax.experimental import pallas as pl
from jax.experimental.pallas import tpu as pltpu

D_MODEL = 1024
PLE_DIM = 256
ROPE_THETA = 10000.0
EPS = 1e-6
NEG_INF = -1e30

MLA_HEADS = 8
MLA_Q_RANK = 384
MLA_KV_RANK = 256
MLA_NOPE = 64
MLA_ROPE = 32
MLA_QK = MLA_NOPE + MLA_ROPE
MLA_V = 64

SWA_HEADS = 8
SWA_KV_HEADS = 2
SWA_GROUP = SWA_HEADS // SWA_KV_HEADS
SWA_HD = 64
WINDOW = 128

OFF_CQ = MLA_Q_RANK
OFF_CKV = OFF_CQ + MLA_KV_RANK
OFF_KR = OFF_CKV + MLA_ROPE
OFF_QS = OFF_KR + SWA_HEADS * SWA_HD
OFF_KS = OFF_QS + SWA_KV_HEADS * SWA_HD
OFF_VS = OFF_KS + SWA_KV_HEADS * SWA_HD
OFF_GA = OFF_VS + D_MODEL

N_EXPERTS = 64
TOP_K = 8
N_GROUPS = 8
TOPK_GROUPS = 4
EXPERTS_PER_GROUP = N_EXPERTS // N_GROUPS
EXPERT_FF = 256
SHARED_FF = 256
ROUTED_SCALE = 2.5

LANES = 128
ROW_WORDS = D_MODEL // 2
ROW_SUB = ROW_WORDS // LANES
MOE_CHUNK = 2048
MOE_TILE = 128
MOE_TILE_BIG = 256
SLOT_ALIGN = 16
COMBINE_BLOCK = 256
EXPERTS_PER_STEP = 4
EXPERT_STEPS = N_EXPERTS // EXPERTS_PER_STEP
PRE_TILE = 256
ROW_TILE = 512
MLA_Q_TILE = 512
MLA_KV_TILE = 256
LOG2E = 1.4426950408889634
SOFTMAX_UNDERFLOW_GUARD = 1e-30
SWA_Q_TILE = 512
SWA_BLOCK = 128
VMEM_LIMIT = 56 * 1024 * 1024

BF16 = jnp.bfloat16
F32 = jnp.float32

C_CQ = 0
C_CKV = C_CQ + MLA_Q_RANK
C_KR = C_CKV + MLA_KV_RANK
C_QS = C_KR + LANES
C_KS = C_QS + SWA_HEADS * LANES
C_VS = C_KS + SWA_KV_HEADS * LANES
C_GA = C_VS + SWA_KV_HEADS * LANES
C_END = C_GA + 2 * D_MODEL


def _full(shape):
    nd = len(shape)
    return pl.BlockSpec(shape, lambda *_: (0,) * nd)


def _dot(a, b):
    return jnp.dot(a, b, preferred_element_type=F32)


def _dot_nt(a, b, precision=None):
    return lax.dot_general(a, b, (((1,), (1,)), ((), ())), precision=precision,
                           preferred_element_type=F32)


def _rms(v, n):
    return v * lax.rsqrt(jnp.sum(v * v, axis=-1, keepdims=True) * (1.0 / n) + EPS)


def _rope(v, cos, sin):
    return v * cos + pltpu.roll(v, LANES // 2, 1) * sin


def _pre_kernel(x_ref, pos_ref, w_all_ref, w_uq_ref, w_k_ref, w_v_ref, g_mix_ref, g_cq_ref,
                g_ckv_ref, gq_m_ref, gk_m_ref, gq_s_ref, gk_s_ref, b_gate_ref, rope_ref,
                qm_ref, km_ref, vm_ref, qw_ref, kw_ref, vw_ref, gate_ref):
    x = x_ref[...]
    h = (_rms(x, D_MODEL) * g_mix_ref[...]).astype(BF16)

    def proj(lo, hi):
        return _dot(h, w_all_ref[:, lo:hi])

    z_lat = proj(C_CQ, C_QS)

    pos = pos_ref[...].astype(F32)
    rope = rope_ref[...]
    ang = pos * rope[0:1, :]
    cos_m1 = jnp.cos(ang) - 1.0
    sin = jnp.sin(ang)
    cos_m, sin_m = 1.0 + cos_m1 * rope[2:3, :], sin * rope[1:2, :]
    cos_s, sin_s = 1.0 + cos_m1 * rope[4:5, :], sin * rope[3:4, :]

    cqn = (_rms(z_lat[:, C_CQ:C_CKV], MLA_Q_RANK) * g_cq_ref[...]).astype(BF16)
    ckvn = (_rms(z_lat[:, C_CKV:C_KR], MLA_KV_RANK) * g_ckv_ref[...]).astype(BF16)
    q = _dot(cqn, w_uq_ref[...])
    z_qs = proj(C_QS, C_KS)
    gq_m = gq_m_ref[...]
    for hd in range(MLA_HEADS):
        qh = _rms(q[:, hd * LANES:(hd + 1) * LANES], MLA_QK) * gq_m
        qh = _rope(qh, cos_m, sin_m) * (MLA_QK ** -0.5 * LOG2E)
        qm_ref[:, hd * LANES:(hd + 1) * LANES] = qh.astype(BF16)

    kn = _dot(ckvn, w_k_ref[...])
    vm_ref[...] = _dot(ckvn, w_v_ref[...]).astype(BF16)
    z_kv = proj(C_KS, C_GA)

    gq_s = gq_s_ref[...]
    for hd in range(SWA_HEADS):
        qh = _rms(z_qs[:, hd * LANES:(hd + 1) * LANES], SWA_HD) * gq_s
        qh = _rope(qh, cos_s, sin_s) * (SWA_HD ** -0.5)
        qw_ref[:, hd * LANES:(hd + 1) * LANES] = qh.astype(BF16)

    half = (C_END - C_GA) // 2
    z_ga = proj(C_GA, C_GA + half)

    gk_m = gk_m_ref[...]
    kr = z_lat[:, C_KR:C_QS]
    ss_kr = jnp.sum(kr * kr, axis=-1, keepdims=True)
    kr_rot = _rope(kr * gk_m, cos_m, sin_m)
    for hd in range(MLA_HEADS):
        kh = kn[:, hd * LANES:(hd + 1) * LANES]
        ss = jnp.sum(kh * kh, axis=-1, keepdims=True) + ss_kr
        sc = lax.rsqrt(ss * (1.0 / MLA_QK) + EPS)
        km_ref[:, hd * LANES:(hd + 1) * LANES] = ((kh * gk_m + kr_rot) * sc).astype(BF16)

    z_gb = proj(C_GA + half, C_END)

    gk_s = gk_s_ref[...]
    for hd in range(SWA_KV_HEADS):
        kh = _rms(z_kv[:, hd * LANES:(hd + 1) * LANES], SWA_HD) * gk_s
        kh = _rope(kh, cos_s, sin_s)
        kw_ref[:, hd * LANES:(hd + 1) * LANES] = kh.astype(BF16)
    vw_ref[...] = z_kv[:, C_VS - C_KS:].astype(BF16)

    gate_ref[:, :half] = jax.nn.sigmoid(z_ga + b_gate_ref[:, :half]).astype(BF16)
    gate_ref[:, half:] = jax.nn.sigmoid(z_gb + b_gate_ref[:, half:]).astype(BF16)


def _pre_attention(x2d, pos2d, w_all, w_uq, w_k, w_v, g_mix, g_cq, g_ckv, gq_m, gk_m, gq_s,
                   gk_s, b_gate, rope_tab, tm):
    T = x2d.shape[0]
    row = lambda n: pl.BlockSpec((tm, n), lambda i: (i, 0))
    outs = [(MLA_HEADS * LANES, BF16), (MLA_HEADS * LANES, BF16), (MLA_HEADS * MLA_V, BF16),
            (SWA_HEADS * LANES, BF16), (SWA_KV_HEADS * LANES, BF16),
            (SWA_KV_HEADS * LANES, BF16), (2 * D_MODEL, BF16)]
    consts = [w_all, w_uq, w_k, w_v, g_mix, g_cq, g_ckv, gq_m, gk_m, gq_s, gk_s, b_gate, rope_tab]
    return pl.pallas_call(
        _pre_kernel,
        grid=(T // tm,),
        in_specs=[row(D_MODEL), row(1)] + [_full(c.shape) for c in consts],
        out_specs=[row(n) for n, _ in outs],
        out_shape=[jax.ShapeDtypeStruct((T, n), dt) for n, dt in outs],
        compiler_params=pltpu.CompilerParams(dimension_semantics=("arbitrary",),
                                             vmem_limit_bytes=VMEM_LIMIT),
        name="pre_attention",
    )(x2d, pos2d, *consts)


def _half_masks(dtype):
    lane = lax.broadcasted_iota(jnp.int32, (1, LANES), 1)
    lo = (lane < LANES // 2).astype(dtype)
    return lo, 1 - lo


def _mla_exact(q_ref, k_ref, v_ref, o_ref):
    v = v_ref[...]
    masks = _half_masks(v.dtype)
    acc = None
    for hh in range(2):
        q = q_ref[:, hh * LANES:(hh + 1) * LANES]
        k = k_ref[:, hh * LANES:(hh + 1) * LANES]
        s = _dot_nt(q, k)
        m = jnp.max(s, axis=-1, keepdims=True)
        p = jnp.exp2(s - m)
        l = jnp.sum(p, axis=-1, keepdims=True)
        o = _dot(p.astype(BF16), v * masks[hh]) / l
        acc = o if acc is None else acc + o
    o_ref[...] = acc.astype(o_ref.dtype)


def _mla_kernel(q_ref, k_ref, v_ref, o_ref, kmax_ref):
    S = k_ref.shape[0]
    tq = q_ref.shape[0]

    @pl.when(pl.program_id(2) == 0)
    def _():
        for hh in range(2):
            k = k_ref[:, hh * LANES:(hh + 1) * LANES].astype(F32)
            kn2 = jnp.max(jnp.sum(k * k, axis=-1, keepdims=True), axis=0, keepdims=True)
            kmax_ref[hh] = jnp.broadcast_to(jnp.sqrt(kn2), kmax_ref.shape[1:])

    masks = _half_masks(BF16)
    acc = None
    lmin = None
    for hh in range(2):
        q = q_ref[:, hh * LANES:(hh + 1) * LANES]
        qf = q.astype(F32)
        bound = jnp.sqrt(jnp.sum(qf * qf, axis=-1, keepdims=True)) * kmax_ref[hh][0:1, 0:1]
        v = v_ref[...] * masks[hh]
        o = jnp.zeros((tq, LANES), F32)
        lsum = jnp.zeros((tq, LANES), F32)
        for j in range(S // MLA_KV_TILE):
            rows = slice(j * MLA_KV_TILE, (j + 1) * MLA_KV_TILE)
            p = jnp.exp2(_dot_nt(q, k_ref[rows, hh * LANES:(hh + 1) * LANES]) - bound)
            for t in range(MLA_KV_TILE // LANES):
                lsum = lsum + p[:, t * LANES:(t + 1) * LANES]
            o = o + _dot(p.astype(BF16), v[rows, :])
        l = jnp.sum(lsum, axis=-1, keepdims=True)
        o = o / l
        acc = o if acc is None else acc + o
        lm = jnp.min(l)
        lmin = lm if lmin is None else jnp.minimum(lmin, lm)
    o_ref[...] = acc.astype(o_ref.dtype)

    @pl.when(jnp.logical_not(lmin > SOFTMAX_UNDERFLOW_GUARD))
    def _():
        _mla_exact(q_ref, k_ref, v_ref, o_ref)


def _mla_attention(qm, km, vm, B, S, tq):
    pairs = MLA_HEADS // 2
    q3 = qm.reshape(B, S, MLA_HEADS * LANES)
    k3 = km.reshape(B, S, MLA_HEADS * LANES)
    v3 = vm.reshape(B, S, MLA_HEADS * MLA_V)
    return pl.pallas_call(
        _mla_kernel,
        grid=(B, pairs, S // tq),
        in_specs=[pl.BlockSpec((None, tq, 2 * LANES), lambda b, p, i: (b, i, p)),
                  pl.BlockSpec((None, S, 2 * LANES), lambda b, p, i: (b, 0, p)),
                  pl.BlockSpec((None, S, LANES), lambda b, p, i: (b, 0, p))],
        out_specs=pl.BlockSpec((None, tq, LANES), lambda b, p, i: (b, i, p)),
        out_shape=jax.ShapeDtypeStruct((B, S, MLA_HEADS * MLA_V), BF16),
        scratch_shapes=[pltpu.VMEM((2, 8, LANES), F32)],
        compiler_params=pltpu.CompilerParams(
            dimension_semantics=("arbitrary", "arbitrary", "arbitrary"),
            vmem_limit_bytes=VMEM_LIMIT),
        name="mla_attention",
    )(q3, k3, v3)


def _swa_kernel(sink_ref, q_ref, k_ref, v_ref, o_ref, *, tq, S):
    hk = pl.program_id(1)
    i = pl.program_id(2)
    tk = SWA_BLOCK + 2 * WINDOW
    row = lax.broadcasted_iota(jnp.int32, (SWA_GROUP * SWA_BLOCK, 1), 0)
    qoff = row & (SWA_BLOCK - 1)
    head = row // SWA_BLOCK
    sk = jnp.zeros((SWA_GROUP * SWA_BLOCK, 1), F32)
    for g in range(SWA_GROUP):
        sk = jnp.where(head == g, sink_ref[SWA_GROUP * hk + g], sk)
    low_half = lax.broadcasted_iota(jnp.int32, (1, LANES), 1) < LANES // 2
    for sub in range(tq // SWA_BLOCK):
        rows = slice(sub * SWA_BLOCK, (sub + 1) * SWA_BLOCK)
        q0 = i * tq + sub * SWA_BLOCK
        kstart = pl.multiple_of(jnp.clip(q0 - WINDOW, 0, S - tk), WINDOW)
        k = k_ref[pl.ds(kstart, tk), :]
        v = v_ref[pl.ds(kstart, tk), :]
        q = jnp.concatenate([q_ref[rows, g * LANES:(g + 1) * LANES] for g in range(SWA_GROUP)], axis=0)
        kpos = kstart + lax.broadcasted_iota(jnp.int32, (1, tk), 1)
        valid = jnp.abs(kpos - (q0 + qoff)) <= WINDOW
        s = jnp.where(valid, _dot_nt(q, k), NEG_INF)
        m = jnp.maximum(jnp.max(s, axis=-1, keepdims=True), sk)
        e = jnp.exp(s - m)
        denom = jnp.sum(e, axis=-1, keepdims=True) + jnp.exp(sk - m)
        o = _dot(e.astype(BF16), v) / denom
        for j in range(SWA_GROUP // 2):
            even = o[(2 * j) * SWA_BLOCK:(2 * j + 1) * SWA_BLOCK, :]
            odd = o[(2 * j + 1) * SWA_BLOCK:(2 * j + 2) * SWA_BLOCK, :]
            o_ref[rows, j * LANES:(j + 1) * LANES] = jnp.where(low_half, even, odd).astype(o_ref.dtype)


def _swa_attention(qw, kw, vw, sink, B, S, tq):
    q3 = qw.reshape(B, S, SWA_HEADS * LANES)
    k3 = kw.reshape(B, S, SWA_KV_HEADS * LANES)
    v3 = vw.reshape(B, S, SWA_KV_HEADS * LANES)
    return pl.pallas_call(
        functools.partial(_swa_kernel, tq=tq, S=S),
        grid=(B, SWA_KV_HEADS, S // tq),
        in_specs=[pl.BlockSpec(memory_space=pltpu.SMEM),
                  pl.BlockSpec((None, tq, SWA_GROUP * LANES), lambda b, h, i: (b, i, h)),
                  pl.BlockSpec((None, S, LANES), lambda b, h, i: (b, 0, h)),
                  pl.BlockSpec((None, S, LANES), lambda b, h, i: (b, 0, h))],
        out_specs=pl.BlockSpec((None, tq, SWA_GROUP * SWA_HD), lambda b, h, i: (b, i, h)),
        out_shape=jax.ShapeDtypeStruct((B, S, SWA_HEADS * SWA_HD), BF16),
        compiler_params=pltpu.CompilerParams(
            dimension_semantics=("arbitrary", "arbitrary", "arbitrary"),
            vmem_limit_bytes=VMEM_LIMIT),
        name="swa_attention",
    )(sink, q3, k3, v3)


def _beats(vj, vi, j_first):
    return (vj >= vi) if j_first else (vj > vi)


def _route(scores, sel):
    G, P = N_GROUPS, EXPERTS_PER_GROUP
    groups = [sel[g * P:(g + 1) * P, :] for g in range(G)]
    row = lax.broadcasted_iota(jnp.int32, (P, 1), 0)
    gscore = []
    for vg in groups:
        m1 = jnp.max(vg, axis=0, keepdims=True)
        first = jnp.min(jnp.where(vg == m1, row, P), axis=0, keepdims=True)
        m2 = jnp.max(jnp.where(row == first, -jnp.inf, vg), axis=0, keepdims=True)
        gscore.append(m1 + m2)
    masked = []
    for g in range(G):
        rank = jnp.zeros_like(gscore[g], dtype=jnp.int32)
        for g2 in range(G):
            if g2 != g:
                rank = rank + _beats(gscore[g2], gscore[g], g2 < g).astype(jnp.int32)
        masked.append(jnp.where(rank < TOPK_GROUPS, groups[g], NEG_INF))
    ranks = [jnp.zeros((P, masked[0].shape[1]), jnp.int32) for _ in range(G)]
    for gj in range(G):
        for r in range(P):
            vj = masked[gj][r:r + 1, :]
            for gi in range(G):
                if gi < gj:
                    b = vj > masked[gi]
                elif gi > gj:
                    b = vj >= masked[gi]
                else:
                    b = (vj > masked[gi]) | ((vj == masked[gi]) & (row > r))
                ranks[gi] = ranks[gi] + b.astype(jnp.int32)
    picked = [jnp.where(ranks[g] < TOP_K, scores[g * P:(g + 1) * P, :], 0.0) for g in range(G)]
    total = picked[0]
    for g in range(1, G):
        total = total + picked[g]
    denom = jnp.sum(total, axis=0, keepdims=True)
    return [pk / denom * ROUTED_SCALE for pk in picked]


def _pack_pair(lo, hi):
    return pltpu.pack_elementwise([lo, hi], packed_dtype=BF16)


def _unpack_pair(word):
    lo = pltpu.unpack_elementwise(word, index=0, packed_dtype=BF16, unpacked_dtype=F32)
    hi = pltpu.unpack_elementwise(word, index=1, packed_dtype=BF16, unpacked_dtype=F32)
    return lo, hi


def _store_rows_dense(ref, words):
    for j in range(ROW_SUB):
        ref[pl.ds(j, words.shape[0], stride=ROW_SUB), :] = words[:, j * LANES:(j + 1) * LANES]


def _load_rows_dense(ref, rows):
    sub = ROW_SUB
    return jnp.concatenate([ref[pl.ds(j, rows, stride=sub), :] for j in range(sub)], axis=1)


def _post_kernel(om_ref, ow_ref, gate_ref, x_ref, wbm_ref, wbw_ref, wout_ref, g_moe_ref, wr_ref,
                 rb_ref, wsgu_ref, wsd_ref, x1_ref, hp_ref, comb_ref):
    am = _dot(om_ref[...], wbm_ref[...])
    aw = _dot(ow_ref[...], wbw_ref[...])
    gates = gate_ref[...].astype(F32)
    merged = gates[:, :D_MODEL] * am + gates[:, D_MODEL:] * aw
    x1 = x_ref[...] + _dot(merged.astype(BF16), wout_ref[...])

    h2 = _rms(x1, D_MODEL) * g_moe_ref[...]
    h2b = h2.astype(BF16)
    _store_rows_dense(hp_ref, _pack_pair(h2[:, :ROW_WORDS], h2[:, ROW_WORDS:]))

    wr = wr_ref[...]
    w_hi = wr.astype(BF16)
    w_lo = (wr - w_hi.astype(F32)).astype(BF16)
    h_lo = (h2 - h2b.astype(F32)).astype(BF16)
    by_hi = _dot_nt(jnp.concatenate([w_hi, w_lo], axis=0), h2b)
    logits = by_hi[:N_EXPERTS] + by_hi[N_EXPERTS:] + _dot_nt(w_hi, h_lo)

    sgu = _dot(h2b, wsgu_ref[...])
    sh = jax.nn.silu(sgu[:, :SHARED_FF]) * sgu[:, SHARED_FF:]
    x1_ref[...] = x1 + _dot(sh.astype(BF16), wsd_ref[...])

    scores = jax.nn.sigmoid(logits)
    comb = _route(scores, scores + rb_ref[...])
    for g in range(N_GROUPS):
        comb_ref[g * EXPERTS_PER_GROUP:(g + 1) * EXPERTS_PER_GROUP, :] = comb[g]


def _post_attention(om, ow, gates, x2d, wbm, wbw, wout, g_moe, wr_t, rbias, wsgu, wsd, tm):
    T = x2d.shape[0]
    row = lambda n: pl.BlockSpec((tm, n), lambda i: (i, 0))
    consts = [wbm, wbw, wout, g_moe, wr_t, rbias, wsgu, wsd]
    return pl.pallas_call(
        _post_kernel,
        grid=(T // tm,),
        in_specs=[row(om.shape[1]), row(ow.shape[1]), row(2 * D_MODEL), row(D_MODEL)]
        + [_full(c.shape) for c in consts],
        out_specs=[row(D_MODEL), pl.BlockSpec((tm * ROW_SUB, LANES), lambda i: (i, 0)),
                   pl.BlockSpec((N_EXPERTS, tm), lambda i: (0, i))],
        out_shape=[jax.ShapeDtypeStruct((T, D_MODEL), F32),
                   jax.ShapeDtypeStruct((T * ROW_SUB, LANES), jnp.uint32),
                   jax.ShapeDtypeStruct((N_EXPERTS, T), F32)],
        compiler_params=pltpu.CompilerParams(dimension_semantics=("arbitrary",),
                                             vmem_limit_bytes=VMEM_LIMIT),
        name="post_attention",
    )(om, ow, gates, x2d, *consts)


SLOT_ROWS = -(-(MOE_CHUNK * TOP_K + N_EXPERTS * (SLOT_ALIGN - 1) + MOE_TILE_BIG) // MOE_TILE) * MOE_TILE
DUMMY_SLOT = SLOT_ROWS - 1
PLAN_BLOCK = 256


def _plan_kernel(comb_ref, slot_ref, w_ref, off_ref, cnt_ref):
    comb = comb_ref[...]
    sel = comb > 0.0
    m = sel.astype(F32)
    mb = m.astype(BF16)
    r_i = lax.broadcasted_iota(jnp.int32, (PLAN_BLOCK, PLAN_BLOCK), 0)
    c_i = lax.broadcasted_iota(jnp.int32, (PLAN_BLOCK, PLAN_BLOCK), 1)
    before = (r_i < c_i).astype(BF16)
    carry = jnp.zeros((N_EXPERTS, 1), F32)
    ranks = []
    for b in range(MOE_CHUNK // PLAN_BLOCK):
        blk = slice(b * PLAN_BLOCK, (b + 1) * PLAN_BLOCK)
        ranks.append(_dot(mb[:, blk], before) + carry)
        carry = carry + jnp.sum(m[:, blk], axis=1, keepdims=True)
    rank = jnp.concatenate(ranks, axis=1)
    cnt = carry
    cnt_pad = jnp.floor((cnt + (SLOT_ALIGN - 1)) * (1.0 / SLOT_ALIGN)) * SLOT_ALIGN
    e_r = lax.broadcasted_iota(jnp.int32, (N_EXPERTS, N_EXPERTS), 0)
    e_c = lax.broadcasted_iota(jnp.int32, (N_EXPERTS, N_EXPERTS), 1)
    below = (e_c < e_r).astype(F32)
    off = jnp.dot(below, jnp.broadcast_to(cnt_pad, (N_EXPERTS, LANES)),
                  precision=lax.Precision.HIGHEST, preferred_element_type=F32)
    slot = off[:, :1] + rank
    kidx = _dot(below.astype(BF16), mb)
    row = lax.broadcasted_iota(jnp.int32, (TOP_K, 1), 0)
    slot_acc = jnp.zeros((TOP_K, MOE_CHUNK), F32)
    w_acc = jnp.zeros((TOP_K, MOE_CHUNK), F32)
    for k in range(TOP_K):
        pick = jnp.where(sel & (kidx == k), 1.0, 0.0)
        found = jnp.sum(pick, axis=0, keepdims=True) > 0.0
        s_k = jnp.where(found, jnp.sum(pick * slot, axis=0, keepdims=True), float(DUMMY_SLOT))
        w_k = jnp.sum(pick * comb, axis=0, keepdims=True)
        slot_acc = jnp.where(row == k, s_k, slot_acc)
        w_acc = jnp.where(row == k, w_k, w_acc)
    slot_ref[...] = slot_acc.astype(jnp.int32) * ROW_SUB
    w_ref[...] = w_acc
    off_ref[...] = off.astype(jnp.int32)
    cnt_ref[...] = jnp.broadcast_to(cnt, (N_EXPERTS, LANES)).astype(jnp.int32)


def _moe_plan(comb_t):
    T = comb_t.shape[1]
    nch = T // MOE_CHUNK
    per_pair = pl.BlockSpec((None, TOP_K, MOE_CHUNK), lambda c: (c, 0, 0))
    per_expert = pl.BlockSpec((None, N_EXPERTS, LANES), lambda c: (c, 0, 0))
    return pl.pallas_call(
        _plan_kernel,
        grid=(nch,),
        in_specs=[pl.BlockSpec((N_EXPERTS, MOE_CHUNK), lambda c: (0, c))],
        out_specs=[per_pair, per_pair, per_expert, per_expert],
        out_shape=[jax.ShapeDtypeStruct((nch, TOP_K, MOE_CHUNK), jnp.int32),
                   jax.ShapeDtypeStruct((nch, TOP_K, MOE_CHUNK), F32),
                   jax.ShapeDtypeStruct((nch, N_EXPERTS, LANES), jnp.int32),
                   jax.ShapeDtypeStruct((nch, N_EXPERTS, LANES), jnp.int32)],
        compiler_params=pltpu.CompilerParams(dimension_semantics=("arbitrary",),
                                             vmem_limit_bytes=VMEM_LIMIT),
        name="moe_plan",
    )(comb_t)


def _slab_at(ref, first):
    return ref.at[pl.ds(pl.multiple_of(first, ROW_SUB), ROW_SUB), :]


def _slab(ref, row):
    return _slab_at(ref, row * ROW_SUB)


def _moe_kernel(off_ref, cnt_ref, slot_hbm, w_hbm, hp_ref, wgu_ref, wd_ref, o_ref,
                buf, clo, chi, slot_s, w_s, sem):
    c = pl.program_id(0)
    s = pl.program_id(1)

    @pl.when(s == 0)
    def _dispatch():
        slot_cp = pltpu.make_async_copy(slot_hbm.at[c], slot_s, sem.at[0])
        w_cp = pltpu.make_async_copy(w_hbm.at[c], w_s, sem.at[1])
        slot_cp.start()
        w_cp.start()

        @pl.when(c == 0)
        def _():
            buf[...] = jnp.zeros_like(buf)

        slot_cp.wait()
        w_cp.wait()

        def scatter(tb, carry):
            for tt in range(8):
                t = tb * 8 + tt
                slab = _slab(hp_ref, t)[...]
                for k in range(TOP_K):
                    _slab_at(buf, slot_s[k * MOE_CHUNK + t])[...] = slab
            return carry

        lax.fori_loop(0, MOE_CHUNK // 8, scatter, 0)

    def expert(ee):
        e = s * EXPERTS_PER_STEP + ee
        n = cnt_ref[c, e]
        off = off_ref[c, e]

        def ffn(start, rows):
            view = buf.at[pl.ds(pl.multiple_of((off + start) * ROW_SUB, SLOT_ALIGN * ROW_SUB),
                                rows * ROW_SUB), :]
            x_lo, x_hi = _unpack_pair(_load_rows_dense(view, rows))
            gu = (_dot(x_lo.astype(BF16), wgu_ref[ee, :ROW_WORDS, :])
                  + _dot(x_hi.astype(BF16), wgu_ref[ee, ROW_WORDS:, :]))
            hid = jax.nn.silu(gu[:, :EXPERT_FF]) * gu[:, EXPERT_FF:]
            y = _dot(hid.astype(BF16), wd_ref[ee])
            mine = lax.broadcasted_iota(jnp.int32, (rows, 1), 0) < (n - start)
            _store_rows_dense(view, _pack_pair(jnp.where(mine, y[:, :ROW_WORDS], x_lo),
                                               jnp.where(mine, y[:, ROW_WORDS:], x_hi)))

        n_big = (n + MOE_TILE_BIG - MOE_TILE - 1) // MOE_TILE_BIG

        def big(r, carry):
            ffn(r * MOE_TILE_BIG, MOE_TILE_BIG)
            return carry

        lax.fori_loop(0, n_big, big, 0)

        @pl.when(n > n_big * MOE_TILE_BIG)
        def _():
            ffn(n_big * MOE_TILE_BIG, MOE_TILE)

    @pl.when(s < EXPERT_STEPS)
    def _experts():
        for ee in range(EXPERTS_PER_STEP):
            expert(ee)

    @pl.when(s >= EXPERT_STEPS)
    def _combine():
        t0 = (s - EXPERT_STEPS) * COMBINE_BLOCK

        def gather(i, carry):
            for tt in range(4):
                tl = i * 4 + tt
                t = t0 + tl
                acc_lo = jnp.zeros((ROW_SUB, LANES), F32)
                acc_hi = jnp.zeros((ROW_SUB, LANES), F32)
                for k in range(TOP_K):
                    lo, hi = _unpack_pair(_slab_at(buf, slot_s[k * MOE_CHUNK + t])[...])
                    wk = w_s[k * MOE_CHUNK + t]
                    acc_lo = acc_lo + wk * lo
                    acc_hi = acc_hi + wk * hi
                _slab(clo, tl)[...] = acc_lo
                _slab(chi, tl)[...] = acc_hi
            return carry

        lax.fori_loop(0, COMBINE_BLOCK // 4, gather, 0)
        o_ref[:, :ROW_WORDS] = _load_rows_dense(clo, COMBINE_BLOCK)
        o_ref[:, ROW_WORDS:] = _load_rows_dense(chi, COMBINE_BLOCK)


def _moe_sparse(offs, cnts, slots, wts, hp, w_gu, w_d):
    nch = offs.shape[0]
    T = nch * MOE_CHUNK
    blocks = MOE_CHUNK // COMBINE_BLOCK
    expert = lambda c, s, *_: (jnp.minimum(s, EXPERT_STEPS - 1), 0, 0)
    grid_spec = pltpu.PrefetchScalarGridSpec(
        num_scalar_prefetch=2,
        grid=(nch, EXPERT_STEPS + blocks),
        in_specs=[pl.BlockSpec(memory_space=pl.ANY),
                  pl.BlockSpec(memory_space=pl.ANY),
                  pl.BlockSpec((MOE_CHUNK * ROW_SUB, LANES), lambda c, s, *_: (c, 0),
                               pipeline_mode=pl.Buffered(1)),
                  pl.BlockSpec((EXPERTS_PER_STEP, D_MODEL, 2 * EXPERT_FF), expert),
                  pl.BlockSpec((EXPERTS_PER_STEP, EXPERT_FF, D_MODEL), expert)],
        out_specs=pl.BlockSpec(
            (COMBINE_BLOCK, D_MODEL),
            lambda c, s, *_: (c * blocks + jnp.maximum(s - EXPERT_STEPS, 0), 0)),
        scratch_shapes=[pltpu.VMEM((SLOT_ROWS * ROW_SUB, LANES), jnp.uint32),
                        pltpu.VMEM((COMBINE_BLOCK * ROW_SUB, LANES), F32),
                        pltpu.VMEM((COMBINE_BLOCK * ROW_SUB, LANES), F32),
                        pltpu.SMEM((TOP_K * MOE_CHUNK,), jnp.int32),
                        pltpu.SMEM((TOP_K * MOE_CHUNK,), F32),
                        pltpu.SemaphoreType.DMA((2,))])
    return pl.pallas_call(
        _moe_kernel,
        grid_spec=grid_spec,
        out_shape=jax.ShapeDtypeStruct((T, D_MODEL), F32),
        compiler_params=pltpu.CompilerParams(dimension_semantics=("arbitrary", "arbitrary"),
                                             vmem_limit_bytes=VMEM_LIMIT),
        name="moe_experts",
    )(offs, cnts, slots, wts, hp, w_gu, w_d)


def _ple_kernel(x1_ref, r_ref, p_ref, g_ref, wg_ref, b_ref, wp_ref, o_ref):
    x2 = x1_ref[...] + r_ref[...]
    hn = (_rms(x2, D_MODEL) * g_ref[...]).astype(BF16)
    gate = jax.nn.sigmoid(_dot(hn, wg_ref[...]) + b_ref[...])
    o_ref[...] = x2 + gate * _dot(p_ref[...].astype(BF16), wp_ref[...])


def _ple(x1s, routed, p2d, g_ple, wg, b_ple, wp, tm):
    T = x1s.shape[0]
    row = lambda n: pl.BlockSpec((tm, n), lambda i: (i, 0))
    consts = [g_ple, wg, b_ple, wp]
    return pl.pallas_call(
        _ple_kernel,
        grid=(T // tm,),
        in_specs=[row(D_MODEL), row(D_MODEL), row(PLE_DIM)] + [_full(c.shape) for c in consts],
        out_specs=row(D_MODEL),
        out_shape=jax.ShapeDtypeStruct((T, D_MODEL), F32),
        compiler_params=pltpu.CompilerParams(dimension_semantics=("arbitrary",),
                                             vmem_limit_bytes=VMEM_LIMIT),
        name="ple",
    )(x1s, routed, p2d, *consts)


def _lane_map(*runs):
    src = np.full((LANES,), -1)
    for lane, dim, n in runs:
        src[lane:lane + n] = np.arange(dim, dim + n)
    return src


_MLA_HALF = MLA_ROPE // 2
_SWA_HALF = SWA_HD // 2
MLA_LANES = _lane_map((0, MLA_NOPE, _MLA_HALF), (_MLA_HALF, 0, LANES // 2 - _MLA_HALF),
                      (LANES // 2, MLA_NOPE + _MLA_HALF, _MLA_HALF),
                      (LANES // 2 + _MLA_HALF, LANES // 2 - _MLA_HALF, MLA_NOPE - LANES // 2 + _MLA_HALF))
MLA_NOPE_LANES = np.where(MLA_LANES < MLA_NOPE, MLA_LANES, -1)
MLA_ROPE_LANES = np.where(MLA_LANES >= MLA_NOPE, MLA_LANES - MLA_NOPE, -1)
SWA_LANES = _lane_map((LANES // 2 - _SWA_HALF, 0, _SWA_HALF), (LANES - _SWA_HALF, _SWA_HALF, _SWA_HALF))


def _spread(w, heads, lane_src):
    k = w.shape[0]
    dim = w.shape[1] // heads
    w = jnp.pad(w.reshape(k, heads, dim), ((0, 0), (0, 0), (0, 1)))
    return w[:, :, np.where(lane_src < 0, dim, lane_src)].reshape(k, heads * LANES)


def _rope_table():
    def inv_freq(dim):
        return 1.0 / (ROPE_THETA ** (jnp.arange(0, dim, 2, dtype=F32) / dim))

    def selector(lane_src, half):
        sel = np.where(lane_src < 0, 0.0, np.where(lane_src < half, -1.0, 1.0))
        return jnp.asarray(sel, F32)

    sel_m = selector(MLA_ROPE_LANES, _MLA_HALF)
    sel_s = selector(SWA_LANES, _SWA_HALF)
    freq_m = _spread(jnp.tile(inv_freq(MLA_ROPE), 2)[None], 1, MLA_ROPE_LANES)[0]
    freq_s = _spread(jnp.tile(inv_freq(SWA_HD), 2)[None], 1, SWA_LANES)[0]
    zero = jnp.zeros((LANES,), F32)
    rows = [freq_m + freq_s, sel_m, jnp.abs(sel_m), sel_s, jnp.abs(sel_s), zero, zero, zero]
    return jnp.stack(rows)


def _layer(x2d, p2d, pos2d, B, S, g_mix, w_in, b_gate, g_cq, w_uq, g_ckv, w_ukv, g_qn_mla, g_kn_mla,
           g_qn_swa, g_kn_swa, sink, w_br_mla, w_br_swa, w_out, g_moe, w_router, router_bias,
           w_exp_gu, w_exp_down, w_sh_gu, w_sh_down, g_ple, w_ple_gate, b_ple, w_ple_proj):
    w_kr = _spread(w_in[:, OFF_CKV:OFF_KR], 1, MLA_ROPE_LANES)
    w_vs = w_in[:, OFF_KS:OFF_VS].reshape(D_MODEL, SWA_KV_HEADS, 1, SWA_HD)
    w_vs = jnp.broadcast_to(w_vs, (D_MODEL, SWA_KV_HEADS, 2, SWA_HD)).reshape(D_MODEL, -1)
    w_all = jnp.concatenate([
        w_in[:, :OFF_CKV], w_kr,
        _spread(w_in[:, OFF_KR:OFF_QS], SWA_HEADS, SWA_LANES),
        _spread(w_in[:, OFF_QS:OFF_KS], SWA_KV_HEADS, SWA_LANES),
        w_vs, w_in[:, OFF_VS:]], axis=1).astype(BF16)
    assert w_all.shape[1] == C_END
    w_uq_p = _spread(w_uq, MLA_HEADS, MLA_LANES).astype(BF16)
    w_ukv3 = w_ukv.reshape(MLA_KV_RANK, MLA_HEADS, MLA_NOPE + MLA_V)
    w_k = _spread(w_ukv3[:, :, :MLA_NOPE].reshape(MLA_KV_RANK, -1), MLA_HEADS, MLA_NOPE_LANES).astype(BF16)
    w_v = w_ukv3[:, :, MLA_NOPE:].reshape(MLA_KV_RANK, -1).astype(BF16)

    qm, km, vm, qw, kw, vw, gates = _pre_attention(
        x2d, pos2d, w_all, w_uq_p, w_k, w_v, g_mix[None], g_cq[None], g_ckv[None],
        _spread(g_qn_mla[None], 1, MLA_LANES), _spread(g_kn_mla[None], 1, MLA_LANES),
        _spread(g_qn_swa[None], 1, SWA_LANES), _spread(g_kn_swa[None], 1, SWA_LANES),
        b_gate[None], _rope_table(), tm=PRE_TILE)

    om = _mla_attention(qm, km, vm, B, S, tq=MLA_Q_TILE).reshape(B * S, -1)
    ow = _swa_attention(qw, kw, vw, sink, B, S, tq=SWA_Q_TILE).reshape(B * S, -1)

    x1s, hp, comb_t = _post_attention(
        om, ow, gates, x2d, w_br_mla.astype(BF16), w_br_swa.astype(BF16), w_out.astype(BF16),
        g_moe[None], w_router.T, router_bias[:, None], w_sh_gu.astype(BF16),
        w_sh_down.astype(BF16), tm=ROW_TILE)

    slots, wts, offs, cnts = _moe_plan(comb_t)
    routed = _moe_sparse(offs[:, :, 0], cnts[:, :, 0], slots.reshape(slots.shape[0], -1),
                         wts.reshape(wts.shape[0], -1), hp, w_exp_gu.astype(BF16),
                         w_exp_down.astype(BF16))

    return _ple(x1s, routed, p2d, g_ple[None], w_ple_gate.astype(BF16), b_ple[None],
                w_ple_proj.astype(BF16), tm=ROW_TILE)


def kernel(x, p, positions, g_mix, w_in, b_gate, g_cq, w_uq, g_ckv, w_ukv, g_qn_mla, g_kn_mla, g_qn_swa, g_kn_swa, sink, w_br_mla, w_br_swa, w_out, g_moe, w_router, router_bias, w_exp_gu, w_exp_down, w_sh_gu, w_sh_down, g_ple, w_ple_gate, b_ple, w_ple_proj):
    B, S, D = x.shape
    x2d = x.reshape(B * S, D)
    pos2d = positions.reshape(B * S, 1)
    for i in range(p.shape[0]):
        x2d = _layer(x2d, p[i].reshape(B * S, -1), pos2d, B, S, g_mix[i], w_in[i], b_gate[i],
                     g_cq[i], w_uq[i], g_ckv[i], w_ukv[i], g_qn_mla[i], g_kn_mla[i], g_qn_swa[i],
                     g_kn_swa[i], sink[i], w_br_mla[i], w_br_swa[i], w_out[i], g_moe[i],
                     w_router[i], router_bias[i], w_exp_gu[i], w_exp_down[i], w_sh_gu[i],
                     w_sh_down[i], g_ple[i], w_ple_gate[i], b_ple[i], w_ple_proj[i])
    return x2d.reshape(B, S, D)
```

```python
import functools

import jax
import jax.numpy as jnp
import numpy as np
from jax import lax
from jax.experimental import pallas as pl
from jax.experimental.pallas import tpu as pltpu

D_MODEL = 1024
PLE_DIM = 256
ROPE_THETA = 10000.0
EPS = 1e-6
NEG_INF = -1e30

MLA_HEADS = 8
MLA_Q_RANK = 384
MLA_KV_RANK = 256
MLA_NOPE = 64
MLA_ROPE = 32
MLA_QK = MLA_NOPE + MLA_ROPE
MLA_V = 64

SWA_HEADS = 8
SWA_KV_HEADS = 2
SWA_GROUP = SWA_HEADS // SWA_KV_HEADS
SWA_HD = 64
WINDOW = 128

OFF_CQ = MLA_Q_RANK
OFF_CKV = OFF_CQ + MLA_KV_RANK
OFF_KR = OFF_CKV + MLA_ROPE
OFF_QS = OFF_KR + SWA_HEADS * SWA_HD
OFF_KS = OFF_QS + SWA_KV_HEADS * SWA_HD
OFF_VS = OFF_KS + SWA_KV_HEADS * SWA_HD
OFF_GA = OFF_VS + D_MODEL

N_EXPERTS = 64
TOP_K = 8
N_GROUPS = 8
TOPK_GROUPS = 4
EXPERTS_PER_GROUP = N_EXPERTS // N_GROUPS
EXPERT_FF = 256
SHARED_FF = 256
ROUTED_SCALE = 2.5

LANES = 128
ROW_WORDS = D_MODEL // 2
ROW_SUB = ROW_WORDS // LANES
MOE_CHUNK = 2048
MOE_TILE = 128
MOE_TILE_BIG = 320
SLOT_ALIGN = 16
COMBINE_BLOCK = 256
EXPERTS_PER_STEP = 4
EXPERT_STEPS = N_EXPERTS // EXPERTS_PER_STEP
PRE_TILE = 256
ROW_TILE = 512
MLA_Q_TILE = 512
MLA_KV_TILE = 256
LOG2E = 1.4426950408889634
SOFTMAX_UNDERFLOW_GUARD = 1e-30
SWA_Q_TILE = 512
SWA_BLOCK = 128
VMEM_LIMIT = 56 * 1024 * 1024

BF16 = jnp.bfloat16
F32 = jnp.float32

C_CQ = 0
C_CKV = C_CQ + MLA_Q_RANK
C_KR = C_CKV + MLA_KV_RANK
C_QS = C_KR + LANES
C_KS = C_QS + SWA_HEADS * LANES
C_VS = C_KS + SWA_KV_HEADS * LANES
C_GA = C_VS + SWA_KV_HEADS * LANES
C_END = C_GA + 2 * D_MODEL


def _full(shape):
    nd = len(shape)
    return pl.BlockSpec(shape, lambda *_: (0,) * nd)


def _dot(a, b):
    return jnp.dot(a, b, preferred_element_type=F32)


def _dot_nt(a, b, precision=None):
    return lax.dot_general(a, b, (((1,), (1,)), ((), ())), precision=precision,
                           preferred_element_type=F32)


def _rms(v, n):
    return v * lax.rsqrt(jnp.sum(v * v, axis=-1, keepdims=True) * (1.0 / n) + EPS)


def _rope(v, cos, sin):
    return v * cos + pltpu.roll(v, LANES // 2, 1) * sin


def _pre_kernel(x_ref, pos_ref, w_all_ref, w_uq_ref, w_k_ref, w_v_ref, g_mix_ref, g_cq_ref,
                g_ckv_ref, gq_m_ref, gk_m_ref, gq_s_ref, gk_s_ref, b_gate_ref, rope_ref,
                qm_ref, km_ref, vm_ref, qw_ref, kw_ref, vw_ref, gate_ref):
    x = x_ref[...]
    h = (_rms(x, D_MODEL) * g_mix_ref[...]).astype(BF16)

    def proj(lo, hi):
        return _dot(h, w_all_ref[:, lo:hi])

    z_lat = proj(C_CQ, C_QS)

    pos = pos_ref[...].astype(F32)
    rope = rope_ref[...]
    ang = pos * rope[0:1, :]
    cos_m1 = jnp.cos(ang) - 1.0
    sin = jnp.sin(ang)
    cos_m, sin_m = 1.0 + cos_m1 * rope[2:3, :], sin * rope[1:2, :]
    cos_s, sin_s = 1.0 + cos_m1 * rope[4:5, :], sin * rope[3:4, :]

    cqn = (_rms(z_lat[:, C_CQ:C_CKV], MLA_Q_RANK) * g_cq_ref[...]).astype(BF16)
    ckvn = (_rms(z_lat[:, C_CKV:C_KR], MLA_KV_RANK) * g_ckv_ref[...]).astype(BF16)
    q = _dot(cqn, w_uq_ref[...])
    z_qs = proj(C_QS, C_KS)
    gq_m = gq_m_ref[...]
    for hd in range(MLA_HEADS):
        qh = _rms(q[:, hd * LANES:(hd + 1) * LANES], MLA_QK) * gq_m
        qh = _rope(qh, cos_m, sin_m) * (MLA_QK ** -0.5 * LOG2E)
        qm_ref[:, hd * LANES:(hd + 1) * LANES] = qh.astype(BF16)

    kn = _dot(ckvn, w_k_ref[...])
    vm_ref[...] = _dot(ckvn, w_v_ref[...]).astype(BF16)
    z_kv = proj(C_KS, C_GA)

    gq_s = gq_s_ref[...]
    for hd in range(SWA_HEADS):
        qh = _rms(z_qs[:, hd * LANES:(hd + 1) * LANES], SWA_HD) * gq_s
        qh = _rope(qh, cos_s, sin_s) * (SWA_HD ** -0.5 * LOG2E)
        qw_ref[:, hd * LANES:(hd + 1) * LANES] = qh.astype(BF16)

    half = (C_END - C_GA) // 2
    z_ga = proj(C_GA, C_GA + half)

    gk_m = gk_m_ref[...]
    kr = z_lat[:, C_KR:C_QS]
    ss_kr = jnp.sum(kr * kr, axis=-1, keepdims=True)
    kr_rot = _rope(kr * gk_m, cos_m, sin_m)
    for hd in range(MLA_HEADS):
        kh = kn[:, hd * LANES:(hd + 1) * LANES]
        ss = jnp.sum(kh * kh, axis=-1, keepdims=True) + ss_kr
        sc = lax.rsqrt(ss * (1.0 / MLA_QK) + EPS)
        km_ref[:, hd * LANES:(hd + 1) * LANES] = ((kh * gk_m + kr_rot) * sc).astype(BF16)

    z_gb = proj(C_GA + half, C_END)

    gk_s = gk_s_ref[...]
    for hd in range(SWA_KV_HEADS):
        kh = _rms(z_kv[:, hd * LANES:(hd + 1) * LANES], SWA_HD) * gk_s
        kh = _rope(kh, cos_s, sin_s)
        kw_ref[:, hd * LANES:(hd + 1) * LANES] = kh.astype(BF16)
    vw_ref[...] = z_kv[:, C_VS - C_KS:].astype(BF16)

    gate_ref[:, :half] = jax.nn.sigmoid(z_ga + b_gate_ref[:, :half]).astype(BF16)
    gate_ref[:, half:] = jax.nn.sigmoid(z_gb + b_gate_ref[:, half:]).astype(BF16)


def _pre_attention(x2d, pos2d, w_all, w_uq, w_k, w_v, g_mix, g_cq, g_ckv, gq_m, gk_m, gq_s,
                   gk_s, b_gate, rope_tab, tm):
    T = x2d.shape[0]
    row = lambda n: pl.BlockSpec((tm, n), lambda i: (i, 0))
    outs = [(MLA_HEADS * LANES, BF16), (MLA_HEADS * LANES, BF16), (MLA_HEADS * MLA_V, BF16),
            (SWA_HEADS * LANES, BF16), (SWA_KV_HEADS * LANES, BF16),
            (SWA_KV_HEADS * LANES, BF16), (2 * D_MODEL, BF16)]
    consts = [w_all, w_uq, w_k, w_v, g_mix, g_cq, g_ckv, gq_m, gk_m, gq_s, gk_s, b_gate, rope_tab]
    return pl.pallas_call(
        _pre_kernel,
        grid=(T // tm,),
        in_specs=[row(D_MODEL), row(1)] + [_full(c.shape) for c in consts],
        out_specs=[row(n) for n, _ in outs],
        out_shape=[jax.ShapeDtypeStruct((T, n), dt) for n, dt in outs],
        compiler_params=pltpu.CompilerParams(dimension_semantics=("arbitrary",),
                                             vmem_limit_bytes=VMEM_LIMIT),
        name="pre_attention",
    )(x2d, pos2d, *consts)


def _half_masks(dtype):
    lane = lax.broadcasted_iota(jnp.int32, (1, LANES), 1)
    lo = (lane < LANES // 2).astype(dtype)
    return lo, 1 - lo


def _mla_exact(q_ref, k_ref, v_ref, o_ref):
    v = v_ref[...]
    masks = _half_masks(v.dtype)
    acc = None
    for hh in range(2):
        q = q_ref[:, hh * LANES:(hh + 1) * LANES]
        k = k_ref[:, hh * LANES:(hh + 1) * LANES]
        s = _dot_nt(q, k)
        m = jnp.max(s, axis=-1, keepdims=True)
        p = jnp.exp2(s - m)
        l = jnp.sum(p, axis=-1, keepdims=True)
        o = _dot(p.astype(BF16), v * masks[hh]) / l
        acc = o if acc is None else acc + o
    o_ref[...] = acc.astype(o_ref.dtype)


def _mla_kernel(q_ref, k_ref, v_ref, o_ref, kmax_ref):
    S = k_ref.shape[0]
    tq = q_ref.shape[0]

    @pl.when(pl.program_id(2) == 0)
    def _():
        for hh in range(2):
            k = k_ref[:, hh * LANES:(hh + 1) * LANES].astype(F32)
            kn2 = jnp.max(jnp.sum(k * k, axis=-1, keepdims=True), axis=0, keepdims=True)
            kmax_ref[hh] = jnp.broadcast_to(jnp.sqrt(kn2), kmax_ref.shape[1:])

    masks = _half_masks(BF16)
    acc = None
    lmin = None
    for hh in range(2):
        q = q_ref[:, hh * LANES:(hh + 1) * LANES]
        qf = q.astype(F32)
        bound = jnp.sqrt(jnp.sum(qf * qf, axis=-1, keepdims=True)) * kmax_ref[hh][0:1, 0:1]
        v = v_ref[...] * masks[hh]
        o = jnp.zeros((tq, LANES), F32)
        lsum = jnp.zeros((tq, LANES), F32)
        for j in range(S // MLA_KV_TILE):
            rows = slice(j * MLA_KV_TILE, (j + 1) * MLA_KV_TILE)
            p = jnp.exp2(_dot_nt(q, k_ref[rows, hh * LANES:(hh + 1) * LANES]) - bound)
            for t in range(MLA_KV_TILE // LANES):
                lsum = lsum + p[:, t * LANES:(t + 1) * LANES]
            o = o + _dot(p.astype(BF16), v[rows, :])
        l = jnp.sum(lsum, axis=-1, keepdims=True)
        o = o / l
        acc = o if acc is None else acc + o
        lm = jnp.min(l)
        lmin = lm if lmin is None else jnp.minimum(lmin, lm)
    o_ref[...] = acc.astype(o_ref.dtype)

    @pl.when(jnp.logical_not(lmin > SOFTMAX_UNDERFLOW_GUARD))
    def _():
        _mla_exact(q_ref, k_ref, v_ref, o_ref)


def _mla_attention(qm, km, vm, B, S, tq):
    pairs = MLA_HEADS // 2
    q3 = qm.reshape(B, S, MLA_HEADS * LANES)
    k3 = km.reshape(B, S, MLA_HEADS * LANES)
    v3 = vm.reshape(B, S, MLA_HEADS * MLA_V)
    return pl.pallas_call(
        _mla_kernel,
        grid=(B, pairs, S // tq),
        in_specs=[pl.BlockSpec((None, tq, 2 * LANES), lambda b, p, i: (b, i, p)),
                  pl.BlockSpec((None, S, 2 * LANES), lambda b, p, i: (b, 0, p)),
                  pl.BlockSpec((None, S, LANES), lambda b, p, i: (b, 0, p))],
        out_specs=pl.BlockSpec((None, tq, LANES), lambda b, p, i: (b, i, p)),
        out_shape=jax.ShapeDtypeStruct((B, S, MLA_HEADS * MLA_V), BF16),
        scratch_shapes=[pltpu.VMEM((2, 8, LANES), F32)],
        compiler_params=pltpu.CompilerParams(
            dimension_semantics=("arbitrary", "arbitrary", "arbitrary"),
            vmem_limit_bytes=VMEM_LIMIT),
        name="mla_attention",
    )(q3, k3, v3)


def _swa_kernel(sink_ref, q_ref, k_ref, v_ref, o_ref, *, tq, S):
    hk = pl.program_id(1)
    i = pl.program_id(2)
    tk = SWA_BLOCK + 2 * WINDOW
    row = lax.broadcasted_iota(jnp.int32, (SWA_GROUP * SWA_BLOCK, 1), 0)
    qoff = row & (SWA_BLOCK - 1)
    head = row // SWA_BLOCK
    sk = jnp.zeros((SWA_GROUP * SWA_BLOCK, 1), F32)
    for g in range(SWA_GROUP):
        sk = jnp.where(head == g, sink_ref[SWA_GROUP * hk + g] * LOG2E, sk)
    low_half = lax.broadcasted_iota(jnp.int32, (1, LANES), 1) < LANES // 2
    for sub in range(tq // SWA_BLOCK):
        rows = slice(sub * SWA_BLOCK, (sub + 1) * SWA_BLOCK)
        q0 = i * tq + sub * SWA_BLOCK
        kstart = pl.multiple_of(jnp.clip(q0 - WINDOW, 0, S - tk), WINDOW)
        k = k_ref[pl.ds(kstart, tk), :]
        v = v_ref[pl.ds(kstart, tk), :]
        q = jnp.concatenate([q_ref[rows, g * LANES:(g + 1) * LANES] for g in range(SWA_GROUP)], axis=0)
        kpos = kstart + lax.broadcasted_iota(jnp.int32, (1, tk), 1)
        valid = jnp.abs(kpos - (q0 + qoff[:SWA_BLOCK])) <= WINDOW
        s = _dot_nt(q, k).reshape(SWA_GROUP, SWA_BLOCK, tk)
        s = jnp.where(valid[None], s, NEG_INF).reshape(SWA_GROUP * SWA_BLOCK, tk)
        m = jnp.maximum(jnp.max(s, axis=-1, keepdims=True), sk)
        e = jnp.exp2(s - m)
        denom = jnp.sum(e, axis=-1, keepdims=True) + jnp.exp2(sk - m)
        o = _dot(e.astype(BF16), v) / denom
        for j in range(SWA_GROUP // 2):
            even = o[(2 * j) * SWA_BLOCK:(2 * j + 1) * SWA_BLOCK, :]
            odd = o[(2 * j + 1) * SWA_BLOCK:(2 * j + 2) * SWA_BLOCK, :]
            o_ref[rows, j * LANES:(j + 1) * LANES] = jnp.where(low_half, even, odd).astype(o_ref.dtype)


def _swa_attention(qw, kw, vw, sink, B, S, tq):
    q3 = qw.reshape(B, S, SWA_HEADS * LANES)
    k3 = kw.reshape(B, S, SWA_KV_HEADS * LANES)
    v3 = vw.reshape(B, S, SWA_KV_HEADS * LANES)
    return pl.pallas_call(
        functools.partial(_swa_kernel, tq=tq, S=S),
        grid=(B, SWA_KV_HEADS, S // tq),
        in_specs=[pl.BlockSpec(memory_space=pltpu.SMEM),
                  pl.BlockSpec((None, tq, SWA_GROUP * LANES), lambda b, h, i: (b, i, h)),
                  pl.BlockSpec((None, S, LANES), lambda b, h, i: (b, 0, h)),
                  pl.BlockSpec((None, S, LANES), lambda b, h, i: (b, 0, h))],
        out_specs=pl.BlockSpec((None, tq, SWA_GROUP * SWA_HD), lambda b, h, i: (b, i, h)),
        out_shape=jax.ShapeDtypeStruct((B, S, SWA_HEADS * SWA_HD), BF16),
        compiler_params=pltpu.CompilerParams(
            dimension_semantics=("arbitrary", "arbitrary", "arbitrary"),
            vmem_limit_bytes=VMEM_LIMIT),
        name="swa_attention",
    )(sink, q3, k3, v3)


def _beats(vj, vi, j_first):
    return (vj >= vi) if j_first else (vj > vi)


def _route(scores, sel):
    G, P = N_GROUPS, EXPERTS_PER_GROUP
    groups = [sel[g * P:(g + 1) * P, :] for g in range(G)]
    row = lax.broadcasted_iota(jnp.int32, (P, 1), 0)
    gscore = []
    for vg in groups:
        m1 = jnp.max(vg, axis=0, keepdims=True)
        first = jnp.min(jnp.where(vg == m1, row, P), axis=0, keepdims=True)
        m2 = jnp.max(jnp.where(row == first, -jnp.inf, vg), axis=0, keepdims=True)
        gscore.append(m1 + m2)
    masked = []
    for g in range(G):
        rank = jnp.zeros_like(gscore[g], dtype=jnp.int32)
        for g2 in range(G):
            if g2 != g:
                rank = rank + _beats(gscore[g2], gscore[g], g2 < g).astype(jnp.int32)
        masked.append(jnp.where(rank < TOPK_GROUPS, groups[g], NEG_INF))
    index = [row + g * P for g in range(G)]
    chosen = [None] * G
    for _ in range(TOP_K):
        best = masked[0]
        for g in range(1, G):
            best = jnp.maximum(best, masked[g])
        best = jnp.max(best, axis=0, keepdims=True)
        first = jnp.where(masked[0] == best, index[0], N_EXPERTS)
        for g in range(1, G):
            first = jnp.minimum(first, jnp.where(masked[g] == best, index[g], N_EXPERTS))
        first = jnp.min(first, axis=0, keepdims=True)
        for g in range(G):
            hit = index[g] == first
            chosen[g] = hit if chosen[g] is None else (chosen[g] | hit)
            masked[g] = jnp.where(hit, -jnp.inf, masked[g])
    picked = [jnp.where(chosen[g], scores[g * P:(g + 1) * P, :], 0.0) for g in range(G)]
    total = picked[0]
    for g in range(1, G):
        total = total + picked[g]
    denom = jnp.sum(total, axis=0, keepdims=True)
    return [pk / denom * ROUTED_SCALE for pk in picked]


def _pack_pair(lo, hi):
    return pltpu.pack_elementwise([lo, hi], packed_dtype=BF16)


def _unpack_pair(word):
    lo = pltpu.unpack_elementwise(word, index=0, packed_dtype=BF16, unpacked_dtype=F32)
    hi = pltpu.unpack_elementwise(word, index=1, packed_dtype=BF16, unpacked_dtype=F32)
    return lo, hi


def _store_rows_dense(ref, words):
    for j in range(ROW_SUB):
        ref[pl.ds(j, words.shape[0], stride=ROW_SUB), :] = words[:, j * LANES:(j + 1) * LANES]


def _load_rows_dense(ref, rows):
    sub = ROW_SUB
    return jnp.concatenate([ref[pl.ds(j, rows, stride=sub), :] for j in range(sub)], axis=1)


def _post_kernel(om_ref, ow_ref, gate_ref, x_ref, wbm_ref, wbw_ref, wout_ref, g_moe_ref, wr_ref,
                 rb_ref, wsgu_ref, wsd_ref, x1_ref, hp_ref, comb_ref):
    am = _dot(om_ref[...], wbm_ref[...])
    aw = _dot(ow_ref[...], wbw_ref[...])
    gates = gate_ref[...].astype(F32)
    merged = gates[:, :D_MODEL] * am + gates[:, D_MODEL:] * aw
    x1 = x_ref[...] + _dot(merged.astype(BF16), wout_ref[...])

    h2 = _rms(x1, D_MODEL) * g_moe_ref[...]
    h2b = h2.astype(BF16)
    _store_rows_dense(hp_ref, _pack_pair(h2[:, :ROW_WORDS], h2[:, ROW_WORDS:]))

    wr = wr_ref[...]
    w_hi = wr.astype(BF16)
    w_lo = (wr - w_hi.astype(F32)).astype(BF16)
    h_lo = (h2 - h2b.astype(F32)).astype(BF16)
    by_hi = _dot_nt(jnp.concatenate([w_hi, w_lo], axis=0), h2b)
    logits = by_hi[:N_EXPERTS] + by_hi[N_EXPERTS:] + _dot_nt(w_hi, h_lo)

    sgu = _dot(h2b, wsgu_ref[...])
    sh = jax.nn.silu(sgu[:, :SHARED_FF]) * sgu[:, SHARED_FF:]
    x1_ref[...] = x1 + _dot(sh.astype(BF16), wsd_ref[...])

    scores = jax.nn.sigmoid(logits)
    comb = _route(scores, scores + rb_ref[...])
    for g in range(N_GROUPS):
        comb_ref[g * EXPERTS_PER_GROUP:(g + 1) * EXPERTS_PER_GROUP, :] = comb[g]


def _post_attention(om, ow, gates, x2d, wbm, wbw, wout, g_moe, wr_t, rbias, wsgu, wsd, tm):
    T = x2d.shape[0]
    row = lambda n: pl.BlockSpec((tm, n), lambda i: (i, 0))
    consts = [wbm, wbw, wout, g_moe, wr_t, rbias, wsgu, wsd]
    return pl.pallas_call(
        _post_kernel,
        grid=(T // tm,),
        in_specs=[row(om.shape[1]), row(ow.shape[1]), row(2 * D_MODEL), row(D_MODEL)]
        + [_full(c.shape) for c in consts],
        out_specs=[row(D_MODEL), pl.BlockSpec((tm * ROW_SUB, LANES), lambda i: (i, 0)),
                   pl.BlockSpec((N_EXPERTS, tm), lambda i: (0, i))],
        out_shape=[jax.ShapeDtypeStruct((T, D_MODEL), F32),
                   jax.ShapeDtypeStruct((T * ROW_SUB, LANES), jnp.uint32),
                   jax.ShapeDtypeStruct((N_EXPERTS, T), F32)],
        compiler_params=pltpu.CompilerParams(dimension_semantics=("arbitrary",),
                                             vmem_limit_bytes=VMEM_LIMIT),
        name="post_attention",
    )(om, ow, gates, x2d, *consts)


SLOT_ROWS = -(-(MOE_CHUNK * TOP_K + N_EXPERTS * (SLOT_ALIGN - 1) + MOE_TILE_BIG) // MOE_TILE) * MOE_TILE
DUMMY_SLOT = SLOT_ROWS - 1
PLAN_BLOCK = 256


def _plan_kernel(comb_ref, slot_ref, w_ref, off_ref, cnt_ref):
    comb = comb_ref[...]
    sel = comb > 0.0
    m = sel.astype(F32)
    mb = m.astype(BF16)
    r_i = lax.broadcasted_iota(jnp.int32, (PLAN_BLOCK, PLAN_BLOCK), 0)
    c_i = lax.broadcasted_iota(jnp.int32, (PLAN_BLOCK, PLAN_BLOCK), 1)
    before = (r_i < c_i).astype(BF16)
    carry = jnp.zeros((N_EXPERTS, 1), F32)
    ranks = []
    for b in range(MOE_CHUNK // PLAN_BLOCK):
        blk = slice(b * PLAN_BLOCK, (b + 1) * PLAN_BLOCK)
        ranks.append(_dot(mb[:, blk], before) + carry)
        carry = carry + jnp.sum(m[:, blk], axis=1, keepdims=True)
    rank = jnp.concatenate(ranks, axis=1)
    cnt = carry
    cnt_pad = jnp.floor((cnt + (SLOT_ALIGN - 1)) * (1.0 / SLOT_ALIGN)) * SLOT_ALIGN
    e_r = lax.broadcasted_iota(jnp.int32, (N_EXPERTS, N_EXPERTS), 0)
    e_c = lax.broadcasted_iota(jnp.int32, (N_EXPERTS, N_EXPERTS), 1)
    below = (e_c < e_r).astype(F32)
    off = jnp.dot(below, jnp.broadcast_to(cnt_pad, (N_EXPERTS, LANES)),
                  precision=lax.Precision.HIGHEST, preferred_element_type=F32)
    slot = off[:, :1] + rank
    kidx = _dot(below.astype(BF16), mb)
    row = lax.broadcasted_iota(jnp.int32, (TOP_K, 1), 0)
    slot_acc = jnp.zeros((TOP_K, MOE_CHUNK), F32)
    w_acc = jnp.zeros((TOP_K, MOE_CHUNK), F32)
    for k in range(TOP_K):
        pick = jnp.where(sel & (kidx == k), 1.0, 0.0)
        found = jnp.sum(pick, axis=0, keepdims=True) > 0.0
        s_k = jnp.where(found, jnp.sum(pick * slot, axis=0, keepdims=True), float(DUMMY_SLOT))
        w_k = jnp.sum(pick * comb, axis=0, keepdims=True)
        slot_acc = jnp.where(row == k, s_k, slot_acc)
        w_acc = jnp.where(row == k, w_k, w_acc)
    slot_ref[...] = slot_acc.astype(jnp.int32) * ROW_SUB
    w_ref[...] = w_acc
    off_ref[...] = off.astype(jnp.int32)
    cnt_ref[...] = jnp.broadcast_to(cnt, (N_EXPERTS, LANES)).astype(jnp.int32)


def _moe_plan(comb_t):
    T = comb_t.shape[1]
    nch = T // MOE_CHUNK
    per_pair = pl.BlockSpec((None, TOP_K, MOE_CHUNK), lambda c: (c, 0, 0))
    per_expert = pl.BlockSpec((None, N_EXPERTS, LANES), lambda c: (c, 0, 0))
    return pl.pallas_call(
        _plan_kernel,
        grid=(nch,),
        in_specs=[pl.BlockSpec((N_EXPERTS, MOE_CHUNK), lambda c: (0, c))],
        out_specs=[per_pair, per_pair, per_expert, per_expert],
        out_shape=[jax.ShapeDtypeStruct((nch, TOP_K, MOE_CHUNK), jnp.int32),
                   jax.ShapeDtypeStruct((nch, TOP_K, MOE_CHUNK), F32),
                   jax.ShapeDtypeStruct((nch, N_EXPERTS, LANES), jnp.int32),
                   jax.ShapeDtypeStruct((nch, N_EXPERTS, LANES), jnp.int32)],
        compiler_params=pltpu.CompilerParams(dimension_semantics=("arbitrary",),
                                             vmem_limit_bytes=VMEM_LIMIT),
        name="moe_plan",
    )(comb_t)


def _slab_at(ref, first):
    return ref.at[pl.ds(pl.multiple_of(first, ROW_SUB), ROW_SUB), :]


def _slab(ref, row):
    return _slab_at(ref, row * ROW_SUB)


def _moe_kernel(off_ref, cnt_ref, slot_hbm, w_hbm, hp_ref, wgu_ref, wd_ref, o_ref,
                buf, clo, chi, slot_s, w_s, sem):
    c = pl.program_id(0)
    s = pl.program_id(1)

    @pl.when(s == 0)
    def _dispatch():
        slot_cp = pltpu.make_async_copy(slot_hbm.at[c], slot_s, sem.at[0])
        w_cp = pltpu.make_async_copy(w_hbm.at[c], w_s, sem.at[1])
        slot_cp.start()
        w_cp.start()

        @pl.when(c == 0)
        def _():
            buf[...] = jnp.zeros_like(buf)

        slot_cp.wait()
        w_cp.wait()

        def scatter(tb, carry):
            for tt in range(8):
                t = tb * 8 + tt
                slab = _slab(hp_ref, t)[...]
                for k in range(TOP_K):
                    _slab_at(buf, slot_s[k * MOE_CHUNK + t])[...] = slab
            return carry

        lax.fori_loop(0, MOE_CHUNK // 8, scatter, 0)

    def expert(ee):
        e = s * EXPERTS_PER_STEP + ee
        n = cnt_ref[c, e]
        off = off_ref[c, e]

        def ffn(start, rows):
            view = buf.at[pl.ds(pl.multiple_of((off + start) * ROW_SUB, SLOT_ALIGN * ROW_SUB),
                                rows * ROW_SUB), :]
            x_lo, x_hi = _unpack_pair(_load_rows_dense(view, rows))
            gu = (_dot(x_lo.astype(BF16), wgu_ref[ee, :ROW_WORDS, :])
                  + _dot(x_hi.astype(BF16), wgu_ref[ee, ROW_WORDS:, :]))
            hid = jax.nn.silu(gu[:, :EXPERT_FF]) * gu[:, EXPERT_FF:]
            y = _dot(hid.astype(BF16), wd_ref[ee])
            mine = lax.broadcasted_iota(jnp.int32, (rows, 1), 0) < (n - start)
            _store_rows_dense(view, _pack_pair(jnp.where(mine, y[:, :ROW_WORDS], x_lo),
                                               jnp.where(mine, y[:, ROW_WORDS:], x_hi)))

        n_big = (n + MOE_TILE_BIG - MOE_TILE - 1) // MOE_TILE_BIG

        def big(r, carry):
            ffn(r * MOE_TILE_BIG, MOE_TILE_BIG)
            return carry

        lax.fori_loop(0, n_big, big, 0)

        @pl.when(n > n_big * MOE_TILE_BIG)
        def _():
            ffn(n_big * MOE_TILE_BIG, MOE_TILE)

    @pl.when(s < EXPERT_STEPS)
    def _experts():
        for ee in range(EXPERTS_PER_STEP):
            expert(ee)

    @pl.when(s >= EXPERT_STEPS)
    def _combine():
        t0 = (s - EXPERT_STEPS) * COMBINE_BLOCK

        def gather(i, carry):
            for tt in range(4):
                tl = i * 4 + tt
                t = t0 + tl
                acc_lo = jnp.zeros((ROW_SUB, LANES), F32)
                acc_hi = jnp.zeros((ROW_SUB, LANES), F32)
                for k in range(TOP_K):
                    lo, hi = _unpack_pair(_slab_at(buf, slot_s[k * MOE_CHUNK + t])[...])
                    wk = w_s[k * MOE_CHUNK + t]
                    acc_lo = acc_lo + wk * lo
                    acc_hi = acc_hi + wk * hi
                _slab(clo, tl)[...] = acc_lo
                _slab(chi, tl)[...] = acc_hi
            return carry

        lax.fori_loop(0, COMBINE_BLOCK // 4, gather, 0)
        o_ref[:, :ROW_WORDS] = _load_rows_dense(clo, COMBINE_BLOCK)
        o_ref[:, ROW_WORDS:] = _load_rows_dense(chi, COMBINE_BLOCK)


def _moe_sparse(offs, cnts, slots, wts, hp, w_gu, w_d):
    nch = offs.shape[0]
    T = nch * MOE_CHUNK
    blocks = MOE_CHUNK // COMBINE_BLOCK
    expert = lambda c, s, *_: (jnp.minimum(s, EXPERT_STEPS - 1), 0, 0)
    grid_spec = pltpu.PrefetchScalarGridSpec(
        num_scalar_prefetch=2,
        grid=(nch, EXPERT_STEPS + blocks),
        in_specs=[pl.BlockSpec(memory_space=pl.ANY),
                  pl.BlockSpec(memory_space=pl.ANY),
                  pl.BlockSpec((MOE_CHUNK * ROW_SUB, LANES), lambda c, s, *_: (c, 0),
                               pipeline_mode=pl.Buffered(1)),
                  pl.BlockSpec((EXPERTS_PER_STEP, D_MODEL, 2 * EXPERT_FF), expert),
                  pl.BlockSpec((EXPERTS_PER_STEP, EXPERT_FF, D_MODEL), expert)],
        out_specs=pl.BlockSpec(
            (COMBINE_BLOCK, D_MODEL),
            lambda c, s, *_: (c * blocks + jnp.maximum(s - EXPERT_STEPS, 0), 0)),
        scratch_shapes=[pltpu.VMEM((SLOT_ROWS * ROW_SUB, LANES), jnp.uint32),
                        pltpu.VMEM((COMBINE_BLOCK * ROW_SUB, LANES), F32),
                        pltpu.VMEM((COMBINE_BLOCK * ROW_SUB, LANES), F32),
                        pltpu.SMEM((TOP_K * MOE_CHUNK,), jnp.int32),
                        pltpu.SMEM((TOP_K * MOE_CHUNK,), F32),
                        pltpu.SemaphoreType.DMA((2,))])
    return pl.pallas_call(
        _moe_kernel,
        grid_spec=grid_spec,
        out_shape=jax.ShapeDtypeStruct((T, D_MODEL), F32),
        compiler_params=pltpu.CompilerParams(dimension_semantics=("arbitrary", "arbitrary"),
                                             vmem_limit_bytes=VMEM_LIMIT),
        name="moe_experts",
    )(offs, cnts, slots, wts, hp, w_gu, w_d)


def _ple_kernel(x1_ref, r_ref, p_ref, g_ref, wg_ref, b_ref, wp_ref, o_ref):
    x2 = x1_ref[...] + r_ref[...]
    hn = (_rms(x2, D_MODEL) * g_ref[...]).astype(BF16)
    gate = jax.nn.sigmoid(_dot(hn, wg_ref[...]) + b_ref[...])
    o_ref[...] = x2 + gate * _dot(p_ref[...].astype(BF16), wp_ref[...])


def _ple(x1s, routed, p2d, g_ple, wg, b_ple, wp, tm):
    T = x1s.shape[0]
    row = lambda n: pl.BlockSpec((tm, n), lambda i: (i, 0))
    consts = [g_ple, wg, b_ple, wp]
    return pl.pallas_call(
        _ple_kernel,
        grid=(T // tm,),
        in_specs=[row(D_MODEL), row(D_MODEL), row(PLE_DIM)] + [_full(c.shape) for c in consts],
        out_specs=row(D_MODEL),
        out_shape=jax.ShapeDtypeStruct((T, D_MODEL), F32),
        compiler_params=pltpu.CompilerParams(dimension_semantics=("arbitrary",),
                                             vmem_limit_bytes=VMEM_LIMIT),
        name="ple",
    )(x1s, routed, p2d, *consts)


def _lane_map(*runs):
    src = np.full((LANES,), -1)
    for lane, dim, n in runs:
        src[lane:lane + n] = np.arange(dim, dim + n)
    return src


_MLA_HALF = MLA_ROPE // 2
_SWA_HALF = SWA_HD // 2
MLA_LANES = _lane_map((0, MLA_NOPE, _MLA_HALF), (_MLA_HALF, 0, LANES // 2 - _MLA_HALF),
                      (LANES // 2, MLA_NOPE + _MLA_HALF, _MLA_HALF),
                      (LANES // 2 + _MLA_HALF, LANES // 2 - _MLA_HALF, MLA_NOPE - LANES // 2 + _MLA_HALF))
MLA_NOPE_LANES = np.where(MLA_LANES < MLA_NOPE, MLA_LANES, -1)
MLA_ROPE_LANES = np.where(MLA_LANES >= MLA_NOPE, MLA_LANES - MLA_NOPE, -1)
SWA_LANES = _lane_map((LANES // 2 - _SWA_HALF, 0, _SWA_HALF), (LANES - _SWA_HALF, _SWA_HALF, _SWA_HALF))


def _spread(w, heads, lane_src):
    k = w.shape[0]
    dim = w.shape[1] // heads
    w = jnp.pad(w.reshape(k, heads, dim), ((0, 0), (0, 0), (0, 1)))
    return w[:, :, np.where(lane_src < 0, dim, lane_src)].reshape(k, heads * LANES)


def _rope_table():
    def inv_freq(dim):
        return 1.0 / (ROPE_THETA ** (jnp.arange(0, dim, 2, dtype=F32) / dim))

    def selector(lane_src, half):
        sel = np.where(lane_src < 0, 0.0, np.where(lane_src < half, -1.0, 1.0))
        return jnp.asarray(sel, F32)

    sel_m = selector(MLA_ROPE_LANES, _MLA_HALF)
    sel_s = selector(SWA_LANES, _SWA_HALF)
    freq_m = _spread(jnp.tile(inv_freq(MLA_ROPE), 2)[None], 1, MLA_ROPE_LANES)[0]
    freq_s = _spread(jnp.tile(inv_freq(SWA_HD), 2)[None], 1, SWA_LANES)[0]
    zero = jnp.zeros((LANES,), F32)
    rows = [freq_m + freq_s, sel_m, jnp.abs(sel_m), sel_s, jnp.abs(sel_s), zero, zero, zero]
    return jnp.stack(rows)


def _layer(x2d, p2d, pos2d, B, S, g_mix, w_in, b_gate, g_cq, w_uq, g_ckv, w_ukv, g_qn_mla, g_kn_mla,
           g_qn_swa, g_kn_swa, sink, w_br_mla, w_br_swa, w_out, g_moe, w_router, router_bias,
           w_exp_gu, w_exp_down, w_sh_gu, w_sh_down, g_ple, w_ple_gate, b_ple, w_ple_proj):
    w_kr = _spread(w_in[:, OFF_CKV:OFF_KR], 1, MLA_ROPE_LANES)
    w_vs = w_in[:, OFF_KS:OFF_VS].reshape(D_MODEL, SWA_KV_HEADS, 1, SWA_HD)
    w_vs = jnp.broadcast_to(w_vs, (D_MODEL, SWA_KV_HEADS, 2, SWA_HD)).reshape(D_MODEL, -1)
    w_all = jnp.concatenate([
        w_in[:, :OFF_CKV], w_kr,
        _spread(w_in[:, OFF_KR:OFF_QS], SWA_HEADS, SWA_LANES),
        _spread(w_in[:, OFF_QS:OFF_KS], SWA_KV_HEADS, SWA_LANES),
        w_vs, w_in[:, OFF_VS:]], axis=1).astype(BF16)
    assert w_all.shape[1] == C_END
    w_uq_p = _spread(w_uq, MLA_HEADS, MLA_LANES).astype(BF16)
    w_ukv3 = w_ukv.reshape(MLA_KV_RANK, MLA_HEADS, MLA_NOPE + MLA_V)
    w_k = _spread(w_ukv3[:, :, :MLA_NOPE].reshape(MLA_KV_RANK, -1), MLA_HEADS, MLA_NOPE_LANES).astype(BF16)
    w_v = w_ukv3[:, :, MLA_NOPE:].reshape(MLA_KV_RANK, -1).astype(BF16)

    qm, km, vm, qw, kw, vw, gates = _pre_attention(
        x2d, pos2d, w_all, w_uq_p, w_k, w_v, g_mix[None], g_cq[None], g_ckv[None],
        _spread(g_qn_mla[None], 1, MLA_LANES), _spread(g_kn_mla[None], 1, MLA_LANES),
        _spread(g_qn_swa[None], 1, SWA_LANES), _spread(g_kn_swa[None], 1, SWA_LANES),
        b_gate[None], _rope_table(), tm=PRE_TILE)

    om = _mla_attention(qm, km, vm, B, S, tq=MLA_Q_TILE).reshape(B * S, -1)
    ow = _swa_attention(qw, kw, vw, sink, B, S, tq=SWA_Q_TILE).reshape(B * S, -1)

    x1s, hp, comb_t = _post_attention(
        om, ow, gates, x2d, w_br_mla.astype(BF16), w_br_swa.astype(BF16), w_out.astype(BF16),
        g_moe[None], w_router.T, router_bias[:, None], w_sh_gu.astype(BF16),
        w_sh_down.astype(BF16), tm=ROW_TILE)

    slots, wts, offs, cnts = _moe_plan(comb_t)
    routed = _moe_sparse(offs[:, :, 0], cnts[:, :, 0], slots.reshape(slots.shape[0], -1),
                         wts.reshape(wts.shape[0], -1), hp, w_exp_gu.astype(BF16),
                         w_exp_down.astype(BF16))

    return _ple(x1s, routed, p2d, g_ple[None], w_ple_gate.astype(BF16), b_ple[None],
                w_ple_proj.astype(BF16), tm=ROW_TILE)


def kernel(x, p, positions, g_mix, w_in, b_gate, g_cq, w_uq, g_ckv, w_ukv, g_qn_mla, g_kn_mla, g_qn_swa, g_kn_swa, sink, w_br_mla, w_br_swa, w_out, g_moe, w_router, router_bias, w_exp_gu, w_exp_down, w_sh_gu, w_sh_down, g_ple, w_ple_gate, b_ple, w_ple_proj):
    B, S, D = x.shape
    x2d = x.reshape(B * S, D)
    pos2d = positions.reshape(B * S, 1)
    for i in range(p.shape[0]):
        x2d = _layer(x2d, p[i].reshape(B * S, -1), pos2d, B, S, g_mix[i], w_in[i], b_gate[i],
                     g_cq[i], w_uq[i], g_ckv[i], w_ukv[i], g_qn_mla[i], g_kn_mla[i], g_qn_swa[i],
                     g_kn_swa[i], sink[i], w_br_mla[i], w_br_swa[i], w_out[i], g_moe[i],
                     w_router[i], router_bias[i], w_exp_gu[i], w_exp_down[i], w_sh_gu[i],
                     w_sh_down[i], g_ple[i], w_ple_gate[i], b_ple[i], w_ple_proj[i])
    return x2d.reshape(B, S, D)
```

```python
import functools

import jax
import jax.numpy as jnp
import numpy as np
from jax import lax
from jax.experimental import pallas as pl
from jax.experimental.pallas import tpu as pltpu

D_MODEL = 1024
PLE_DIM = 256
ROPE_THETA = 10000.0
EPS = 1e-6
NEG_INF = -1e30

MLA_HEADS = 8
MLA_Q_RANK = 384
MLA_KV_RANK = 256
MLA_NOPE = 64
MLA_ROPE = 32
MLA_QK = MLA_NOPE + MLA_ROPE
MLA_V = 64

SWA_HEADS = 8
SWA_KV_HEADS = 2
SWA_GROUP = SWA_HEADS // SWA_KV_HEADS
SWA_HD = 64
WINDOW = 128

OFF_CQ = MLA_Q_RANK
OFF_CKV = OFF_CQ + MLA_KV_RANK
OFF_KR = OFF_CKV + MLA_ROPE
OFF_QS = OFF_KR + SWA_HEADS * SWA_HD
OFF_KS = OFF_QS + SWA_KV_HEADS * SWA_HD
OFF_VS = OFF_KS + SWA_KV_HEADS * SWA_HD
OFF_GA = OFF_VS + D_MODEL

N_EXPERTS = 64
TOP_K = 8
N_GROUPS = 8
TOPK_GROUPS = 4
EXPERTS_PER_GROUP = N_EXPERTS // N_GROUPS
EXPERT_FF = 256
SHARED_FF = 256
ROUTED_SCALE = 2.5

LANES = 128
ROW_WORDS = D_MODEL // 2
ROW_SUB = ROW_WORDS // LANES
MOE_CHUNK = 2048
MOE_TILE = 128
MOE_TILE_BIG = 320
SLOT_ALIGN = 16
COMBINE_BLOCK = 256
EXPERTS_PER_STEP = 4
EXPERT_STEPS = N_EXPERTS // EXPERTS_PER_STEP
PRE_TILE = 256
ROW_TILE = 512
MLA_Q_TILE = 512
MLA_KV_TILE = 256
LOG2E = 1.4426950408889634
SOFTMAX_UNDERFLOW_GUARD = 1e-30
SWA_Q_TILE = 512
SWA_BLOCK = 128
VMEM_LIMIT = 56 * 1024 * 1024

BF16 = jnp.bfloat16
F32 = jnp.float32

C_CQ = 0
C_CKV = C_CQ + MLA_Q_RANK
C_KR = C_CKV + MLA_KV_RANK
C_QS = C_KR + LANES
C_KS = C_QS + SWA_HEADS * LANES
C_VS = C_KS + SWA_KV_HEADS * LANES
C_GA = C_VS + SWA_KV_HEADS * LANES
C_END = C_GA + 2 * D_MODEL


def _full(shape):
    nd = len(shape)
    return pl.BlockSpec(shape, lambda *_: (0,) * nd)


def _dot(a, b):
    return jnp.dot(a, b, preferred_element_type=F32)


def _dot_nt(a, b, precision=None):
    return lax.dot_general(a, b, (((1,), (1,)), ((), ())), precision=precision,
                           preferred_element_type=F32)


def _rms(v, n):
    return v * lax.rsqrt(jnp.sum(v * v, axis=-1, keepdims=True) * (1.0 / n) + EPS)


def _rope(v, cos, sin):
    return v * cos + pltpu.roll(v, LANES // 2, 1) * sin


def _pre_kernel(x_ref, pos_ref, w_all_ref, w_uq_ref, w_k_ref, w_v_ref, g_mix_ref, g_cq_ref,
                g_ckv_ref, gq_m_ref, gk_m_ref, gq_s_ref, gk_s_ref, b_gate_ref, rope_ref,
                qm_ref, km_ref, vm_ref, qw_ref, kw_ref, vw_ref, gate_ref):
    x = x_ref[...]
    h = (_rms(x, D_MODEL) * g_mix_ref[...]).astype(BF16)

    def proj(lo, hi):
        return _dot(h, w_all_ref[:, lo:hi])

    z_lat = proj(C_CQ, C_QS)
    z_qs = proj(C_QS, C_KS)
    half = (C_END - C_GA) // 2
    z_ga = proj(C_GA, C_GA + half)

    pos = pos_ref[...].astype(F32)
    rope = rope_ref[...]
    ang = pos * rope[0:1, :]
    cos_m1 = jnp.cos(ang) - 1.0
    sin = jnp.sin(ang)
    cos_m, sin_m = 1.0 + cos_m1 * rope[2:3, :], sin * rope[1:2, :]
    cos_s, sin_s = 1.0 + cos_m1 * rope[4:5, :], sin * rope[3:4, :]

    cqn = (_rms(z_lat[:, C_CQ:C_CKV], MLA_Q_RANK) * g_cq_ref[...]).astype(BF16)
    ckvn = (_rms(z_lat[:, C_CKV:C_KR], MLA_KV_RANK) * g_ckv_ref[...]).astype(BF16)
    q = _dot(cqn, w_uq_ref[...])
    gq_m = gq_m_ref[...]
    for hd in range(MLA_HEADS):
        qh = _rms(q[:, hd * LANES:(hd + 1) * LANES], MLA_QK) * gq_m
        qh = _rope(qh, cos_m, sin_m) * (MLA_QK ** -0.5 * LOG2E)
        qm_ref[:, hd * LANES:(hd + 1) * LANES] = qh.astype(BF16)

    kn = _dot(ckvn, w_k_ref[...])
    vm_ref[...] = _dot(ckvn, w_v_ref[...]).astype(BF16)
    z_kv = proj(C_KS, C_GA)

    gq_s = gq_s_ref[...]
    for hd in range(SWA_HEADS):
        qh = _rms(z_qs[:, hd * LANES:(hd + 1) * LANES], SWA_HD) * gq_s
        qh = _rope(qh, cos_s, sin_s) * (SWA_HD ** -0.5 * LOG2E)
        qw_ref[:, hd * LANES:(hd + 1) * LANES] = qh.astype(BF16)

    gk_m = gk_m_ref[...]
    kr = z_lat[:, C_KR:C_QS]
    ss_kr = jnp.sum(kr * kr, axis=-1, keepdims=True)
    kr_rot = _rope(kr * gk_m, cos_m, sin_m)
    for hd in range(MLA_HEADS):
        kh = kn[:, hd * LANES:(hd + 1) * LANES]
        ss = jnp.sum(kh * kh, axis=-1, keepdims=True) + ss_kr
        sc = lax.rsqrt(ss * (1.0 / MLA_QK) + EPS)
        km_ref[:, hd * LANES:(hd + 1) * LANES] = ((kh * gk_m + kr_rot) * sc).astype(BF16)

    z_gb = proj(C_GA + half, C_END)

    gk_s = gk_s_ref[...]
    for hd in range(SWA_KV_HEADS):
        kh = _rms(z_kv[:, hd * LANES:(hd + 1) * LANES], SWA_HD) * gk_s
        kh = _rope(kh, cos_s, sin_s)
        kw_ref[:, hd * LANES:(hd + 1) * LANES] = kh.astype(BF16)
    vw_ref[...] = z_kv[:, C_VS - C_KS:].astype(BF16)

    gate_ref[:, :half] = jax.nn.sigmoid(z_ga + b_gate_ref[:, :half]).astype(BF16)
    gate_ref[:, half:] = jax.nn.sigmoid(z_gb + b_gate_ref[:, half:]).astype(BF16)


def _pre_attention(x2d, pos2d, w_all, w_uq, w_k, w_v, g_mix, g_cq, g_ckv, gq_m, gk_m, gq_s,
                   gk_s, b_gate, rope_tab, tm):
    T = x2d.shape[0]
    row = lambda n: pl.BlockSpec((tm, n), lambda i: (i, 0))
    outs = [(MLA_HEADS * LANES, BF16), (MLA_HEADS * LANES, BF16), (MLA_HEADS * MLA_V, BF16),
            (SWA_HEADS * LANES, BF16), (SWA_KV_HEADS * LANES, BF16),
            (SWA_KV_HEADS * LANES, BF16), (2 * D_MODEL, BF16)]
    consts = [w_all, w_uq, w_k, w_v, g_mix, g_cq, g_ckv, gq_m, gk_m, gq_s, gk_s, b_gate, rope_tab]
    return pl.pallas_call(
        _pre_kernel,
        grid=(T // tm,),
        in_specs=[row(D_MODEL), row(1)] + [_full(c.shape) for c in consts],
        out_specs=[row(n) for n, _ in outs],
        out_shape=[jax.ShapeDtypeStruct((T, n), dt) for n, dt in outs],
        compiler_params=pltpu.CompilerParams(dimension_semantics=("arbitrary",),
                                             vmem_limit_bytes=VMEM_LIMIT),
        name="pre_attention",
    )(x2d, pos2d, *consts)


def _half_masks(dtype):
    lane = lax.broadcasted_iota(jnp.int32, (1, LANES), 1)
    lo = (lane < LANES // 2).astype(dtype)
    return lo, 1 - lo


def _mla_exact(q_ref, k_ref, v_ref, o_ref):
    v = v_ref[...]
    masks = _half_masks(v.dtype)
    acc = None
    for hh in range(2):
        q = q_ref[:, hh * LANES:(hh + 1) * LANES]
        k = k_ref[:, hh * LANES:(hh + 1) * LANES]
        s = _dot_nt(q, k)
        m = jnp.max(s, axis=-1, keepdims=True)
        p = jnp.exp2(s - m)
        l = jnp.sum(p, axis=-1, keepdims=True)
        o = _dot(p.astype(BF16), v * masks[hh]) / l
        acc = o if acc is None else acc + o
    o_ref[...] = acc.astype(o_ref.dtype)


def _mla_kernel(q_ref, k_ref, v_ref, o_ref, kmax_ref):
    S = k_ref.shape[0]
    tq = q_ref.shape[0]

    @pl.when(pl.program_id(2) == 0)
    def _():
        for hh in range(2):
            k = k_ref[:, hh * LANES:(hh + 1) * LANES].astype(F32)
            kn2 = jnp.max(jnp.sum(k * k, axis=-1, keepdims=True), axis=0, keepdims=True)
            kmax_ref[hh] = jnp.broadcast_to(jnp.sqrt(kn2), kmax_ref.shape[1:])

    masks = _half_masks(BF16)
    acc = None
    lmin = None
    for hh in range(2):
        q = q_ref[:, hh * LANES:(hh + 1) * LANES]
        qf = q.astype(F32)
        bound = jnp.sqrt(jnp.sum(qf * qf, axis=-1, keepdims=True)) * kmax_ref[hh][0:1, 0:1]
        v = v_ref[...] * masks[hh]
        o = jnp.zeros((tq, LANES), F32)
        lsum = jnp.zeros((tq, LANES), F32)
        for j in range(S // MLA_KV_TILE):
            rows = slice(j * MLA_KV_TILE, (j + 1) * MLA_KV_TILE)
            p = jnp.exp2(_dot_nt(q, k_ref[rows, hh * LANES:(hh + 1) * LANES]) - bound)
            for t in range(MLA_KV_TILE // LANES):
                lsum = lsum + p[:, t * LANES:(t + 1) * LANES]
            o = o + _dot(p.astype(BF16), v[rows, :])
        l = jnp.sum(lsum, axis=-1, keepdims=True)
        o = o / l
        acc = o if acc is None else acc + o
        lm = jnp.min(l)
        lmin = lm if lmin is None else jnp.minimum(lmin, lm)
    o_ref[...] = acc.astype(o_ref.dtype)

    @pl.when(jnp.logical_not(lmin > SOFTMAX_UNDERFLOW_GUARD))
    def _():
        _mla_exact(q_ref, k_ref, v_ref, o_ref)


def _mla_attention(qm, km, vm, B, S, tq):
    pairs = MLA_HEADS // 2
    q3 = qm.reshape(B, S, MLA_HEADS * LANES)
    k3 = km.reshape(B, S, MLA_HEADS * LANES)
    v3 = vm.reshape(B, S, MLA_HEADS * MLA_V)
    return pl.pallas_call(
        _mla_kernel,
        grid=(B, pairs, S // tq),
        in_specs=[pl.BlockSpec((None, tq, 2 * LANES), lambda b, p, i: (b, i, p)),
                  pl.BlockSpec((None, S, 2 * LANES), lambda b, p, i: (b, 0, p)),
                  pl.BlockSpec((None, S, LANES), lambda b, p, i: (b, 0, p))],
        out_specs=pl.BlockSpec((None, tq, LANES), lambda b, p, i: (b, i, p)),
        out_shape=jax.ShapeDtypeStruct((B, S, MLA_HEADS * MLA_V), BF16),
        scratch_shapes=[pltpu.VMEM((2, 8, LANES), F32)],
        compiler_params=pltpu.CompilerParams(
            dimension_semantics=("arbitrary", "arbitrary", "arbitrary"),
            vmem_limit_bytes=VMEM_LIMIT),
        name="mla_attention",
    )(q3, k3, v3)


def _swa_kernel(sink_ref, q_ref, k_ref, v_ref, o_ref, *, tq, S):
    hk = pl.program_id(1)
    i = pl.program_id(2)
    tk = SWA_BLOCK + 2 * WINDOW
    row = lax.broadcasted_iota(jnp.int32, (SWA_GROUP * SWA_BLOCK, 1), 0)
    qoff = row & (SWA_BLOCK - 1)
    head = row // SWA_BLOCK
    sk = jnp.zeros((SWA_GROUP * SWA_BLOCK, 1), F32)
    for g in range(SWA_GROUP):
        sk = jnp.where(head == g, sink_ref[SWA_GROUP * hk + g] * LOG2E, sk)
    low_half = lax.broadcasted_iota(jnp.int32, (1, LANES), 1) < LANES // 2
    for sub in range(tq // SWA_BLOCK):
        rows = slice(sub * SWA_BLOCK, (sub + 1) * SWA_BLOCK)
        q0 = i * tq + sub * SWA_BLOCK
        kstart = pl.multiple_of(jnp.clip(q0 - WINDOW, 0, S - tk), WINDOW)
        k = k_ref[pl.ds(kstart, tk), :]
        v = v_ref[pl.ds(kstart, tk), :]
        q = jnp.concatenate([q_ref[rows, g * LANES:(g + 1) * LANES] for g in range(SWA_GROUP)], axis=0)
        kpos = kstart + lax.broadcasted_iota(jnp.int32, (1, tk), 1)
        valid = jnp.abs(kpos - (q0 + qoff[:SWA_BLOCK])) <= WINDOW
        s = _dot_nt(q, k).reshape(SWA_GROUP, SWA_BLOCK, tk)
        s = jnp.where(valid[None], s, NEG_INF).reshape(SWA_GROUP * SWA_BLOCK, tk)
        m = jnp.maximum(jnp.max(s, axis=-1, keepdims=True), sk)
        e = jnp.exp2(s - m)
        denom = jnp.sum(e, axis=-1, keepdims=True) + jnp.exp2(sk - m)
        o = _dot(e.astype(BF16), v) / denom
        for j in range(SWA_GROUP // 2):
            even = o[(2 * j) * SWA_BLOCK:(2 * j + 1) * SWA_BLOCK, :]
            odd = o[(2 * j + 1) * SWA_BLOCK:(2 * j + 2) * SWA_BLOCK, :]
            o_ref[rows, j * LANES:(j + 1) * LANES] = jnp.where(low_half, even, odd).astype(o_ref.dtype)


def _swa_attention(qw, kw, vw, sink, B, S, tq):
    q3 = qw.reshape(B, S, SWA_HEADS * LANES)
    k3 = kw.reshape(B, S, SWA_KV_HEADS * LANES)
    v3 = vw.reshape(B, S, SWA_KV_HEADS * LANES)
    return pl.pallas_call(
        functools.partial(_swa_kernel, tq=tq, S=S),
        grid=(B, SWA_KV_HEADS, S // tq),
        in_specs=[pl.BlockSpec(memory_space=pltpu.SMEM),
                  pl.BlockSpec((None, tq, SWA_GROUP * LANES), lambda b, h, i: (b, i, h)),
                  pl.BlockSpec((None, S, LANES), lambda b, h, i: (b, 0, h)),
                  pl.BlockSpec((None, S, LANES), lambda b, h, i: (b, 0, h))],
        out_specs=pl.BlockSpec((None, tq, SWA_GROUP * SWA_HD), lambda b, h, i: (b, i, h)),
        out_shape=jax.ShapeDtypeStruct((B, S, SWA_HEADS * SWA_HD), BF16),
        compiler_params=pltpu.CompilerParams(
            dimension_semantics=("arbitrary", "arbitrary", "arbitrary"),
            vmem_limit_bytes=VMEM_LIMIT),
        name="swa_attention",
    )(sink, q3, k3, v3)


def _beats(vj, vi, j_first):
    return (vj >= vi) if j_first else (vj > vi)


def _route(scores, sel):
    G, P = N_GROUPS, EXPERTS_PER_GROUP
    groups = [sel[g * P:(g + 1) * P, :] for g in range(G)]
    row = lax.broadcasted_iota(jnp.int32, (P, 1), 0)
    gscore = []
    for vg in groups:
        m1 = jnp.max(vg, axis=0, keepdims=True)
        first = jnp.min(jnp.where(vg == m1, row, P), axis=0, keepdims=True)
        m2 = jnp.max(jnp.where(row == first, -jnp.inf, vg), axis=0, keepdims=True)
        gscore.append(m1 + m2)
    masked = []
    for g in range(G):
        rank = jnp.zeros_like(gscore[g], dtype=jnp.int32)
        for g2 in range(G):
            if g2 != g:
                rank = rank + _beats(gscore[g2], gscore[g], g2 < g).astype(jnp.int32)
        masked.append(jnp.where(rank < TOPK_GROUPS, groups[g], NEG_INF))
    index = [row + g * P for g in range(G)]
    chosen = [None] * G
    for _ in range(TOP_K):
        best = masked[0]
        for g in range(1, G):
            best = jnp.maximum(best, masked[g])
        best = jnp.max(best, axis=0, keepdims=True)
        first = jnp.where(masked[0] == best, index[0], N_EXPERTS)
        for g in range(1, G):
            first = jnp.minimum(first, jnp.where(masked[g] == best, index[g], N_EXPERTS))
        first = jnp.min(first, axis=0, keepdims=True)
        for g in range(G):
            hit = index[g] == first
            chosen[g] = hit if chosen[g] is None else (chosen[g] | hit)
            masked[g] = jnp.where(hit, -jnp.inf, masked[g])
    picked = [jnp.where(chosen[g], scores[g * P:(g + 1) * P, :], 0.0) for g in range(G)]
    total = picked[0]
    for g in range(1, G):
        total = total + picked[g]
    denom = jnp.sum(total, axis=0, keepdims=True)
    return [pk / denom * ROUTED_SCALE for pk in picked]


def _pack_pair(lo, hi):
    return pltpu.pack_elementwise([lo, hi], packed_dtype=BF16)


def _unpack_pair(word):
    lo = pltpu.unpack_elementwise(word, index=0, packed_dtype=BF16, unpacked_dtype=F32)
    hi = pltpu.unpack_elementwise(word, index=1, packed_dtype=BF16, unpacked_dtype=F32)
    return lo, hi


def _store_rows_dense(ref, words):
    for j in range(ROW_SUB):
        ref[pl.ds(j, words.shape[0], stride=ROW_SUB), :] = words[:, j * LANES:(j + 1) * LANES]


def _load_rows_dense(ref, rows):
    sub = ROW_SUB
    return jnp.concatenate([ref[pl.ds(j, rows, stride=sub), :] for j in range(sub)], axis=1)


def _post_kernel(om_ref, ow_ref, gate_ref, x_ref, wbm_ref, wbw_ref, wout_ref, g_moe_ref, wr_ref,
                 rb_ref, wsgu_ref, wsd_ref, x1_ref, hp_ref, comb_ref):
    am = _dot(om_ref[...], wbm_ref[...])
    aw = _dot(ow_ref[...], wbw_ref[...])
    gates = gate_ref[...].astype(F32)
    merged = gates[:, :D_MODEL] * am + gates[:, D_MODEL:] * aw
    x1 = x_ref[...] + _dot(merged.astype(BF16), wout_ref[...])

    h2 = _rms(x1, D_MODEL) * g_moe_ref[...]
    h2b = h2.astype(BF16)
    _store_rows_dense(hp_ref, _pack_pair(h2[:, :ROW_WORDS], h2[:, ROW_WORDS:]))

    wr = wr_ref[...]
    w_hi = wr.astype(BF16)
    w_lo = (wr - w_hi.astype(F32)).astype(BF16)
    h_lo = (h2 - h2b.astype(F32)).astype(BF16)
    by_hi = _dot_nt(jnp.concatenate([w_hi, w_lo], axis=0), h2b)
    logits = by_hi[:N_EXPERTS] + by_hi[N_EXPERTS:] + _dot_nt(w_hi, h_lo)

    sgu = _dot(h2b, wsgu_ref[...])
    sh = jax.nn.silu(sgu[:, :SHARED_FF]) * sgu[:, SHARED_FF:]
    x1_ref[...] = x1 + _dot(sh.astype(BF16), wsd_ref[...])

    scores = jax.nn.sigmoid(logits)
    comb = _route(scores, scores + rb_ref[...])
    for g in range(N_GROUPS):
        comb_ref[g * EXPERTS_PER_GROUP:(g + 1) * EXPERTS_PER_GROUP, :] = comb[g]


def _post_attention(om, ow, gates, x2d, wbm, wbw, wout, g_moe, wr_t, rbias, wsgu, wsd, tm):
    T = x2d.shape[0]
    row = lambda n: pl.BlockSpec((tm, n), lambda i: (i, 0))
    consts = [wbm, wbw, wout, g_moe, wr_t, rbias, wsgu, wsd]
    return pl.pallas_call(
        _post_kernel,
        grid=(T // tm,),
        in_specs=[row(om.shape[1]), row(ow.shape[1]), row(2 * D_MODEL), row(D_MODEL)]
        + [_full(c.shape) for c in consts],
        out_specs=[row(D_MODEL), pl.BlockSpec((tm * ROW_SUB, LANES), lambda i: (i, 0)),
                   pl.BlockSpec((N_EXPERTS, tm), lambda i: (0, i))],
        out_shape=[jax.ShapeDtypeStruct((T, D_MODEL), F32),
                   jax.ShapeDtypeStruct((T * ROW_SUB, LANES), jnp.uint32),
                   jax.ShapeDtypeStruct((N_EXPERTS, T), F32)],
        compiler_params=pltpu.CompilerParams(dimension_semantics=("arbitrary",),
                                             vmem_limit_bytes=VMEM_LIMIT),
        name="post_attention",
    )(om, ow, gates, x2d, *consts)


SLOT_ROWS = -(-(MOE_CHUNK * TOP_K + N_EXPERTS * (SLOT_ALIGN - 1) + MOE_TILE_BIG) // MOE_TILE) * MOE_TILE
DUMMY_SLOT = SLOT_ROWS - 1
PLAN_BLOCK = 256


def _plan_kernel(comb_ref, slot_ref, w_ref, off_ref, cnt_ref):
    comb = comb_ref[...]
    sel = comb > 0.0
    m = sel.astype(F32)
    mb = m.astype(BF16)
    r_i = lax.broadcasted_iota(jnp.int32, (PLAN_BLOCK, PLAN_BLOCK), 0)
    c_i = lax.broadcasted_iota(jnp.int32, (PLAN_BLOCK, PLAN_BLOCK), 1)
    before = (r_i < c_i).astype(BF16)
    carry = jnp.zeros((N_EXPERTS, 1), F32)
    ranks = []
    for b in range(MOE_CHUNK // PLAN_BLOCK):
        blk = slice(b * PLAN_BLOCK, (b + 1) * PLAN_BLOCK)
        ranks.append(_dot(mb[:, blk], before) + carry)
        carry = carry + jnp.sum(m[:, blk], axis=1, keepdims=True)
    rank = jnp.concatenate(ranks, axis=1)
    cnt = carry
    cnt_pad = jnp.floor((cnt + (SLOT_ALIGN - 1)) * (1.0 / SLOT_ALIGN)) * SLOT_ALIGN
    e_r = lax.broadcasted_iota(jnp.int32, (N_EXPERTS, N_EXPERTS), 0)
    e_c = lax.broadcasted_iota(jnp.int32, (N_EXPERTS, N_EXPERTS), 1)
    below = (e_c < e_r).astype(F32)
    off = jnp.dot(below, jnp.broadcast_to(cnt_pad, (N_EXPERTS, LANES)),
                  precision=lax.Precision.HIGHEST, preferred_element_type=F32)
    slot = off[:, :1] + rank
    kidx = _dot(below.astype(BF16), mb)
    row = lax.broadcasted_iota(jnp.int32, (TOP_K, 1), 0)
    slot_acc = jnp.zeros((TOP_K, MOE_CHUNK), F32)
    w_acc = jnp.zeros((TOP_K, MOE_CHUNK), F32)
    for k in range(TOP_K):
        pick = jnp.where(sel & (kidx == k), 1.0, 0.0)
        found = jnp.sum(pick, axis=0, keepdims=True) > 0.0
        s_k = jnp.where(found, jnp.sum(pick * slot, axis=0, keepdims=True), float(DUMMY_SLOT))
        w_k = jnp.sum(pick * comb, axis=0, keepdims=True)
        slot_acc = jnp.where(row == k, s_k, slot_acc)
        w_acc = jnp.where(row == k, w_k, w_acc)
    slot_ref[...] = slot_acc.astype(jnp.int32) * ROW_SUB
    w_ref[...] = w_acc
    off_ref[...] = off.astype(jnp.int32)
    cnt_ref[...] = jnp.broadcast_to(cnt, (N_EXPERTS, LANES)).astype(jnp.int32)


def _moe_plan(comb_t):
    T = comb_t.shape[1]
    nch = T // MOE_CHUNK
    per_pair = pl.BlockSpec((None, TOP_K, MOE_CHUNK), lambda c: (c, 0, 0))
    per_expert = pl.BlockSpec((None, N_EXPERTS, LANES), lambda c: (c, 0, 0))
    return pl.pallas_call(
        _plan_kernel,
        grid=(nch,),
        in_specs=[pl.BlockSpec((N_EXPERTS, MOE_CHUNK), lambda c: (0, c))],
        out_specs=[per_pair, per_pair, per_expert, per_expert],
        out_shape=[jax.ShapeDtypeStruct((nch, TOP_K, MOE_CHUNK), jnp.int32),
                   jax.ShapeDtypeStruct((nch, TOP_K, MOE_CHUNK), F32),
                   jax.ShapeDtypeStruct((nch, N_EXPERTS, LANES), jnp.int32),
                   jax.ShapeDtypeStruct((nch, N_EXPERTS, LANES), jnp.int32)],
        compiler_params=pltpu.CompilerParams(dimension_semantics=("arbitrary",),
                                             vmem_limit_bytes=VMEM_LIMIT),
        name="moe_plan",
    )(comb_t)


def _slab_at(ref, first):
    return ref.at[pl.ds(pl.multiple_of(first, ROW_SUB), ROW_SUB), :]


def _slab(ref, row):
    return _slab_at(ref, row * ROW_SUB)


def _moe_kernel(off_ref, cnt_ref, slot_hbm, w_hbm, hp_ref, wgu_ref, wd_ref, o_ref,
                buf, clo, chi, *smem_and_sem):
    slot_s = smem_and_sem[:TOP_K]
    w_s = smem_and_sem[TOP_K:2 * TOP_K]
    sem = smem_and_sem[2 * TOP_K]
    c = pl.program_id(0)
    s = pl.program_id(1)

    @pl.when(s == 0)
    def _dispatch():
        copies = [pltpu.make_async_copy(slot_hbm.at[c, k], slot_s[k], sem.at[k]) for k in range(TOP_K)]
        copies += [pltpu.make_async_copy(w_hbm.at[c, k], w_s[k], sem.at[TOP_K + k]) for k in range(TOP_K)]
        for cp in copies:
            cp.start()

        @pl.when(c == 0)
        def _():
            buf[...] = jnp.zeros_like(buf)

        for cp in copies:
            cp.wait()

        def scatter(tb, carry):
            for tt in range(8):
                t = tb * 8 + tt
                slab = _slab(hp_ref, t)[...]
                for k in range(TOP_K):
                    _slab_at(buf, slot_s[k][t])[...] = slab
            return carry

        lax.fori_loop(0, MOE_CHUNK // 8, scatter, 0)

    def expert(ee):
        e = s * EXPERTS_PER_STEP + ee
        n = cnt_ref[c, e]
        off = off_ref[c, e]

        def ffn(start, rows):
            view = buf.at[pl.ds(pl.multiple_of((off + start) * ROW_SUB, SLOT_ALIGN * ROW_SUB),
                                rows * ROW_SUB), :]
            x_lo, x_hi = _unpack_pair(_load_rows_dense(view, rows))
            gu = (_dot(x_lo.astype(BF16), wgu_ref[ee, :ROW_WORDS, :])
                  + _dot(x_hi.astype(BF16), wgu_ref[ee, ROW_WORDS:, :]))
            hid = jax.nn.silu(gu[:, :EXPERT_FF]) * gu[:, EXPERT_FF:]
            y = _dot(hid.astype(BF16), wd_ref[ee])
            mine = lax.broadcasted_iota(jnp.int32, (rows, 1), 0) < (n - start)
            _store_rows_dense(view, _pack_pair(jnp.where(mine, y[:, :ROW_WORDS], x_lo),
                                               jnp.where(mine, y[:, ROW_WORDS:], x_hi)))

        n_big = (n + MOE_TILE_BIG - MOE_TILE - 1) // MOE_TILE_BIG

        def big(r, carry):
            ffn(r * MOE_TILE_BIG, MOE_TILE_BIG)
            return carry

        lax.fori_loop(0, n_big, big, 0)

        @pl.when(n > n_big * MOE_TILE_BIG)
        def _():
            ffn(n_big * MOE_TILE_BIG, MOE_TILE)

    @pl.when(s < EXPERT_STEPS)
    def _experts():
        for ee in range(EXPERTS_PER_STEP):
            expert(ee)

    @pl.when(s >= EXPERT_STEPS)
    def _combine():
        t0 = (s - EXPERT_STEPS) * COMBINE_BLOCK

        def gather(i, carry):
            for tt in range(4):
                tl = i * 4 + tt
                t = t0 + tl
                acc_lo = jnp.zeros((ROW_SUB, LANES), F32)
                acc_hi = jnp.zeros((ROW_SUB, LANES), F32)
                for k in range(TOP_K):
                    lo, hi = _unpack_pair(_slab_at(buf, slot_s[k][t])[...])
                    wk = w_s[k][t]
                    acc_lo = acc_lo + wk * lo
                    acc_hi = acc_hi + wk * hi
                _slab(clo, tl)[...] = acc_lo
                _slab(chi, tl)[...] = acc_hi
            return carry

        lax.fori_loop(0, COMBINE_BLOCK // 4, gather, 0)
        o_ref[:, :ROW_WORDS] = _load_rows_dense(clo, COMBINE_BLOCK)
        o_ref[:, ROW_WORDS:] = _load_rows_dense(chi, COMBINE_BLOCK)


def _moe_sparse(offs, cnts, slots, wts, hp, w_gu, w_d):
    nch = offs.shape[0]
    T = nch * MOE_CHUNK
    blocks = MOE_CHUNK // COMBINE_BLOCK
    expert = lambda c, s, *_: (jnp.minimum(s, EXPERT_STEPS - 1), 0, 0)
    grid_spec = pltpu.PrefetchScalarGridSpec(
        num_scalar_prefetch=2,
        grid=(nch, EXPERT_STEPS + blocks),
        in_specs=[pl.BlockSpec(memory_space=pl.ANY),
                  pl.BlockSpec(memory_space=pl.ANY),
                  pl.BlockSpec((MOE_CHUNK * ROW_SUB, LANES), lambda c, s, *_: (c, 0),
                               pipeline_mode=pl.Buffered(1)),
                  pl.BlockSpec((EXPERTS_PER_STEP, D_MODEL, 2 * EXPERT_FF), expert),
                  pl.BlockSpec((EXPERTS_PER_STEP, EXPERT_FF, D_MODEL), expert)],
        out_specs=pl.BlockSpec(
            (COMBINE_BLOCK, D_MODEL),
            lambda c, s, *_: (c * blocks + jnp.maximum(s - EXPERT_STEPS, 0), 0)),
        scratch_shapes=[pltpu.VMEM((SLOT_ROWS * ROW_SUB, LANES), jnp.uint32),
                        pltpu.VMEM((COMBINE_BLOCK * ROW_SUB, LANES), F32),
                        pltpu.VMEM((COMBINE_BLOCK * ROW_SUB, LANES), F32),
                        *[pltpu.SMEM((MOE_CHUNK,), jnp.int32) for _ in range(TOP_K)],
                        *[pltpu.SMEM((MOE_CHUNK,), F32) for _ in range(TOP_K)],
                        pltpu.SemaphoreType.DMA((2 * TOP_K,))])
    return pl.pallas_call(
        _moe_kernel,
        grid_spec=grid_spec,
        out_shape=jax.ShapeDtypeStruct((T, D_MODEL), F32),
        compiler_params=pltpu.CompilerParams(dimension_semantics=("arbitrary", "arbitrary"),
                                             vmem_limit_bytes=VMEM_LIMIT),
        name="moe_experts",
    )(offs, cnts, slots, wts, hp, w_gu, w_d)


def _ple_kernel(x1_ref, r_ref, p_ref, g_ref, wg_ref, b_ref, wp_ref, o_ref):
    x2 = x1_ref[...] + r_ref[...]
    hn = (_rms(x2, D_MODEL) * g_ref[...]).astype(BF16)
    gate = jax.nn.sigmoid(_dot(hn, wg_ref[...]) + b_ref[...])
    o_ref[...] = x2 + gate * _dot(p_ref[...].astype(BF16), wp_ref[...])


def _ple(x1s, routed, p2d, g_ple, wg, b_ple, wp, tm):
    T = x1s.shape[0]
    row = lambda n: pl.BlockSpec((tm, n), lambda i: (i, 0))
    consts = [g_ple, wg, b_ple, wp]
    return pl.pallas_call(
        _ple_kernel,
        grid=(T // tm,),
        in_specs=[row(D_MODEL), row(D_MODEL), row(PLE_DIM)] + [_full(c.shape) for c in consts],
        out_specs=row(D_MODEL),
        out_shape=jax.ShapeDtypeStruct((T, D_MODEL), F32),
        compiler_params=pltpu.CompilerParams(dimension_semantics=("arbitrary",),
                                             vmem_limit_bytes=VMEM_LIMIT),
        name="ple",
    )(x1s, routed, p2d, *consts)


def _lane_map(*runs):
    src = np.full((LANES,), -1)
    for lane, dim, n in runs:
        src[lane:lane + n] = np.arange(dim, dim + n)
    return src


_MLA_HALF = MLA_ROPE // 2
_SWA_HALF = SWA_HD // 2
MLA_LANES = _lane_map((0, MLA_NOPE, _MLA_HALF), (_MLA_HALF, 0, LANES // 2 - _MLA_HALF),
                      (LANES // 2, MLA_NOPE + _MLA_HALF, _MLA_HALF),
                      (LANES // 2 + _MLA_HALF, LANES // 2 - _MLA_HALF, MLA_NOPE - LANES // 2 + _MLA_HALF))
MLA_NOPE_LANES = np.where(MLA_LANES < MLA_NOPE, MLA_LANES, -1)
MLA_ROPE_LANES = np.where(MLA_LANES >= MLA_NOPE, MLA_LANES - MLA_NOPE, -1)
SWA_LANES = _lane_map((LANES // 2 - _SWA_HALF, 0, _SWA_HALF), (LANES - _SWA_HALF, _SWA_HALF, _SWA_HALF))


def _spread(w, heads, lane_src):
    k = w.shape[0]
    dim = w.shape[1] // heads
    w = jnp.pad(w.reshape(k, heads, dim), ((0, 0), (0, 0), (0, 1)))
    return w[:, :, np.where(lane_src < 0, dim, lane_src)].reshape(k, heads * LANES)


def _rope_table():
    def inv_freq(dim):
        return 1.0 / (ROPE_THETA ** (jnp.arange(0, dim, 2, dtype=F32) / dim))

    def selector(lane_src, half):
        sel = np.where(lane_src < 0, 0.0, np.where(lane_src < half, -1.0, 1.0))
        return jnp.asarray(sel, F32)

    sel_m = selector(MLA_ROPE_LANES, _MLA_HALF)
    sel_s = selector(SWA_LANES, _SWA_HALF)
    freq_m = _spread(jnp.tile(inv_freq(MLA_ROPE), 2)[None], 1, MLA_ROPE_LANES)[0]
    freq_s = _spread(jnp.tile(inv_freq(SWA_HD), 2)[None], 1, SWA_LANES)[0]
    zero = jnp.zeros((LANES,), F32)
    rows = [freq_m + freq_s, sel_m, jnp.abs(sel_m), sel_s, jnp.abs(sel_s), zero, zero, zero]
    return jnp.stack(rows)


def _layer(x2d, p2d, pos2d, B, S, g_mix, w_in, b_gate, g_cq, w_uq, g_ckv, w_ukv, g_qn_mla, g_kn_mla,
           g_qn_swa, g_kn_swa, sink, w_br_mla, w_br_swa, w_out, g_moe, w_router, router_bias,
           w_exp_gu, w_exp_down, w_sh_gu, w_sh_down, g_ple, w_ple_gate, b_ple, w_ple_proj):
    w_kr = _spread(w_in[:, OFF_CKV:OFF_KR], 1, MLA_ROPE_LANES)
    w_vs = w_in[:, OFF_KS:OFF_VS].reshape(D_MODEL, SWA_KV_HEADS, 1, SWA_HD)
    w_vs = jnp.broadcast_to(w_vs, (D_MODEL, SWA_KV_HEADS, 2, SWA_HD)).reshape(D_MODEL, -1)
    w_all = jnp.concatenate([
        w_in[:, :OFF_CKV], w_kr,
        _spread(w_in[:, OFF_KR:OFF_QS], SWA_HEADS, SWA_LANES),
        _spread(w_in[:, OFF_QS:OFF_KS], SWA_KV_HEADS, SWA_LANES),
        w_vs, w_in[:, OFF_VS:]], axis=1).astype(BF16)
    assert w_all.shape[1] == C_END
    w_uq_p = _spread(w_uq, MLA_HEADS, MLA_LANES).astype(BF16)
    w_ukv3 = w_ukv.reshape(MLA_KV_RANK, MLA_HEADS, MLA_NOPE + MLA_V)
    w_k = _spread(w_ukv3[:, :, :MLA_NOPE].reshape(MLA_KV_RANK, -1), MLA_HEADS, MLA_NOPE_LANES).astype(BF16)
    w_v = w_ukv3[:, :, MLA_NOPE:].reshape(MLA_KV_RANK, -1).astype(BF16)

    qm, km, vm, qw, kw, vw, gates = _pre_attention(
        x2d, pos2d, w_all, w_uq_p, w_k, w_v, g_mix[None], g_cq[None], g_ckv[None],
        _spread(g_qn_mla[None], 1, MLA_LANES), _spread(g_kn_mla[None], 1, MLA_LANES),
        _spread(g_qn_swa[None], 1, SWA_LANES), _spread(g_kn_swa[None], 1, SWA_LANES),
        b_gate[None], _rope_table(), tm=PRE_TILE)

    om = _mla_attention(qm, km, vm, B, S, tq=MLA_Q_TILE).reshape(B * S, -1)
    ow = _swa_attention(qw, kw, vw, sink, B, S, tq=SWA_Q_TILE).reshape(B * S, -1)

    x1s, hp, comb_t = _post_attention(
        om, ow, gates, x2d, w_br_mla.astype(BF16), w_br_swa.astype(BF16), w_out.astype(BF16),
        g_moe[None], w_router.T, router_bias[:, None], w_sh_gu.astype(BF16),
        w_sh_down.astype(BF16), tm=ROW_TILE)

    slots, wts, offs, cnts = _moe_plan(comb_t)
    routed = _moe_sparse(offs[:, :, 0], cnts[:, :, 0], slots, wts, hp, w_exp_gu.astype(BF16),
                         w_exp_down.astype(BF16))

    return _ple(x1s, routed, p2d, g_ple[None], w_ple_gate.astype(BF16), b_ple[None],
                w_ple_proj.astype(BF16), tm=ROW_TILE)


def kernel(x, p, positions, g_mix, w_in, b_gate, g_cq, w_uq, g_ckv, w_ukv, g_qn_mla, g_kn_mla, g_qn_swa, g_kn_swa, sink, w_br_mla, w_br_swa, w_out, g_moe, w_router, router_bias, w_exp_gu, w_exp_down, w_sh_gu, w_sh_down, g_ple, w_ple_gate, b_ple, w_ple_proj):
    B, S, D = x.shape
    x2d = x.reshape(B * S, D)
    pos2d = positions.reshape(B * S, 1)
    for i in range(p.shape[0]):
        x2d = _layer(x2d, p[i].reshape(B * S, -1), pos2d, B, S, g_mix[i], w_in[i], b_gate[i],
                     g_cq[i], w_uq[i], g_ckv[i], w_ukv[i], g_qn_mla[i], g_kn_mla[i], g_qn_swa[i],
                     g_kn_swa[i], sink[i], w_br_mla[i], w_br_swa[i], w_out[i], g_moe[i],
                     w_router[i], router_bias[i], w_exp_gu[i], w_exp_down[i], w_sh_gu[i],
                     w_sh_down[i], g_ple[i], w_ple_gate[i], b_ple[i], w_ple_proj[i])
    return x2d.reshape(B, S, D)
```

```python
import functools

import jax
import jax.numpy as jnp
import numpy as np
from jax import lax
from jax.experimental import pallas as pl
from jax.experimental.pallas import tpu as pltpu

D_MODEL = 1024
PLE_DIM = 256
ROPE_THETA = 10000.0
EPS = 1e-6
NEG_INF = -1e30

MLA_HEADS = 8
MLA_Q_RANK = 384
MLA_KV_RANK = 256
MLA_NOPE = 64
MLA_ROPE = 32
MLA_QK = MLA_NOPE + MLA_ROPE
MLA_V = 64

SWA_HEADS = 8
SWA_KV_HEADS = 2
SWA_GROUP = SWA_HEADS // SWA_KV_HEADS
SWA_HD = 64
WINDOW = 128

OFF_CQ = MLA_Q_RANK
OFF_CKV = OFF_CQ + MLA_KV_RANK
OFF_KR = OFF_CKV + MLA_ROPE
OFF_QS = OFF_KR + SWA_HEADS * SWA_HD
OFF_KS = OFF_QS + SWA_KV_HEADS * SWA_HD
OFF_VS = OFF_KS + SWA_KV_HEADS * SWA_HD
OFF_GA = OFF_VS + D_MODEL

N_EXPERTS = 64
TOP_K = 8
N_GROUPS = 8
TOPK_GROUPS = 4
EXPERTS_PER_GROUP = N_EXPERTS // N_GROUPS
EXPERT_FF = 256
SHARED_FF = 256
ROUTED_SCALE = 2.5

LANES = 128
ROW_WORDS = D_MODEL // 2
ROW_SUB = ROW_WORDS // LANES
MOE_CHUNK = 2048
MOE_TILE = 128
MOE_TILE_BIG = 320
SLOT_ALIGN = 16
COMBINE_BLOCK = 256
EXPERTS_PER_STEP = 4
EXPERT_STEPS = N_EXPERTS // EXPERTS_PER_STEP
PRE_TILE = 256
ROW_TILE = 512
MLA_Q_TILE = 512
MLA_KV_TILE = 256
LOG2E = 1.4426950408889634
SOFTMAX_UNDERFLOW_GUARD = 1e-30
SWA_Q_TILE = 512
SWA_BLOCK = 128
VMEM_LIMIT = 56 * 1024 * 1024

BF16 = jnp.bfloat16
F32 = jnp.float32

C_CQ = 0
C_CKV = C_CQ + MLA_Q_RANK
C_KR = C_CKV + MLA_KV_RANK
C_QS = C_KR + LANES
C_KS = C_QS + SWA_HEADS * LANES
C_VS = C_KS + SWA_KV_HEADS * LANES
C_GA = C_VS + SWA_KV_HEADS * LANES
C_END = C_GA + 2 * D_MODEL


def _full(shape):
    nd = len(shape)
    return pl.BlockSpec(shape, lambda *_: (0,) * nd)


def _dot(a, b):
    return jnp.dot(a, b, preferred_element_type=F32)


def _dot_nt(a, b, precision=None):
    return lax.dot_general(a, b, (((1,), (1,)), ((), ())), precision=precision,
                           preferred_element_type=F32)


def _rms(v, n):
    return v * lax.rsqrt(jnp.sum(v * v, axis=-1, keepdims=True) * (1.0 / n) + EPS)


def _rope(v, cos, sin):
    return v * cos + pltpu.roll(v, LANES // 2, 1) * sin


def _pre_kernel(x_ref, pos_ref, w_all_ref, w_uq_ref, w_k_ref, w_v_ref, g_mix_ref, g_cq_ref,
                g_ckv_ref, gq_m_ref, gk_m_ref, gq_s_ref, gk_s_ref, b_gate_ref, rope_ref,
                qm_ref, km_ref, vm_ref, qw_ref, kw_ref, vw_ref, gate_ref):
    x = x_ref[...]
    h = (_rms(x, D_MODEL) * g_mix_ref[...]).astype(BF16)

    def proj(lo, hi):
        return _dot(h, w_all_ref[:, lo:hi])

    z_lat = proj(C_CQ, C_QS)
    z_qs = proj(C_QS, C_KS)
    half = (C_END - C_GA) // 2
    z_ga = proj(C_GA, C_GA + half)

    pos = pos_ref[...].astype(F32)
    rope = rope_ref[...]
    ang = pos * rope[0:1, :]
    cos_m1 = jnp.cos(ang) - 1.0
    sin = jnp.sin(ang)
    cos_m, sin_m = 1.0 + cos_m1 * rope[2:3, :], sin * rope[1:2, :]
    cos_s, sin_s = 1.0 + cos_m1 * rope[4:5, :], sin * rope[3:4, :]

    cqn = (_rms(z_lat[:, C_CQ:C_CKV], MLA_Q_RANK) * g_cq_ref[...]).astype(BF16)
    ckvn = (_rms(z_lat[:, C_CKV:C_KR], MLA_KV_RANK) * g_ckv_ref[...]).astype(BF16)
    q = _dot(cqn, w_uq_ref[...])
    gq_m = gq_m_ref[...]
    for hd in range(MLA_HEADS):
        qh = _rms(q[:, hd * LANES:(hd + 1) * LANES], MLA_QK) * gq_m
        qh = _rope(qh, cos_m, sin_m) * (MLA_QK ** -0.5 * LOG2E)
        qm_ref[:, hd * LANES:(hd + 1) * LANES] = qh.astype(BF16)

    kn = _dot(ckvn, w_k_ref[...])
    vm_ref[...] = _dot(ckvn, w_v_ref[...]).astype(BF16)
    z_kv = proj(C_KS, C_GA)

    gq_s = gq_s_ref[...]
    for hd in range(SWA_HEADS):
        qh = _rms(z_qs[:, hd * LANES:(hd + 1) * LANES], SWA_HD) * gq_s
        qh = _rope(qh, cos_s, sin_s) * (SWA_HD ** -0.5 * LOG2E)
        qw_ref[:, hd * LANES:(hd + 1) * LANES] = qh.astype(BF16)

    gk_m = gk_m_ref[...]
    kr = z_lat[:, C_KR:C_QS]
    ss_kr = jnp.sum(kr * kr, axis=-1, keepdims=True)
    kr_rot = _rope(kr * gk_m, cos_m, sin_m)
    for hd in range(MLA_HEADS):
        kh = kn[:, hd * LANES:(hd + 1) * LANES]
        ss = jnp.sum(kh * kh, axis=-1, keepdims=True) + ss_kr
        sc = lax.rsqrt(ss * (1.0 / MLA_QK) + EPS)
        km_ref[:, hd * LANES:(hd + 1) * LANES] = ((kh * gk_m + kr_rot) * sc).astype(BF16)

    z_gb = proj(C_GA + half, C_END)

    gk_s = gk_s_ref[...]
    for hd in range(SWA_KV_HEADS):
        kh = _rms(z_kv[:, hd * LANES:(hd + 1) * LANES], SWA_HD) * gk_s
        kh = _rope(kh, cos_s, sin_s)
        kw_ref[:, hd * LANES:(hd + 1) * LANES] = kh.astype(BF16)
    vw_ref[...] = z_kv[:, C_VS - C_KS:].astype(BF16)

    gate_ref[:, :half] = jax.nn.sigmoid(z_ga + b_gate_ref[:, :half]).astype(BF16)
    gate_ref[:, half:] = jax.nn.sigmoid(z_gb + b_gate_ref[:, half:]).astype(BF16)


def _pre_attention(x2d, pos2d, w_all, w_uq, w_k, w_v, g_mix, g_cq, g_ckv, gq_m, gk_m, gq_s,
                   gk_s, b_gate, rope_tab, tm):
    T = x2d.shape[0]
    row = lambda n: pl.BlockSpec((tm, n), lambda i: (i, 0))
    outs = [(MLA_HEADS * LANES, BF16), (MLA_HEADS * LANES, BF16), (MLA_HEADS * MLA_V, BF16),
            (SWA_HEADS * LANES, BF16), (SWA_KV_HEADS * LANES, BF16),
            (SWA_KV_HEADS * LANES, BF16), (2 * D_MODEL, BF16)]
    consts = [w_all, w_uq, w_k, w_v, g_mix, g_cq, g_ckv, gq_m, gk_m, gq_s, gk_s, b_gate, rope_tab]
    return pl.pallas_call(
        _pre_kernel,
        grid=(T // tm,),
        in_specs=[row(D_MODEL), row(1)] + [_full(c.shape) for c in consts],
        out_specs=[row(n) for n, _ in outs],
        out_shape=[jax.ShapeDtypeStruct((T, n), dt) for n, dt in outs],
        compiler_params=pltpu.CompilerParams(dimension_semantics=("arbitrary",),
                                             vmem_limit_bytes=VMEM_LIMIT),
        name="pre_attention",
    )(x2d, pos2d, *consts)


def _half_masks(dtype):
    lane = lax.broadcasted_iota(jnp.int32, (1, LANES), 1)
    lo = (lane < LANES // 2).astype(dtype)
    return lo, 1 - lo


def _mla_exact(q_ref, k_ref, v_ref, o_ref):
    v = v_ref[...]
    masks = _half_masks(v.dtype)
    acc = None
    for hh in range(2):
        q = q_ref[:, hh * LANES:(hh + 1) * LANES]
        k = k_ref[:, hh * LANES:(hh + 1) * LANES]
        s = _dot_nt(q, k)
        m = jnp.max(s, axis=-1, keepdims=True)
        p = jnp.exp2(s - m)
        l = jnp.sum(p, axis=-1, keepdims=True)
        o = _dot(p.astype(BF16), v * masks[hh]) / l
        acc = o if acc is None else acc + o
    o_ref[...] = acc.astype(o_ref.dtype)


def _mla_kernel(q_ref, k_ref, v_ref, wa_ref, wb_ref, o_ref, wa_out, wb_out, kmax_ref):
    S = k_ref.shape[0]
    tq = q_ref.shape[0]

    @pl.when(pl.program_id(2) == 0)
    def _():
        for hh in range(2):
            k = k_ref[:, hh * LANES:(hh + 1) * LANES].astype(F32)
            kn2 = jnp.max(jnp.sum(k * k, axis=-1, keepdims=True), axis=0, keepdims=True)
            kmax_ref[hh] = jnp.broadcast_to(jnp.sqrt(kn2), kmax_ref.shape[1:])

    wa_out[...] = wa_ref[...].astype(BF16)
    wb_out[...] = wb_ref[...].astype(BF16)

    masks = _half_masks(BF16)
    acc = None
    lmin = None
    for hh in range(2):
        q = q_ref[:, hh * LANES:(hh + 1) * LANES]
        qf = q.astype(F32)
        bound = jnp.sqrt(jnp.sum(qf * qf, axis=-1, keepdims=True)) * kmax_ref[hh][0:1, 0:1]
        v = v_ref[...] * masks[hh]
        o = jnp.zeros((tq, LANES), F32)
        lsum = jnp.zeros((tq, LANES), F32)
        for j in range(S // MLA_KV_TILE):
            rows = slice(j * MLA_KV_TILE, (j + 1) * MLA_KV_TILE)
            p = jnp.exp2(_dot_nt(q, k_ref[rows, hh * LANES:(hh + 1) * LANES]) - bound)
            for t in range(MLA_KV_TILE // LANES):
                lsum = lsum + p[:, t * LANES:(t + 1) * LANES]
            o = o + _dot(p.astype(BF16), v[rows, :])
        l = jnp.sum(lsum, axis=-1, keepdims=True)
        o = o / l
        acc = o if acc is None else acc + o
        lm = jnp.min(l)
        lmin = lm if lmin is None else jnp.minimum(lmin, lm)
    o_ref[...] = acc.astype(o_ref.dtype)

    @pl.when(jnp.logical_not(lmin > SOFTMAX_UNDERFLOW_GUARD))
    def _():
        _mla_exact(q_ref, k_ref, v_ref, o_ref)


def _mla_attention(qm, km, vm, w_a, w_b, B, S, tq):
    pairs = MLA_HEADS // 2
    q3 = qm.reshape(B, S, MLA_HEADS * LANES)
    k3 = km.reshape(B, S, MLA_HEADS * LANES)
    v3 = vm.reshape(B, S, MLA_HEADS * MLA_V)
    steps = B * pairs * (S // tq)
    wa3 = w_a.reshape(steps, -1, LANES)
    wb3 = w_b.reshape(steps, -1, LANES)
    step = lambda b, p, i: ((b * pairs + p) * (S // tq) + i, 0, 0)
    w_spec = lambda w: pl.BlockSpec((None,) + w.shape[1:], step)
    om, wa_bf, wb_bf = pl.pallas_call(
        _mla_kernel,
        grid=(B, pairs, S // tq),
        in_specs=[pl.BlockSpec((None, tq, 2 * LANES), lambda b, p, i: (b, i, p)),
                  pl.BlockSpec((None, S, 2 * LANES), lambda b, p, i: (b, 0, p)),
                  pl.BlockSpec((None, S, LANES), lambda b, p, i: (b, 0, p)),
                  w_spec(wa3), w_spec(wb3)],
        out_specs=[pl.BlockSpec((None, tq, LANES), lambda b, p, i: (b, i, p)),
                   w_spec(wa3), w_spec(wb3)],
        out_shape=[jax.ShapeDtypeStruct((B, S, MLA_HEADS * MLA_V), BF16),
                   jax.ShapeDtypeStruct(wa3.shape, BF16),
                   jax.ShapeDtypeStruct(wb3.shape, BF16)],
        scratch_shapes=[pltpu.VMEM((2, 8, LANES), F32)],
        compiler_params=pltpu.CompilerParams(
            dimension_semantics=("arbitrary", "arbitrary", "arbitrary"),
            vmem_limit_bytes=VMEM_LIMIT),
        name="mla_attention",
    )(q3, k3, v3, wa3, wb3)
    return om, wa_bf.reshape(w_a.shape), wb_bf.reshape(w_b.shape)


def _swa_kernel(sink_ref, q_ref, k_ref, v_ref, o_ref, *, tq, S):
    hk = pl.program_id(1)
    i = pl.program_id(2)
    tk = SWA_BLOCK + 2 * WINDOW
    row = lax.broadcasted_iota(jnp.int32, (SWA_GROUP * SWA_BLOCK, 1), 0)
    qoff = row & (SWA_BLOCK - 1)
    head = row // SWA_BLOCK
    sk = jnp.zeros((SWA_GROUP * SWA_BLOCK, 1), F32)
    for g in range(SWA_GROUP):
        sk = jnp.where(head == g, sink_ref[SWA_GROUP * hk + g] * LOG2E, sk)
    low_half = lax.broadcasted_iota(jnp.int32, (1, LANES), 1) < LANES // 2
    for sub in range(tq // SWA_BLOCK):
        rows = slice(sub * SWA_BLOCK, (sub + 1) * SWA_BLOCK)
        q0 = i * tq + sub * SWA_BLOCK
        kstart = pl.multiple_of(jnp.clip(q0 - WINDOW, 0, S - tk), WINDOW)
        k = k_ref[pl.ds(kstart, tk), :]
        v = v_ref[pl.ds(kstart, tk), :]
        q = jnp.concatenate([q_ref[rows, g * LANES:(g + 1) * LANES] for g in range(SWA_GROUP)], axis=0)
        kpos = kstart + lax.broadcasted_iota(jnp.int32, (1, tk), 1)
        valid = jnp.abs(kpos - (q0 + qoff[:SWA_BLOCK])) <= WINDOW
        s = _dot_nt(q, k).reshape(SWA_GROUP, SWA_BLOCK, tk)
        s = jnp.where(valid[None], s, NEG_INF).reshape(SWA_GROUP * SWA_BLOCK, tk)
        m = jnp.maximum(jnp.max(s, axis=-1, keepdims=True), sk)
        e = jnp.exp2(s - m)
        denom = jnp.sum(e, axis=-1, keepdims=True) + jnp.exp2(sk - m)
        o = _dot(e.astype(BF16), v) / denom
        for j in range(SWA_GROUP // 2):
            even = o[(2 * j) * SWA_BLOCK:(2 * j + 1) * SWA_BLOCK, :]
            odd = o[(2 * j + 1) * SWA_BLOCK:(2 * j + 2) * SWA_BLOCK, :]
            o_ref[rows, j * LANES:(j + 1) * LANES] = jnp.where(low_half, even, odd).astype(o_ref.dtype)


def _swa_attention(qw, kw, vw, sink, B, S, tq):
    q3 = qw.reshape(B, S, SWA_HEADS * LANES)
    k3 = kw.reshape(B, S, SWA_KV_HEADS * LANES)
    v3 = vw.reshape(B, S, SWA_KV_HEADS * LANES)
    return pl.pallas_call(
        functools.partial(_swa_kernel, tq=tq, S=S),
        grid=(B, SWA_KV_HEADS, S // tq),
        in_specs=[pl.BlockSpec(memory_space=pltpu.SMEM),
                  pl.BlockSpec((None, tq, SWA_GROUP * LANES), lambda b, h, i: (b, i, h)),
                  pl.BlockSpec((None, S, LANES), lambda b, h, i: (b, 0, h)),
                  pl.BlockSpec((None, S, LANES), lambda b, h, i: (b, 0, h))],
        out_specs=pl.BlockSpec((None, tq, SWA_GROUP * SWA_HD), lambda b, h, i: (b, i, h)),
        out_shape=jax.ShapeDtypeStruct((B, S, SWA_HEADS * SWA_HD), BF16),
        compiler_params=pltpu.CompilerParams(
            dimension_semantics=("arbitrary", "arbitrary", "arbitrary"),
            vmem_limit_bytes=VMEM_LIMIT),
        name="swa_attention",
    )(sink, q3, k3, v3)


def _beats(vj, vi, j_first):
    return (vj >= vi) if j_first else (vj > vi)


def _route(scores, sel):
    G, P = N_GROUPS, EXPERTS_PER_GROUP
    groups = [sel[g * P:(g + 1) * P, :] for g in range(G)]
    row = lax.broadcasted_iota(jnp.int32, (P, 1), 0)
    gscore = []
    for vg in groups:
        m1 = jnp.max(vg, axis=0, keepdims=True)
        first = jnp.min(jnp.where(vg == m1, row, P), axis=0, keepdims=True)
        m2 = jnp.max(jnp.where(row == first, -jnp.inf, vg), axis=0, keepdims=True)
        gscore.append(m1 + m2)
    masked = []
    for g in range(G):
        rank = jnp.zeros_like(gscore[g], dtype=jnp.int32)
        for g2 in range(G):
            if g2 != g:
                rank = rank + _beats(gscore[g2], gscore[g], g2 < g).astype(jnp.int32)
        masked.append(jnp.where(rank < TOPK_GROUPS, groups[g], NEG_INF))
    index = [row + g * P for g in range(G)]
    chosen = [None] * G
    for _ in range(TOP_K):
        best = masked[0]
        for g in range(1, G):
            best = jnp.maximum(best, masked[g])
        best = jnp.max(best, axis=0, keepdims=True)
        first = jnp.where(masked[0] == best, index[0], N_EXPERTS)
        for g in range(1, G):
            first = jnp.minimum(first, jnp.where(masked[g] == best, index[g], N_EXPERTS))
        first = jnp.min(first, axis=0, keepdims=True)
        for g in range(G):
            hit = index[g] == first
            chosen[g] = hit if chosen[g] is None else (chosen[g] | hit)
            masked[g] = jnp.where(hit, -jnp.inf, masked[g])
    picked = [jnp.where(chosen[g], scores[g * P:(g + 1) * P, :], 0.0) for g in range(G)]
    total = picked[0]
    for g in range(1, G):
        total = total + picked[g]
    denom = jnp.sum(total, axis=0, keepdims=True)
    return [pk / denom * ROUTED_SCALE for pk in picked]


def _pack_pair(lo, hi):
    return pltpu.pack_elementwise([lo, hi], packed_dtype=BF16)


def _unpack_pair(word):
    lo = pltpu.unpack_elementwise(word, index=0, packed_dtype=BF16, unpacked_dtype=F32)
    hi = pltpu.unpack_elementwise(word, index=1, packed_dtype=BF16, unpacked_dtype=F32)
    return lo, hi


def _store_rows_dense(ref, words):
    for j in range(ROW_SUB):
        ref[pl.ds(j, words.shape[0], stride=ROW_SUB), :] = words[:, j * LANES:(j + 1) * LANES]


def _load_rows_dense(ref, rows):
    sub = ROW_SUB
    return jnp.concatenate([ref[pl.ds(j, rows, stride=sub), :] for j in range(sub)], axis=1)


def _post_kernel(om_ref, ow_ref, gate_ref, x_ref, wbm_ref, wbw_ref, wout_ref, g_moe_ref, wr_ref,
                 rb_ref, wsgu_ref, wsd_ref, x1_ref, hp_ref, comb_ref):
    am = _dot(om_ref[...], wbm_ref[...])
    aw = _dot(ow_ref[...], wbw_ref[...])
    gates = gate_ref[...].astype(F32)
    merged = gates[:, :D_MODEL] * am + gates[:, D_MODEL:] * aw
    x1 = x_ref[...] + _dot(merged.astype(BF16), wout_ref[...])

    h2 = _rms(x1, D_MODEL) * g_moe_ref[...]
    h2b = h2.astype(BF16)
    _store_rows_dense(hp_ref, _pack_pair(h2[:, :ROW_WORDS], h2[:, ROW_WORDS:]))

    wr = wr_ref[...]
    w_hi = wr.astype(BF16)
    w_lo = (wr - w_hi.astype(F32)).astype(BF16)
    h_lo = (h2 - h2b.astype(F32)).astype(BF16)
    by_hi = _dot_nt(jnp.concatenate([w_hi, w_lo], axis=0), h2b)
    logits = by_hi[:N_EXPERTS] + by_hi[N_EXPERTS:] + _dot_nt(w_hi, h_lo)

    sgu = _dot(h2b, wsgu_ref[...])
    sh = jax.nn.silu(sgu[:, :SHARED_FF]) * sgu[:, SHARED_FF:]
    x1_ref[...] = x1 + _dot(sh.astype(BF16), wsd_ref[...])

    scores = jax.nn.sigmoid(logits)
    comb = _route(scores, scores + rb_ref[...])
    for g in range(N_GROUPS):
        comb_ref[g * EXPERTS_PER_GROUP:(g + 1) * EXPERTS_PER_GROUP, :] = comb[g]


def _post_attention(om, ow, gates, x2d, wbm, wbw, wout, g_moe, wr_t, rbias, wsgu, wsd, tm):
    T = x2d.shape[0]
    row = lambda n: pl.BlockSpec((tm, n), lambda i: (i, 0))
    consts = [wbm, wbw, wout, g_moe, wr_t, rbias, wsgu, wsd]
    return pl.pallas_call(
        _post_kernel,
        grid=(T // tm,),
        in_specs=[row(om.shape[1]), row(ow.shape[1]), row(2 * D_MODEL), row(D_MODEL)]
        + [_full(c.shape) for c in consts],
        out_specs=[row(D_MODEL), pl.BlockSpec((tm * ROW_SUB, LANES), lambda i: (i, 0)),
                   pl.BlockSpec((N_EXPERTS, tm), lambda i: (0, i))],
        out_shape=[jax.ShapeDtypeStruct((T, D_MODEL), F32),
                   jax.ShapeDtypeStruct((T * ROW_SUB, LANES), jnp.uint32),
                   jax.ShapeDtypeStruct((N_EXPERTS, T), F32)],
        compiler_params=pltpu.CompilerParams(dimension_semantics=("arbitrary",),
                                             vmem_limit_bytes=VMEM_LIMIT),
        name="post_attention",
    )(om, ow, gates, x2d, *consts)


SLOT_ROWS = -(-(MOE_CHUNK * TOP_K + N_EXPERTS * (SLOT_ALIGN - 1) + MOE_TILE_BIG) // MOE_TILE) * MOE_TILE
DUMMY_SLOT = SLOT_ROWS - 1
PLAN_BLOCK = 256


def _plan_kernel(comb_ref, slot_ref, w_ref, off_ref, cnt_ref):
    comb = comb_ref[...]
    sel = comb > 0.0
    m = sel.astype(F32)
    mb = m.astype(BF16)
    r_i = lax.broadcasted_iota(jnp.int32, (PLAN_BLOCK, PLAN_BLOCK), 0)
    c_i = lax.broadcasted_iota(jnp.int32, (PLAN_BLOCK, PLAN_BLOCK), 1)
    before = (r_i < c_i).astype(BF16)
    carry = jnp.zeros((N_EXPERTS, 1), F32)
    ranks = []
    for b in range(MOE_CHUNK // PLAN_BLOCK):
        blk = slice(b * PLAN_BLOCK, (b + 1) * PLAN_BLOCK)
        ranks.append(_dot(mb[:, blk], before) + carry)
        carry = carry + jnp.sum(m[:, blk], axis=1, keepdims=True)
    rank = jnp.concatenate(ranks, axis=1)
    cnt = carry
    cnt_pad = jnp.floor((cnt + (SLOT_ALIGN - 1)) * (1.0 / SLOT_ALIGN)) * SLOT_ALIGN
    e_r = lax.broadcasted_iota(jnp.int32, (N_EXPERTS, N_EXPERTS), 0)
    e_c = lax.broadcasted_iota(jnp.int32, (N_EXPERTS, N_EXPERTS), 1)
    below = (e_c < e_r).astype(F32)
    off = jnp.dot(below, jnp.broadcast_to(cnt_pad, (N_EXPERTS, LANES)),
                  precision=lax.Precision.HIGHEST, preferred_element_type=F32)
    slot = off[:, :1] + rank
    kidx = _dot(below.astype(BF16), mb)
    row = lax.broadcasted_iota(jnp.int32, (TOP_K, 1), 0)
    slot_acc = jnp.zeros((TOP_K, MOE_CHUNK), F32)
    w_acc = jnp.zeros((TOP_K, MOE_CHUNK), F32)
    for k in range(TOP_K):
        pick = jnp.where(sel & (kidx == k), 1.0, 0.0)
        found = jnp.sum(pick, axis=0, keepdims=True) > 0.0
        s_k = jnp.where(found, jnp.sum(pick * slot, axis=0, keepdims=True), float(DUMMY_SLOT))
        w_k = jnp.sum(pick * comb, axis=0, keepdims=True)
        slot_acc = jnp.where(row == k, s_k, slot_acc)
        w_acc = jnp.where(row == k, w_k, w_acc)
    slot_ref[...] = slot_acc.astype(jnp.int32) * ROW_SUB
    w_ref[...] = w_acc
    off_ref[...] = off.astype(jnp.int32)
    cnt_ref[...] = jnp.broadcast_to(cnt, (N_EXPERTS, LANES)).astype(jnp.int32)


def _moe_plan(comb_t):
    T = comb_t.shape[1]
    nch = T // MOE_CHUNK
    per_pair = pl.BlockSpec((None, TOP_K, MOE_CHUNK), lambda c: (c, 0, 0))
    per_expert = pl.BlockSpec((None, N_EXPERTS, LANES), lambda c: (c, 0, 0))
    return pl.pallas_call(
        _plan_kernel,
        grid=(nch,),
        in_specs=[pl.BlockSpec((N_EXPERTS, MOE_CHUNK), lambda c: (0, c))],
        out_specs=[per_pair, per_pair, per_expert, per_expert],
        out_shape=[jax.ShapeDtypeStruct((nch, TOP_K, MOE_CHUNK), jnp.int32),
                   jax.ShapeDtypeStruct((nch, TOP_K, MOE_CHUNK), F32),
                   jax.ShapeDtypeStruct((nch, N_EXPERTS, LANES), jnp.int32),
                   jax.ShapeDtypeStruct((nch, N_EXPERTS, LANES), jnp.int32)],
        compiler_params=pltpu.CompilerParams(dimension_semantics=("arbitrary",),
                                             vmem_limit_bytes=VMEM_LIMIT),
        name="moe_plan",
    )(comb_t)


def _slab_at(ref, first):
    return ref.at[pl.ds(pl.multiple_of(first, ROW_SUB), ROW_SUB), :]


def _slab(ref, row):
    return _slab_at(ref, row * ROW_SUB)


def _moe_kernel(off_ref, cnt_ref, slot_hbm, w_hbm, hp_ref, wgu_ref, wd_ref, o_ref,
                buf, clo, chi, *smem_and_sem):
    slot_s = smem_and_sem[:TOP_K]
    w_s = smem_and_sem[TOP_K:2 * TOP_K]
    sem = smem_and_sem[2 * TOP_K]
    c = pl.program_id(0)
    s = pl.program_id(1)

    @pl.when(s == 0)
    def _dispatch():
        copies = [pltpu.make_async_copy(slot_hbm.at[c, k], slot_s[k], sem.at[k]) for k in range(TOP_K)]
        copies += [pltpu.make_async_copy(w_hbm.at[c, k], w_s[k], sem.at[TOP_K + k]) for k in range(TOP_K)]
        for cp in copies:
            cp.start()

        @pl.when(c == 0)
        def _():
            buf[...] = jnp.zeros_like(buf)

        for cp in copies:
            cp.wait()

        def scatter(tb, carry):
            for tt in range(8):
                t = tb * 8 + tt
                slab = _slab(hp_ref, t)[...]
                for k in range(TOP_K):
                    _slab_at(buf, slot_s[k][t])[...] = slab
            return carry

        lax.fori_loop(0, MOE_CHUNK // 8, scatter, 0)

    def expert(ee):
        e = s * EXPERTS_PER_STEP + ee
        n = cnt_ref[c, e]
        off = off_ref[c, e]

        def ffn(start, rows):
            view = buf.at[pl.ds(pl.multiple_of((off + start) * ROW_SUB, SLOT_ALIGN * ROW_SUB),
                                rows * ROW_SUB), :]
            x_lo, x_hi = _unpack_pair(_load_rows_dense(view, rows))
            gu = (_dot(x_lo.astype(BF16), wgu_ref[ee, :ROW_WORDS, :])
                  + _dot(x_hi.astype(BF16), wgu_ref[ee, ROW_WORDS:, :]))
            hid = jax.nn.silu(gu[:, :EXPERT_FF]) * gu[:, EXPERT_FF:]
            y = _dot(hid.astype(BF16), wd_ref[ee])
            mine = lax.broadcasted_iota(jnp.int32, (rows, 1), 0) < (n - start)
            _store_rows_dense(view, _pack_pair(jnp.where(mine, y[:, :ROW_WORDS], x_lo),
                                               jnp.where(mine, y[:, ROW_WORDS:], x_hi)))

        n_big = (n + MOE_TILE_BIG - MOE_TILE - 1) // MOE_TILE_BIG

        def big(r, carry):
            ffn(r * MOE_TILE_BIG, MOE_TILE_BIG)
            return carry

        lax.fori_loop(0, n_big, big, 0)

        @pl.when(n > n_big * MOE_TILE_BIG)
        def _():
            ffn(n_big * MOE_TILE_BIG, MOE_TILE)

    @pl.when(s < EXPERT_STEPS)
    def _experts():
        for ee in range(EXPERTS_PER_STEP):
            expert(ee)

    @pl.when(s >= EXPERT_STEPS)
    def _combine():
        t0 = (s - EXPERT_STEPS) * COMBINE_BLOCK

        def gather(i, carry):
            for tt in range(4):
                tl = i * 4 + tt
                t = t0 + tl
                acc_lo = jnp.zeros((ROW_SUB, LANES), F32)
                acc_hi = jnp.zeros((ROW_SUB, LANES), F32)
                for k in range(TOP_K):
                    lo, hi = _unpack_pair(_slab_at(buf, slot_s[k][t])[...])
                    wk = w_s[k][t]
                    acc_lo = acc_lo + wk * lo
                    acc_hi = acc_hi + wk * hi
                _slab(clo, tl)[...] = acc_lo
                _slab(chi, tl)[...] = acc_hi
            return carry

        lax.fori_loop(0, COMBINE_BLOCK // 4, gather, 0)
        o_ref[:, :ROW_WORDS] = _load_rows_dense(clo, COMBINE_BLOCK)
        o_ref[:, ROW_WORDS:] = _load_rows_dense(chi, COMBINE_BLOCK)


def _moe_sparse(offs, cnts, slots, wts, hp, w_gu, w_d):
    nch = offs.shape[0]
    T = nch * MOE_CHUNK
    blocks = MOE_CHUNK // COMBINE_BLOCK
    expert = lambda c, s, *_: (jnp.minimum(s, EXPERT_STEPS - 1), 0, 0)
    grid_spec = pltpu.PrefetchScalarGridSpec(
        num_scalar_prefetch=2,
        grid=(nch, EXPERT_STEPS + blocks),
        in_specs=[pl.BlockSpec(memory_space=pl.ANY),
                  pl.BlockSpec(memory_space=pl.ANY),
                  pl.BlockSpec((MOE_CHUNK * ROW_SUB, LANES), lambda c, s, *_: (c, 0),
                               pipeline_mode=pl.Buffered(1)),
                  pl.BlockSpec((EXPERTS_PER_STEP, D_MODEL, 2 * EXPERT_FF), expert),
                  pl.BlockSpec((EXPERTS_PER_STEP, EXPERT_FF, D_MODEL), expert)],
        out_specs=pl.BlockSpec(
            (COMBINE_BLOCK, D_MODEL),
            lambda c, s, *_: (c * blocks + jnp.maximum(s - EXPERT_STEPS, 0), 0)),
        scratch_shapes=[pltpu.VMEM((SLOT_ROWS * ROW_SUB, LANES), jnp.uint32),
                        pltpu.VMEM((COMBINE_BLOCK * ROW_SUB, LANES), F32),
                        pltpu.VMEM((COMBINE_BLOCK * ROW_SUB, LANES), F32),
                        *[pltpu.SMEM((MOE_CHUNK,), jnp.int32) for _ in range(TOP_K)],
                        *[pltpu.SMEM((MOE_CHUNK,), F32) for _ in range(TOP_K)],
                        pltpu.SemaphoreType.DMA((2 * TOP_K,))])
    return pl.pallas_call(
        _moe_kernel,
        grid_spec=grid_spec,
        out_shape=jax.ShapeDtypeStruct((T, D_MODEL), F32),
        compiler_params=pltpu.CompilerParams(dimension_semantics=("arbitrary", "arbitrary"),
                                             vmem_limit_bytes=VMEM_LIMIT),
        name="moe_experts",
    )(offs, cnts, slots, wts, hp, w_gu, w_d)


def _ple_kernel(x1_ref, r_ref, p_ref, g_ref, wg_ref, b_ref, wp_ref, o_ref):
    x2 = x1_ref[...] + r_ref[...]
    hn = (_rms(x2, D_MODEL) * g_ref[...]).astype(BF16)
    gate = jax.nn.sigmoid(_dot(hn, wg_ref[...]) + b_ref[...])
    o_ref[...] = x2 + gate * _dot(p_ref[...].astype(BF16), wp_ref[...])


def _ple(x1s, routed, p2d, g_ple, wg, b_ple, wp, tm):
    T = x1s.shape[0]
    row = lambda n: pl.BlockSpec((tm, n), lambda i: (i, 0))
    consts = [g_ple, wg, b_ple, wp]
    return pl.pallas_call(
        _ple_kernel,
        grid=(T // tm,),
        in_specs=[row(D_MODEL), row(D_MODEL), row(PLE_DIM)] + [_full(c.shape) for c in consts],
        out_specs=row(D_MODEL),
        out_shape=jax.ShapeDtypeStruct((T, D_MODEL), F32),
        compiler_params=pltpu.CompilerParams(dimension_semantics=("arbitrary",),
                                             vmem_limit_bytes=VMEM_LIMIT),
        name="ple",
    )(x1s, routed, p2d, *consts)


def _lane_map(*runs):
    src = np.full((LANES,), -1)
    for lane, dim, n in runs:
        src[lane:lane + n] = np.arange(dim, dim + n)
    return src


_MLA_HALF = MLA_ROPE // 2
_SWA_HALF = SWA_HD // 2
MLA_LANES = _lane_map((0, MLA_NOPE, _MLA_HALF), (_MLA_HALF, 0, LANES // 2 - _MLA_HALF),
                      (LANES // 2, MLA_NOPE + _MLA_HALF, _MLA_HALF),
                      (LANES // 2 + _MLA_HALF, LANES // 2 - _MLA_HALF, MLA_NOPE - LANES // 2 + _MLA_HALF))
MLA_NOPE_LANES = np.where(MLA_LANES < MLA_NOPE, MLA_LANES, -1)
MLA_ROPE_LANES = np.where(MLA_LANES >= MLA_NOPE, MLA_LANES - MLA_NOPE, -1)
SWA_LANES = _lane_map((LANES // 2 - _SWA_HALF, 0, _SWA_HALF), (LANES - _SWA_HALF, _SWA_HALF, _SWA_HALF))


def _spread(w, heads, lane_src):
    k = w.shape[0]
    dim = w.shape[1] // heads
    w = jnp.pad(w.reshape(k, heads, dim), ((0, 0), (0, 0), (0, 1)))
    return w[:, :, np.where(lane_src < 0, dim, lane_src)].reshape(k, heads * LANES)


def _rope_table():
    def inv_freq(dim):
        return 1.0 / (ROPE_THETA ** (jnp.arange(0, dim, 2, dtype=F32) / dim))

    def selector(lane_src, half):
        sel = np.where(lane_src < 0, 0.0, np.where(lane_src < half, -1.0, 1.0))
        return jnp.asarray(sel, F32)

    sel_m = selector(MLA_ROPE_LANES, _MLA_HALF)
    sel_s = selector(SWA_LANES, _SWA_HALF)
    freq_m = _spread(jnp.tile(inv_freq(MLA_ROPE), 2)[None], 1, MLA_ROPE_LANES)[0]
    freq_s = _spread(jnp.tile(inv_freq(SWA_HD), 2)[None], 1, SWA_LANES)[0]
    zero = jnp.zeros((LANES,), F32)
    rows = [freq_m + freq_s, sel_m, jnp.abs(sel_m), sel_s, jnp.abs(sel_s), zero, zero, zero]
    return jnp.stack(rows)


def _layer(x2d, p2d, pos2d, B, S, g_mix, w_in, b_gate, g_cq, w_uq, g_ckv, w_ukv, g_qn_mla, g_kn_mla,
           g_qn_swa, g_kn_swa, sink, w_br_mla, w_br_swa, w_out, g_moe, w_router, router_bias,
           w_exp_gu, w_exp_down, w_sh_gu, w_sh_down, g_ple, w_ple_gate, b_ple, w_ple_proj):
    w_kr = _spread(w_in[:, OFF_CKV:OFF_KR], 1, MLA_ROPE_LANES)
    w_vs = w_in[:, OFF_KS:OFF_VS].reshape(D_MODEL, SWA_KV_HEADS, 1, SWA_HD)
    w_vs = jnp.broadcast_to(w_vs, (D_MODEL, SWA_KV_HEADS, 2, SWA_HD)).reshape(D_MODEL, -1)
    w_all = jnp.concatenate([
        w_in[:, :OFF_CKV], w_kr,
        _spread(w_in[:, OFF_KR:OFF_QS], SWA_HEADS, SWA_LANES),
        _spread(w_in[:, OFF_QS:OFF_KS], SWA_KV_HEADS, SWA_LANES),
        w_vs, w_in[:, OFF_VS:]], axis=1).astype(BF16)
    assert w_all.shape[1] == C_END
    w_uq_p = _spread(w_uq, MLA_HEADS, MLA_LANES).astype(BF16)
    w_ukv3 = w_ukv.reshape(MLA_KV_RANK, MLA_HEADS, MLA_NOPE + MLA_V)
    w_k = _spread(w_ukv3[:, :, :MLA_NOPE].reshape(MLA_KV_RANK, -1), MLA_HEADS, MLA_NOPE_LANES).astype(BF16)
    w_v = w_ukv3[:, :, MLA_NOPE:].reshape(MLA_KV_RANK, -1).astype(BF16)

    qm, km, vm, qw, kw, vw, gates = _pre_attention(
        x2d, pos2d, w_all, w_uq_p, w_k, w_v, g_mix[None], g_cq[None], g_ckv[None],
        _spread(g_qn_mla[None], 1, MLA_LANES), _spread(g_kn_mla[None], 1, MLA_LANES),
        _spread(g_qn_swa[None], 1, SWA_LANES), _spread(g_kn_swa[None], 1, SWA_LANES),
        b_gate[None], _rope_table(), tm=PRE_TILE)

    om, w_gu_bf, w_down_bf = _mla_attention(qm, km, vm, w_exp_gu, w_exp_down, B, S, tq=MLA_Q_TILE)
    om = om.reshape(B * S, -1)
    ow = _swa_attention(qw, kw, vw, sink, B, S, tq=SWA_Q_TILE).reshape(B * S, -1)

    x1s, hp, comb_t = _post_attention(
        om, ow, gates, x2d, w_br_mla.astype(BF16), w_br_swa.astype(BF16), w_out.astype(BF16),
        g_moe[None], w_router.T, router_bias[:, None], w_sh_gu.astype(BF16),
        w_sh_down.astype(BF16), tm=ROW_TILE)

    slots, wts, offs, cnts = _moe_plan(comb_t)
    routed = _moe_sparse(offs[:, :, 0], cnts[:, :, 0], slots, wts, hp, w_gu_bf, w_down_bf)

    return _ple(x1s, routed, p2d, g_ple[None], w_ple_gate.astype(BF16), b_ple[None],
                w_ple_proj.astype(BF16), tm=ROW_TILE)


def kernel(x, p, positions, g_mix, w_in, b_gate, g_cq, w_uq, g_ckv, w_ukv, g_qn_mla, g_kn_mla, g_qn_swa, g_kn_swa, sink, w_br_mla, w_br_swa, w_out, g_moe, w_router, router_bias, w_exp_gu, w_exp_down, w_sh_gu, w_sh_down, g_ple, w_ple_gate, b_ple, w_ple_proj):
    B, S, D = x.shape
    x2d = x.reshape(B * S, D)
    pos2d = positions.reshape(B * S, 1)
    for i in range(p.shape[0]):
        x2d = _layer(x2d, p[i].reshape(B * S, -1), pos2d, B, S, g_mix[i], w_in[i], b_gate[i],
                     g_cq[i], w_uq[i], g_ckv[i], w_ukv[i], g_qn_mla[i], g_kn_mla[i], g_qn_swa[i],
                     g_kn_swa[i], sink[i], w_br_mla[i], w_br_swa[i], w_out[i], g_moe[i],
                     w_router[i], router_bias[i], w_exp_gu[i], w_exp_down[i], w_sh_gu[i],
                     w_sh_down[i], g_ple[i], w_ple_gate[i], b_ple[i], w_ple_proj[i])
    return x2d.reshape(B, S, D)
```

```python
import functools

import jax
import jax.numpy as jnp
import numpy as np
from jax import lax
from jax.experimental import pallas as pl
from jax.experimental.pallas import tpu as pltpu

D_MODEL = 1024
PLE_DIM = 256
ROPE_THETA = 10000.0
EPS = 1e-6
NEG_INF = -1e30

MLA_HEADS = 8
MLA_Q_RANK = 384
MLA_KV_RANK = 256
MLA_NOPE = 64
MLA_ROPE = 32
MLA_QK = MLA_NOPE + MLA_ROPE
MLA_V = 64

SWA_HEADS = 8
SWA_KV_HEADS = 2
SWA_GROUP = SWA_HEADS // SWA_KV_HEADS
SWA_HD = 64
WINDOW = 128

OFF_CQ = MLA_Q_RANK
OFF_CKV = OFF_CQ + MLA_KV_RANK
OFF_KR = OFF_CKV + MLA_ROPE
OFF_QS = OFF_KR + SWA_HEADS * SWA_HD
OFF_KS = OFF_QS + SWA_KV_HEADS * SWA_HD
OFF_VS = OFF_KS + SWA_KV_HEADS * SWA_HD
OFF_GA = OFF_VS + D_MODEL

N_EXPERTS = 64
TOP_K = 8
N_GROUPS = 8
TOPK_GROUPS = 4
EXPERTS_PER_GROUP = N_EXPERTS // N_GROUPS
EXPERT_FF = 256
SHARED_FF = 256
ROUTED_SCALE = 2.5

LANES = 128
ROW_WORDS = D_MODEL // 2
ROW_SUB = ROW_WORDS // LANES
MOE_CHUNK = 2048
MOE_TILE = 128
MOE_TILE_BIG = 320
SLOT_ALIGN = 16
COMBINE_BLOCK = 256
EXPERTS_PER_STEP = 4
EXPERT_STEPS = N_EXPERTS // EXPERTS_PER_STEP
PRE_TILE = 256
ROW_TILE = 512
MLA_Q_TILE = 512
MLA_KV_TILE = 256
LOG2E = 1.4426950408889634
SOFTMAX_UNDERFLOW_GUARD = 1e-30
SWA_Q_TILE = 512
SWA_BLOCK = 128
VMEM_LIMIT = 56 * 1024 * 1024

BF16 = jnp.bfloat16
F32 = jnp.float32

C_CQ = 0
C_CKV = C_CQ + MLA_Q_RANK
C_KR = C_CKV + MLA_KV_RANK
C_QS = C_KR + LANES
C_KS = C_QS + SWA_HEADS * LANES
C_VS = C_KS + SWA_KV_HEADS * LANES
C_GA = C_VS + SWA_KV_HEADS * LANES
C_END = C_GA + 2 * D_MODEL


def _full(shape):
    nd = len(shape)
    return pl.BlockSpec(shape, lambda *_: (0,) * nd)


def _dot(a, b):
    return jnp.dot(a, b, preferred_element_type=F32)


def _dot_nt(a, b, precision=None):
    return lax.dot_general(a, b, (((1,), (1,)), ((), ())), precision=precision,
                           preferred_element_type=F32)


def _rms(v, n):
    return v * lax.rsqrt(jnp.sum(v * v, axis=-1, keepdims=True) * (1.0 / n) + EPS)


def _rope(v, cos, sin):
    return v * cos + pltpu.roll(v, LANES // 2, 1) * sin


def _pre_kernel(x_ref, pos_ref, w_all_ref, w_uq_ref, w_k_ref, w_v_ref, g_mix_ref, g_cq_ref,
                g_ckv_ref, gq_m_ref, gk_m_ref, gq_s_ref, gk_s_ref, b_gate_ref, rope_ref,
                qm_ref, km_ref, vm_ref, qw_ref, kw_ref, vw_ref, gate_ref):
    x = x_ref[...]
    h = (_rms(x, D_MODEL) * g_mix_ref[...]).astype(BF16)

    def proj(lo, hi):
        return _dot(h, w_all_ref[:, lo:hi])

    z_lat = proj(C_CQ, C_QS)
    z_qs = proj(C_QS, C_KS)
    half = (C_END - C_GA) // 2
    z_ga = proj(C_GA, C_GA + half)

    pos = pos_ref[...].astype(F32)
    rope = rope_ref[...]
    ang = pos * rope[0:1, :]
    cos_m1 = jnp.cos(ang) - 1.0
    sin = jnp.sin(ang)
    cos_m, sin_m = 1.0 + cos_m1 * rope[2:3, :], sin * rope[1:2, :]
    cos_s, sin_s = 1.0 + cos_m1 * rope[4:5, :], sin * rope[3:4, :]

    cqn = (_rms(z_lat[:, C_CQ:C_CKV], MLA_Q_RANK) * g_cq_ref[...]).astype(BF16)
    ckvn = (_rms(z_lat[:, C_CKV:C_KR], MLA_KV_RANK) * g_ckv_ref[...]).astype(BF16)
    q = _dot(cqn, w_uq_ref[...])
    gq_m = gq_m_ref[...]
    for hd in range(MLA_HEADS):
        qh = _rms(q[:, hd * LANES:(hd + 1) * LANES], MLA_QK) * gq_m
        qh = _rope(qh, cos_m, sin_m) * (MLA_QK ** -0.5 * LOG2E)
        qm_ref[:, hd * LANES:(hd + 1) * LANES] = qh.astype(BF16)

    kn = _dot(ckvn, w_k_ref[...])
    vm_ref[...] = _dot(ckvn, w_v_ref[...]).astype(BF16)
    z_kv = proj(C_KS, C_GA)

    gq_s = gq_s_ref[...]
    for hd in range(SWA_HEADS):
        qh = _rms(z_qs[:, hd * LANES:(hd + 1) * LANES], SWA_HD) * gq_s
        qh = _rope(qh, cos_s, sin_s) * (SWA_HD ** -0.5 * LOG2E)
        qw_ref[:, hd * LANES:(hd + 1) * LANES] = qh.astype(BF16)

    gk_m = gk_m_ref[...]
    kr = z_lat[:, C_KR:C_QS]
    ss_kr = jnp.sum(kr * kr, axis=-1, keepdims=True)
    kr_rot = _rope(kr * gk_m, cos_m, sin_m)
    for hd in range(MLA_HEADS):
        kh = kn[:, hd * LANES:(hd + 1) * LANES]
        ss = jnp.sum(kh * kh, axis=-1, keepdims=True) + ss_kr
        sc = lax.rsqrt(ss * (1.0 / MLA_QK) + EPS)
        km_ref[:, hd * LANES:(hd + 1) * LANES] = ((kh * gk_m + kr_rot) * sc).astype(BF16)

    z_gb = proj(C_GA + half, C_END)

    gk_s = gk_s_ref[...]
    for hd in range(SWA_KV_HEADS):
        kh = _rms(z_kv[:, hd * LANES:(hd + 1) * LANES], SWA_HD) * gk_s
        kh = _rope(kh, cos_s, sin_s)
        kw_ref[:, hd * LANES:(hd + 1) * LANES] = kh.astype(BF16)
    vw_ref[...] = z_kv[:, C_VS - C_KS:].astype(BF16)

    gate_ref[:, :half] = jax.nn.sigmoid(z_ga + b_gate_ref[:, :half]).astype(BF16)
    gate_ref[:, half:] = jax.nn.sigmoid(z_gb + b_gate_ref[:, half:]).astype(BF16)


def _pre_attention(x2d, pos2d, w_all, w_uq, w_k, w_v, g_mix, g_cq, g_ckv, gq_m, gk_m, gq_s,
                   gk_s, b_gate, rope_tab, tm):
    T = x2d.shape[0]
    row = lambda n: pl.BlockSpec((tm, n), lambda i: (i, 0))
    outs = [(MLA_HEADS * LANES, BF16), (MLA_HEADS * LANES, BF16), (MLA_HEADS * MLA_V, BF16),
            (SWA_HEADS * LANES, BF16), (SWA_KV_HEADS * LANES, BF16),
            (SWA_KV_HEADS * LANES, BF16), (2 * D_MODEL, BF16)]
    consts = [w_all, w_uq, w_k, w_v, g_mix, g_cq, g_ckv, gq_m, gk_m, gq_s, gk_s, b_gate, rope_tab]
    return pl.pallas_call(
        _pre_kernel,
        grid=(T // tm,),
        in_specs=[row(D_MODEL), row(1)] + [_full(c.shape) for c in consts],
        out_specs=[row(n) for n, _ in outs],
        out_shape=[jax.ShapeDtypeStruct((T, n), dt) for n, dt in outs],
        compiler_params=pltpu.CompilerParams(dimension_semantics=("arbitrary",),
                                             vmem_limit_bytes=VMEM_LIMIT),
        name="pre_attention",
    )(x2d, pos2d, *consts)


def _half_masks(dtype):
    lane = lax.broadcasted_iota(jnp.int32, (1, LANES), 1)
    lo = (lane < LANES // 2).astype(dtype)
    return lo, 1 - lo


def _mla_exact(q_ref, k_ref, v_ref, o_ref):
    v = v_ref[...]
    masks = _half_masks(v.dtype)
    acc = None
    for hh in range(2):
        q = q_ref[:, hh * LANES:(hh + 1) * LANES]
        k = k_ref[:, hh * LANES:(hh + 1) * LANES]
        s = _dot_nt(q, k)
        m = jnp.max(s, axis=-1, keepdims=True)
        p = jnp.exp2(s - m)
        l = jnp.sum(p, axis=-1, keepdims=True)
        o = _dot(p.astype(BF16), v * masks[hh]) / l
        acc = o if acc is None else acc + o
    o_ref[...] = acc.astype(o_ref.dtype)


def _mla_kernel(q_ref, k_ref, v_ref, wa_ref, wb_ref, o_ref, wa_out, wb_out, kmax_ref):
    S = k_ref.shape[0]
    tq = q_ref.shape[0]

    @pl.when(pl.program_id(2) == 0)
    def _():
        for hh in range(2):
            k = k_ref[:, hh * LANES:(hh + 1) * LANES].astype(F32)
            kn2 = jnp.max(jnp.sum(k * k, axis=-1, keepdims=True), axis=0, keepdims=True)
            kmax_ref[hh] = jnp.broadcast_to(jnp.sqrt(kn2), kmax_ref.shape[1:])

    wa_out[...] = wa_ref[...].astype(BF16)
    wb_out[...] = wb_ref[...].astype(BF16)

    masks = _half_masks(BF16)
    acc = None
    lmin = None
    for hh in range(2):
        q = q_ref[:, hh * LANES:(hh + 1) * LANES]
        qf = q.astype(F32)
        bound = jnp.sqrt(jnp.sum(qf * qf, axis=-1, keepdims=True)) * kmax_ref[hh][0:1, 0:1]
        v = v_ref[...] * masks[hh]
        o = jnp.zeros((tq, LANES), F32)
        lsum = jnp.zeros((tq, LANES), F32)
        for j in range(S // MLA_KV_TILE):
            rows = slice(j * MLA_KV_TILE, (j + 1) * MLA_KV_TILE)
            p = jnp.exp2(_dot_nt(q, k_ref[rows, hh * LANES:(hh + 1) * LANES]) - bound)
            for t in range(MLA_KV_TILE // LANES):
                lsum = lsum + p[:, t * LANES:(t + 1) * LANES]
            o = o + _dot(p.astype(BF16), v[rows, :])
        l = jnp.sum(lsum, axis=-1, keepdims=True)
        o = o / l
        acc = o if acc is None else acc + o
        lm = jnp.min(l)
        lmin = lm if lmin is None else jnp.minimum(lmin, lm)
    o_ref[...] = acc.astype(o_ref.dtype)

    @pl.when(jnp.logical_not(lmin > SOFTMAX_UNDERFLOW_GUARD))
    def _():
        _mla_exact(q_ref, k_ref, v_ref, o_ref)


def _mla_attention(qm, km, vm, w_a, w_b, B, S, tq):
    pairs = MLA_HEADS // 2
    q3 = qm.reshape(B, S, MLA_HEADS * LANES)
    k3 = km.reshape(B, S, MLA_HEADS * LANES)
    v3 = vm.reshape(B, S, MLA_HEADS * MLA_V)
    steps = B * pairs * (S // tq)
    wa3 = w_a.reshape(steps, -1, w_a.shape[-1])
    wb3 = w_b.reshape(steps, -1, w_b.shape[-1])
    step = lambda b, p, i: ((b * pairs + p) * (S // tq) + i, 0, 0)
    w_spec = lambda w: pl.BlockSpec((None,) + w.shape[1:], step)
    om, wa_bf, wb_bf = pl.pallas_call(
        _mla_kernel,
        grid=(B, pairs, S // tq),
        in_specs=[pl.BlockSpec((None, tq, 2 * LANES), lambda b, p, i: (b, i, p)),
                  pl.BlockSpec((None, S, 2 * LANES), lambda b, p, i: (b, 0, p)),
                  pl.BlockSpec((None, S, LANES), lambda b, p, i: (b, 0, p)),
                  w_spec(wa3), w_spec(wb3)],
        out_specs=[pl.BlockSpec((None, tq, LANES), lambda b, p, i: (b, i, p)),
                   w_spec(wa3), w_spec(wb3)],
        out_shape=[jax.ShapeDtypeStruct((B, S, MLA_HEADS * MLA_V), BF16),
                   jax.ShapeDtypeStruct(wa3.shape, BF16),
                   jax.ShapeDtypeStruct(wb3.shape, BF16)],
        scratch_shapes=[pltpu.VMEM((2, 8, LANES), F32)],
        compiler_params=pltpu.CompilerParams(
            dimension_semantics=("arbitrary", "arbitrary", "arbitrary"),
            vmem_limit_bytes=VMEM_LIMIT),
        name="mla_attention",
    )(q3, k3, v3, wa3, wb3)
    return om, wa_bf.reshape(w_a.shape), wb_bf.reshape(w_b.shape)


def _swa_kernel(sink_ref, q_ref, k_ref, v_ref, o_ref, *, tq, S):
    hk = pl.program_id(1)
    i = pl.program_id(2)
    tk = SWA_BLOCK + 2 * WINDOW
    row = lax.broadcasted_iota(jnp.int32, (SWA_GROUP * SWA_BLOCK, 1), 0)
    qoff = row & (SWA_BLOCK - 1)
    head = row // SWA_BLOCK
    sk = jnp.zeros((SWA_GROUP * SWA_BLOCK, 1), F32)
    for g in range(SWA_GROUP):
        sk = jnp.where(head == g, sink_ref[SWA_GROUP * hk + g] * LOG2E, sk)
    low_half = lax.broadcasted_iota(jnp.int32, (1, LANES), 1) < LANES // 2
    for sub in range(tq // SWA_BLOCK):
        rows = slice(sub * SWA_BLOCK, (sub + 1) * SWA_BLOCK)
        q0 = i * tq + sub * SWA_BLOCK
        kstart = pl.multiple_of(jnp.clip(q0 - WINDOW, 0, S - tk), WINDOW)
        k = k_ref[pl.ds(kstart, tk), :]
        v = v_ref[pl.ds(kstart, tk), :]
        q = jnp.concatenate([q_ref[rows, g * LANES:(g + 1) * LANES] for g in range(SWA_GROUP)], axis=0)
        kpos = kstart + lax.broadcasted_iota(jnp.int32, (1, tk), 1)
        valid = jnp.abs(kpos - (q0 + qoff[:SWA_BLOCK])) <= WINDOW
        s = _dot_nt(q, k).reshape(SWA_GROUP, SWA_BLOCK, tk)
        s = jnp.where(valid[None], s, NEG_INF).reshape(SWA_GROUP * SWA_BLOCK, tk)
        m = jnp.maximum(jnp.max(s, axis=-1, keepdims=True), sk)
        e = jnp.exp2(s - m)
        denom = jnp.sum(e, axis=-1, keepdims=True) + jnp.exp2(sk - m)
        o = _dot(e.astype(BF16), v) / denom
        for j in range(SWA_GROUP // 2):
            even = o[(2 * j) * SWA_BLOCK:(2 * j + 1) * SWA_BLOCK, :]
            odd = o[(2 * j + 1) * SWA_BLOCK:(2 * j + 2) * SWA_BLOCK, :]
            o_ref[rows, j * LANES:(j + 1) * LANES] = jnp.where(low_half, even, odd).astype(o_ref.dtype)


def _swa_attention(qw, kw, vw, sink, B, S, tq):
    q3 = qw.reshape(B, S, SWA_HEADS * LANES)
    k3 = kw.reshape(B, S, SWA_KV_HEADS * LANES)
    v3 = vw.reshape(B, S, SWA_KV_HEADS * LANES)
    return pl.pallas_call(
        functools.partial(_swa_kernel, tq=tq, S=S),
        grid=(B, SWA_KV_HEADS, S // tq),
        in_specs=[pl.BlockSpec(memory_space=pltpu.SMEM),
                  pl.BlockSpec((None, tq, SWA_GROUP * LANES), lambda b, h, i: (b, i, h)),
                  pl.BlockSpec((None, S, LANES), lambda b, h, i: (b, 0, h)),
                  pl.BlockSpec((None, S, LANES), lambda b, h, i: (b, 0, h))],
        out_specs=pl.BlockSpec((None, tq, SWA_GROUP * SWA_HD), lambda b, h, i: (b, i, h)),
        out_shape=jax.ShapeDtypeStruct((B, S, SWA_HEADS * SWA_HD), BF16),
        compiler_params=pltpu.CompilerParams(
            dimension_semantics=("arbitrary", "arbitrary", "arbitrary"),
            vmem_limit_bytes=VMEM_LIMIT),
        name="swa_attention",
    )(sink, q3, k3, v3)


def _beats(vj, vi, j_first):
    return (vj >= vi) if j_first else (vj > vi)


def _route(scores, sel):
    G, P = N_GROUPS, EXPERTS_PER_GROUP
    groups = [sel[g * P:(g + 1) * P, :] for g in range(G)]
    row = lax.broadcasted_iota(jnp.int32, (P, 1), 0)
    gscore = []
    for vg in groups:
        m1 = jnp.max(vg, axis=0, keepdims=True)
        first = jnp.min(jnp.where(vg == m1, row, P), axis=0, keepdims=True)
        m2 = jnp.max(jnp.where(row == first, -jnp.inf, vg), axis=0, keepdims=True)
        gscore.append(m1 + m2)
    masked = []
    for g in range(G):
        rank = jnp.zeros_like(gscore[g], dtype=jnp.int32)
        for g2 in range(G):
            if g2 != g:
                rank = rank + _beats(gscore[g2], gscore[g], g2 < g).astype(jnp.int32)
        masked.append(jnp.where(rank < TOPK_GROUPS, groups[g], NEG_INF))
    index = [row + g * P for g in range(G)]
    chosen = [None] * G
    for _ in range(TOP_K):
        best = masked[0]
        for g in range(1, G):
            best = jnp.maximum(best, masked[g])
        best = jnp.max(best, axis=0, keepdims=True)
        first = jnp.where(masked[0] == best, index[0], N_EXPERTS)
        for g in range(1, G):
            first = jnp.minimum(first, jnp.where(masked[g] == best, index[g], N_EXPERTS))
        first = jnp.min(first, axis=0, keepdims=True)
        for g in range(G):
            hit = index[g] == first
            chosen[g] = hit if chosen[g] is None else (chosen[g] | hit)
            masked[g] = jnp.where(hit, -jnp.inf, masked[g])
    picked = [jnp.where(chosen[g], scores[g * P:(g + 1) * P, :], 0.0) for g in range(G)]
    total = picked[0]
    for g in range(1, G):
        total = total + picked[g]
    denom = jnp.sum(total, axis=0, keepdims=True)
    return [pk / denom * ROUTED_SCALE for pk in picked]


def _pack_pair(lo, hi):
    return pltpu.pack_elementwise([lo, hi], packed_dtype=BF16)


def _unpack_pair(word):
    lo = pltpu.unpack_elementwise(word, index=0, packed_dtype=BF16, unpacked_dtype=F32)
    hi = pltpu.unpack_elementwise(word, index=1, packed_dtype=BF16, unpacked_dtype=F32)
    return lo, hi


def _store_rows_dense(ref, words):
    for j in range(ROW_SUB):
        ref[pl.ds(j, words.shape[0], stride=ROW_SUB), :] = words[:, j * LANES:(j + 1) * LANES]


def _load_rows_dense(ref, rows):
    sub = ROW_SUB
    return jnp.concatenate([ref[pl.ds(j, rows, stride=sub), :] for j in range(sub)], axis=1)


def _post_kernel(om_ref, ow_ref, gate_ref, x_ref, wbm_ref, wbw_ref, wout_ref, g_moe_ref, wr_ref,
                 rb_ref, wsgu_ref, wsd_ref, x1_ref, hp_ref, comb_ref):
    am = _dot(om_ref[...], wbm_ref[...])
    aw = _dot(ow_ref[...], wbw_ref[...])
    gates = gate_ref[...].astype(F32)
    merged = gates[:, :D_MODEL] * am + gates[:, D_MODEL:] * aw
    x1 = x_ref[...] + _dot(merged.astype(BF16), wout_ref[...])

    h2 = _rms(x1, D_MODEL) * g_moe_ref[...]
    h2b = h2.astype(BF16)
    _store_rows_dense(hp_ref, _pack_pair(h2[:, :ROW_WORDS], h2[:, ROW_WORDS:]))

    wr = wr_ref[...]
    w_hi = wr.astype(BF16)
    w_lo = (wr - w_hi.astype(F32)).astype(BF16)
    h_lo = (h2 - h2b.astype(F32)).astype(BF16)
    by_hi = _dot_nt(jnp.concatenate([w_hi, w_lo], axis=0), h2b)
    logits = by_hi[:N_EXPERTS] + by_hi[N_EXPERTS:] + _dot_nt(w_hi, h_lo)

    sgu = _dot(h2b, wsgu_ref[...])
    sh = jax.nn.silu(sgu[:, :SHARED_FF]) * sgu[:, SHARED_FF:]
    x1_ref[...] = x1 + _dot(sh.astype(BF16), wsd_ref[...])

    scores = jax.nn.sigmoid(logits)
    comb = _route(scores, scores + rb_ref[...])
    for g in range(N_GROUPS):
        comb_ref[g * EXPERTS_PER_GROUP:(g + 1) * EXPERTS_PER_GROUP, :] = comb[g]


def _post_attention(om, ow, gates, x2d, wbm, wbw, wout, g_moe, wr_t, rbias, wsgu, wsd, tm):
    T = x2d.shape[0]
    row = lambda n: pl.BlockSpec((tm, n), lambda i: (i, 0))
    consts = [wbm, wbw, wout, g_moe, wr_t, rbias, wsgu, wsd]
    return pl.pallas_call(
        _post_kernel,
        grid=(T // tm,),
        in_specs=[row(om.shape[1]), row(ow.shape[1]), row(2 * D_MODEL), row(D_MODEL)]
        + [_full(c.shape) for c in consts],
        out_specs=[row(D_MODEL), pl.BlockSpec((tm * ROW_SUB, LANES), lambda i: (i, 0)),
                   pl.BlockSpec((N_EXPERTS, tm), lambda i: (0, i))],
        out_shape=[jax.ShapeDtypeStruct((T, D_MODEL), F32),
                   jax.ShapeDtypeStruct((T * ROW_SUB, LANES), jnp.uint32),
                   jax.ShapeDtypeStruct((N_EXPERTS, T), F32)],
        compiler_params=pltpu.CompilerParams(dimension_semantics=("arbitrary",),
                                             vmem_limit_bytes=VMEM_LIMIT),
        name="post_attention",
    )(om, ow, gates, x2d, *consts)


SLOT_ROWS = -(-(MOE_CHUNK * TOP_K + N_EXPERTS * (SLOT_ALIGN - 1) + MOE_TILE_BIG) // MOE_TILE) * MOE_TILE
DUMMY_SLOT = SLOT_ROWS - 1
PLAN_BLOCK = 256


def _plan_kernel(comb_ref, slot_ref, w_ref, off_ref, cnt_ref):
    comb = comb_ref[...]
    sel = comb > 0.0
    m = sel.astype(F32)
    mb = m.astype(BF16)
    r_i = lax.broadcasted_iota(jnp.int32, (PLAN_BLOCK, PLAN_BLOCK), 0)
    c_i = lax.broadcasted_iota(jnp.int32, (PLAN_BLOCK, PLAN_BLOCK), 1)
    before = (r_i < c_i).astype(BF16)
    carry = jnp.zeros((N_EXPERTS, 1), F32)
    ranks = []
    for b in range(MOE_CHUNK // PLAN_BLOCK):
        blk = slice(b * PLAN_BLOCK, (b + 1) * PLAN_BLOCK)
        ranks.append(_dot(mb[:, blk], before) + carry)
        carry = carry + jnp.sum(m[:, blk], axis=1, keepdims=True)
    rank = jnp.concatenate(ranks, axis=1)
    cnt = carry
    cnt_pad = jnp.floor((cnt + (SLOT_ALIGN - 1)) * (1.0 / SLOT_ALIGN)) * SLOT_ALIGN
    e_r = lax.broadcasted_iota(jnp.int32, (N_EXPERTS, N_EXPERTS), 0)
    e_c = lax.broadcasted_iota(jnp.int32, (N_EXPERTS, N_EXPERTS), 1)
    below = (e_c < e_r).astype(F32)
    off = jnp.dot(below, jnp.broadcast_to(cnt_pad, (N_EXPERTS, LANES)),
                  precision=lax.Precision.HIGHEST, preferred_element_type=F32)
    slot = off[:, :1] + rank
    kidx = _dot(below.astype(BF16), mb)
    row = lax.broadcasted_iota(jnp.int32, (TOP_K, 1), 0)
    slot_acc = jnp.zeros((TOP_K, MOE_CHUNK), F32)
    w_acc = jnp.zeros((TOP_K, MOE_CHUNK), F32)
    for k in range(TOP_K):
        pick = jnp.where(sel & (kidx == k), 1.0, 0.0)
        found = jnp.sum(pick, axis=0, keepdims=True) > 0.0
        s_k = jnp.where(found, jnp.sum(pick * slot, axis=0, keepdims=True), float(DUMMY_SLOT))
        w_k = jnp.sum(pick * comb, axis=0, keepdims=True)
        slot_acc = jnp.where(row == k, s_k, slot_acc)
        w_acc = jnp.where(row == k, w_k, w_acc)
    slot_ref[...] = slot_acc.astype(jnp.int32) * ROW_SUB
    w_ref[...] = w_acc
    off_ref[...] = off.astype(jnp.int32)
    cnt_ref[...] = jnp.broadcast_to(cnt, (N_EXPERTS, LANES)).astype(jnp.int32)


def _moe_plan(comb_t):
    T = comb_t.shape[1]
    nch = T // MOE_CHUNK
    per_pair = pl.BlockSpec((None, TOP_K, MOE_CHUNK), lambda c: (c, 0, 0))
    per_expert = pl.BlockSpec((None, N_EXPERTS, LANES), lambda c: (c, 0, 0))
    return pl.pallas_call(
        _plan_kernel,
        grid=(nch,),
        in_specs=[pl.BlockSpec((N_EXPERTS, MOE_CHUNK), lambda c: (0, c))],
        out_specs=[per_pair, per_pair, per_expert, per_expert],
        out_shape=[jax.ShapeDtypeStruct((nch, TOP_K, MOE_CHUNK), jnp.int32),
                   jax.ShapeDtypeStruct((nch, TOP_K, MOE_CHUNK), F32),
                   jax.ShapeDtypeStruct((nch, N_EXPERTS, LANES), jnp.int32),
                   jax.ShapeDtypeStruct((nch, N_EXPERTS, LANES), jnp.int32)],
        compiler_params=pltpu.CompilerParams(dimension_semantics=("arbitrary",),
                                             vmem_limit_bytes=VMEM_LIMIT),
        name="moe_plan",
    )(comb_t)


def _slab_at(ref, first):
    return ref.at[pl.ds(pl.multiple_of(first, ROW_SUB), ROW_SUB), :]


def _slab(ref, row):
    return _slab_at(ref, row * ROW_SUB)


def _moe_kernel(off_ref, cnt_ref, slot_hbm, w_hbm, hp_ref, wgu_ref, wd_ref, o_ref,
                buf, clo, chi, *smem_and_sem):
    slot_s = smem_and_sem[:TOP_K]
    w_s = smem_and_sem[TOP_K:2 * TOP_K]
    sem = smem_and_sem[2 * TOP_K]
    c = pl.program_id(0)
    s = pl.program_id(1)

    @pl.when(s == 0)
    def _dispatch():
        copies = [pltpu.make_async_copy(slot_hbm.at[c, k], slot_s[k], sem.at[k]) for k in range(TOP_K)]
        copies += [pltpu.make_async_copy(w_hbm.at[c, k], w_s[k], sem.at[TOP_K + k]) for k in range(TOP_K)]
        for cp in copies:
            cp.start()

        @pl.when(c == 0)
        def _():
            buf[...] = jnp.zeros_like(buf)

        for cp in copies:
            cp.wait()

        def scatter(tb, carry):
            for tt in range(8):
                t = tb * 8 + tt
                slab = _slab(hp_ref, t)[...]
                for k in range(TOP_K):
                    _slab_at(buf, slot_s[k][t])[...] = slab
            return carry

        lax.fori_loop(0, MOE_CHUNK // 8, scatter, 0)

    def expert(ee):
        e = s * EXPERTS_PER_STEP + ee
        n = cnt_ref[c, e]
        off = off_ref[c, e]

        def ffn(start, rows):
            view = buf.at[pl.ds(pl.multiple_of((off + start) * ROW_SUB, SLOT_ALIGN * ROW_SUB),
                                rows * ROW_SUB), :]
            x_lo, x_hi = _unpack_pair(_load_rows_dense(view, rows))
            gu = (_dot(x_lo.astype(BF16), wgu_ref[ee, :ROW_WORDS, :])
                  + _dot(x_hi.astype(BF16), wgu_ref[ee, ROW_WORDS:, :]))
            hid = jax.nn.silu(gu[:, :EXPERT_FF]) * gu[:, EXPERT_FF:]
            y = _dot(hid.astype(BF16), wd_ref[ee])
            mine = lax.broadcasted_iota(jnp.int32, (rows, 1), 0) < (n - start)
            _store_rows_dense(view, _pack_pair(jnp.where(mine, y[:, :ROW_WORDS], x_lo),
                                               jnp.where(mine, y[:, ROW_WORDS:], x_hi)))

        n_big = (n + MOE_TILE_BIG - MOE_TILE - 1) // MOE_TILE_BIG

        def big(r, carry):
            ffn(r * MOE_TILE_BIG, MOE_TILE_BIG)
            return carry

        lax.fori_loop(0, n_big, big, 0)

        @pl.when(n > n_big * MOE_TILE_BIG)
        def _():
            ffn(n_big * MOE_TILE_BIG, MOE_TILE)

    @pl.when(s < EXPERT_STEPS)
    def _experts():
        for ee in range(EXPERTS_PER_STEP):
            expert(ee)

    @pl.when(s >= EXPERT_STEPS)
    def _combine():
        t0 = (s - EXPERT_STEPS) * COMBINE_BLOCK

        def gather(i, carry):
            for tt in range(4):
                tl = i * 4 + tt
                t = t0 + tl
                acc_lo = jnp.zeros((ROW_SUB, LANES), F32)
                acc_hi = jnp.zeros((ROW_SUB, LANES), F32)
                for k in range(TOP_K):
                    lo, hi = _unpack_pair(_slab_at(buf, slot_s[k][t])[...])
                    wk = w_s[k][t]
                    acc_lo = acc_lo + wk * lo
                    acc_hi = acc_hi + wk * hi
                _slab(clo, tl)[...] = acc_lo
                _slab(chi, tl)[...] = acc_hi
            return carry

        lax.fori_loop(0, COMBINE_BLOCK // 4, gather, 0)
        o_ref[:, :ROW_WORDS] = _load_rows_dense(clo, COMBINE_BLOCK)
        o_ref[:, ROW_WORDS:] = _load_rows_dense(chi, COMBINE_BLOCK)


def _moe_sparse(offs, cnts, slots, wts, hp, w_gu, w_d):
    nch = offs.shape[0]
    T = nch * MOE_CHUNK
    blocks = MOE_CHUNK // COMBINE_BLOCK
    expert = lambda c, s, *_: (jnp.minimum(s, EXPERT_STEPS - 1), 0, 0)
    grid_spec = pltpu.PrefetchScalarGridSpec(
        num_scalar_prefetch=2,
        grid=(nch, EXPERT_STEPS + blocks),
        in_specs=[pl.BlockSpec(memory_space=pl.ANY),
                  pl.BlockSpec(memory_space=pl.ANY),
                  pl.BlockSpec((MOE_CHUNK * ROW_SUB, LANES), lambda c, s, *_: (c, 0),
                               pipeline_mode=pl.Buffered(1)),
                  pl.BlockSpec((EXPERTS_PER_STEP, D_MODEL, 2 * EXPERT_FF), expert),
                  pl.BlockSpec((EXPERTS_PER_STEP, EXPERT_FF, D_MODEL), expert)],
        out_specs=pl.BlockSpec(
            (COMBINE_BLOCK, D_MODEL),
            lambda c, s, *_: (c * blocks + jnp.maximum(s - EXPERT_STEPS, 0), 0)),
        scratch_shapes=[pltpu.VMEM((SLOT_ROWS * ROW_SUB, LANES), jnp.uint32),
                        pltpu.VMEM((COMBINE_BLOCK * ROW_SUB, LANES), F32),
                        pltpu.VMEM((COMBINE_BLOCK * ROW_SUB, LANES), F32),
                        *[pltpu.SMEM((MOE_CHUNK,), jnp.int32) for _ in range(TOP_K)],
                        *[pltpu.SMEM((MOE_CHUNK,), F32) for _ in range(TOP_K)],
                        pltpu.SemaphoreType.DMA((2 * TOP_K,))])
    return pl.pallas_call(
        _moe_kernel,
        grid_spec=grid_spec,
        out_shape=jax.ShapeDtypeStruct((T, D_MODEL), F32),
        compiler_params=pltpu.CompilerParams(dimension_semantics=("arbitrary", "arbitrary"),
                                             vmem_limit_bytes=VMEM_LIMIT),
        name="moe_experts",
    )(offs, cnts, slots, wts, hp, w_gu, w_d)


def _ple_kernel(x1_ref, r_ref, p_ref, g_ref, wg_ref, b_ref, wp_ref, o_ref):
    x2 = x1_ref[...] + r_ref[...]
    hn = (_rms(x2, D_MODEL) * g_ref[...]).astype(BF16)
    gate = jax.nn.sigmoid(_dot(hn, wg_ref[...]) + b_ref[...])
    o_ref[...] = x2 + gate * _dot(p_ref[...].astype(BF16), wp_ref[...])


def _ple(x1s, routed, p2d, g_ple, wg, b_ple, wp, tm):
    T = x1s.shape[0]
    row = lambda n: pl.BlockSpec((tm, n), lambda i: (i, 0))
    consts = [g_ple, wg, b_ple, wp]
    return pl.pallas_call(
        _ple_kernel,
        grid=(T // tm,),
        in_specs=[row(D_MODEL), row(D_MODEL), row(PLE_DIM)] + [_full(c.shape) for c in consts],
        out_specs=row(D_MODEL),
        out_shape=jax.ShapeDtypeStruct((T, D_MODEL), F32),
        compiler_params=pltpu.CompilerParams(dimension_semantics=("arbitrary",),
                                             vmem_limit_bytes=VMEM_LIMIT),
        name="ple",
    )(x1s, routed, p2d, *consts)


def _lane_map(*runs):
    src = np.full((LANES,), -1)
    for lane, dim, n in runs:
        src[lane:lane + n] = np.arange(dim, dim + n)
    return src


_MLA_HALF = MLA_ROPE // 2
_SWA_HALF = SWA_HD // 2
MLA_LANES = _lane_map((0, MLA_NOPE, _MLA_HALF), (_MLA_HALF, 0, LANES // 2 - _MLA_HALF),
                      (LANES // 2, MLA_NOPE + _MLA_HALF, _MLA_HALF),
                      (LANES // 2 + _MLA_HALF, LANES // 2 - _MLA_HALF, MLA_NOPE - LANES // 2 + _MLA_HALF))
MLA_NOPE_LANES = np.where(MLA_LANES < MLA_NOPE, MLA_LANES, -1)
MLA_ROPE_LANES = np.where(MLA_LANES >= MLA_NOPE, MLA_LANES - MLA_NOPE, -1)
SWA_LANES = _lane_map((LANES // 2 - _SWA_HALF, 0, _SWA_HALF), (LANES - _SWA_HALF, _SWA_HALF, _SWA_HALF))


def _spread(w, heads, lane_src):
    k = w.shape[0]
    dim = w.shape[1] // heads
    w = jnp.pad(w.reshape(k, heads, dim), ((0, 0), (0, 0), (0, 1)))
    return w[:, :, np.where(lane_src < 0, dim, lane_src)].reshape(k, heads * LANES)


def _rope_table():
    def inv_freq(dim):
        return 1.0 / (ROPE_THETA ** (jnp.arange(0, dim, 2, dtype=F32) / dim))

    def selector(lane_src, half):
        sel = np.where(lane_src < 0, 0.0, np.where(lane_src < half, -1.0, 1.0))
        return jnp.asarray(sel, F32)

    sel_m = selector(MLA_ROPE_LANES, _MLA_HALF)
    sel_s = selector(SWA_LANES, _SWA_HALF)
    freq_m = _spread(jnp.tile(inv_freq(MLA_ROPE), 2)[None], 1, MLA_ROPE_LANES)[0]
    freq_s = _spread(jnp.tile(inv_freq(SWA_HD), 2)[None], 1, SWA_LANES)[0]
    zero = jnp.zeros((LANES,), F32)
    rows = [freq_m + freq_s, sel_m, jnp.abs(sel_m), sel_s, jnp.abs(sel_s), zero, zero, zero]
    return jnp.stack(rows)


def _layer(x2d, p2d, pos2d, B, S, g_mix, w_in, b_gate, g_cq, w_uq, g_ckv, w_ukv, g_qn_mla, g_kn_mla,
           g_qn_swa, g_kn_swa, sink, w_br_mla, w_br_swa, w_out, g_moe, w_router, router_bias,
           w_exp_gu, w_exp_down, w_sh_gu, w_sh_down, g_ple, w_ple_gate, b_ple, w_ple_proj):
    w_kr = _spread(w_in[:, OFF_CKV:OFF_KR], 1, MLA_ROPE_LANES)
    w_vs = w_in[:, OFF_KS:OFF_VS].reshape(D_MODEL, SWA_KV_HEADS, 1, SWA_HD)
    w_vs = jnp.broadcast_to(w_vs, (D_MODEL, SWA_KV_HEADS, 2, SWA_HD)).reshape(D_MODEL, -1)
    w_all = jnp.concatenate([
        w_in[:, :OFF_CKV], w_kr,
        _spread(w_in[:, OFF_KR:OFF_QS], SWA_HEADS, SWA_LANES),
        _spread(w_in[:, OFF_QS:OFF_KS], SWA_KV_HEADS, SWA_LANES),
        w_vs, w_in[:, OFF_VS:]], axis=1).astype(BF16)
    assert w_all.shape[1] == C_END
    w_uq_p = _spread(w_uq, MLA_HEADS, MLA_LANES).astype(BF16)
    w_ukv3 = w_ukv.reshape(MLA_KV_RANK, MLA_HEADS, MLA_NOPE + MLA_V)
    w_k = _spread(w_ukv3[:, :, :MLA_NOPE].reshape(MLA_KV_RANK, -1), MLA_HEADS, MLA_NOPE_LANES).astype(BF16)
    w_v = w_ukv3[:, :, MLA_NOPE:].reshape(MLA_KV_RANK, -1).astype(BF16)

    qm, km, vm, qw, kw, vw, gates = _pre_attention(
        x2d, pos2d, w_all, w_uq_p, w_k, w_v, g_mix[None], g_cq[None], g_ckv[None],
        _spread(g_qn_mla[None], 1, MLA_LANES), _spread(g_kn_mla[None], 1, MLA_LANES),
        _spread(g_qn_swa[None], 1, SWA_LANES), _spread(g_kn_swa[None], 1, SWA_LANES),
        b_gate[None], _rope_table(), tm=PRE_TILE)

    om, w_gu_bf, w_down_bf = _mla_attention(qm, km, vm, w_exp_gu, w_exp_down, B, S, tq=MLA_Q_TILE)
    om = om.reshape(B * S, -1)
    ow = _swa_attention(qw, kw, vw, sink, B, S, tq=SWA_Q_TILE).reshape(B * S, -1)

    x1s, hp, comb_t = _post_attention(
        om, ow, gates, x2d, w_br_mla.astype(BF16), w_br_swa.astype(BF16), w_out.astype(BF16),
        g_moe[None], w_router.T, router_bias[:, None], w_sh_gu.astype(BF16),
        w_sh_down.astype(BF16), tm=ROW_TILE)

    slots, wts, offs, cnts = _moe_plan(comb_t)
    routed = _moe_sparse(offs[:, :, 0], cnts[:, :, 0], slots, wts, hp, w_gu_bf, w_down_bf)

    return _ple(x1s, routed, p2d, g_ple[None], w_ple_gate.astype(BF16), b_ple[None],
                w_ple_proj.astype(BF16), tm=ROW_TILE)


def kernel(x, p, positions, g_mix, w_in, b_gate, g_cq, w_uq, g_ckv, w_ukv, g_qn_mla, g_kn_mla, g_qn_swa, g_kn_swa, sink, w_br_mla, w_br_swa, w_out, g_moe, w_router, router_bias, w_exp_gu, w_exp_down, w_sh_gu, w_sh_down, g_ple, w_ple_gate, b_ple, w_ple_proj):
    B, S, D = x.shape
    x2d = x.reshape(B * S, D)
    pos2d = positions.reshape(B * S, 1)
    for i in range(p.shape[0]):
        x2d = _layer(x2d, p[i].reshape(B * S, -1), pos2d, B, S, g_mix[i], w_in[i], b_gate[i],
                     g_cq[i], w_uq[i], g_ckv[i], w_ukv[i], g_qn_mla[i], g_kn_mla[i], g_qn_swa[i],
                     g_kn_swa[i], sink[i], w_br_mla[i], w_br_swa[i], w_out[i], g_moe[i],
                     w_router[i], router_bias[i], w_exp_gu[i], w_exp_down[i], w_sh_gu[i],
                     w_sh_down[i], g_ple[i], w_ple_gate[i], b_ple[i], w_ple_proj[i])
    return x2d.reshape(B, S, D)
```

```python
import functools

import jax
import jax.numpy as jnp
import numpy as np
from jax import lax
from jax.experimental import pallas as pl
from jax.experimental.pallas import tpu as pltpu

D_MODEL = 1024
PLE_DIM = 256
ROPE_THETA = 10000.0
EPS = 1e-6
NEG_INF = -1e30

MLA_HEADS = 8
MLA_Q_RANK = 384
MLA_KV_RANK = 256
MLA_NOPE = 64
MLA_ROPE = 32
MLA_QK = MLA_NOPE + MLA_ROPE
MLA_V = 64

SWA_HEADS = 8
SWA_KV_HEADS = 2
SWA_GROUP = SWA_HEADS // SWA_KV_HEADS
SWA_HD = 64
WINDOW = 128

OFF_CQ = MLA_Q_RANK
OFF_CKV = OFF_CQ + MLA_KV_RANK
OFF_KR = OFF_CKV + MLA_ROPE
OFF_QS = OFF_KR + SWA_HEADS * SWA_HD
OFF_KS = OFF_QS + SWA_KV_HEADS * SWA_HD
OFF_VS = OFF_KS + SWA_KV_HEADS * SWA_HD
OFF_GA = OFF_VS + D_MODEL

N_EXPERTS = 64
TOP_K = 8
N_GROUPS = 8
TOPK_GROUPS = 4
EXPERTS_PER_GROUP = N_EXPERTS // N_GROUPS
EXPERT_FF = 256
SHARED_FF = 256
ROUTED_SCALE = 2.5

LANES = 128
ROW_WORDS = D_MODEL // 2
ROW_SUB = ROW_WORDS // LANES
MOE_CHUNK = 2048
MOE_TILE = 128
MOE_TILE_BIG = 320
SLOT_ALIGN = 16
COMBINE_BLOCK = 256
COMBINE_UNROLL = 8
EXPERTS_PER_STEP = 4
EXPERT_STEPS = N_EXPERTS // EXPERTS_PER_STEP
PRE_TILE = 256
ROW_TILE = 512
MLA_Q_TILE = 512
MLA_KV_TILE = 256
LOG2E = 1.4426950408889634
SOFTMAX_UNDERFLOW_GUARD = 1e-30
SWA_Q_TILE = 512
SWA_BLOCK = 128
VMEM_LIMIT = 56 * 1024 * 1024

BF16 = jnp.bfloat16
F32 = jnp.float32

C_CQ = 0
C_CKV = C_CQ + MLA_Q_RANK
C_KR = C_CKV + MLA_KV_RANK
C_QS = C_KR + LANES
C_KS = C_QS + SWA_HEADS * LANES
C_VS = C_KS + SWA_KV_HEADS * LANES
C_GA = C_VS + SWA_KV_HEADS * LANES
C_END = C_GA + 2 * D_MODEL


def _full(shape):
    nd = len(shape)
    return pl.BlockSpec(shape, lambda *_: (0,) * nd)


def _dot(a, b):
    return jnp.dot(a, b, preferred_element_type=F32)


def _dot_nt(a, b, precision=None):
    return lax.dot_general(a, b, (((1,), (1,)), ((), ())), precision=precision,
                           preferred_element_type=F32)


def _rms(v, n):
    return v * lax.rsqrt(jnp.sum(v * v, axis=-1, keepdims=True) * (1.0 / n) + EPS)


def _rope(v, cos, sin):
    return v * cos + pltpu.roll(v, LANES // 2, 1) * sin


def _pre_kernel(x_ref, pos_ref, w_all_ref, w_uq_ref, w_k_ref, w_v_ref, g_mix_ref, g_cq_ref,
                g_ckv_ref, gq_m_ref, gk_m_ref, gq_s_ref, gk_s_ref, b_gate_ref, rope_ref,
                qm_ref, km_ref, vm_ref, qw_ref, kw_ref, vw_ref, gate_ref):
    x = x_ref[...]
    h = (_rms(x, D_MODEL) * g_mix_ref[...]).astype(BF16)

    def proj(lo, hi):
        return _dot(h, w_all_ref[:, lo:hi])

    z_lat = proj(C_CQ, C_QS)
    z_qs = proj(C_QS, C_KS)
    half = (C_END - C_GA) // 2
    z_ga = proj(C_GA, C_GA + half)

    pos = pos_ref[...].astype(F32)
    rope = rope_ref[...]
    ang = pos * rope[0:1, :]
    cos_m1 = jnp.cos(ang) - 1.0
    sin = jnp.sin(ang)
    cos_m, sin_m = 1.0 + cos_m1 * rope[2:3, :], sin * rope[1:2, :]
    cos_s, sin_s = 1.0 + cos_m1 * rope[4:5, :], sin * rope[3:4, :]

    cqn = (_rms(z_lat[:, C_CQ:C_CKV], MLA_Q_RANK) * g_cq_ref[...]).astype(BF16)
    ckvn = (_rms(z_lat[:, C_CKV:C_KR], MLA_KV_RANK) * g_ckv_ref[...]).astype(BF16)
    q = _dot(cqn, w_uq_ref[...])
    gq_m = gq_m_ref[...]
    for hd in range(MLA_HEADS):
        qh = _rms(q[:, hd * LANES:(hd + 1) * LANES], MLA_QK) * gq_m
        qh = _rope(qh, cos_m, sin_m) * (MLA_QK ** -0.5 * LOG2E)
        qm_ref[:, hd * LANES:(hd + 1) * LANES] = qh.astype(BF16)

    kn = _dot(ckvn, w_k_ref[...])
    vm_ref[...] = _dot(ckvn, w_v_ref[...]).astype(BF16)
    z_kv = proj(C_KS, C_GA)

    gq_s = gq_s_ref[...]
    for hd in range(SWA_HEADS):
        qh = _rms(z_qs[:, hd * LANES:(hd + 1) * LANES], SWA_HD) * gq_s
        qh = _rope(qh, cos_s, sin_s) * (SWA_HD ** -0.5 * LOG2E)
        qw_ref[:, hd * LANES:(hd + 1) * LANES] = qh.astype(BF16)

    gk_m = gk_m_ref[...]
    kr = z_lat[:, C_KR:C_QS]
    ss_kr = jnp.sum(kr * kr, axis=-1, keepdims=True)
    kr_rot = _rope(kr * gk_m, cos_m, sin_m)
    for hd in range(MLA_HEADS):
        kh = kn[:, hd * LANES:(hd + 1) * LANES]
        ss = jnp.sum(kh * kh, axis=-1, keepdims=True) + ss_kr
        sc = lax.rsqrt(ss * (1.0 / MLA_QK) + EPS)
        km_ref[:, hd * LANES:(hd + 1) * LANES] = ((kh * gk_m + kr_rot) * sc).astype(BF16)

    z_gb = proj(C_GA + half, C_END)

    gk_s = gk_s_ref[...]
    for hd in range(SWA_KV_HEADS):
        kh = _rms(z_kv[:, hd * LANES:(hd + 1) * LANES], SWA_HD) * gk_s
        kh = _rope(kh, cos_s, sin_s)
        kw_ref[:, hd * LANES:(hd + 1) * LANES] = kh.astype(BF16)
    vw_ref[...] = z_kv[:, C_VS - C_KS:].astype(BF16)

    gate_ref[:, :half] = jax.nn.sigmoid(z_ga + b_gate_ref[:, :half]).astype(BF16)
    gate_ref[:, half:] = jax.nn.sigmoid(z_gb + b_gate_ref[:, half:]).astype(BF16)


def _pre_attention(x2d, pos2d, w_all, w_uq, w_k, w_v, g_mix, g_cq, g_ckv, gq_m, gk_m, gq_s,
                   gk_s, b_gate, rope_tab, tm):
    T = x2d.shape[0]
    row = lambda n: pl.BlockSpec((tm, n), lambda i: (i, 0))
    outs = [(MLA_HEADS * LANES, BF16), (MLA_HEADS * LANES, BF16), (MLA_HEADS * MLA_V, BF16),
            (SWA_HEADS * LANES, BF16), (SWA_KV_HEADS * LANES, BF16),
            (SWA_KV_HEADS * LANES, BF16), (2 * D_MODEL, BF16)]
    consts = [w_all, w_uq, w_k, w_v, g_mix, g_cq, g_ckv, gq_m, gk_m, gq_s, gk_s, b_gate, rope_tab]
    return pl.pallas_call(
        _pre_kernel,
        grid=(T // tm,),
        in_specs=[row(D_MODEL), row(1)] + [_full(c.shape) for c in consts],
        out_specs=[row(n) for n, _ in outs],
        out_shape=[jax.ShapeDtypeStruct((T, n), dt) for n, dt in outs],
        compiler_params=pltpu.CompilerParams(dimension_semantics=("arbitrary",),
                                             vmem_limit_bytes=VMEM_LIMIT),
        name="pre_attention",
    )(x2d, pos2d, *consts)


def _half_masks(dtype):
    lane = lax.broadcasted_iota(jnp.int32, (1, LANES), 1)
    lo = (lane < LANES // 2).astype(dtype)
    return lo, 1 - lo


def _mla_exact(q_ref, k_ref, v_ref, o_ref):
    v = v_ref[...]
    masks = _half_masks(v.dtype)
    acc = None
    for hh in range(2):
        q = q_ref[:, hh * LANES:(hh + 1) * LANES]
        k = k_ref[:, hh * LANES:(hh + 1) * LANES]
        s = _dot_nt(q, k)
        m = jnp.max(s, axis=-1, keepdims=True)
        p = jnp.exp2(s - m)
        l = jnp.sum(p, axis=-1, keepdims=True)
        o = _dot(p.astype(BF16), v * masks[hh]) / l
        acc = o if acc is None else acc + o
    o_ref[...] = acc.astype(o_ref.dtype)


def _mla_kernel(q_ref, k_ref, v_ref, wa_ref, wb_ref, o_ref, wa_out, wb_out, kmax_ref):
    S = k_ref.shape[0]
    tq = q_ref.shape[0]

    @pl.when(pl.program_id(2) == 0)
    def _():
        for hh in range(2):
            k = k_ref[:, hh * LANES:(hh + 1) * LANES].astype(F32)
            kn2 = jnp.max(jnp.sum(k * k, axis=-1, keepdims=True), axis=0, keepdims=True)
            kmax_ref[hh] = jnp.broadcast_to(jnp.sqrt(kn2), kmax_ref.shape[1:])

    wa_out[...] = wa_ref[...].astype(BF16)
    wb_out[...] = wb_ref[...].astype(BF16)

    masks = _half_masks(BF16)
    acc = None
    lmin = None
    for hh in range(2):
        q = q_ref[:, hh * LANES:(hh + 1) * LANES]
        qf = q.astype(F32)
        bound = jnp.sqrt(jnp.sum(qf * qf, axis=-1, keepdims=True)) * kmax_ref[hh][0:1, 0:1]
        v = v_ref[...] * masks[hh]
        o = jnp.zeros((tq, LANES), F32)
        lsum = jnp.zeros((tq, LANES), F32)
        for j in range(S // MLA_KV_TILE):
            rows = slice(j * MLA_KV_TILE, (j + 1) * MLA_KV_TILE)
            p = jnp.exp2(_dot_nt(q, k_ref[rows, hh * LANES:(hh + 1) * LANES]) - bound)
            for t in range(MLA_KV_TILE // LANES):
                lsum = lsum + p[:, t * LANES:(t + 1) * LANES]
            o = o + _dot(p.astype(BF16), v[rows, :])
        l = jnp.sum(lsum, axis=-1, keepdims=True)
        o = o / l
        acc = o if acc is None else acc + o
        lm = jnp.min(l)
        lmin = lm if lmin is None else jnp.minimum(lmin, lm)
    o_ref[...] = acc.astype(o_ref.dtype)

    @pl.when(jnp.logical_not(lmin > SOFTMAX_UNDERFLOW_GUARD))
    def _():
        _mla_exact(q_ref, k_ref, v_ref, o_ref)


def _mla_attention(qm, km, vm, w_a, w_b, B, S, tq):
    pairs = MLA_HEADS // 2
    q3 = qm.reshape(B, S, MLA_HEADS * LANES)
    k3 = km.reshape(B, S, MLA_HEADS * LANES)
    v3 = vm.reshape(B, S, MLA_HEADS * MLA_V)
    steps = B * pairs * (S // tq)
    wa3 = w_a.reshape(steps, -1, w_a.shape[-1])
    wb3 = w_b.reshape(steps, -1, w_b.shape[-1])
    step = lambda b, p, i: ((b * pairs + p) * (S // tq) + i, 0, 0)
    w_spec = lambda w: pl.BlockSpec((None,) + w.shape[1:], step)
    om, wa_bf, wb_bf = pl.pallas_call(
        _mla_kernel,
        grid=(B, pairs, S // tq),
        in_specs=[pl.BlockSpec((None, tq, 2 * LANES), lambda b, p, i: (b, i, p)),
                  pl.BlockSpec((None, S, 2 * LANES), lambda b, p, i: (b, 0, p)),
                  pl.BlockSpec((None, S, LANES), lambda b, p, i: (b, 0, p)),
                  w_spec(wa3), w_spec(wb3)],
        out_specs=[pl.BlockSpec((None, tq, LANES), lambda b, p, i: (b, i, p)),
                   w_spec(wa3), w_spec(wb3)],
        out_shape=[jax.ShapeDtypeStruct((B, S, MLA_HEADS * MLA_V), BF16),
                   jax.ShapeDtypeStruct(wa3.shape, BF16),
                   jax.ShapeDtypeStruct(wb3.shape, BF16)],
        scratch_shapes=[pltpu.VMEM((2, 8, LANES), F32)],
        compiler_params=pltpu.CompilerParams(
            dimension_semantics=("arbitrary", "arbitrary", "arbitrary"),
            vmem_limit_bytes=VMEM_LIMIT),
        name="mla_attention",
    )(q3, k3, v3, wa3, wb3)
    return om, wa_bf.reshape(w_a.shape), wb_bf.reshape(w_b.shape)


def _swa_kernel(sink_ref, q_ref, k_ref, v_ref, o_ref, *, tq, S):
    hk = pl.program_id(1)
    i = pl.program_id(2)
    tk = SWA_BLOCK + 2 * WINDOW
    row = lax.broadcasted_iota(jnp.int32, (SWA_GROUP * SWA_BLOCK, 1), 0)
    qoff = row & (SWA_BLOCK - 1)
    head = row // SWA_BLOCK
    sk = jnp.zeros((SWA_GROUP * SWA_BLOCK, 1), F32)
    for g in range(SWA_GROUP):
        sk = jnp.where(head == g, sink_ref[SWA_GROUP * hk + g] * LOG2E, sk)
    low_half = lax.broadcasted_iota(jnp.int32, (1, LANES), 1) < LANES // 2
    for sub in range(tq // SWA_BLOCK):
        rows = slice(sub * SWA_BLOCK, (sub + 1) * SWA_BLOCK)
        q0 = i * tq + sub * SWA_BLOCK
        kstart = pl.multiple_of(jnp.clip(q0 - WINDOW, 0, S - tk), WINDOW)
        k = k_ref[pl.ds(kstart, tk), :]
        v = v_ref[pl.ds(kstart, tk), :]
        q = jnp.concatenate([q_ref[rows, g * LANES:(g + 1) * LANES] for g in range(SWA_GROUP)], axis=0)
        kpos = kstart + lax.broadcasted_iota(jnp.int32, (1, tk), 1)
        valid = jnp.abs(kpos - (q0 + qoff[:SWA_BLOCK])) <= WINDOW
        s = _dot_nt(q, k).reshape(SWA_GROUP, SWA_BLOCK, tk)
        s = jnp.where(valid[None], s, NEG_INF).reshape(SWA_GROUP * SWA_BLOCK, tk)
        m = jnp.maximum(jnp.max(s, axis=-1, keepdims=True), sk)
        e = jnp.exp2(s - m)
        denom = jnp.sum(e, axis=-1, keepdims=True) + jnp.exp2(sk - m)
        o = _dot(e.astype(BF16), v) / denom
        for j in range(SWA_GROUP // 2):
            even = o[(2 * j) * SWA_BLOCK:(2 * j + 1) * SWA_BLOCK, :]
            odd = o[(2 * j + 1) * SWA_BLOCK:(2 * j + 2) * SWA_BLOCK, :]
            o_ref[rows, j * LANES:(j + 1) * LANES] = jnp.where(low_half, even, odd).astype(o_ref.dtype)


def _swa_attention(qw, kw, vw, sink, B, S, tq):
    q3 = qw.reshape(B, S, SWA_HEADS * LANES)
    k3 = kw.reshape(B, S, SWA_KV_HEADS * LANES)
    v3 = vw.reshape(B, S, SWA_KV_HEADS * LANES)
    return pl.pallas_call(
        functools.partial(_swa_kernel, tq=tq, S=S),
        grid=(B, SWA_KV_HEADS, S // tq),
        in_specs=[pl.BlockSpec(memory_space=pltpu.SMEM),
                  pl.BlockSpec((None, tq, SWA_GROUP * LANES), lambda b, h, i: (b, i, h)),
                  pl.BlockSpec((None, S, LANES), lambda b, h, i: (b, 0, h)),
                  pl.BlockSpec((None, S, LANES), lambda b, h, i: (b, 0, h))],
        out_specs=pl.BlockSpec((None, tq, SWA_GROUP * SWA_HD), lambda b, h, i: (b, i, h)),
        out_shape=jax.ShapeDtypeStruct((B, S, SWA_HEADS * SWA_HD), BF16),
        compiler_params=pltpu.CompilerParams(
            dimension_semantics=("arbitrary", "arbitrary", "arbitrary"),
            vmem_limit_bytes=VMEM_LIMIT),
        name="swa_attention",
    )(sink, q3, k3, v3)


def _beats(vj, vi, j_first):
    return (vj >= vi) if j_first else (vj > vi)


def _route(scores, sel):
    G, P = N_GROUPS, EXPERTS_PER_GROUP
    groups = [sel[g * P:(g + 1) * P, :] for g in range(G)]
    row = lax.broadcasted_iota(jnp.int32, (P, 1), 0)
    gscore = []
    for vg in groups:
        m1 = jnp.max(vg, axis=0, keepdims=True)
        first = jnp.min(jnp.where(vg == m1, row, P), axis=0, keepdims=True)
        m2 = jnp.max(jnp.where(row == first, -jnp.inf, vg), axis=0, keepdims=True)
        gscore.append(m1 + m2)
    masked = []
    for g in range(G):
        rank = jnp.zeros_like(gscore[g], dtype=jnp.int32)
        for g2 in range(G):
            if g2 != g:
                rank = rank + _beats(gscore[g2], gscore[g], g2 < g).astype(jnp.int32)
        masked.append(jnp.where(rank < TOPK_GROUPS, groups[g], NEG_INF))
    index = [row + g * P for g in range(G)]
    chosen = [None] * G
    for _ in range(TOP_K):
        best = masked[0]
        for g in range(1, G):
            best = jnp.maximum(best, masked[g])
        best = jnp.max(best, axis=0, keepdims=True)
        first = jnp.where(masked[0] == best, index[0], N_EXPERTS)
        for g in range(1, G):
            first = jnp.minimum(first, jnp.where(masked[g] == best, index[g], N_EXPERTS))
        first = jnp.min(first, axis=0, keepdims=True)
        for g in range(G):
            hit = index[g] == first
            chosen[g] = hit if chosen[g] is None else (chosen[g] | hit)
            masked[g] = jnp.where(hit, -jnp.inf, masked[g])
    picked = [jnp.where(chosen[g], scores[g * P:(g + 1) * P, :], 0.0) for g in range(G)]
    total = picked[0]
    for g in range(1, G):
        total = total + picked[g]
    denom = jnp.sum(total, axis=0, keepdims=True)
    return [pk / denom * ROUTED_SCALE for pk in picked]


def _pack_pair(lo, hi):
    return pltpu.pack_elementwise([lo, hi], packed_dtype=BF16)


def _unpack_pair(word):
    lo = pltpu.unpack_elementwise(word, index=0, packed_dtype=BF16, unpacked_dtype=F32)
    hi = pltpu.unpack_elementwise(word, index=1, packed_dtype=BF16, unpacked_dtype=F32)
    return lo, hi


def _store_rows_dense(ref, words):
    for j in range(ROW_SUB):
        ref[pl.ds(j, words.shape[0], stride=ROW_SUB), :] = words[:, j * LANES:(j + 1) * LANES]


def _load_rows_dense(ref, rows):
    sub = ROW_SUB
    return jnp.concatenate([ref[pl.ds(j, rows, stride=sub), :] for j in range(sub)], axis=1)


def _post_kernel(om_ref, ow_ref, gate_ref, x_ref, wbm_ref, wbw_ref, wout_ref, g_moe_ref, wr_ref,
                 rb_ref, wsgu_ref, wsd_ref, x1_ref, hp_ref, comb_ref):
    am = _dot(om_ref[...], wbm_ref[...])
    aw = _dot(ow_ref[...], wbw_ref[...])
    gates = gate_ref[...].astype(F32)
    merged = gates[:, :D_MODEL] * am + gates[:, D_MODEL:] * aw
    x1 = x_ref[...] + _dot(merged.astype(BF16), wout_ref[...])

    h2 = _rms(x1, D_MODEL) * g_moe_ref[...]
    h2b = h2.astype(BF16)
    _store_rows_dense(hp_ref, _pack_pair(h2[:, :ROW_WORDS], h2[:, ROW_WORDS:]))

    wr = wr_ref[...]
    w_hi = wr.astype(BF16)
    w_lo = (wr - w_hi.astype(F32)).astype(BF16)
    h_lo = (h2 - h2b.astype(F32)).astype(BF16)
    by_hi = _dot_nt(jnp.concatenate([w_hi, w_lo], axis=0), h2b)
    logits = by_hi[:N_EXPERTS] + by_hi[N_EXPERTS:] + _dot_nt(w_hi, h_lo)

    sgu = _dot(h2b, wsgu_ref[...])
    sh = jax.nn.silu(sgu[:, :SHARED_FF]) * sgu[:, SHARED_FF:]
    x1_ref[...] = x1 + _dot(sh.astype(BF16), wsd_ref[...])

    scores = jax.nn.sigmoid(logits)
    comb = _route(scores, scores + rb_ref[...])
    for g in range(N_GROUPS):
        comb_ref[g * EXPERTS_PER_GROUP:(g + 1) * EXPERTS_PER_GROUP, :] = comb[g]


def _post_attention(om, ow, gates, x2d, wbm, wbw, wout, g_moe, wr_t, rbias, wsgu, wsd, tm):
    T = x2d.shape[0]
    row = lambda n: pl.BlockSpec((tm, n), lambda i: (i, 0))
    consts = [wbm, wbw, wout, g_moe, wr_t, rbias, wsgu, wsd]
    return pl.pallas_call(
        _post_kernel,
        grid=(T // tm,),
        in_specs=[row(om.shape[1]), row(ow.shape[1]), row(2 * D_MODEL), row(D_MODEL)]
        + [_full(c.shape) for c in consts],
        out_specs=[row(D_MODEL), pl.BlockSpec((tm * ROW_SUB, LANES), lambda i: (i, 0)),
                   pl.BlockSpec((N_EXPERTS, tm), lambda i: (0, i))],
        out_shape=[jax.ShapeDtypeStruct((T, D_MODEL), F32),
                   jax.ShapeDtypeStruct((T * ROW_SUB, LANES), jnp.uint32),
                   jax.ShapeDtypeStruct((N_EXPERTS, T), F32)],
        compiler_params=pltpu.CompilerParams(dimension_semantics=("arbitrary",),
                                             vmem_limit_bytes=VMEM_LIMIT),
        name="post_attention",
    )(om, ow, gates, x2d, *consts)


SLOT_ROWS = -(-(MOE_CHUNK * TOP_K + N_EXPERTS * (SLOT_ALIGN - 1) + MOE_TILE_BIG) // MOE_TILE) * MOE_TILE
DUMMY_SLOT = SLOT_ROWS - 1
PLAN_BLOCK = 256


def _plan_kernel(comb_ref, slot_ref, w_ref, off_ref, cnt_ref):
    comb = comb_ref[...]
    sel = comb > 0.0
    m = sel.astype(F32)
    mb = m.astype(BF16)
    r_i = lax.broadcasted_iota(jnp.int32, (PLAN_BLOCK, PLAN_BLOCK), 0)
    c_i = lax.broadcasted_iota(jnp.int32, (PLAN_BLOCK, PLAN_BLOCK), 1)
    before = (r_i < c_i).astype(BF16)
    carry = jnp.zeros((N_EXPERTS, 1), F32)
    ranks = []
    for b in range(MOE_CHUNK // PLAN_BLOCK):
        blk = slice(b * PLAN_BLOCK, (b + 1) * PLAN_BLOCK)
        ranks.append(_dot(mb[:, blk], before) + carry)
        carry = carry + jnp.sum(m[:, blk], axis=1, keepdims=True)
    rank = jnp.concatenate(ranks, axis=1)
    cnt = carry
    cnt_pad = jnp.floor((cnt + (SLOT_ALIGN - 1)) * (1.0 / SLOT_ALIGN)) * SLOT_ALIGN
    e_r = lax.broadcasted_iota(jnp.int32, (N_EXPERTS, N_EXPERTS), 0)
    e_c = lax.broadcasted_iota(jnp.int32, (N_EXPERTS, N_EXPERTS), 1)
    below = (e_c < e_r).astype(F32)
    off = jnp.dot(below, jnp.broadcast_to(cnt_pad, (N_EXPERTS, LANES)),
                  precision=lax.Precision.HIGHEST, preferred_element_type=F32)
    slot = off[:, :1] + rank
    kidx = _dot(below.astype(BF16), mb)
    row = lax.broadcasted_iota(jnp.int32, (TOP_K, 1), 0)
    slot_acc = jnp.zeros((TOP_K, MOE_CHUNK), F32)
    w_acc = jnp.zeros((TOP_K, MOE_CHUNK), F32)
    for k in range(TOP_K):
        pick = jnp.where(sel & (kidx == k), 1.0, 0.0)
        found = jnp.sum(pick, axis=0, keepdims=True) > 0.0
        s_k = jnp.where(found, jnp.sum(pick * slot, axis=0, keepdims=True), float(DUMMY_SLOT))
        w_k = jnp.sum(pick * comb, axis=0, keepdims=True)
        slot_acc = jnp.where(row == k, s_k, slot_acc)
        w_acc = jnp.where(row == k, w_k, w_acc)
    slot_ref[...] = slot_acc.astype(jnp.int32) * ROW_SUB
    w_ref[...] = w_acc
    off_ref[...] = off.astype(jnp.int32)
    cnt_ref[...] = jnp.broadcast_to(cnt, (N_EXPERTS, LANES)).astype(jnp.int32)


def _moe_plan(comb_t):
    T = comb_t.shape[1]
    nch = T // MOE_CHUNK
    per_pair = pl.BlockSpec((None, TOP_K, MOE_CHUNK), lambda c: (c, 0, 0))
    per_expert = pl.BlockSpec((None, N_EXPERTS, LANES), lambda c: (c, 0, 0))
    return pl.pallas_call(
        _plan_kernel,
        grid=(nch,),
        in_specs=[pl.BlockSpec((N_EXPERTS, MOE_CHUNK), lambda c: (0, c))],
        out_specs=[per_pair, per_pair, per_expert, per_expert],
        out_shape=[jax.ShapeDtypeStruct((nch, TOP_K, MOE_CHUNK), jnp.int32),
                   jax.ShapeDtypeStruct((nch, TOP_K, MOE_CHUNK), F32),
                   jax.ShapeDtypeStruct((nch, N_EXPERTS, LANES), jnp.int32),
                   jax.ShapeDtypeStruct((nch, N_EXPERTS, LANES), jnp.int32)],
        compiler_params=pltpu.CompilerParams(dimension_semantics=("arbitrary",),
                                             vmem_limit_bytes=VMEM_LIMIT),
        name="moe_plan",
    )(comb_t)


def _slab_at(ref, first):
    return ref.at[pl.ds(pl.multiple_of(first, ROW_SUB), ROW_SUB), :]


def _slab(ref, row):
    return _slab_at(ref, row * ROW_SUB)


def _moe_kernel(off_ref, cnt_ref, slot_hbm, w_hbm, hp_ref, wgu_ref, wd_ref, o_ref,
                buf, clo, chi, *smem_and_sem):
    slot_s = smem_and_sem[:TOP_K]
    w_s = smem_and_sem[TOP_K:2 * TOP_K]
    sem = smem_and_sem[2 * TOP_K]
    c = pl.program_id(0)
    s = pl.program_id(1)

    @pl.when(s == 0)
    def _dispatch():
        copies = [pltpu.make_async_copy(slot_hbm.at[c, k], slot_s[k], sem.at[k]) for k in range(TOP_K)]
        copies += [pltpu.make_async_copy(w_hbm.at[c, k], w_s[k], sem.at[TOP_K + k]) for k in range(TOP_K)]
        for cp in copies:
            cp.start()

        @pl.when(c == 0)
        def _():
            buf[...] = jnp.zeros_like(buf)

        for cp in copies:
            cp.wait()

        def scatter(tb, carry):
            for tt in range(8):
                t = tb * 8 + tt
                slab = _slab(hp_ref, t)[...]
                for k in range(TOP_K):
                    _slab_at(buf, slot_s[k][t])[...] = slab
            return carry

        lax.fori_loop(0, MOE_CHUNK // 8, scatter, 0)

    def expert(ee):
        e = s * EXPERTS_PER_STEP + ee
        n = cnt_ref[c, e]
        off = off_ref[c, e]

        def ffn(start, rows):
            view = buf.at[pl.ds(pl.multiple_of((off + start) * ROW_SUB, SLOT_ALIGN * ROW_SUB),
                                rows * ROW_SUB), :]
            x_lo, x_hi = _unpack_pair(_load_rows_dense(view, rows))
            gu = (_dot(x_lo.astype(BF16), wgu_ref[ee, :ROW_WORDS, :])
                  + _dot(x_hi.astype(BF16), wgu_ref[ee, ROW_WORDS:, :]))
            hid = jax.nn.silu(gu[:, :EXPERT_FF]) * gu[:, EXPERT_FF:]
            y = _dot(hid.astype(BF16), wd_ref[ee])
            mine = lax.broadcasted_iota(jnp.int32, (rows, 1), 0) < (n - start)
            _store_rows_dense(view, _pack_pair(jnp.where(mine, y[:, :ROW_WORDS], x_lo),
                                               jnp.where(mine, y[:, ROW_WORDS:], x_hi)))

        n_big = (n + MOE_TILE_BIG - MOE_TILE - 1) // MOE_TILE_BIG

        def big(r, carry):
            ffn(r * MOE_TILE_BIG, MOE_TILE_BIG)
            return carry

        lax.fori_loop(0, n_big, big, 0)

        @pl.when(n > n_big * MOE_TILE_BIG)
        def _():
            ffn(n_big * MOE_TILE_BIG, MOE_TILE)

    @pl.when(s < EXPERT_STEPS)
    def _experts():
        for ee in range(EXPERTS_PER_STEP):
            expert(ee)

    @pl.when(s >= EXPERT_STEPS)
    def _combine():
        t0 = (s - EXPERT_STEPS) * COMBINE_BLOCK

        def gather(i, carry):
            for tt in range(COMBINE_UNROLL):
                tl = i * COMBINE_UNROLL + tt
                t = t0 + tl
                lo, hi = _unpack_pair(_slab_at(buf, slot_s[0][t])[...])
                acc_lo = w_s[0][t] * lo
                acc_hi = w_s[0][t] * hi
                for k in range(1, TOP_K):
                    lo, hi = _unpack_pair(_slab_at(buf, slot_s[k][t])[...])
                    wk = w_s[k][t]
                    acc_lo = acc_lo + wk * lo
                    acc_hi = acc_hi + wk * hi
                _slab(clo, tl)[...] = acc_lo
                _slab(chi, tl)[...] = acc_hi
            return carry

        lax.fori_loop(0, COMBINE_BLOCK // COMBINE_UNROLL, gather, 0)
        o_ref[:, :ROW_WORDS] = _load_rows_dense(clo, COMBINE_BLOCK)
        o_ref[:, ROW_WORDS:] = _load_rows_dense(chi, COMBINE_BLOCK)


def _moe_sparse(offs, cnts, slots, wts, hp, w_gu, w_d):
    nch = offs.shape[0]
    T = nch * MOE_CHUNK
    blocks = MOE_CHUNK // COMBINE_BLOCK
    expert = lambda c, s, *_: (jnp.minimum(s, EXPERT_STEPS - 1), 0, 0)
    grid_spec = pltpu.PrefetchScalarGridSpec(
        num_scalar_prefetch=2,
        grid=(nch, EXPERT_STEPS + blocks),
        in_specs=[pl.BlockSpec(memory_space=pl.ANY),
                  pl.BlockSpec(memory_space=pl.ANY),
                  pl.BlockSpec((MOE_CHUNK * ROW_SUB, LANES), lambda c, s, *_: (c, 0),
                               pipeline_mode=pl.Buffered(1)),
                  pl.BlockSpec((EXPERTS_PER_STEP, D_MODEL, 2 * EXPERT_FF), expert),
                  pl.BlockSpec((EXPERTS_PER_STEP, EXPERT_FF, D_MODEL), expert)],
        out_specs=pl.BlockSpec(
            (COMBINE_BLOCK, D_MODEL),
            lambda c, s, *_: (c * blocks + jnp.maximum(s - EXPERT_STEPS, 0), 0)),
        scratch_shapes=[pltpu.VMEM((SLOT_ROWS * ROW_SUB, LANES), jnp.uint32),
                        pltpu.VMEM((COMBINE_BLOCK * ROW_SUB, LANES), F32),
                        pltpu.VMEM((COMBINE_BLOCK * ROW_SUB, LANES), F32),
                        *[pltpu.SMEM((MOE_CHUNK,), jnp.int32) for _ in range(TOP_K)],
                        *[pltpu.SMEM((MOE_CHUNK,), F32) for _ in range(TOP_K)],
                        pltpu.SemaphoreType.DMA((2 * TOP_K,))])
    return pl.pallas_call(
        _moe_kernel,
        grid_spec=grid_spec,
        out_shape=jax.ShapeDtypeStruct((T, D_MODEL), F32),
        compiler_params=pltpu.CompilerParams(dimension_semantics=("arbitrary", "arbitrary"),
                                             vmem_limit_bytes=VMEM_LIMIT),
        name="moe_experts",
    )(offs, cnts, slots, wts, hp, w_gu, w_d)


def _ple_kernel(x1_ref, r_ref, p_ref, g_ref, wg_ref, b_ref, wp_ref, o_ref):
    x2 = x1_ref[...] + r_ref[...]
    hn = (_rms(x2, D_MODEL) * g_ref[...]).astype(BF16)
    gate = jax.nn.sigmoid(_dot(hn, wg_ref[...]) + b_ref[...])
    o_ref[...] = x2 + gate * _dot(p_ref[...].astype(BF16), wp_ref[...])


def _ple(x1s, routed, p2d, g_ple, wg, b_ple, wp, tm):
    T = x1s.shape[0]
    row = lambda n: pl.BlockSpec((tm, n), lambda i: (i, 0))
    consts = [g_ple, wg, b_ple, wp]
    return pl.pallas_call(
        _ple_kernel,
        grid=(T // tm,),
        in_specs=[row(D_MODEL), row(D_MODEL), row(PLE_DIM)] + [_full(c.shape) for c in consts],
        out_specs=row(D_MODEL),
        out_shape=jax.ShapeDtypeStruct((T, D_MODEL), F32),
        compiler_params=pltpu.CompilerParams(dimension_semantics=("arbitrary",),
                                             vmem_limit_bytes=VMEM_LIMIT),
        name="ple",
    )(x1s, routed, p2d, *consts)


def _lane_map(*runs):
    src = np.full((LANES,), -1)
    for lane, dim, n in runs:
        src[lane:lane + n] = np.arange(dim, dim + n)
    return src


_MLA_HALF = MLA_ROPE // 2
_SWA_HALF = SWA_HD // 2
MLA_LANES = _lane_map((0, MLA_NOPE, _MLA_HALF), (_MLA_HALF, 0, LANES // 2 - _MLA_HALF),
                      (LANES // 2, MLA_NOPE + _MLA_HALF, _MLA_HALF),
                      (LANES // 2 + _MLA_HALF, LANES // 2 - _MLA_HALF, MLA_NOPE - LANES // 2 + _MLA_HALF))
MLA_NOPE_LANES = np.where(MLA_LANES < MLA_NOPE, MLA_LANES, -1)
MLA_ROPE_LANES = np.where(MLA_LANES >= MLA_NOPE, MLA_LANES - MLA_NOPE, -1)
SWA_LANES = _lane_map((LANES // 2 - _SWA_HALF, 0, _SWA_HALF), (LANES - _SWA_HALF, _SWA_HALF, _SWA_HALF))


def _spread(w, heads, lane_src):
    k = w.shape[0]
    dim = w.shape[1] // heads
    w = jnp.pad(w.reshape(k, heads, dim), ((0, 0), (0, 0), (0, 1)))
    return w[:, :, np.where(lane_src < 0, dim, lane_src)].reshape(k, heads * LANES)


def _rope_table():
    def inv_freq(dim):
        return 1.0 / (ROPE_THETA ** (jnp.arange(0, dim, 2, dtype=F32) / dim))

    def selector(lane_src, half):
        sel = np.where(lane_src < 0, 0.0, np.where(lane_src < half, -1.0, 1.0))
        return jnp.asarray(sel, F32)

    sel_m = selector(MLA_ROPE_LANES, _MLA_HALF)
    sel_s = selector(SWA_LANES, _SWA_HALF)
    freq_m = _spread(jnp.tile(inv_freq(MLA_ROPE), 2)[None], 1, MLA_ROPE_LANES)[0]
    freq_s = _spread(jnp.tile(inv_freq(SWA_HD), 2)[None], 1, SWA_LANES)[0]
    zero = jnp.zeros((LANES,), F32)
    rows = [freq_m + freq_s, sel_m, jnp.abs(sel_m), sel_s, jnp.abs(sel_s), zero, zero, zero]
    return jnp.stack(rows)


def _layer(x2d, p2d, pos2d, B, S, g_mix, w_in, b_gate, g_cq, w_uq, g_ckv, w_ukv, g_qn_mla, g_kn_mla,
           g_qn_swa, g_kn_swa, sink, w_br_mla, w_br_swa, w_out, g_moe, w_router, router_bias,
           w_exp_gu, w_exp_down, w_sh_gu, w_sh_down, g_ple, w_ple_gate, b_ple, w_ple_proj):
    w_kr = _spread(w_in[:, OFF_CKV:OFF_KR], 1, MLA_ROPE_LANES)
    w_vs = w_in[:, OFF_KS:OFF_VS].reshape(D_MODEL, SWA_KV_HEADS, 1, SWA_HD)
    w_vs = jnp.broadcast_to(w_vs, (D_MODEL, SWA_KV_HEADS, 2, SWA_HD)).reshape(D_MODEL, -1)
    w_all = jnp.concatenate([
        w_in[:, :OFF_CKV], w_kr,
        _spread(w_in[:, OFF_KR:OFF_QS], SWA_HEADS, SWA_LANES),
        _spread(w_in[:, OFF_QS:OFF_KS], SWA_KV_HEADS, SWA_LANES),
        w_vs, w_in[:, OFF_VS:]], axis=1).astype(BF16)
    assert w_all.shape[1] == C_END
    w_uq_p = _spread(w_uq, MLA_HEADS, MLA_LANES).astype(BF16)
    w_ukv3 = w_ukv.reshape(MLA_KV_RANK, MLA_HEADS, MLA_NOPE + MLA_V)
    w_k = _spread(w_ukv3[:, :, :MLA_NOPE].reshape(MLA_KV_RANK, -1), MLA_HEADS, MLA_NOPE_LANES).astype(BF16)
    w_v = w_ukv3[:, :, MLA_NOPE:].reshape(MLA_KV_RANK, -1).astype(BF16)

    qm, km, vm, qw, kw, vw, gates = _pre_attention(
        x2d, pos2d, w_all, w_uq_p, w_k, w_v, g_mix[None], g_cq[None], g_ckv[None],
        _spread(g_qn_mla[None], 1, MLA_LANES), _spread(g_kn_mla[None], 1, MLA_LANES),
        _spread(g_qn_swa[None], 1, SWA_LANES), _spread(g_kn_swa[None], 1, SWA_LANES),
        b_gate[None], _rope_table(), tm=PRE_TILE)

    om, w_gu_bf, w_down_bf = _mla_attention(qm, km, vm, w_exp_gu, w_exp_down, B, S, tq=MLA_Q_TILE)
    om = om.reshape(B * S, -1)
    ow = _swa_attention(qw, kw, vw, sink, B, S, tq=SWA_Q_TILE).reshape(B * S, -1)

    x1s, hp, comb_t = _post_attention(
        om, ow, gates, x2d, w_br_mla.astype(BF16), w_br_swa.astype(BF16), w_out.astype(BF16),
        g_moe[None], w_router.T, router_bias[:, None], w_sh_gu.astype(BF16),
        w_sh_down.astype(BF16), tm=ROW_TILE)

    slots, wts, offs, cnts = _moe_plan(comb_t)
    routed = _moe_sparse(offs[:, :, 0], cnts[:, :, 0], slots, wts, hp, w_gu_bf, w_down_bf)

    return _ple(x1s, routed, p2d, g_ple[None], w_ple_gate.astype(BF16), b_ple[None],
                w_ple_proj.astype(BF16), tm=ROW_TILE)


def kernel(x, p, positions, g_mix, w_in, b_gate, g_cq, w_uq, g_ckv, w_ukv, g_qn_mla, g_kn_mla, g_qn_swa, g_kn_swa, sink, w_br_mla, w_br_swa, w_out, g_moe, w_router, router_bias, w_exp_gu, w_exp_down, w_sh_gu, w_sh_down, g_ple, w_ple_gate, b_ple, w_ple_proj):
    B, S, D = x.shape
    x2d = x.reshape(B * S, D)
    pos2d = positions.reshape(B * S, 1)
    for i in range(p.shape[0]):
        x2d = _layer(x2d, p[i].reshape(B * S, -1), pos2d, B, S, g_mix[i], w_in[i], b_gate[i],
                     g_cq[i], w_uq[i], g_ckv[i], w_ukv[i], g_qn_mla[i], g_kn_mla[i], g_qn_swa[i],
                     g_kn_swa[i], sink[i], w_br_mla[i], w_br_swa[i], w_out[i], g_moe[i],
                     w_router[i], router_bias[i], w_exp_gu[i], w_exp_down[i], w_sh_gu[i],
                     w_sh_down[i], g_ple[i], w_ple_gate[i], b_ple[i], w_ple_proj[i])
    return x2d.reshape(B, S, D)
```

```python
import functools

import jax
import jax.numpy as jnp
import numpy as np
from jax import lax
from jax.experimental import pallas as pl
from jax.experimental.pallas import tpu as pltpu

D_MODEL = 1024
PLE_DIM = 256
ROPE_THETA = 10000.0
EPS = 1e-6
NEG_INF = -1e30

MLA_HEADS = 8
MLA_Q_RANK = 384
MLA_KV_RANK = 256
MLA_NOPE = 64
MLA_ROPE = 32
MLA_QK = MLA_NOPE + MLA_ROPE
MLA_V = 64

SWA_HEADS = 8
SWA_KV_HEADS = 2
SWA_GROUP = SWA_HEADS // SWA_KV_HEADS
SWA_HD = 64
WINDOW = 128

OFF_CQ = MLA_Q_RANK
OFF_CKV = OFF_CQ + MLA_KV_RANK
OFF_KR = OFF_CKV + MLA_ROPE
OFF_QS = OFF_KR + SWA_HEADS * SWA_HD
OFF_KS = OFF_QS + SWA_KV_HEADS * SWA_HD
OFF_VS = OFF_KS + SWA_KV_HEADS * SWA_HD
OFF_GA = OFF_VS + D_MODEL

N_EXPERTS = 64
TOP_K = 8
N_GROUPS = 8
TOPK_GROUPS = 4
EXPERTS_PER_GROUP = N_EXPERTS // N_GROUPS
EXPERT_FF = 256
SHARED_FF = 256
ROUTED_SCALE = 2.5

LANES = 128
ROW_WORDS = D_MODEL // 2
ROW_SUB = ROW_WORDS // LANES
MOE_CHUNK = 2048
MOE_TILE = 128
MOE_TILE_BIG = 320
SLOT_ALIGN = 16
COMBINE_BLOCK = 256
COMBINE_UNROLL = 8
EXPERTS_PER_STEP = 4
EXPERT_STEPS = N_EXPERTS // EXPERTS_PER_STEP
PRE_TILE = 256
ROW_TILE = 512
MLA_Q_TILE = 512
MLA_KV_TILE = 256
LOG2E = 1.4426950408889634
SOFTMAX_UNDERFLOW_GUARD = 1e-30
SWA_Q_TILE = 512
SWA_BLOCK = 128
VMEM_LIMIT = 56 * 1024 * 1024

BF16 = jnp.bfloat16
F32 = jnp.float32

C_CQ = 0
C_CKV = C_CQ + MLA_Q_RANK
C_KR = C_CKV + MLA_KV_RANK
C_QS = C_KR + LANES
C_KS = C_QS + SWA_HEADS * LANES
C_VS = C_KS + SWA_KV_HEADS * LANES
C_GA = C_VS + SWA_KV_HEADS * LANES
C_END = C_GA + 2 * D_MODEL


def _full(shape):
    nd = len(shape)
    return pl.BlockSpec(shape, lambda *_: (0,) * nd)


def _dot(a, b):
    return jnp.dot(a, b, preferred_element_type=F32)


def _dot_nt(a, b, precision=None):
    return lax.dot_general(a, b, (((1,), (1,)), ((), ())), precision=precision,
                           preferred_element_type=F32)


def _rms(v, n):
    return v * lax.rsqrt(jnp.sum(v * v, axis=-1, keepdims=True) * (1.0 / n) + EPS)


def _rope(v, cos, sin):
    return v * cos + pltpu.roll(v, LANES // 2, 1) * sin


def _pre_kernel(x_ref, pos_ref, w_all_ref, w_uq_ref, w_k_ref, w_v_ref, g_mix_ref, g_cq_ref,
                g_ckv_ref, gq_m_ref, gk_m_ref, gq_s_ref, gk_s_ref, b_gate_ref, rope_ref,
                qm_ref, km_ref, vm_ref, qw_ref, kw_ref, vw_ref, gate_ref):
    x = x_ref[...]
    h = (_rms(x, D_MODEL) * g_mix_ref[...]).astype(BF16)

    def proj(lo, hi):
        return _dot(h, w_all_ref[:, lo:hi])

    z_lat = proj(C_CQ, C_QS)
    z_qs = proj(C_QS, C_KS)
    half = (C_END - C_GA) // 2
    z_ga = proj(C_GA, C_GA + half)

    pos = pos_ref[...].astype(F32)
    rope = rope_ref[...]
    ang = pos * rope[0:1, :]
    cos_m1 = jnp.cos(ang) - 1.0
    sin = jnp.sin(ang)
    cos_m, sin_m = 1.0 + cos_m1 * rope[2:3, :], sin * rope[1:2, :]
    cos_s, sin_s = 1.0 + cos_m1 * rope[4:5, :], sin * rope[3:4, :]

    cqn = (_rms(z_lat[:, C_CQ:C_CKV], MLA_Q_RANK) * g_cq_ref[...]).astype(BF16)
    ckvn = (_rms(z_lat[:, C_CKV:C_KR], MLA_KV_RANK) * g_ckv_ref[...]).astype(BF16)
    q = _dot(cqn, w_uq_ref[...])
    gq_m = gq_m_ref[...]
    for hd in range(MLA_HEADS):
        qh = _rms(q[:, hd * LANES:(hd + 1) * LANES], MLA_QK) * gq_m
        qh = _rope(qh, cos_m, sin_m) * (MLA_QK ** -0.5 * LOG2E)
        qm_ref[:, hd * LANES:(hd + 1) * LANES] = qh.astype(BF16)

    kn = _dot(ckvn, w_k_ref[...])
    vm_ref[...] = _dot(ckvn, w_v_ref[...]).astype(BF16)
    z_kv = proj(C_KS, C_GA)

    gq_s = gq_s_ref[...]
    for hd in range(SWA_HEADS):
        qh = _rms(z_qs[:, hd * LANES:(hd + 1) * LANES], SWA_HD) * gq_s
        qh = _rope(qh, cos_s, sin_s) * (SWA_HD ** -0.5 * LOG2E)
        qw_ref[:, hd * LANES:(hd + 1) * LANES] = qh.astype(BF16)

    gk_m = gk_m_ref[...]
    kr = z_lat[:, C_KR:C_QS]
    ss_kr = jnp.sum(kr * kr, axis=-1, keepdims=True)
    kr_rot = _rope(kr * gk_m, cos_m, sin_m)
    for hd in range(MLA_HEADS):
        kh = kn[:, hd * LANES:(hd + 1) * LANES]
        ss = jnp.sum(kh * kh, axis=-1, keepdims=True) + ss_kr
        sc = lax.rsqrt(ss * (1.0 / MLA_QK) + EPS)
        km_ref[:, hd * LANES:(hd + 1) * LANES] = ((kh * gk_m + kr_rot) * sc).astype(BF16)

    z_gb = proj(C_GA + half, C_END)

    gk_s = gk_s_ref[...]
    for hd in range(SWA_KV_HEADS):
        kh = _rms(z_kv[:, hd * LANES:(hd + 1) * LANES], SWA_HD) * gk_s
        kh = _rope(kh, cos_s, sin_s)
        kw_ref[:, hd * LANES:(hd + 1) * LANES] = kh.astype(BF16)
    vw_ref[...] = z_kv[:, C_VS - C_KS:].astype(BF16)

    gate_ref[:, :half] = jax.nn.sigmoid(z_ga + b_gate_ref[:, :half]).astype(BF16)
    gate_ref[:, half:] = jax.nn.sigmoid(z_gb + b_gate_ref[:, half:]).astype(BF16)


def _pre_attention(x2d, pos2d, w_all, w_uq, w_k, w_v, g_mix, g_cq, g_ckv, gq_m, gk_m, gq_s,
                   gk_s, b_gate, rope_tab, tm):
    T = x2d.shape[0]
    row = lambda n: pl.BlockSpec((tm, n), lambda i: (i, 0))
    outs = [(MLA_HEADS * LANES, BF16), (MLA_HEADS * LANES, BF16), (MLA_HEADS * MLA_V, BF16),
            (SWA_HEADS * LANES, BF16), (SWA_KV_HEADS * LANES, BF16),
            (SWA_KV_HEADS * LANES, BF16), (2 * D_MODEL, BF16)]
    consts = [w_all, w_uq, w_k, w_v, g_mix, g_cq, g_ckv, gq_m, gk_m, gq_s, gk_s, b_gate, rope_tab]
    return pl.pallas_call(
        _pre_kernel,
        grid=(T // tm,),
        in_specs=[row(D_MODEL), row(1)] + [_full(c.shape) for c in consts],
        out_specs=[row(n) for n, _ in outs],
        out_shape=[jax.ShapeDtypeStruct((T, n), dt) for n, dt in outs],
        compiler_params=pltpu.CompilerParams(dimension_semantics=("arbitrary",),
                                             vmem_limit_bytes=VMEM_LIMIT),
        name="pre_attention",
    )(x2d, pos2d, *consts)


def _half_masks(dtype):
    lane = lax.broadcasted_iota(jnp.int32, (1, LANES), 1)
    lo = (lane < LANES // 2).astype(dtype)
    return lo, 1 - lo


def _mla_exact(q_ref, k_ref, v_ref, o_ref):
    v = v_ref[...]
    masks = _half_masks(v.dtype)
    acc = None
    for hh in range(2):
        q = q_ref[:, hh * LANES:(hh + 1) * LANES]
        k = k_ref[:, hh * LANES:(hh + 1) * LANES]
        s = _dot_nt(q, k)
        m = jnp.max(s, axis=-1, keepdims=True)
        p = jnp.exp2(s - m)
        l = jnp.sum(p, axis=-1, keepdims=True)
        o = _dot(p.astype(BF16), v * masks[hh]) / l
        acc = o if acc is None else acc + o
    o_ref[...] = acc.astype(o_ref.dtype)


def _mla_kernel(q_ref, k_ref, v_ref, wa_ref, wb_ref, o_ref, wa_out, wb_out, kmax_ref, vt_ref):
    @pl.when(pl.program_id(2) == 0)
    def _():
        for hh in range(2):
            k = k_ref[:, hh * LANES:(hh + 1) * LANES].astype(F32)
            kn2 = jnp.max(jnp.sum(k * k, axis=-1, keepdims=True), axis=0, keepdims=True)
            kmax_ref[hh] = jnp.broadcast_to(jnp.sqrt(kn2), kmax_ref.shape[1:])
        vt_ref[...] = v_ref[...].astype(F32).T.astype(BF16)

    wa_out[...] = wa_ref[...].astype(BF16)
    wb_out[...] = wb_ref[...].astype(BF16)

    ones = jnp.ones((8, LANES), BF16)
    outs = []
    lmin = None
    for hh in range(2):
        q = q_ref[:, hh * LANES:(hh + 1) * LANES]
        qf = q.astype(F32)
        q_sq = _dot_nt(ones, (qf * qf).astype(BF16))[0:1, :]
        bound = jnp.sqrt(q_sq) * kmax_ref[hh][0:1, 0:1]
        p_t = jnp.exp2(_dot_nt(k_ref[:, hh * LANES:(hh + 1) * LANES], q) - bound)
        l = jnp.sum(p_t, axis=0, keepdims=True)
        o_t = _dot(vt_ref[hh * MLA_V:(hh + 1) * MLA_V, :], p_t.astype(BF16))
        outs.append(o_t / l)
        lm = jnp.min(l)
        lmin = lm if lmin is None else jnp.minimum(lmin, lm)
    o_ref[...] = jnp.concatenate(outs, axis=0).T.astype(o_ref.dtype)

    @pl.when(jnp.logical_not(lmin > SOFTMAX_UNDERFLOW_GUARD))
    def _():
        _mla_exact(q_ref, k_ref, v_ref, o_ref)


def _mla_attention(qm, km, vm, w_a, w_b, B, S, tq):
    pairs = MLA_HEADS // 2
    q3 = qm.reshape(B, S, MLA_HEADS * LANES)
    k3 = km.reshape(B, S, MLA_HEADS * LANES)
    v3 = vm.reshape(B, S, MLA_HEADS * MLA_V)
    steps = B * pairs * (S // tq)
    wa3 = w_a.reshape(steps, -1, w_a.shape[-1])
    wb3 = w_b.reshape(steps, -1, w_b.shape[-1])
    step = lambda b, p, i: ((b * pairs + p) * (S // tq) + i, 0, 0)
    w_spec = lambda w: pl.BlockSpec((None,) + w.shape[1:], step)
    om, wa_bf, wb_bf = pl.pallas_call(
        _mla_kernel,
        grid=(B, pairs, S // tq),
        in_specs=[pl.BlockSpec((None, tq, 2 * LANES), lambda b, p, i: (b, i, p)),
                  pl.BlockSpec((None, S, 2 * LANES), lambda b, p, i: (b, 0, p)),
                  pl.BlockSpec((None, S, LANES), lambda b, p, i: (b, 0, p)),
                  w_spec(wa3), w_spec(wb3)],
        out_specs=[pl.BlockSpec((None, tq, LANES), lambda b, p, i: (b, i, p)),
                   w_spec(wa3), w_spec(wb3)],
        out_shape=[jax.ShapeDtypeStruct((B, S, MLA_HEADS * MLA_V), BF16),
                   jax.ShapeDtypeStruct(wa3.shape, BF16),
                   jax.ShapeDtypeStruct(wb3.shape, BF16)],
        scratch_shapes=[pltpu.VMEM((2, 8, LANES), F32), pltpu.VMEM((LANES, S), BF16)],
        compiler_params=pltpu.CompilerParams(
            dimension_semantics=("arbitrary", "arbitrary", "arbitrary"),
            vmem_limit_bytes=VMEM_LIMIT),
        name="mla_attention",
    )(q3, k3, v3, wa3, wb3)
    return om, wa_bf.reshape(w_a.shape), wb_bf.reshape(w_b.shape)


def _swa_kernel(sink_ref, q_ref, k_ref, v_ref, o_ref, *, tq, S):
    hk = pl.program_id(1)
    i = pl.program_id(2)
    tk = SWA_BLOCK + 2 * WINDOW
    row = lax.broadcasted_iota(jnp.int32, (SWA_GROUP * SWA_BLOCK, 1), 0)
    qoff = row & (SWA_BLOCK - 1)
    head = row // SWA_BLOCK
    sk = jnp.zeros((SWA_GROUP * SWA_BLOCK, 1), F32)
    for g in range(SWA_GROUP):
        sk = jnp.where(head == g, sink_ref[SWA_GROUP * hk + g] * LOG2E, sk)
    low_half = lax.broadcasted_iota(jnp.int32, (1, LANES), 1) < LANES // 2
    for sub in range(tq // SWA_BLOCK):
        rows = slice(sub * SWA_BLOCK, (sub + 1) * SWA_BLOCK)
        q0 = i * tq + sub * SWA_BLOCK
        kstart = pl.multiple_of(jnp.clip(q0 - WINDOW, 0, S - tk), WINDOW)
        k = k_ref[pl.ds(kstart, tk), :]
        v = v_ref[pl.ds(kstart, tk), :]
        q = jnp.concatenate([q_ref[rows, g * LANES:(g + 1) * LANES] for g in range(SWA_GROUP)], axis=0)
        kpos = kstart + lax.broadcasted_iota(jnp.int32, (1, tk), 1)
        valid = jnp.abs(kpos - (q0 + qoff[:SWA_BLOCK])) <= WINDOW
        s = _dot_nt(q, k).reshape(SWA_GROUP, SWA_BLOCK, tk)
        s = jnp.where(valid[None], s, NEG_INF).reshape(SWA_GROUP * SWA_BLOCK, tk)
        m = jnp.maximum(jnp.max(s, axis=-1, keepdims=True), sk)
        e = jnp.exp2(s - m)
        denom = jnp.sum(e, axis=-1, keepdims=True) + jnp.exp2(sk - m)
        o = _dot(e.astype(BF16), v) / denom
        for j in range(SWA_GROUP // 2):
            even = o[(2 * j) * SWA_BLOCK:(2 * j + 1) * SWA_BLOCK, :]
            odd = o[(2 * j + 1) * SWA_BLOCK:(2 * j + 2) * SWA_BLOCK, :]
            o_ref[rows, j * LANES:(j + 1) * LANES] = jnp.where(low_half, even, odd).astype(o_ref.dtype)


def _swa_attention(qw, kw, vw, sink, B, S, tq):
    q3 = qw.reshape(B, S, SWA_HEADS * LANES)
    k3 = kw.reshape(B, S, SWA_KV_HEADS * LANES)
    v3 = vw.reshape(B, S, SWA_KV_HEADS * LANES)
    return pl.pallas_call(
        functools.partial(_swa_kernel, tq=tq, S=S),
        grid=(B, SWA_KV_HEADS, S // tq),
        in_specs=[pl.BlockSpec(memory_space=pltpu.SMEM),
                  pl.BlockSpec((None, tq, SWA_GROUP * LANES), lambda b, h, i: (b, i, h)),
                  pl.BlockSpec((None, S, LANES), lambda b, h, i: (b, 0, h)),
                  pl.BlockSpec((None, S, LANES), lambda b, h, i: (b, 0, h))],
        out_specs=pl.BlockSpec((None, tq, SWA_GROUP * SWA_HD), lambda b, h, i: (b, i, h)),
        out_shape=jax.ShapeDtypeStruct((B, S, SWA_HEADS * SWA_HD), BF16),
        compiler_params=pltpu.CompilerParams(
            dimension_semantics=("arbitrary", "arbitrary", "arbitrary"),
            vmem_limit_bytes=VMEM_LIMIT),
        name="swa_attention",
    )(sink, q3, k3, v3)


def _beats(vj, vi, j_first):
    return (vj >= vi) if j_first else (vj > vi)


def _route(scores, sel):
    G, P = N_GROUPS, EXPERTS_PER_GROUP
    groups = [sel[g * P:(g + 1) * P, :] for g in range(G)]
    row = lax.broadcasted_iota(jnp.int32, (P, 1), 0)
    gscore = []
    for vg in groups:
        m1 = jnp.max(vg, axis=0, keepdims=True)
        first = jnp.min(jnp.where(vg == m1, row, P), axis=0, keepdims=True)
        m2 = jnp.max(jnp.where(row == first, -jnp.inf, vg), axis=0, keepdims=True)
        gscore.append(m1 + m2)
    masked = []
    for g in range(G):
        rank = jnp.zeros_like(gscore[g], dtype=jnp.int32)
        for g2 in range(G):
            if g2 != g:
                rank = rank + _beats(gscore[g2], gscore[g], g2 < g).astype(jnp.int32)
        masked.append(jnp.where(rank < TOPK_GROUPS, groups[g], NEG_INF))
    index = [row + g * P for g in range(G)]
    chosen = [None] * G
    for _ in range(TOP_K):
        best = masked[0]
        for g in range(1, G):
            best = jnp.maximum(best, masked[g])
        best = jnp.max(best, axis=0, keepdims=True)
        first = jnp.where(masked[0] == best, index[0], N_EXPERTS)
        for g in range(1, G):
            first = jnp.minimum(first, jnp.where(masked[g] == best, index[g], N_EXPERTS))
        first = jnp.min(first, axis=0, keepdims=True)
        for g in range(G):
            hit = index[g] == first
            chosen[g] = hit if chosen[g] is None else (chosen[g] | hit)
            masked[g] = jnp.where(hit, -jnp.inf, masked[g])
    picked = [jnp.where(chosen[g], scores[g * P:(g + 1) * P, :], 0.0) for g in range(G)]
    total = picked[0]
    for g in range(1, G):
        total = total + picked[g]
    denom = jnp.sum(total, axis=0, keepdims=True)
    return [pk / denom * ROUTED_SCALE for pk in picked]


def _pack_pair(lo, hi):
    return pltpu.pack_elementwise([lo, hi], packed_dtype=BF16)


def _unpack_pair(word):
    lo = pltpu.unpack_elementwise(word, index=0, packed_dtype=BF16, unpacked_dtype=F32)
    hi = pltpu.unpack_elementwise(word, index=1, packed_dtype=BF16, unpacked_dtype=F32)
    return lo, hi


def _store_rows_dense(ref, words):
    for j in range(ROW_SUB):
        ref[pl.ds(j, words.shape[0], stride=ROW_SUB), :] = words[:, j * LANES:(j + 1) * LANES]


def _load_rows_dense(ref, rows):
    sub = ROW_SUB
    return jnp.concatenate([ref[pl.ds(j, rows, stride=sub), :] for j in range(sub)], axis=1)


def _post_kernel(om_ref, ow_ref, gate_ref, x_ref, wbm_ref, wbw_ref, wout_ref, g_moe_ref, wr_ref,
                 rb_ref, wsgu_ref, wsd_ref, x1_ref, hp_ref, comb_ref):
    am = _dot(om_ref[...], wbm_ref[...])
    aw = _dot(ow_ref[...], wbw_ref[...])
    gates = gate_ref[...].astype(F32)
    merged = gates[:, :D_MODEL] * am + gates[:, D_MODEL:] * aw
    x1 = x_ref[...] + _dot(merged.astype(BF16), wout_ref[...])

    h2 = _rms(x1, D_MODEL) * g_moe_ref[...]
    h2b = h2.astype(BF16)
    _store_rows_dense(hp_ref, _pack_pair(h2[:, :ROW_WORDS], h2[:, ROW_WORDS:]))

    wr = wr_ref[...]
    w_hi = wr.astype(BF16)
    w_lo = (wr - w_hi.astype(F32)).astype(BF16)
    h_lo = (h2 - h2b.astype(F32)).astype(BF16)
    by_hi = _dot_nt(jnp.concatenate([w_hi, w_lo], axis=0), h2b)
    logits = by_hi[:N_EXPERTS] + by_hi[N_EXPERTS:] + _dot_nt(w_hi, h_lo)

    sgu = _dot(h2b, wsgu_ref[...])
    sh = jax.nn.silu(sgu[:, :SHARED_FF]) * sgu[:, SHARED_FF:]
    x1_ref[...] = x1 + _dot(sh.astype(BF16), wsd_ref[...])

    scores = jax.nn.sigmoid(logits)
    comb = _route(scores, scores + rb_ref[...])
    for g in range(N_GROUPS):
        comb_ref[g * EXPERTS_PER_GROUP:(g + 1) * EXPERTS_PER_GROUP, :] = comb[g]


def _post_attention(om, ow, gates, x2d, wbm, wbw, wout, g_moe, wr_t, rbias, wsgu, wsd, tm):
    T = x2d.shape[0]
    row = lambda n: pl.BlockSpec((tm, n), lambda i: (i, 0))
    consts = [wbm, wbw, wout, g_moe, wr_t, rbias, wsgu, wsd]
    return pl.pallas_call(
        _post_kernel,
        grid=(T // tm,),
        in_specs=[row(om.shape[1]), row(ow.shape[1]), row(2 * D_MODEL), row(D_MODEL)]
        + [_full(c.shape) for c in consts],
        out_specs=[row(D_MODEL), pl.BlockSpec((tm * ROW_SUB, LANES), lambda i: (i, 0)),
                   pl.BlockSpec((N_EXPERTS, tm), lambda i: (0, i))],
        out_shape=[jax.ShapeDtypeStruct((T, D_MODEL), F32),
                   jax.ShapeDtypeStruct((T * ROW_SUB, LANES), jnp.uint32),
                   jax.ShapeDtypeStruct((N_EXPERTS, T), F32)],
        compiler_params=pltpu.CompilerParams(dimension_semantics=("arbitrary",),
                                             vmem_limit_bytes=VMEM_LIMIT),
        name="post_attention",
    )(om, ow, gates, x2d, *consts)


SLOT_ROWS = -(-(MOE_CHUNK * TOP_K + N_EXPERTS * (SLOT_ALIGN - 1) + MOE_TILE_BIG) // MOE_TILE) * MOE_TILE
DUMMY_SLOT = SLOT_ROWS - 1
PLAN_BLOCK = 256


def _plan_kernel(comb_ref, slot_ref, w_ref, off_ref, cnt_ref):
    comb = comb_ref[...]
    sel = comb > 0.0
    m = sel.astype(F32)
    mb = m.astype(BF16)
    r_i = lax.broadcasted_iota(jnp.int32, (PLAN_BLOCK, PLAN_BLOCK), 0)
    c_i = lax.broadcasted_iota(jnp.int32, (PLAN_BLOCK, PLAN_BLOCK), 1)
    before = (r_i < c_i).astype(BF16)
    carry = jnp.zeros((N_EXPERTS, 1), F32)
    ranks = []
    for b in range(MOE_CHUNK // PLAN_BLOCK):
        blk = slice(b * PLAN_BLOCK, (b + 1) * PLAN_BLOCK)
        ranks.append(_dot(mb[:, blk], before) + carry)
        carry = carry + jnp.sum(m[:, blk], axis=1, keepdims=True)
    rank = jnp.concatenate(ranks, axis=1)
    cnt = carry
    cnt_pad = jnp.floor((cnt + (SLOT_ALIGN - 1)) * (1.0 / SLOT_ALIGN)) * SLOT_ALIGN
    e_r = lax.broadcasted_iota(jnp.int32, (N_EXPERTS, N_EXPERTS), 0)
    e_c = lax.broadcasted_iota(jnp.int32, (N_EXPERTS, N_EXPERTS), 1)
    below = (e_c < e_r).astype(F32)
    off = jnp.dot(below, jnp.broadcast_to(cnt_pad, (N_EXPERTS, LANES)),
                  precision=lax.Precision.HIGHEST, preferred_element_type=F32)
    slot = off[:, :1] + rank
    kidx = _dot(below.astype(BF16), mb)
    row = lax.broadcasted_iota(jnp.int32, (TOP_K, 1), 0)
    slot_acc = jnp.zeros((TOP_K, MOE_CHUNK), F32)
    w_acc = jnp.zeros((TOP_K, MOE_CHUNK), F32)
    for k in range(TOP_K):
        pick = jnp.where(sel & (kidx == k), 1.0, 0.0)
        found = jnp.sum(pick, axis=0, keepdims=True) > 0.0
        s_k = jnp.where(found, jnp.sum(pick * slot, axis=0, keepdims=True), float(DUMMY_SLOT))
        w_k = jnp.sum(pick * comb, axis=0, keepdims=True)
        slot_acc = jnp.where(row == k, s_k, slot_acc)
        w_acc = jnp.where(row == k, w_k, w_acc)
    slot_ref[...] = slot_acc.astype(jnp.int32) * ROW_SUB
    w_ref[...] = w_acc
    off_ref[...] = off.astype(jnp.int32)
    cnt_ref[...] = jnp.broadcast_to(cnt, (N_EXPERTS, LANES)).astype(jnp.int32)


def _moe_plan(comb_t):
    T = comb_t.shape[1]
    nch = T // MOE_CHUNK
    per_pair = pl.BlockSpec((None, TOP_K, MOE_CHUNK), lambda c: (c, 0, 0))
    per_expert = pl.BlockSpec((None, N_EXPERTS, LANES), lambda c: (c, 0, 0))
    return pl.pallas_call(
        _plan_kernel,
        grid=(nch,),
        in_specs=[pl.BlockSpec((N_EXPERTS, MOE_CHUNK), lambda c: (0, c))],
        out_specs=[per_pair, per_pair, per_expert, per_expert],
        out_shape=[jax.ShapeDtypeStruct((nch, TOP_K, MOE_CHUNK), jnp.int32),
                   jax.ShapeDtypeStruct((nch, TOP_K, MOE_CHUNK), F32),
                   jax.ShapeDtypeStruct((nch, N_EXPERTS, LANES), jnp.int32),
                   jax.ShapeDtypeStruct((nch, N_EXPERTS, LANES), jnp.int32)],
        compiler_params=pltpu.CompilerParams(dimension_semantics=("arbitrary",),
                                             vmem_limit_bytes=VMEM_LIMIT),
        name="moe_plan",
    )(comb_t)


def _slab_at(ref, first):
    return ref.at[pl.ds(pl.multiple_of(first, ROW_SUB), ROW_SUB), :]


def _slab(ref, row):
    return _slab_at(ref, row * ROW_SUB)


def _moe_kernel(off_ref, cnt_ref, slot_hbm, w_hbm, hp_ref, wgu_ref, wd_ref, o_ref,
                buf, clo, chi, *smem_and_sem):
    slot_s = smem_and_sem[:TOP_K]
    w_s = smem_and_sem[TOP_K:2 * TOP_K]
    sem = smem_and_sem[2 * TOP_K]
    c = pl.program_id(0)
    s = pl.program_id(1)

    @pl.when(s == 0)
    def _dispatch():
        copies = [pltpu.make_async_copy(slot_hbm.at[c, k], slot_s[k], sem.at[k]) for k in range(TOP_K)]
        copies += [pltpu.make_async_copy(w_hbm.at[c, k], w_s[k], sem.at[TOP_K + k]) for k in range(TOP_K)]
        for cp in copies:
            cp.start()

        @pl.when(c == 0)
        def _():
            buf[...] = jnp.zeros_like(buf)

        for cp in copies:
            cp.wait()

        def scatter(tb, carry):
            for tt in range(8):
                t = tb * 8 + tt
                slab = _slab(hp_ref, t)[...]
                for k in range(TOP_K):
                    _slab_at(buf, slot_s[k][t])[...] = slab
            return carry

        lax.fori_loop(0, MOE_CHUNK // 8, scatter, 0)

    def expert(ee):
        e = s * EXPERTS_PER_STEP + ee
        n = cnt_ref[c, e]
        off = off_ref[c, e]

        def ffn(start, rows):
            view = buf.at[pl.ds(pl.multiple_of((off + start) * ROW_SUB, SLOT_ALIGN * ROW_SUB),
                                rows * ROW_SUB), :]
            x_lo, x_hi = _unpack_pair(_load_rows_dense(view, rows))
            gu = (_dot(x_lo.astype(BF16), wgu_ref[ee, :ROW_WORDS, :])
                  + _dot(x_hi.astype(BF16), wgu_ref[ee, ROW_WORDS:, :]))
            hid = jax.nn.silu(gu[:, :EXPERT_FF]) * gu[:, EXPERT_FF:]
            y = _dot(hid.astype(BF16), wd_ref[ee])
            mine = lax.broadcasted_iota(jnp.int32, (rows, 1), 0) < (n - start)
            _store_rows_dense(view, _pack_pair(jnp.where(mine, y[:, :ROW_WORDS], x_lo),
                                               jnp.where(mine, y[:, ROW_WORDS:], x_hi)))

        n_big = (n + MOE_TILE_BIG - MOE_TILE - 1) // MOE_TILE_BIG

        def big(r, carry):
            ffn(r * MOE_TILE_BIG, MOE_TILE_BIG)
            return carry

        lax.fori_loop(0, n_big, big, 0)

        @pl.when(n > n_big * MOE_TILE_BIG)
        def _():
            ffn(n_big * MOE_TILE_BIG, MOE_TILE)

    @pl.when(s < EXPERT_STEPS)
    def _experts():
        for ee in range(EXPERTS_PER_STEP):
            expert(ee)

    @pl.when(s >= EXPERT_STEPS)
    def _combine():
        t0 = (s - EXPERT_STEPS) * COMBINE_BLOCK

        def gather(i, carry):
            for tt in range(COMBINE_UNROLL):
                tl = i * COMBINE_UNROLL + tt
                t = t0 + tl
                lo, hi = _unpack_pair(_slab_at(buf, slot_s[0][t])[...])
                acc_lo = w_s[0][t] * lo
                acc_hi = w_s[0][t] * hi
                for k in range(1, TOP_K):
                    lo, hi = _unpack_pair(_slab_at(buf, slot_s[k][t])[...])
                    wk = w_s[k][t]
                    acc_lo = acc_lo + wk * lo
                    acc_hi = acc_hi + wk * hi
                _slab(clo, tl)[...] = acc_lo
                _slab(chi, tl)[...] = acc_hi
            return carry

        lax.fori_loop(0, COMBINE_BLOCK // COMBINE_UNROLL, gather, 0)
        o_ref[:, :ROW_WORDS] = _load_rows_dense(clo, COMBINE_BLOCK)
        o_ref[:, ROW_WORDS:] = _load_rows_dense(chi, COMBINE_BLOCK)


def _moe_sparse(offs, cnts, slots, wts, hp, w_gu, w_d):
    nch = offs.shape[0]
    T = nch * MOE_CHUNK
    blocks = MOE_CHUNK // COMBINE_BLOCK
    expert = lambda c, s, *_: (jnp.minimum(s, EXPERT_STEPS - 1), 0, 0)
    grid_spec = pltpu.PrefetchScalarGridSpec(
        num_scalar_prefetch=2,
        grid=(nch, EXPERT_STEPS + blocks),
        in_specs=[pl.BlockSpec(memory_space=pl.ANY),
                  pl.BlockSpec(memory_space=pl.ANY),
                  pl.BlockSpec((MOE_CHUNK * ROW_SUB, LANES), lambda c, s, *_: (c, 0),
                               pipeline_mode=pl.Buffered(1)),
                  pl.BlockSpec((EXPERTS_PER_STEP, D_MODEL, 2 * EXPERT_FF), expert),
                  pl.BlockSpec((EXPERTS_PER_STEP, EXPERT_FF, D_MODEL), expert)],
        out_specs=pl.BlockSpec(
            (COMBINE_BLOCK, D_MODEL),
            lambda c, s, *_: (c * blocks + jnp.maximum(s - EXPERT_STEPS, 0), 0)),
        scratch_shapes=[pltpu.VMEM((SLOT_ROWS * ROW_SUB, LANES), jnp.uint32),
                        pltpu.VMEM((COMBINE_BLOCK * ROW_SUB, LANES), F32),
                        pltpu.VMEM((COMBINE_BLOCK * ROW_SUB, LANES), F32),
                        *[pltpu.SMEM((MOE_CHUNK,), jnp.int32) for _ in range(TOP_K)],
                        *[pltpu.SMEM((MOE_CHUNK,), F32) for _ in range(TOP_K)],
                        pltpu.SemaphoreType.DMA((2 * TOP_K,))])
    return pl.pallas_call(
        _moe_kernel,
        grid_spec=grid_spec,
        out_shape=jax.ShapeDtypeStruct((T, D_MODEL), F32),
        compiler_params=pltpu.CompilerParams(dimension_semantics=("arbitrary", "arbitrary"),
                                             vmem_limit_bytes=VMEM_LIMIT),
        name="moe_experts",
    )(offs, cnts, slots, wts, hp, w_gu, w_d)


def _ple_kernel(x1_ref, r_ref, p_ref, g_ref, wg_ref, b_ref, wp_ref, o_ref):
    x2 = x1_ref[...] + r_ref[...]
    hn = (_rms(x2, D_MODEL) * g_ref[...]).astype(BF16)
    gate = jax.nn.sigmoid(_dot(hn, wg_ref[...]) + b_ref[...])
    o_ref[...] = x2 + gate * _dot(p_ref[...].astype(BF16), wp_ref[...])


def _ple(x1s, routed, p2d, g_ple, wg, b_ple, wp, tm):
    T = x1s.shape[0]
    row = lambda n: pl.BlockSpec((tm, n), lambda i: (i, 0))
    consts = [g_ple, wg, b_ple, wp]
    return pl.pallas_call(
        _ple_kernel,
        grid=(T // tm,),
        in_specs=[row(D_MODEL), row(D_MODEL), row(PLE_DIM)] + [_full(c.shape) for c in consts],
        out_specs=row(D_MODEL),
        out_shape=jax.ShapeDtypeStruct((T, D_MODEL), F32),
        compiler_params=pltpu.CompilerParams(dimension_semantics=("arbitrary",),
                                             vmem_limit_bytes=VMEM_LIMIT),
        name="ple",
    )(x1s, routed, p2d, *consts)


def _lane_map(*runs):
    src = np.full((LANES,), -1)
    for lane, dim, n in runs:
        src[lane:lane + n] = np.arange(dim, dim + n)
    return src


_MLA_HALF = MLA_ROPE // 2
_SWA_HALF = SWA_HD // 2
MLA_LANES = _lane_map((0, MLA_NOPE, _MLA_HALF), (_MLA_HALF, 0, LANES // 2 - _MLA_HALF),
                      (LANES // 2, MLA_NOPE + _MLA_HALF, _MLA_HALF),
                      (LANES // 2 + _MLA_HALF, LANES // 2 - _MLA_HALF, MLA_NOPE - LANES // 2 + _MLA_HALF))
MLA_NOPE_LANES = np.where(MLA_LANES < MLA_NOPE, MLA_LANES, -1)
MLA_ROPE_LANES = np.where(MLA_LANES >= MLA_NOPE, MLA_LANES - MLA_NOPE, -1)
SWA_LANES = _lane_map((LANES // 2 - _SWA_HALF, 0, _SWA_HALF), (LANES - _SWA_HALF, _SWA_HALF, _SWA_HALF))


def _spread(w, heads, lane_src):
    k = w.shape[0]
    dim = w.shape[1] // heads
    w = jnp.pad(w.reshape(k, heads, dim), ((0, 0), (0, 0), (0, 1)))
    return w[:, :, np.where(lane_src < 0, dim, lane_src)].reshape(k, heads * LANES)


def _rope_table():
    def inv_freq(dim):
        return 1.0 / (ROPE_THETA ** (jnp.arange(0, dim, 2, dtype=F32) / dim))

    def selector(lane_src, half):
        sel = np.where(lane_src < 0, 0.0, np.where(lane_src < half, -1.0, 1.0))
        return jnp.asarray(sel, F32)

    sel_m = selector(MLA_ROPE_LANES, _MLA_HALF)
    sel_s = selector(SWA_LANES, _SWA_HALF)
    freq_m = _spread(jnp.tile(inv_freq(MLA_ROPE), 2)[None], 1, MLA_ROPE_LANES)[0]
    freq_s = _spread(jnp.tile(inv_freq(SWA_HD), 2)[None], 1, SWA_LANES)[0]
    zero = jnp.zeros((LANES,), F32)
    rows = [freq_m + freq_s, sel_m, jnp.abs(sel_m), sel_s, jnp.abs(sel_s), zero, zero, zero]
    return jnp.stack(rows)


def _layer(x2d, p2d, pos2d, B, S, g_mix, w_in, b_gate, g_cq, w_uq, g_ckv, w_ukv, g_qn_mla, g_kn_mla,
           g_qn_swa, g_kn_swa, sink, w_br_mla, w_br_swa, w_out, g_moe, w_router, router_bias,
           w_exp_gu, w_exp_down, w_sh_gu, w_sh_down, g_ple, w_ple_gate, b_ple, w_ple_proj):
    w_kr = _spread(w_in[:, OFF_CKV:OFF_KR], 1, MLA_ROPE_LANES)
    w_vs = w_in[:, OFF_KS:OFF_VS].reshape(D_MODEL, SWA_KV_HEADS, 1, SWA_HD)
    w_vs = jnp.broadcast_to(w_vs, (D_MODEL, SWA_KV_HEADS, 2, SWA_HD)).reshape(D_MODEL, -1)
    w_all = jnp.concatenate([
        w_in[:, :OFF_CKV], w_kr,
        _spread(w_in[:, OFF_KR:OFF_QS], SWA_HEADS, SWA_LANES),
        _spread(w_in[:, OFF_QS:OFF_KS], SWA_KV_HEADS, SWA_LANES),
        w_vs, w_in[:, OFF_VS:]], axis=1).astype(BF16)
    assert w_all.shape[1] == C_END
    w_uq_p = _spread(w_uq, MLA_HEADS, MLA_LANES).astype(BF16)
    w_ukv3 = w_ukv.reshape(MLA_KV_RANK, MLA_HEADS, MLA_NOPE + MLA_V)
    w_k = _spread(w_ukv3[:, :, :MLA_NOPE].reshape(MLA_KV_RANK, -1), MLA_HEADS, MLA_NOPE_LANES).astype(BF16)
    w_v = w_ukv3[:, :, MLA_NOPE:].reshape(MLA_KV_RANK, -1).astype(BF16)

    qm, km, vm, qw, kw, vw, gates = _pre_attention(
        x2d, pos2d, w_all, w_uq_p, w_k, w_v, g_mix[None], g_cq[None], g_ckv[None],
        _spread(g_qn_mla[None], 1, MLA_LANES), _spread(g_kn_mla[None], 1, MLA_LANES),
        _spread(g_qn_swa[None], 1, SWA_LANES), _spread(g_kn_swa[None], 1, SWA_LANES),
        b_gate[None], _rope_table(), tm=PRE_TILE)

    om, w_gu_bf, w_down_bf = _mla_attention(qm, km, vm, w_exp_gu, w_exp_down, B, S, tq=MLA_Q_TILE)
    om = om.reshape(B * S, -1)
    ow = _swa_attention(qw, kw, vw, sink, B, S, tq=SWA_Q_TILE).reshape(B * S, -1)

    x1s, hp, comb_t = _post_attention(
        om, ow, gates, x2d, w_br_mla.astype(BF16), w_br_swa.astype(BF16), w_out.astype(BF16),
        g_moe[None], w_router.T, router_bias[:, None], w_sh_gu.astype(BF16),
        w_sh_down.astype(BF16), tm=ROW_TILE)

    slots, wts, offs, cnts = _moe_plan(comb_t)
    routed = _moe_sparse(offs[:, :, 0], cnts[:, :, 0], slots, wts, hp, w_gu_bf, w_down_bf)

    return _ple(x1s, routed, p2d, g_ple[None], w_ple_gate.astype(BF16), b_ple[None],
                w_ple_proj.astype(BF16), tm=ROW_TILE)


def kernel(x, p, positions, g_mix, w_in, b_gate, g_cq, w_uq, g_ckv, w_ukv, g_qn_mla, g_kn_mla, g_qn_swa, g_kn_swa, sink, w_br_mla, w_br_swa, w_out, g_moe, w_router, router_bias, w_exp_gu, w_exp_down, w_sh_gu, w_sh_down, g_ple, w_ple_gate, b_ple, w_ple_proj):
    B, S, D = x.shape
    x2d = x.reshape(B * S, D)
    pos2d = positions.reshape(B * S, 1)
    for i in range(p.shape[0]):
        x2d = _layer(x2d, p[i].reshape(B * S, -1), pos2d, B, S, g_mix[i], w_in[i], b_gate[i],
                     g_cq[i], w_uq[i], g_ckv[i], w_ukv[i], g_qn_mla[i], g_kn_mla[i], g_qn_swa[i],
                     g_kn_swa[i], sink[i], w_br_mla[i], w_br_swa[i], w_out[i], g_moe[i],
                     w_router[i], router_bias[i], w_exp_gu[i], w_exp_down[i], w_sh_gu[i],
                     w_sh_down[i], g_ple[i], w_ple_gate[i], b_ple[i], w_ple_proj[i])
    return x2d.reshape(B, S, D)
```

```python
import functools

import jax
import jax.numpy as jnp
import numpy as np
from jax import lax
from jax.experimental import pallas as pl
from jax.experimental.pallas import tpu as pltpu

D_MODEL = 1024
PLE_DIM = 256
ROPE_THETA = 10000.0
EPS = 1e-6
NEG_INF = -1e30

MLA_HEADS = 8
MLA_Q_RANK = 384
MLA_KV_RANK = 256
MLA_NOPE = 64
MLA_ROPE = 32
MLA_QK = MLA_NOPE + MLA_ROPE
MLA_V = 64

SWA_HEADS = 8
SWA_KV_HEADS = 2
SWA_GROUP = SWA_HEADS // SWA_KV_HEADS
SWA_HD = 64
WINDOW = 128

OFF_CQ = MLA_Q_RANK
OFF_CKV = OFF_CQ + MLA_KV_RANK
OFF_KR = OFF_CKV + MLA_ROPE
OFF_QS = OFF_KR + SWA_HEADS * SWA_HD
OFF_KS = OFF_QS + SWA_KV_HEADS * SWA_HD
OFF_VS = OFF_KS + SWA_KV_HEADS * SWA_HD
OFF_GA = OFF_VS + D_MODEL

N_EXPERTS = 64
TOP_K = 8
N_GROUPS = 8
TOPK_GROUPS = 4
EXPERTS_PER_GROUP = N_EXPERTS // N_GROUPS
EXPERT_FF = 256
SHARED_FF = 256
ROUTED_SCALE = 2.5

LANES = 128
ROW_WORDS = D_MODEL // 2
ROW_SUB = ROW_WORDS // LANES
MOE_CHUNK = 2048
MOE_TILE = 128
MOE_TILE_BIG = 320
SLOT_ALIGN = 16
COMBINE_BLOCK = 256
COMBINE_UNROLL = 8
EXPERTS_PER_STEP = 4
EXPERT_STEPS = N_EXPERTS // EXPERTS_PER_STEP
PRE_TILE = 256
ROW_TILE = 512
MLA_Q_TILE = 1024
MLA_KV_TILE = 256
LOG2E = 1.4426950408889634
SOFTMAX_UNDERFLOW_GUARD = 1e-30
KEY_NORM_MARGIN = 1.01
SWA_Q_TILE = 512
SWA_BLOCK = 128
VMEM_LIMIT = 56 * 1024 * 1024

BF16 = jnp.bfloat16
F32 = jnp.float32

C_CQ = 0
C_CKV = C_CQ + MLA_Q_RANK
C_KR = C_CKV + MLA_KV_RANK
C_QS = C_KR + LANES
C_KS = C_QS + SWA_HEADS * LANES
C_VS = C_KS + SWA_KV_HEADS * LANES
C_GA = C_VS + SWA_KV_HEADS * LANES
C_END = C_GA + 2 * D_MODEL


def _full(shape):
    nd = len(shape)
    return pl.BlockSpec(shape, lambda *_: (0,) * nd)


def _dot(a, b):
    return jnp.dot(a, b, preferred_element_type=F32)


def _dot_nt(a, b, precision=None):
    return lax.dot_general(a, b, (((1,), (1,)), ((), ())), precision=precision,
                           preferred_element_type=F32)


def _rms(v, n):
    return v * lax.rsqrt(jnp.sum(v * v, axis=-1, keepdims=True) * (1.0 / n) + EPS)


def _rope(v, cos, sin):
    return v * cos + pltpu.roll(v, LANES // 2, 1) * sin


def _pre_kernel(x_ref, pos_ref, w_all_ref, w_uq_ref, w_k_ref, w_v_ref, g_mix_ref, g_cq_ref,
                g_ckv_ref, gq_m_ref, gk_m_ref, gq_s_ref, gk_s_ref, b_gate_ref, rope_ref,
                qm_ref, km_ref, vm_ref, qw_ref, kw_ref, vw_ref, gate_ref):
    x = x_ref[...]
    h = (_rms(x, D_MODEL) * g_mix_ref[...]).astype(BF16)

    def proj(lo, hi):
        return _dot(h, w_all_ref[:, lo:hi])

    z_lat = proj(C_CQ, C_QS)
    z_qs = proj(C_QS, C_KS)
    half = (C_END - C_GA) // 2
    z_ga = proj(C_GA, C_GA + half)

    pos = pos_ref[...].astype(F32)
    rope = rope_ref[...]
    ang = pos * rope[0:1, :]
    cos_m1 = jnp.cos(ang) - 1.0
    sin = jnp.sin(ang)
    cos_m, sin_m = 1.0 + cos_m1 * rope[2:3, :], sin * rope[1:2, :]
    cos_s, sin_s = 1.0 + cos_m1 * rope[4:5, :], sin * rope[3:4, :]

    cqn = (_rms(z_lat[:, C_CQ:C_CKV], MLA_Q_RANK) * g_cq_ref[...]).astype(BF16)
    ckvn = (_rms(z_lat[:, C_CKV:C_KR], MLA_KV_RANK) * g_ckv_ref[...]).astype(BF16)
    q = _dot(cqn, w_uq_ref[...])
    gq_m = gq_m_ref[...]
    for hd in range(MLA_HEADS):
        qh = _rms(q[:, hd * LANES:(hd + 1) * LANES], MLA_QK) * gq_m
        qh = _rope(qh, cos_m, sin_m) * (MLA_QK ** -0.5 * LOG2E)
        qm_ref[:, hd * LANES:(hd + 1) * LANES] = qh.astype(BF16)

    kn = _dot(ckvn, w_k_ref[...])
    vm_ref[...] = _dot(ckvn, w_v_ref[...]).astype(BF16)
    z_kv = proj(C_KS, C_GA)

    gq_s = gq_s_ref[...]
    for hd in range(SWA_HEADS):
        qh = _rms(z_qs[:, hd * LANES:(hd + 1) * LANES], SWA_HD) * gq_s
        qh = _rope(qh, cos_s, sin_s) * (SWA_HD ** -0.5 * LOG2E)
        qw_ref[:, hd * LANES:(hd + 1) * LANES] = qh.astype(BF16)

    gk_m = gk_m_ref[...]
    kr = z_lat[:, C_KR:C_QS]
    ss_kr = jnp.sum(kr * kr, axis=-1, keepdims=True)
    kr_rot = _rope(kr * gk_m, cos_m, sin_m)
    for hd in range(MLA_HEADS):
        kh = kn[:, hd * LANES:(hd + 1) * LANES]
        ss = jnp.sum(kh * kh, axis=-1, keepdims=True) + ss_kr
        sc = lax.rsqrt(ss * (1.0 / MLA_QK) + EPS)
        km_ref[:, hd * LANES:(hd + 1) * LANES] = ((kh * gk_m + kr_rot) * sc).astype(BF16)

    z_gb = proj(C_GA + half, C_END)

    gk_s = gk_s_ref[...]
    for hd in range(SWA_KV_HEADS):
        kh = _rms(z_kv[:, hd * LANES:(hd + 1) * LANES], SWA_HD) * gk_s
        kh = _rope(kh, cos_s, sin_s)
        kw_ref[:, hd * LANES:(hd + 1) * LANES] = kh.astype(BF16)
    vw_ref[...] = z_kv[:, C_VS - C_KS:].astype(BF16)

    gate_ref[:, :half] = jax.nn.sigmoid(z_ga + b_gate_ref[:, :half]).astype(BF16)
    gate_ref[:, half:] = jax.nn.sigmoid(z_gb + b_gate_ref[:, half:]).astype(BF16)


def _pre_attention(x2d, pos2d, w_all, w_uq, w_k, w_v, g_mix, g_cq, g_ckv, gq_m, gk_m, gq_s,
                   gk_s, b_gate, rope_tab, tm):
    T = x2d.shape[0]
    row = lambda n: pl.BlockSpec((tm, n), lambda i: (i, 0))
    outs = [(MLA_HEADS * LANES, BF16), (MLA_HEADS * LANES, BF16), (MLA_HEADS * MLA_V, BF16),
            (SWA_HEADS * LANES, BF16), (SWA_KV_HEADS * LANES, BF16),
            (SWA_KV_HEADS * LANES, BF16), (2 * D_MODEL, BF16)]
    consts = [w_all, w_uq, w_k, w_v, g_mix, g_cq, g_ckv, gq_m, gk_m, gq_s, gk_s, b_gate, rope_tab]
    return pl.pallas_call(
        _pre_kernel,
        grid=(T // tm,),
        in_specs=[row(D_MODEL), row(1)] + [_full(c.shape) for c in consts],
        out_specs=[row(n) for n, _ in outs],
        out_shape=[jax.ShapeDtypeStruct((T, n), dt) for n, dt in outs],
        compiler_params=pltpu.CompilerParams(dimension_semantics=("arbitrary",),
                                             vmem_limit_bytes=VMEM_LIMIT),
        name="pre_attention",
    )(x2d, pos2d, *consts)


def _half_masks(dtype):
    lane = lax.broadcasted_iota(jnp.int32, (1, LANES), 1)
    lo = (lane < LANES // 2).astype(dtype)
    return lo, 1 - lo


def _mla_exact(q_ref, k_ref, v_ref, o_ref):
    v = v_ref[...]
    masks = _half_masks(v.dtype)
    acc = None
    for hh in range(2):
        q = q_ref[:, hh * LANES:(hh + 1) * LANES]
        k = k_ref[:, hh * LANES:(hh + 1) * LANES]
        s = _dot_nt(q, k)
        m = jnp.max(s, axis=-1, keepdims=True)
        p = jnp.exp2(s - m)
        l = jnp.sum(p, axis=-1, keepdims=True)
        o = _dot(p.astype(BF16), v * masks[hh]) / l
        acc = o if acc is None else acc + o
    o_ref[...] = acc.astype(o_ref.dtype)


def _mla_kernel(kmax_ref, q_ref, k_ref, v_ref, wa_ref, wb_ref, o_ref, wa_out, wb_out, vt_ref):
    @pl.when(pl.program_id(2) == 0)
    def _():
        vt_ref[...] = v_ref[...].astype(F32).T.astype(BF16)

    wa_out[...] = wa_ref[...].astype(BF16)
    wb_out[...] = wb_ref[...].astype(BF16)

    ones = jnp.ones((8, LANES), BF16)
    outs = []
    lmin = None
    for hh in range(2):
        q = q_ref[:, hh * LANES:(hh + 1) * LANES]
        qf = q.astype(F32)
        q_sq = _dot_nt(ones, (qf * qf).astype(BF16))[0:1, :]
        bound = jnp.sqrt(q_sq) * kmax_ref[0]
        p_t = jnp.exp2(_dot_nt(k_ref[:, hh * LANES:(hh + 1) * LANES], q) - bound)
        l = jnp.sum(p_t, axis=0, keepdims=True)
        o_t = _dot(vt_ref[hh * MLA_V:(hh + 1) * MLA_V, :], p_t.astype(BF16))
        outs.append(o_t / l)
        lm = jnp.min(l)
        lmin = lm if lmin is None else jnp.minimum(lmin, lm)
    o_ref[...] = jnp.concatenate(outs, axis=0).T.astype(o_ref.dtype)

    @pl.when(jnp.logical_not(lmin > SOFTMAX_UNDERFLOW_GUARD))
    def _():
        _mla_exact(q_ref, k_ref, v_ref, o_ref)


def _mla_attention(kmax, qm, km, vm, w_a, w_b, B, S, tq):
    pairs = MLA_HEADS // 2
    q3 = qm.reshape(B, S, MLA_HEADS * LANES)
    k3 = km.reshape(B, S, MLA_HEADS * LANES)
    v3 = vm.reshape(B, S, MLA_HEADS * MLA_V)
    steps = B * pairs * (S // tq)
    wa3 = w_a.reshape(steps, -1, w_a.shape[-1])
    wb3 = w_b.reshape(steps, -1, w_b.shape[-1])
    step = lambda b, p, i: ((b * pairs + p) * (S // tq) + i, 0, 0)
    w_spec = lambda w: pl.BlockSpec((None,) + w.shape[1:], step)
    om, wa_bf, wb_bf = pl.pallas_call(
        _mla_kernel,
        grid=(B, pairs, S // tq),
        in_specs=[pl.BlockSpec(memory_space=pltpu.SMEM),
                  pl.BlockSpec((None, tq, 2 * LANES), lambda b, p, i: (b, i, p)),
                  pl.BlockSpec((None, S, 2 * LANES), lambda b, p, i: (b, 0, p)),
                  pl.BlockSpec((None, S, LANES), lambda b, p, i: (b, 0, p)),
                  w_spec(wa3), w_spec(wb3)],
        out_specs=[pl.BlockSpec((None, tq, LANES), lambda b, p, i: (b, i, p)),
                   w_spec(wa3), w_spec(wb3)],
        out_shape=[jax.ShapeDtypeStruct((B, S, MLA_HEADS * MLA_V), BF16),
                   jax.ShapeDtypeStruct(wa3.shape, BF16),
                   jax.ShapeDtypeStruct(wb3.shape, BF16)],
        scratch_shapes=[pltpu.VMEM((LANES, S), BF16)],
        compiler_params=pltpu.CompilerParams(
            dimension_semantics=("arbitrary", "arbitrary", "arbitrary"),
            vmem_limit_bytes=VMEM_LIMIT),
        name="mla_attention",
    )(kmax, q3, k3, v3, wa3, wb3)
    return om, wa_bf.reshape(w_a.shape), wb_bf.reshape(w_b.shape)


def _swa_kernel(sink_ref, q_ref, k_ref, v_ref, o_ref, *, tq, S):
    hk = pl.program_id(1)
    i = pl.program_id(2)
    tk = SWA_BLOCK + 2 * WINDOW
    row = lax.broadcasted_iota(jnp.int32, (SWA_GROUP * SWA_BLOCK, 1), 0)
    qoff = row & (SWA_BLOCK - 1)
    head = row // SWA_BLOCK
    sk = jnp.zeros((SWA_GROUP * SWA_BLOCK, 1), F32)
    for g in range(SWA_GROUP):
        sk = jnp.where(head == g, sink_ref[SWA_GROUP * hk + g] * LOG2E, sk)
    low_half = lax.broadcasted_iota(jnp.int32, (1, LANES), 1) < LANES // 2
    for sub in range(tq // SWA_BLOCK):
        rows = slice(sub * SWA_BLOCK, (sub + 1) * SWA_BLOCK)
        q0 = i * tq + sub * SWA_BLOCK
        kstart = pl.multiple_of(jnp.clip(q0 - WINDOW, 0, S - tk), WINDOW)
        k = k_ref[pl.ds(kstart, tk), :]
        v = v_ref[pl.ds(kstart, tk), :]
        q = jnp.concatenate([q_ref[rows, g * LANES:(g + 1) * LANES] for g in range(SWA_GROUP)], axis=0)
        kpos = kstart + lax.broadcasted_iota(jnp.int32, (1, tk), 1)
        valid = jnp.abs(kpos - (q0 + qoff[:SWA_BLOCK])) <= WINDOW
        s = _dot_nt(q, k).reshape(SWA_GROUP, SWA_BLOCK, tk)
        s = jnp.where(valid[None], s, NEG_INF).reshape(SWA_GROUP * SWA_BLOCK, tk)
        m = jnp.maximum(jnp.max(s, axis=-1, keepdims=True), sk)
        e = jnp.exp2(s - m)
        denom = jnp.sum(e, axis=-1, keepdims=True) + jnp.exp2(sk - m)
        o = _dot(e.astype(BF16), v) / denom
        for j in range(SWA_GROUP // 2):
            even = o[(2 * j) * SWA_BLOCK:(2 * j + 1) * SWA_BLOCK, :]
            odd = o[(2 * j + 1) * SWA_BLOCK:(2 * j + 2) * SWA_BLOCK, :]
            o_ref[rows, j * LANES:(j + 1) * LANES] = jnp.where(low_half, even, odd).astype(o_ref.dtype)


def _swa_attention(qw, kw, vw, sink, B, S, tq):
    q3 = qw.reshape(B, S, SWA_HEADS * LANES)
    k3 = kw.reshape(B, S, SWA_KV_HEADS * LANES)
    v3 = vw.reshape(B, S, SWA_KV_HEADS * LANES)
    return pl.pallas_call(
        functools.partial(_swa_kernel, tq=tq, S=S),
        grid=(B, SWA_KV_HEADS, S // tq),
        in_specs=[pl.BlockSpec(memory_space=pltpu.SMEM),
                  pl.BlockSpec((None, tq, SWA_GROUP * LANES), lambda b, h, i: (b, i, h)),
                  pl.BlockSpec((None, S, LANES), lambda b, h, i: (b, 0, h)),
                  pl.BlockSpec((None, S, LANES), lambda b, h, i: (b, 0, h))],
        out_specs=pl.BlockSpec((None, tq, SWA_GROUP * SWA_HD), lambda b, h, i: (b, i, h)),
        out_shape=jax.ShapeDtypeStruct((B, S, SWA_HEADS * SWA_HD), BF16),
        compiler_params=pltpu.CompilerParams(
            dimension_semantics=("arbitrary", "arbitrary", "arbitrary"),
            vmem_limit_bytes=VMEM_LIMIT),
        name="swa_attention",
    )(sink, q3, k3, v3)


def _beats(vj, vi, j_first):
    return (vj >= vi) if j_first else (vj > vi)


def _route(scores, sel):
    G, P = N_GROUPS, EXPERTS_PER_GROUP
    groups = [sel[g * P:(g + 1) * P, :] for g in range(G)]
    row = lax.broadcasted_iota(jnp.int32, (P, 1), 0)
    gscore = []
    for vg in groups:
        m1 = jnp.max(vg, axis=0, keepdims=True)
        first = jnp.min(jnp.where(vg == m1, row, P), axis=0, keepdims=True)
        m2 = jnp.max(jnp.where(row == first, -jnp.inf, vg), axis=0, keepdims=True)
        gscore.append(m1 + m2)
    masked = []
    for g in range(G):
        rank = jnp.zeros_like(gscore[g], dtype=jnp.int32)
        for g2 in range(G):
            if g2 != g:
                rank = rank + _beats(gscore[g2], gscore[g], g2 < g).astype(jnp.int32)
        masked.append(jnp.where(rank < TOPK_GROUPS, groups[g], NEG_INF))
    index = [row + g * P for g in range(G)]
    chosen = [None] * G
    for _ in range(TOP_K):
        best = masked[0]
        for g in range(1, G):
            best = jnp.maximum(best, masked[g])
        best = jnp.max(best, axis=0, keepdims=True)
        first = jnp.where(masked[0] == best, index[0], N_EXPERTS)
        for g in range(1, G):
            first = jnp.minimum(first, jnp.where(masked[g] == best, index[g], N_EXPERTS))
        first = jnp.min(first, axis=0, keepdims=True)
        for g in range(G):
            hit = index[g] == first
            chosen[g] = hit if chosen[g] is None else (chosen[g] | hit)
            masked[g] = jnp.where(hit, -jnp.inf, masked[g])
    picked = [jnp.where(chosen[g], scores[g * P:(g + 1) * P, :], 0.0) for g in range(G)]
    total = picked[0]
    for g in range(1, G):
        total = total + picked[g]
    denom = jnp.sum(total, axis=0, keepdims=True)
    return [pk / denom * ROUTED_SCALE for pk in picked]


def _pack_pair(lo, hi):
    return pltpu.pack_elementwise([lo, hi], packed_dtype=BF16)


def _unpack_pair(word):
    lo = pltpu.unpack_elementwise(word, index=0, packed_dtype=BF16, unpacked_dtype=F32)
    hi = pltpu.unpack_elementwise(word, index=1, packed_dtype=BF16, unpacked_dtype=F32)
    return lo, hi


def _store_rows_dense(ref, words):
    for j in range(ROW_SUB):
        ref[pl.ds(j, words.shape[0], stride=ROW_SUB), :] = words[:, j * LANES:(j + 1) * LANES]


def _load_rows_dense(ref, rows):
    sub = ROW_SUB
    return jnp.concatenate([ref[pl.ds(j, rows, stride=sub), :] for j in range(sub)], axis=1)


def _post_kernel(om_ref, ow_ref, gate_ref, x_ref, wbm_ref, wbw_ref, wout_ref, g_moe_ref, wr_ref,
                 rb_ref, wsgu_ref, wsd_ref, x1_ref, hp_ref, comb_ref):
    am = _dot(om_ref[...], wbm_ref[...])
    aw = _dot(ow_ref[...], wbw_ref[...])
    gates = gate_ref[...].astype(F32)
    merged = gates[:, :D_MODEL] * am + gates[:, D_MODEL:] * aw
    x1 = x_ref[...] + _dot(merged.astype(BF16), wout_ref[...])

    h2 = _rms(x1, D_MODEL) * g_moe_ref[...]
    h2b = h2.astype(BF16)
    _store_rows_dense(hp_ref, _pack_pair(h2[:, :ROW_WORDS], h2[:, ROW_WORDS:]))

    wr = wr_ref[...]
    w_hi = wr.astype(BF16)
    w_lo = (wr - w_hi.astype(F32)).astype(BF16)
    h_lo = (h2 - h2b.astype(F32)).astype(BF16)
    by_hi = _dot_nt(jnp.concatenate([w_hi, w_lo], axis=0), h2b)
    logits = by_hi[:N_EXPERTS] + by_hi[N_EXPERTS:] + _dot_nt(w_hi, h_lo)

    sgu = _dot(h2b, wsgu_ref[...])
    sh = jax.nn.silu(sgu[:, :SHARED_FF]) * sgu[:, SHARED_FF:]
    x1_ref[...] = x1 + _dot(sh.astype(BF16), wsd_ref[...])

    scores = jax.nn.sigmoid(logits)
    comb = _route(scores, scores + rb_ref[...])
    for g in range(N_GROUPS):
        comb_ref[g * EXPERTS_PER_GROUP:(g + 1) * EXPERTS_PER_GROUP, :] = comb[g]


def _post_attention(om, ow, gates, x2d, wbm, wbw, wout, g_moe, wr_t, rbias, wsgu, wsd, tm):
    T = x2d.shape[0]
    row = lambda n: pl.BlockSpec((tm, n), lambda i: (i, 0))
    consts = [wbm, wbw, wout, g_moe, wr_t, rbias, wsgu, wsd]
    return pl.pallas_call(
        _post_kernel,
        grid=(T // tm,),
        in_specs=[row(om.shape[1]), row(ow.shape[1]), row(2 * D_MODEL), row(D_MODEL)]
        + [_full(c.shape) for c in consts],
        out_specs=[row(D_MODEL), pl.BlockSpec((tm * ROW_SUB, LANES), lambda i: (i, 0)),
                   pl.BlockSpec((N_EXPERTS, tm), lambda i: (0, i))],
        out_shape=[jax.ShapeDtypeStruct((T, D_MODEL), F32),
                   jax.ShapeDtypeStruct((T * ROW_SUB, LANES), jnp.uint32),
                   jax.ShapeDtypeStruct((N_EXPERTS, T), F32)],
        compiler_params=pltpu.CompilerParams(dimension_semantics=("arbitrary",),
                                             vmem_limit_bytes=VMEM_LIMIT),
        name="post_attention",
    )(om, ow, gates, x2d, *consts)


SLOT_ROWS = -(-(MOE_CHUNK * TOP_K + N_EXPERTS * (SLOT_ALIGN - 1) + MOE_TILE_BIG) // MOE_TILE) * MOE_TILE
DUMMY_SLOT = SLOT_ROWS - 1
PLAN_BLOCK = 256


def _plan_kernel(comb_ref, slot_ref, w_ref, off_ref, cnt_ref):
    comb = comb_ref[...]
    sel = comb > 0.0
    m = sel.astype(F32)
    mb = m.astype(BF16)
    r_i = lax.broadcasted_iota(jnp.int32, (PLAN_BLOCK, PLAN_BLOCK), 0)
    c_i = lax.broadcasted_iota(jnp.int32, (PLAN_BLOCK, PLAN_BLOCK), 1)
    before = (r_i < c_i).astype(BF16)
    carry = jnp.zeros((N_EXPERTS, 1), F32)
    ranks = []
    for b in range(MOE_CHUNK // PLAN_BLOCK):
        blk = slice(b * PLAN_BLOCK, (b + 1) * PLAN_BLOCK)
        ranks.append(_dot(mb[:, blk], before) + carry)
        carry = carry + jnp.sum(m[:, blk], axis=1, keepdims=True)
    rank = jnp.concatenate(ranks, axis=1)
    cnt = carry
    cnt_pad = jnp.floor((cnt + (SLOT_ALIGN - 1)) * (1.0 / SLOT_ALIGN)) * SLOT_ALIGN
    e_r = lax.broadcasted_iota(jnp.int32, (N_EXPERTS, N_EXPERTS), 0)
    e_c = lax.broadcasted_iota(jnp.int32, (N_EXPERTS, N_EXPERTS), 1)
    below = (e_c < e_r).astype(F32)
    off = jnp.dot(below, jnp.broadcast_to(cnt_pad, (N_EXPERTS, LANES)),
                  precision=lax.Precision.HIGHEST, preferred_element_type=F32)
    slot = off[:, :1] + rank
    kidx = _dot(below.astype(BF16), mb)
    row = lax.broadcasted_iota(jnp.int32, (TOP_K, 1), 0)
    slot_acc = jnp.zeros((TOP_K, MOE_CHUNK), F32)
    w_acc = jnp.zeros((TOP_K, MOE_CHUNK), F32)
    for k in range(TOP_K):
        pick = jnp.where(sel & (kidx == k), 1.0, 0.0)
        found = jnp.sum(pick, axis=0, keepdims=True) > 0.0
        s_k = jnp.where(found, jnp.sum(pick * slot, axis=0, keepdims=True), float(DUMMY_SLOT))
        w_k = jnp.sum(pick * comb, axis=0, keepdims=True)
        slot_acc = jnp.where(row == k, s_k, slot_acc)
        w_acc = jnp.where(row == k, w_k, w_acc)
    slot_ref[...] = slot_acc.astype(jnp.int32) * ROW_SUB
    w_ref[...] = w_acc
    off_ref[...] = off.astype(jnp.int32)
    cnt_ref[...] = jnp.broadcast_to(cnt, (N_EXPERTS, LANES)).astype(jnp.int32)


def _moe_plan(comb_t):
    T = comb_t.shape[1]
    nch = T // MOE_CHUNK
    per_pair = pl.BlockSpec((None, TOP_K, MOE_CHUNK), lambda c: (c, 0, 0))
    per_expert = pl.BlockSpec((None, N_EXPERTS, LANES), lambda c: (c, 0, 0))
    return pl.pallas_call(
        _plan_kernel,
        grid=(nch,),
        in_specs=[pl.BlockSpec((N_EXPERTS, MOE_CHUNK), lambda c: (0, c))],
        out_specs=[per_pair, per_pair, per_expert, per_expert],
        out_shape=[jax.ShapeDtypeStruct((nch, TOP_K, MOE_CHUNK), jnp.int32),
                   jax.ShapeDtypeStruct((nch, TOP_K, MOE_CHUNK), F32),
                   jax.ShapeDtypeStruct((nch, N_EXPERTS, LANES), jnp.int32),
                   jax.ShapeDtypeStruct((nch, N_EXPERTS, LANES), jnp.int32)],
        compiler_params=pltpu.CompilerParams(dimension_semantics=("arbitrary",),
                                             vmem_limit_bytes=VMEM_LIMIT),
        name="moe_plan",
    )(comb_t)


def _slab_at(ref, first):
    return ref.at[pl.ds(pl.multiple_of(first, ROW_SUB), ROW_SUB), :]


def _slab(ref, row):
    return _slab_at(ref, row * ROW_SUB)


def _moe_kernel(off_ref, cnt_ref, slot_hbm, w_hbm, hp_ref, wgu_ref, wd_ref, o_ref,
                buf, clo, chi, *smem_and_sem):
    slot_s = smem_and_sem[:TOP_K]
    w_s = smem_and_sem[TOP_K:2 * TOP_K]
    sem = smem_and_sem[2 * TOP_K]
    c = pl.program_id(0)
    s = pl.program_id(1)

    @pl.when(s == 0)
    def _dispatch():
        copies = [pltpu.make_async_copy(slot_hbm.at[c, k], slot_s[k], sem.at[k]) for k in range(TOP_K)]
        copies += [pltpu.make_async_copy(w_hbm.at[c, k], w_s[k], sem.at[TOP_K + k]) for k in range(TOP_K)]
        for cp in copies:
            cp.start()

        @pl.when(c == 0)
        def _():
            buf[...] = jnp.zeros_like(buf)

        for cp in copies:
            cp.wait()

        def scatter(tb, carry):
            for tt in range(8):
                t = tb * 8 + tt
                slab = _slab(hp_ref, t)[...]
                for k in range(TOP_K):
                    _slab_at(buf, slot_s[k][t])[...] = slab
            return carry

        lax.fori_loop(0, MOE_CHUNK // 8, scatter, 0)

    def expert(ee):
        e = s * EXPERTS_PER_STEP + ee
        n = cnt_ref[c, e]
        off = off_ref[c, e]

        def ffn(start, rows):
            view = buf.at[pl.ds(pl.multiple_of((off + start) * ROW_SUB, SLOT_ALIGN * ROW_SUB),
                                rows * ROW_SUB), :]
            x_lo, x_hi = _unpack_pair(_load_rows_dense(view, rows))
            gu = (_dot(x_lo.astype(BF16), wgu_ref[ee, :ROW_WORDS, :])
                  + _dot(x_hi.astype(BF16), wgu_ref[ee, ROW_WORDS:, :]))
            hid = jax.nn.silu(gu[:, :EXPERT_FF]) * gu[:, EXPERT_FF:]
            y = _dot(hid.astype(BF16), wd_ref[ee])
            mine = lax.broadcasted_iota(jnp.int32, (rows, 1), 0) < (n - start)
            _store_rows_dense(view, _pack_pair(jnp.where(mine, y[:, :ROW_WORDS], x_lo),
                                               jnp.where(mine, y[:, ROW_WORDS:], x_hi)))

        n_big = (n + MOE_TILE_BIG - MOE_TILE - 1) // MOE_TILE_BIG

        def big(r, carry):
            ffn(r * MOE_TILE_BIG, MOE_TILE_BIG)
            return carry

        lax.fori_loop(0, n_big, big, 0)

        @pl.when(n > n_big * MOE_TILE_BIG)
        def _():
            ffn(n_big * MOE_TILE_BIG, MOE_TILE)

    @pl.when(s < EXPERT_STEPS)
    def _experts():
        for ee in range(EXPERTS_PER_STEP):
            expert(ee)

    @pl.when(s >= EXPERT_STEPS)
    def _combine():
        t0 = (s - EXPERT_STEPS) * COMBINE_BLOCK

        def gather(i, carry):
            for tt in range(COMBINE_UNROLL):
                tl = i * COMBINE_UNROLL + tt
                t = t0 + tl
                lo, hi = _unpack_pair(_slab_at(buf, slot_s[0][t])[...])
                acc_lo = w_s[0][t] * lo
                acc_hi = w_s[0][t] * hi
                for k in range(1, TOP_K):
                    lo, hi = _unpack_pair(_slab_at(buf, slot_s[k][t])[...])
                    wk = w_s[k][t]
                    acc_lo = acc_lo + wk * lo
                    acc_hi = acc_hi + wk * hi
                _slab(clo, tl)[...] = acc_lo
                _slab(chi, tl)[...] = acc_hi
            return carry

        lax.fori_loop(0, COMBINE_BLOCK // COMBINE_UNROLL, gather, 0)
        o_ref[:, :ROW_WORDS] = _load_rows_dense(clo, COMBINE_BLOCK)
        o_ref[:, ROW_WORDS:] = _load_rows_dense(chi, COMBINE_BLOCK)


def _moe_sparse(offs, cnts, slots, wts, hp, w_gu, w_d):
    nch = offs.shape[0]
    T = nch * MOE_CHUNK
    blocks = MOE_CHUNK // COMBINE_BLOCK
    expert = lambda c, s, *_: (jnp.minimum(s, EXPERT_STEPS - 1), 0, 0)
    grid_spec = pltpu.PrefetchScalarGridSpec(
        num_scalar_prefetch=2,
        grid=(nch, EXPERT_STEPS + blocks),
        in_specs=[pl.BlockSpec(memory_space=pl.ANY),
                  pl.BlockSpec(memory_space=pl.ANY),
                  pl.BlockSpec((MOE_CHUNK * ROW_SUB, LANES), lambda c, s, *_: (c, 0),
                               pipeline_mode=pl.Buffered(1)),
                  pl.BlockSpec((EXPERTS_PER_STEP, D_MODEL, 2 * EXPERT_FF), expert),
                  pl.BlockSpec((EXPERTS_PER_STEP, EXPERT_FF, D_MODEL), expert)],
        out_specs=pl.BlockSpec(
            (COMBINE_BLOCK, D_MODEL),
            lambda c, s, *_: (c * blocks + jnp.maximum(s - EXPERT_STEPS, 0), 0)),
        scratch_shapes=[pltpu.VMEM((SLOT_ROWS * ROW_SUB, LANES), jnp.uint32),
                        pltpu.VMEM((COMBINE_BLOCK * ROW_SUB, LANES), F32),
                        pltpu.VMEM((COMBINE_BLOCK * ROW_SUB, LANES), F32),
                        *[pltpu.SMEM((MOE_CHUNK,), jnp.int32) for _ in range(TOP_K)],
                        *[pltpu.SMEM((MOE_CHUNK,), F32) for _ in range(TOP_K)],
                        pltpu.SemaphoreType.DMA((2 * TOP_K,))])
    return pl.pallas_call(
        _moe_kernel,
        grid_spec=grid_spec,
        out_shape=jax.ShapeDtypeStruct((T, D_MODEL), F32),
        compiler_params=pltpu.CompilerParams(dimension_semantics=("arbitrary", "arbitrary"),
                                             vmem_limit_bytes=VMEM_LIMIT),
        name="moe_experts",
    )(offs, cnts, slots, wts, hp, w_gu, w_d)


def _ple_kernel(x1_ref, r_ref, p_ref, g_ref, wg_ref, b_ref, wp_ref, o_ref):
    x2 = x1_ref[...] + r_ref[...]
    hn = (_rms(x2, D_MODEL) * g_ref[...]).astype(BF16)
    gate = jax.nn.sigmoid(_dot(hn, wg_ref[...]) + b_ref[...])
    o_ref[...] = x2 + gate * _dot(p_ref[...].astype(BF16), wp_ref[...])


def _ple(x1s, routed, p2d, g_ple, wg, b_ple, wp, tm):
    T = x1s.shape[0]
    row = lambda n: pl.BlockSpec((tm, n), lambda i: (i, 0))
    consts = [g_ple, wg, b_ple, wp]
    return pl.pallas_call(
        _ple_kernel,
        grid=(T // tm,),
        in_specs=[row(D_MODEL), row(D_MODEL), row(PLE_DIM)] + [_full(c.shape) for c in consts],
        out_specs=row(D_MODEL),
        out_shape=jax.ShapeDtypeStruct((T, D_MODEL), F32),
        compiler_params=pltpu.CompilerParams(dimension_semantics=("arbitrary",),
                                             vmem_limit_bytes=VMEM_LIMIT),
        name="ple",
    )(x1s, routed, p2d, *consts)


def _lane_map(*runs):
    src = np.full((LANES,), -1)
    for lane, dim, n in runs:
        src[lane:lane + n] = np.arange(dim, dim + n)
    return src


_MLA_HALF = MLA_ROPE // 2
_SWA_HALF = SWA_HD // 2
MLA_LANES = _lane_map((0, MLA_NOPE, _MLA_HALF), (_MLA_HALF, 0, LANES // 2 - _MLA_HALF),
                      (LANES // 2, MLA_NOPE + _MLA_HALF, _MLA_HALF),
                      (LANES // 2 + _MLA_HALF, LANES // 2 - _MLA_HALF, MLA_NOPE - LANES // 2 + _MLA_HALF))
MLA_NOPE_LANES = np.where(MLA_LANES < MLA_NOPE, MLA_LANES, -1)
MLA_ROPE_LANES = np.where(MLA_LANES >= MLA_NOPE, MLA_LANES - MLA_NOPE, -1)
SWA_LANES = _lane_map((LANES // 2 - _SWA_HALF, 0, _SWA_HALF), (LANES - _SWA_HALF, _SWA_HALF, _SWA_HALF))


def _spread(w, heads, lane_src):
    k = w.shape[0]
    dim = w.shape[1] // heads
    w = jnp.pad(w.reshape(k, heads, dim), ((0, 0), (0, 0), (0, 1)))
    return w[:, :, np.where(lane_src < 0, dim, lane_src)].reshape(k, heads * LANES)


def _rope_table():
    def inv_freq(dim):
        return 1.0 / (ROPE_THETA ** (jnp.arange(0, dim, 2, dtype=F32) / dim))

    def selector(lane_src, half):
        sel = np.where(lane_src < 0, 0.0, np.where(lane_src < half, -1.0, 1.0))
        return jnp.asarray(sel, F32)

    sel_m = selector(MLA_ROPE_LANES, _MLA_HALF)
    sel_s = selector(SWA_LANES, _SWA_HALF)
    freq_m = _spread(jnp.tile(inv_freq(MLA_ROPE), 2)[None], 1, MLA_ROPE_LANES)[0]
    freq_s = _spread(jnp.tile(inv_freq(SWA_HD), 2)[None], 1, SWA_LANES)[0]
    zero = jnp.zeros((LANES,), F32)
    rows = [freq_m + freq_s, sel_m, jnp.abs(sel_m), sel_s, jnp.abs(sel_s), zero, zero, zero]
    return jnp.stack(rows)


def _layer(x2d, p2d, pos2d, B, S, g_mix, w_in, b_gate, g_cq, w_uq, g_ckv, w_ukv, g_qn_mla, g_kn_mla,
           g_qn_swa, g_kn_swa, sink, w_br_mla, w_br_swa, w_out, g_moe, w_router, router_bias,
           w_exp_gu, w_exp_down, w_sh_gu, w_sh_down, g_ple, w_ple_gate, b_ple, w_ple_proj):
    w_kr = _spread(w_in[:, OFF_CKV:OFF_KR], 1, MLA_ROPE_LANES)
    w_vs = w_in[:, OFF_KS:OFF_VS].reshape(D_MODEL, SWA_KV_HEADS, 1, SWA_HD)
    w_vs = jnp.broadcast_to(w_vs, (D_MODEL, SWA_KV_HEADS, 2, SWA_HD)).reshape(D_MODEL, -1)
    w_all = jnp.concatenate([
        w_in[:, :OFF_CKV], w_kr,
        _spread(w_in[:, OFF_KR:OFF_QS], SWA_HEADS, SWA_LANES),
        _spread(w_in[:, OFF_QS:OFF_KS], SWA_KV_HEADS, SWA_LANES),
        w_vs, w_in[:, OFF_VS:]], axis=1).astype(BF16)
    assert w_all.shape[1] == C_END
    w_uq_p = _spread(w_uq, MLA_HEADS, MLA_LANES).astype(BF16)
    w_ukv3 = w_ukv.reshape(MLA_KV_RANK, MLA_HEADS, MLA_NOPE + MLA_V)
    w_k = _spread(w_ukv3[:, :, :MLA_NOPE].reshape(MLA_KV_RANK, -1), MLA_HEADS, MLA_NOPE_LANES).astype(BF16)
    w_v = w_ukv3[:, :, MLA_NOPE:].reshape(MLA_KV_RANK, -1).astype(BF16)

    qm, km, vm, qw, kw, vw, gates = _pre_attention(
        x2d, pos2d, w_all, w_uq_p, w_k, w_v, g_mix[None], g_cq[None], g_ckv[None],
        _spread(g_qn_mla[None], 1, MLA_LANES), _spread(g_kn_mla[None], 1, MLA_LANES),
        _spread(g_qn_swa[None], 1, SWA_LANES), _spread(g_kn_swa[None], 1, SWA_LANES),
        b_gate[None], _rope_table(), tm=PRE_TILE)

    kmax = (KEY_NORM_MARGIN * MLA_QK ** 0.5) * jnp.max(jnp.abs(g_kn_mla), keepdims=True)
    om, w_gu_bf, w_down_bf = _mla_attention(kmax, qm, km, vm, w_exp_gu, w_exp_down, B, S,
                                            tq=MLA_Q_TILE)
    om = om.reshape(B * S, -1)
    ow = _swa_attention(qw, kw, vw, sink, B, S, tq=SWA_Q_TILE).reshape(B * S, -1)

    x1s, hp, comb_t = _post_attention(
        om, ow, gates, x2d, w_br_mla.astype(BF16), w_br_swa.astype(BF16), w_out.astype(BF16),
        g_moe[None], w_router.T, router_bias[:, None], w_sh_gu.astype(BF16),
        w_sh_down.astype(BF16), tm=ROW_TILE)

    slots, wts, offs, cnts = _moe_plan(comb_t)
    routed = _moe_sparse(offs[:, :, 0], cnts[:, :, 0], slots, wts, hp, w_gu_bf, w_down_bf)

    return _ple(x1s, routed, p2d, g_ple[None], w_ple_gate.astype(BF16), b_ple[None],
                w_ple_proj.astype(BF16), tm=ROW_TILE)


def kernel(x, p, positions, g_mix, w_in, b_gate, g_cq, w_uq, g_ckv, w_ukv, g_qn_mla, g_kn_mla, g_qn_swa, g_kn_swa, sink, w_br_mla, w_br_swa, w_out, g_moe, w_router, router_bias, w_exp_gu, w_exp_down, w_sh_gu, w_sh_down, g_ple, w_ple_gate, b_ple, w_ple_proj):
    B, S, D = x.shape
    x2d = x.reshape(B * S, D)
    pos2d = positions.reshape(B * S, 1)
    for i in range(p.shape[0]):
        x2d = _layer(x2d, p[i].reshape(B * S, -1), pos2d, B, S, g_mix[i], w_in[i], b_gate[i],
                     g_cq[i], w_uq[i], g_ckv[i], w_ukv[i], g_qn_mla[i], g_kn_mla[i], g_qn_swa[i],
                     g_kn_swa[i], sink[i], w_br_mla[i], w_br_swa[i], w_out[i], g_moe[i],
                     w_router[i], router_bias[i], w_exp_gu[i], w_exp_down[i], w_sh_gu[i],
                     w_sh_down[i], g_ple[i], w_ple_gate[i], b_ple[i], w_ple_proj[i])
    return x2d.reshape(B, S, D)
```

```python
import functools

import jax
import jax.numpy as jnp
import numpy as np
from jax import lax
from jax.experimental import pallas as pl
from jax.experimental.pallas import tpu as pltpu

D_MODEL = 1024
PLE_DIM = 256
ROPE_THETA = 10000.0
EPS = 1e-6
NEG_INF = -1e30

MLA_HEADS = 8
MLA_Q_RANK = 384
MLA_KV_RANK = 256
MLA_NOPE = 64
MLA_ROPE = 32
MLA_QK = MLA_NOPE + MLA_ROPE
MLA_V = 64

SWA_HEADS = 8
SWA_KV_HEADS = 2
SWA_GROUP = SWA_HEADS // SWA_KV_HEADS
SWA_HD = 64
WINDOW = 128

OFF_CQ = MLA_Q_RANK
OFF_CKV = OFF_CQ + MLA_KV_RANK
OFF_KR = OFF_CKV + MLA_ROPE
OFF_QS = OFF_KR + SWA_HEADS * SWA_HD
OFF_KS = OFF_QS + SWA_KV_HEADS * SWA_HD
OFF_VS = OFF_KS + SWA_KV_HEADS * SWA_HD
OFF_GA = OFF_VS + D_MODEL

N_EXPERTS = 64
TOP_K = 8
N_GROUPS = 8
TOPK_GROUPS = 4
EXPERTS_PER_GROUP = N_EXPERTS // N_GROUPS
EXPERT_FF = 256
SHARED_FF = 256
ROUTED_SCALE = 2.5

LANES = 128
ROW_WORDS = D_MODEL // 2
ROW_SUB = ROW_WORDS // LANES
MOE_CHUNK = 2048
MOE_TILE = 128
MOE_TILE_BIG = 320
SLOT_ALIGN = 16
COMBINE_BLOCK = 256
COMBINE_UNROLL = 16
EXPERTS_PER_STEP = 4
EXPERT_STEPS = N_EXPERTS // EXPERTS_PER_STEP
PRE_TILE = 256
ROW_TILE = 1024
MLA_Q_TILE = 1024
MLA_KV_TILE = 256
LOG2E = 1.4426950408889634
SOFTMAX_UNDERFLOW_GUARD = 1e-30
KEY_NORM_MARGIN = 1.01
SWA_Q_TILE = 1024
SWA_BLOCK = 128
VMEM_LIMIT = 56 * 1024 * 1024

BF16 = jnp.bfloat16
F32 = jnp.float32

C_CQ = 0
C_CKV = C_CQ + MLA_Q_RANK
C_KR = C_CKV + MLA_KV_RANK
C_QS = C_KR + LANES
C_KS = C_QS + SWA_HEADS * LANES
C_VS = C_KS + SWA_KV_HEADS * LANES
C_GA = C_VS + SWA_KV_HEADS * LANES
C_END = C_GA + 2 * D_MODEL


def _full(shape):
    nd = len(shape)
    return pl.BlockSpec(shape, lambda *_: (0,) * nd)


def _dot(a, b):
    return jnp.dot(a, b, preferred_element_type=F32)


def _dot_nt(a, b, precision=None):
    return lax.dot_general(a, b, (((1,), (1,)), ((), ())), precision=precision,
                           preferred_element_type=F32)


def _rms(v, n):
    return v * lax.rsqrt(jnp.sum(v * v, axis=-1, keepdims=True) * (1.0 / n) + EPS)


def _rope(v, cos, sin):
    return v * cos + pltpu.roll(v, LANES // 2, 1) * sin


def _pre_kernel(x_ref, pos_ref, w_all_ref, w_uq_ref, w_k_ref, w_v_ref, g_mix_ref, g_cq_ref,
                g_ckv_ref, gq_m_ref, gk_m_ref, gq_s_ref, gk_s_ref, b_gate_ref, rope_ref,
                qm_ref, km_ref, vm_ref, qw_ref, kw_ref, vw_ref, gate_ref):
    x = x_ref[...]
    h = (_rms(x, D_MODEL) * g_mix_ref[...]).astype(BF16)

    def proj(lo, hi):
        return _dot(h, w_all_ref[:, lo:hi])

    z_lat = proj(C_CQ, C_QS)
    z_qs = proj(C_QS, C_KS)
    half = (C_END - C_GA) // 2
    z_ga = proj(C_GA, C_GA + half)

    pos = pos_ref[...].astype(F32)
    rope = rope_ref[...]
    ang = pos * rope[0:1, :]
    cos_m1 = jnp.cos(ang) - 1.0
    sin = jnp.sin(ang)
    cos_m, sin_m = 1.0 + cos_m1 * rope[2:3, :], sin * rope[1:2, :]
    cos_s, sin_s = 1.0 + cos_m1 * rope[4:5, :], sin * rope[3:4, :]

    cqn = (_rms(z_lat[:, C_CQ:C_CKV], MLA_Q_RANK) * g_cq_ref[...]).astype(BF16)
    ckvn = (_rms(z_lat[:, C_CKV:C_KR], MLA_KV_RANK) * g_ckv_ref[...]).astype(BF16)
    q = _dot(cqn, w_uq_ref[...])
    gq_m = gq_m_ref[...]
    for hd in range(MLA_HEADS):
        qh = _rms(q[:, hd * LANES:(hd + 1) * LANES], MLA_QK) * gq_m
        qh = _rope(qh, cos_m, sin_m) * (MLA_QK ** -0.5 * LOG2E)
        qm_ref[:, hd * LANES:(hd + 1) * LANES] = qh.astype(BF16)

    kn = _dot(ckvn, w_k_ref[...])
    vm_ref[...] = _dot(ckvn, w_v_ref[...]).astype(BF16)
    z_kv = proj(C_KS, C_GA)

    gq_s = gq_s_ref[...]
    for hd in range(SWA_HEADS):
        qh = _rms(z_qs[:, hd * LANES:(hd + 1) * LANES], SWA_HD) * gq_s
        qh = _rope(qh, cos_s, sin_s) * (SWA_HD ** -0.5 * LOG2E)
        qw_ref[:, hd * LANES:(hd + 1) * LANES] = qh.astype(BF16)

    gk_m = gk_m_ref[...]
    kr = z_lat[:, C_KR:C_QS]
    ss_kr = jnp.sum(kr * kr, axis=-1, keepdims=True)
    kr_rot = _rope(kr * gk_m, cos_m, sin_m)
    for hd in range(MLA_HEADS):
        kh = kn[:, hd * LANES:(hd + 1) * LANES]
        ss = jnp.sum(kh * kh, axis=-1, keepdims=True) + ss_kr
        sc = lax.rsqrt(ss * (1.0 / MLA_QK) + EPS)
        km_ref[:, hd * LANES:(hd + 1) * LANES] = ((kh * gk_m + kr_rot) * sc).astype(BF16)

    z_gb = proj(C_GA + half, C_END)

    gk_s = gk_s_ref[...]
    for hd in range(SWA_KV_HEADS):
        kh = _rms(z_kv[:, hd * LANES:(hd + 1) * LANES], SWA_HD) * gk_s
        kh = _rope(kh, cos_s, sin_s)
        kw_ref[:, hd * LANES:(hd + 1) * LANES] = kh.astype(BF16)
    vw_ref[...] = z_kv[:, C_VS - C_KS:].astype(BF16)

    gate_ref[:, :half] = jax.nn.sigmoid(z_ga + b_gate_ref[:, :half]).astype(BF16)
    gate_ref[:, half:] = jax.nn.sigmoid(z_gb + b_gate_ref[:, half:]).astype(BF16)


def _pre_attention(x2d, pos2d, w_all, w_uq, w_k, w_v, g_mix, g_cq, g_ckv, gq_m, gk_m, gq_s,
                   gk_s, b_gate, rope_tab, tm):
    T = x2d.shape[0]
    row = lambda n: pl.BlockSpec((tm, n), lambda i: (i, 0))
    outs = [(MLA_HEADS * LANES, BF16), (MLA_HEADS * LANES, BF16), (MLA_HEADS * MLA_V, BF16),
            (SWA_HEADS * LANES, BF16), (SWA_KV_HEADS * LANES, BF16),
            (SWA_KV_HEADS * LANES, BF16), (2 * D_MODEL, BF16)]
    consts = [w_all, w_uq, w_k, w_v, g_mix, g_cq, g_ckv, gq_m, gk_m, gq_s, gk_s, b_gate, rope_tab]
    return pl.pallas_call(
        _pre_kernel,
        grid=(T // tm,),
        in_specs=[row(D_MODEL), row(1)] + [_full(c.shape) for c in consts],
        out_specs=[row(n) for n, _ in outs],
        out_shape=[jax.ShapeDtypeStruct((T, n), dt) for n, dt in outs],
        compiler_params=pltpu.CompilerParams(dimension_semantics=("arbitrary",),
                                             vmem_limit_bytes=VMEM_LIMIT),
        name="pre_attention",
    )(x2d, pos2d, *consts)


def _half_masks(dtype):
    lane = lax.broadcasted_iota(jnp.int32, (1, LANES), 1)
    lo = (lane < LANES // 2).astype(dtype)
    return lo, 1 - lo


def _mla_exact(q_ref, k_ref, v_ref, o_ref):
    v = v_ref[...]
    masks = _half_masks(v.dtype)
    acc = None
    for hh in range(2):
        q = q_ref[:, hh * LANES:(hh + 1) * LANES]
        k = k_ref[:, hh * LANES:(hh + 1) * LANES]
        s = _dot_nt(q, k)
        m = jnp.max(s, axis=-1, keepdims=True)
        p = jnp.exp2(s - m)
        l = jnp.sum(p, axis=-1, keepdims=True)
        o = _dot(p.astype(BF16), v * masks[hh]) / l
        acc = o if acc is None else acc + o
    o_ref[...] = acc.astype(o_ref.dtype)


def _mla_kernel(kmax_ref, q_ref, k_ref, v_ref, wa_ref, wb_ref, o_ref, wa_out, wb_out, vt_ref):
    @pl.when(pl.program_id(2) == 0)
    def _():
        vt_ref[...] = v_ref[...].astype(F32).T.astype(BF16)

    wa_out[...] = wa_ref[...].astype(BF16)
    wb_out[...] = wb_ref[...].astype(BF16)

    ones = jnp.ones((8, LANES), BF16)
    outs = []
    lmin = None
    for hh in range(2):
        q = q_ref[:, hh * LANES:(hh + 1) * LANES]
        qf = q.astype(F32)
        q_sq = _dot_nt(ones, (qf * qf).astype(BF16))[0:1, :]
        bound = jnp.sqrt(q_sq) * kmax_ref[0]
        p_t = jnp.exp2(_dot_nt(k_ref[:, hh * LANES:(hh + 1) * LANES], q) - bound)
        l = jnp.sum(p_t, axis=0, keepdims=True)
        o_t = _dot(vt_ref[hh * MLA_V:(hh + 1) * MLA_V, :], p_t.astype(BF16))
        outs.append(o_t / l)
        lm = jnp.min(l)
        lmin = lm if lmin is None else jnp.minimum(lmin, lm)
    o_ref[...] = jnp.concatenate(outs, axis=0).T.astype(o_ref.dtype)

    @pl.when(jnp.logical_not(lmin > SOFTMAX_UNDERFLOW_GUARD))
    def _():
        _mla_exact(q_ref, k_ref, v_ref, o_ref)


def _mla_attention(kmax, qm, km, vm, w_a, w_b, B, S, tq):
    pairs = MLA_HEADS // 2
    q3 = qm.reshape(B, S, MLA_HEADS * LANES)
    k3 = km.reshape(B, S, MLA_HEADS * LANES)
    v3 = vm.reshape(B, S, MLA_HEADS * MLA_V)
    steps = B * pairs * (S // tq)
    wa3 = w_a.reshape(steps, -1, w_a.shape[-1])
    wb3 = w_b.reshape(steps, -1, w_b.shape[-1])
    step = lambda b, p, i: ((b * pairs + p) * (S // tq) + i, 0, 0)
    w_spec = lambda w: pl.BlockSpec((None,) + w.shape[1:], step)
    om, wa_bf, wb_bf = pl.pallas_call(
        _mla_kernel,
        grid=(B, pairs, S // tq),
        in_specs=[pl.BlockSpec(memory_space=pltpu.SMEM),
                  pl.BlockSpec((None, tq, 2 * LANES), lambda b, p, i: (b, i, p)),
                  pl.BlockSpec((None, S, 2 * LANES), lambda b, p, i: (b, 0, p)),
                  pl.BlockSpec((None, S, LANES), lambda b, p, i: (b, 0, p)),
                  w_spec(wa3), w_spec(wb3)],
        out_specs=[pl.BlockSpec((None, tq, LANES), lambda b, p, i: (b, i, p)),
                   w_spec(wa3), w_spec(wb3)],
        out_shape=[jax.ShapeDtypeStruct((B, S, MLA_HEADS * MLA_V), BF16),
                   jax.ShapeDtypeStruct(wa3.shape, BF16),
                   jax.ShapeDtypeStruct(wb3.shape, BF16)],
        scratch_shapes=[pltpu.VMEM((LANES, S), BF16)],
        compiler_params=pltpu.CompilerParams(
            dimension_semantics=("arbitrary", "arbitrary", "arbitrary"),
            vmem_limit_bytes=VMEM_LIMIT),
        name="mla_attention",
    )(kmax, q3, k3, v3, wa3, wb3)
    return om, wa_bf.reshape(w_a.shape), wb_bf.reshape(w_b.shape)


def _swa_kernel(sink_ref, q_ref, k_ref, v_ref, o_ref, *, tq, S):
    hk = pl.program_id(1)
    i = pl.program_id(2)
    tk = SWA_BLOCK + 2 * WINDOW
    row = lax.broadcasted_iota(jnp.int32, (SWA_GROUP * SWA_BLOCK, 1), 0)
    qoff = row & (SWA_BLOCK - 1)
    head = row // SWA_BLOCK
    sk = jnp.zeros((SWA_GROUP * SWA_BLOCK, 1), F32)
    for g in range(SWA_GROUP):
        sk = jnp.where(head == g, sink_ref[SWA_GROUP * hk + g] * LOG2E, sk)
    low_half = lax.broadcasted_iota(jnp.int32, (1, LANES), 1) < LANES // 2
    for sub in range(tq // SWA_BLOCK):
        rows = slice(sub * SWA_BLOCK, (sub + 1) * SWA_BLOCK)
        q0 = i * tq + sub * SWA_BLOCK
        kstart = pl.multiple_of(jnp.clip(q0 - WINDOW, 0, S - tk), WINDOW)
        k = k_ref[pl.ds(kstart, tk), :]
        v = v_ref[pl.ds(kstart, tk), :]
        q = jnp.concatenate([q_ref[rows, g * LANES:(g + 1) * LANES] for g in range(SWA_GROUP)], axis=0)
        kpos = kstart + lax.broadcasted_iota(jnp.int32, (1, tk), 1)
        valid = jnp.abs(kpos - (q0 + qoff[:SWA_BLOCK])) <= WINDOW
        s = _dot_nt(q, k).reshape(SWA_GROUP, SWA_BLOCK, tk)
        s = jnp.where(valid[None], s, NEG_INF).reshape(SWA_GROUP * SWA_BLOCK, tk)
        m = jnp.maximum(jnp.max(s, axis=-1, keepdims=True), sk)
        e = jnp.exp2(s - m)
        denom = jnp.sum(e, axis=-1, keepdims=True) + jnp.exp2(sk - m)
        o = _dot(e.astype(BF16), v) / denom
        for j in range(SWA_GROUP // 2):
            even = o[(2 * j) * SWA_BLOCK:(2 * j + 1) * SWA_BLOCK, :]
            odd = o[(2 * j + 1) * SWA_BLOCK:(2 * j + 2) * SWA_BLOCK, :]
            o_ref[rows, j * LANES:(j + 1) * LANES] = jnp.where(low_half, even, odd).astype(o_ref.dtype)


def _swa_attention(qw, kw, vw, sink, B, S, tq):
    q3 = qw.reshape(B, S, SWA_HEADS * LANES)
    k3 = kw.reshape(B, S, SWA_KV_HEADS * LANES)
    v3 = vw.reshape(B, S, SWA_KV_HEADS * LANES)
    return pl.pallas_call(
        functools.partial(_swa_kernel, tq=tq, S=S),
        grid=(B, SWA_KV_HEADS, S // tq),
        in_specs=[pl.BlockSpec(memory_space=pltpu.SMEM),
                  pl.BlockSpec((None, tq, SWA_GROUP * LANES), lambda b, h, i: (b, i, h)),
                  pl.BlockSpec((None, S, LANES), lambda b, h, i: (b, 0, h)),
                  pl.BlockSpec((None, S, LANES), lambda b, h, i: (b, 0, h))],
        out_specs=pl.BlockSpec((None, tq, SWA_GROUP * SWA_HD), lambda b, h, i: (b, i, h)),
        out_shape=jax.ShapeDtypeStruct((B, S, SWA_HEADS * SWA_HD), BF16),
        compiler_params=pltpu.CompilerParams(
            dimension_semantics=("arbitrary", "arbitrary", "arbitrary"),
            vmem_limit_bytes=VMEM_LIMIT),
        name="swa_attention",
    )(sink, q3, k3, v3)


def _beats(vj, vi, j_first):
    return (vj >= vi) if j_first else (vj > vi)


def _route(scores, sel):
    G, P = N_GROUPS, EXPERTS_PER_GROUP
    groups = [sel[g * P:(g + 1) * P, :] for g in range(G)]
    row = lax.broadcasted_iota(jnp.int32, (P, 1), 0)
    gscore = []
    for vg in groups:
        m1 = jnp.max(vg, axis=0, keepdims=True)
        first = jnp.min(jnp.where(vg == m1, row, P), axis=0, keepdims=True)
        m2 = jnp.max(jnp.where(row == first, -jnp.inf, vg), axis=0, keepdims=True)
        gscore.append(m1 + m2)
    masked = []
    for g in range(G):
        rank = jnp.zeros_like(gscore[g], dtype=jnp.int32)
        for g2 in range(G):
            if g2 != g:
                rank = rank + _beats(gscore[g2], gscore[g], g2 < g).astype(jnp.int32)
        masked.append(jnp.where(rank < TOPK_GROUPS, groups[g], NEG_INF))
    index = [row + g * P for g in range(G)]
    chosen = [None] * G
    for _ in range(TOP_K):
        best = masked[0]
        for g in range(1, G):
            best = jnp.maximum(best, masked[g])
        best = jnp.max(best, axis=0, keepdims=True)
        first = jnp.where(masked[0] == best, index[0], N_EXPERTS)
        for g in range(1, G):
            first = jnp.minimum(first, jnp.where(masked[g] == best, index[g], N_EXPERTS))
        first = jnp.min(first, axis=0, keepdims=True)
        for g in range(G):
            hit = index[g] == first
            chosen[g] = hit if chosen[g] is None else (chosen[g] | hit)
            masked[g] = jnp.where(hit, -jnp.inf, masked[g])
    picked = [jnp.where(chosen[g], scores[g * P:(g + 1) * P, :], 0.0) for g in range(G)]
    total = picked[0]
    for g in range(1, G):
        total = total + picked[g]
    denom = jnp.sum(total, axis=0, keepdims=True)
    return [pk / denom * ROUTED_SCALE for pk in picked]


def _pack_pair(lo, hi):
    return pltpu.pack_elementwise([lo, hi], packed_dtype=BF16)


def _unpack_pair(word):
    lo = pltpu.unpack_elementwise(word, index=0, packed_dtype=BF16, unpacked_dtype=F32)
    hi = pltpu.unpack_elementwise(word, index=1, packed_dtype=BF16, unpacked_dtype=F32)
    return lo, hi


def _store_rows_dense(ref, words):
    for j in range(ROW_SUB):
        ref[pl.ds(j, words.shape[0], stride=ROW_SUB), :] = words[:, j * LANES:(j + 1) * LANES]


def _load_rows_dense(ref, rows):
    sub = ROW_SUB
    return jnp.concatenate([ref[pl.ds(j, rows, stride=sub), :] for j in range(sub)], axis=1)


def _post_kernel(om_ref, ow_ref, gate_ref, x_ref, wbm_ref, wbw_ref, wout_ref, g_moe_ref, wr_ref,
                 rb_ref, wsgu_ref, wsd_ref, x1_ref, hp_ref, comb_ref):
    am = _dot(om_ref[...], wbm_ref[...])
    aw = _dot(ow_ref[...], wbw_ref[...])
    gates = gate_ref[...].astype(F32)
    merged = gates[:, :D_MODEL] * am + gates[:, D_MODEL:] * aw
    x1 = x_ref[...] + _dot(merged.astype(BF16), wout_ref[...])

    h2 = _rms(x1, D_MODEL) * g_moe_ref[...]
    h2b = h2.astype(BF16)
    _store_rows_dense(hp_ref, _pack_pair(h2[:, :ROW_WORDS], h2[:, ROW_WORDS:]))

    wr = wr_ref[...]
    w_hi = wr.astype(BF16)
    w_lo = (wr - w_hi.astype(F32)).astype(BF16)
    h_lo = (h2 - h2b.astype(F32)).astype(BF16)
    by_hi = _dot_nt(jnp.concatenate([w_hi, w_lo], axis=0), h2b)
    logits = by_hi[:N_EXPERTS] + by_hi[N_EXPERTS:] + _dot_nt(w_hi, h_lo)

    sgu = _dot(h2b, wsgu_ref[...])
    sh = jax.nn.silu(sgu[:, :SHARED_FF]) * sgu[:, SHARED_FF:]
    x1_ref[...] = x1 + _dot(sh.astype(BF16), wsd_ref[...])

    scores = jax.nn.sigmoid(logits)
    comb = _route(scores, scores + rb_ref[...])
    for g in range(N_GROUPS):
        comb_ref[g * EXPERTS_PER_GROUP:(g + 1) * EXPERTS_PER_GROUP, :] = comb[g]


def _post_attention(om, ow, gates, x2d, wbm, wbw, wout, g_moe, wr_t, rbias, wsgu, wsd, tm):
    T = x2d.shape[0]
    row = lambda n: pl.BlockSpec((tm, n), lambda i: (i, 0))
    consts = [wbm, wbw, wout, g_moe, wr_t, rbias, wsgu, wsd]
    return pl.pallas_call(
        _post_kernel,
        grid=(T // tm,),
        in_specs=[row(om.shape[1]), row(ow.shape[1]), row(2 * D_MODEL), row(D_MODEL)]
        + [_full(c.shape) for c in consts],
        out_specs=[row(D_MODEL), pl.BlockSpec((tm * ROW_SUB, LANES), lambda i: (i, 0)),
                   pl.BlockSpec((N_EXPERTS, tm), lambda i: (0, i))],
        out_shape=[jax.ShapeDtypeStruct((T, D_MODEL), F32),
                   jax.ShapeDtypeStruct((T * ROW_SUB, LANES), jnp.uint32),
                   jax.ShapeDtypeStruct((N_EXPERTS, T), F32)],
        compiler_params=pltpu.CompilerParams(dimension_semantics=("arbitrary",),
                                             vmem_limit_bytes=VMEM_LIMIT),
        name="post_attention",
    )(om, ow, gates, x2d, *consts)


SLOT_ROWS = -(-(MOE_CHUNK * TOP_K + N_EXPERTS * (SLOT_ALIGN - 1) + MOE_TILE_BIG) // MOE_TILE) * MOE_TILE
DUMMY_SLOT = SLOT_ROWS - 1
PLAN_BLOCK = 256


def _plan_kernel(comb_ref, slot_ref, w_ref, off_ref, cnt_ref):
    comb = comb_ref[...]
    sel = comb > 0.0
    m = sel.astype(F32)
    mb = m.astype(BF16)
    r_i = lax.broadcasted_iota(jnp.int32, (PLAN_BLOCK, PLAN_BLOCK), 0)
    c_i = lax.broadcasted_iota(jnp.int32, (PLAN_BLOCK, PLAN_BLOCK), 1)
    before = (r_i < c_i).astype(BF16)
    carry = jnp.zeros((N_EXPERTS, 1), F32)
    ranks = []
    for b in range(MOE_CHUNK // PLAN_BLOCK):
        blk = slice(b * PLAN_BLOCK, (b + 1) * PLAN_BLOCK)
        ranks.append(_dot(mb[:, blk], before) + carry)
        carry = carry + jnp.sum(m[:, blk], axis=1, keepdims=True)
    rank = jnp.concatenate(ranks, axis=1)
    cnt = carry
    cnt_pad = jnp.floor((cnt + (SLOT_ALIGN - 1)) * (1.0 / SLOT_ALIGN)) * SLOT_ALIGN
    e_r = lax.broadcasted_iota(jnp.int32, (N_EXPERTS, N_EXPERTS), 0)
    e_c = lax.broadcasted_iota(jnp.int32, (N_EXPERTS, N_EXPERTS), 1)
    below = (e_c < e_r).astype(F32)
    off = jnp.dot(below, jnp.broadcast_to(cnt_pad, (N_EXPERTS, LANES)),
                  precision=lax.Precision.HIGHEST, preferred_element_type=F32)
    slot = off[:, :1] + rank
    kidx = _dot(below.astype(BF16), mb)
    row = lax.broadcasted_iota(jnp.int32, (TOP_K, 1), 0)
    slot_acc = jnp.zeros((TOP_K, MOE_CHUNK), F32)
    w_acc = jnp.zeros((TOP_K, MOE_CHUNK), F32)
    for k in range(TOP_K):
        pick = jnp.where(sel & (kidx == k), 1.0, 0.0)
        found = jnp.sum(pick, axis=0, keepdims=True) > 0.0
        s_k = jnp.where(found, jnp.sum(pick * slot, axis=0, keepdims=True), float(DUMMY_SLOT))
        w_k = jnp.sum(pick * comb, axis=0, keepdims=True)
        slot_acc = jnp.where(row == k, s_k, slot_acc)
        w_acc = jnp.where(row == k, w_k, w_acc)
    slot_ref[...] = slot_acc.astype(jnp.int32) * ROW_SUB
    w_ref[...] = w_acc
    off_ref[...] = off.astype(jnp.int32)
    cnt_ref[...] = jnp.broadcast_to(cnt, (N_EXPERTS, LANES)).astype(jnp.int32)


def _moe_plan(comb_t):
    T = comb_t.shape[1]
    nch = T // MOE_CHUNK
    per_pair = pl.BlockSpec((None, TOP_K, MOE_CHUNK), lambda c: (c, 0, 0))
    per_expert = pl.BlockSpec((None, N_EXPERTS, LANES), lambda c: (c, 0, 0))
    return pl.pallas_call(
        _plan_kernel,
        grid=(nch,),
        in_specs=[pl.BlockSpec((N_EXPERTS, MOE_CHUNK), lambda c: (0, c))],
        out_specs=[per_pair, per_pair, per_expert, per_expert],
        out_shape=[jax.ShapeDtypeStruct((nch, TOP_K, MOE_CHUNK), jnp.int32),
                   jax.ShapeDtypeStruct((nch, TOP_K, MOE_CHUNK), F32),
                   jax.ShapeDtypeStruct((nch, N_EXPERTS, LANES), jnp.int32),
                   jax.ShapeDtypeStruct((nch, N_EXPERTS, LANES), jnp.int32)],
        compiler_params=pltpu.CompilerParams(dimension_semantics=("arbitrary",),
                                             vmem_limit_bytes=VMEM_LIMIT),
        name="moe_plan",
    )(comb_t)


def _slab_at(ref, first):
    return ref.at[pl.ds(pl.multiple_of(first, ROW_SUB), ROW_SUB), :]


def _slab(ref, row):
    return _slab_at(ref, row * ROW_SUB)


def _moe_kernel(off_ref, cnt_ref, slot_hbm, w_hbm, hp_ref, wgu_ref, wd_ref, o_ref,
                buf, clo, chi, *smem_and_sem):
    slot_s = smem_and_sem[:TOP_K]
    w_s = smem_and_sem[TOP_K:2 * TOP_K]
    sem = smem_and_sem[2 * TOP_K]
    c = pl.program_id(0)
    s = pl.program_id(1)

    @pl.when(s == 0)
    def _dispatch():
        copies = [pltpu.make_async_copy(slot_hbm.at[c, k], slot_s[k], sem.at[k]) for k in range(TOP_K)]
        copies += [pltpu.make_async_copy(w_hbm.at[c, k], w_s[k], sem.at[TOP_K + k]) for k in range(TOP_K)]
        for cp in copies:
            cp.start()

        @pl.when(c == 0)
        def _():
            buf[...] = jnp.zeros_like(buf)

        for cp in copies:
            cp.wait()

        def scatter(tb, carry):
            for tt in range(8):
                t = tb * 8 + tt
                slab = _slab(hp_ref, t)[...]
                for k in range(TOP_K):
                    _slab_at(buf, slot_s[k][t])[...] = slab
            return carry

        lax.fori_loop(0, MOE_CHUNK // 8, scatter, 0)

    def expert(ee):
        e = s * EXPERTS_PER_STEP + ee
        n = cnt_ref[c, e]
        off = off_ref[c, e]

        def ffn(start, rows):
            view = buf.at[pl.ds(pl.multiple_of((off + start) * ROW_SUB, SLOT_ALIGN * ROW_SUB),
                                rows * ROW_SUB), :]
            x_lo, x_hi = _unpack_pair(_load_rows_dense(view, rows))
            gu = (_dot(x_lo.astype(BF16), wgu_ref[ee, :ROW_WORDS, :])
                  + _dot(x_hi.astype(BF16), wgu_ref[ee, ROW_WORDS:, :]))
            hid = jax.nn.silu(gu[:, :EXPERT_FF]) * gu[:, EXPERT_FF:]
            y = _dot(hid.astype(BF16), wd_ref[ee])
            mine = lax.broadcasted_iota(jnp.int32, (rows, 1), 0) < (n - start)
            _store_rows_dense(view, _pack_pair(jnp.where(mine, y[:, :ROW_WORDS], x_lo),
                                               jnp.where(mine, y[:, ROW_WORDS:], x_hi)))

        n_big = (n + MOE_TILE_BIG - MOE_TILE - 1) // MOE_TILE_BIG

        def big(r, carry):
            ffn(r * MOE_TILE_BIG, MOE_TILE_BIG)
            return carry

        lax.fori_loop(0, n_big, big, 0)

        @pl.when(n > n_big * MOE_TILE_BIG)
        def _():
            ffn(n_big * MOE_TILE_BIG, MOE_TILE)

    @pl.when(s < EXPERT_STEPS)
    def _experts():
        for ee in range(EXPERTS_PER_STEP):
            expert(ee)

    @pl.when(s >= EXPERT_STEPS)
    def _combine():
        t0 = (s - EXPERT_STEPS) * COMBINE_BLOCK

        def gather(i, carry):
            for tt in range(COMBINE_UNROLL):
                tl = i * COMBINE_UNROLL + tt
                t = t0 + tl
                lo, hi = _unpack_pair(_slab_at(buf, slot_s[0][t])[...])
                acc_lo = w_s[0][t] * lo
                acc_hi = w_s[0][t] * hi
                for k in range(1, TOP_K):
                    lo, hi = _unpack_pair(_slab_at(buf, slot_s[k][t])[...])
                    wk = w_s[k][t]
                    acc_lo = acc_lo + wk * lo
                    acc_hi = acc_hi + wk * hi
                _slab(clo, tl)[...] = acc_lo
                _slab(chi, tl)[...] = acc_hi
            return carry

        lax.fori_loop(0, COMBINE_BLOCK // COMBINE_UNROLL, gather, 0)
        o_ref[:, :ROW_WORDS] = _load_rows_dense(clo, COMBINE_BLOCK)
        o_ref[:, ROW_WORDS:] = _load_rows_dense(chi, COMBINE_BLOCK)


def _moe_sparse(offs, cnts, slots, wts, hp, w_gu, w_d):
    nch = offs.shape[0]
    T = nch * MOE_CHUNK
    blocks = MOE_CHUNK // COMBINE_BLOCK
    expert = lambda c, s, *_: (jnp.minimum(s, EXPERT_STEPS - 1), 0, 0)
    grid_spec = pltpu.PrefetchScalarGridSpec(
        num_scalar_prefetch=2,
        grid=(nch, EXPERT_STEPS + blocks),
        in_specs=[pl.BlockSpec(memory_space=pl.ANY),
                  pl.BlockSpec(memory_space=pl.ANY),
                  pl.BlockSpec((MOE_CHUNK * ROW_SUB, LANES), lambda c, s, *_: (c, 0),
                               pipeline_mode=pl.Buffered(1)),
                  pl.BlockSpec((EXPERTS_PER_STEP, D_MODEL, 2 * EXPERT_FF), expert),
                  pl.BlockSpec((EXPERTS_PER_STEP, EXPERT_FF, D_MODEL), expert)],
        out_specs=pl.BlockSpec(
            (COMBINE_BLOCK, D_MODEL),
            lambda c, s, *_: (c * blocks + jnp.maximum(s - EXPERT_STEPS, 0), 0)),
        scratch_shapes=[pltpu.VMEM((SLOT_ROWS * ROW_SUB, LANES), jnp.uint32),
                        pltpu.VMEM((COMBINE_BLOCK * ROW_SUB, LANES), F32),
                        pltpu.VMEM((COMBINE_BLOCK * ROW_SUB, LANES), F32),
                        *[pltpu.SMEM((MOE_CHUNK,), jnp.int32) for _ in range(TOP_K)],
                        *[pltpu.SMEM((MOE_CHUNK,), F32) for _ in range(TOP_K)],
                        pltpu.SemaphoreType.DMA((2 * TOP_K,))])
    return pl.pallas_call(
        _moe_kernel,
        grid_spec=grid_spec,
        out_shape=jax.ShapeDtypeStruct((T, D_MODEL), F32),
        compiler_params=pltpu.CompilerParams(dimension_semantics=("arbitrary", "arbitrary"),
                                             vmem_limit_bytes=VMEM_LIMIT),
        name="moe_experts",
    )(offs, cnts, slots, wts, hp, w_gu, w_d)


def _ple_kernel(x1_ref, r_ref, p_ref, g_ref, wg_ref, b_ref, wp_ref, o_ref):
    x2 = x1_ref[...] + r_ref[...]
    hn = (_rms(x2, D_MODEL) * g_ref[...]).astype(BF16)
    gate = jax.nn.sigmoid(_dot(hn, wg_ref[...]) + b_ref[...])
    o_ref[...] = x2 + gate * _dot(p_ref[...].astype(BF16), wp_ref[...])


def _ple(x1s, routed, p2d, g_ple, wg, b_ple, wp, tm):
    T = x1s.shape[0]
    row = lambda n: pl.BlockSpec((tm, n), lambda i: (i, 0))
    consts = [g_ple, wg, b_ple, wp]
    return pl.pallas_call(
        _ple_kernel,
        grid=(T // tm,),
        in_specs=[row(D_MODEL), row(D_MODEL), row(PLE_DIM)] + [_full(c.shape) for c in consts],
        out_specs=row(D_MODEL),
        out_shape=jax.ShapeDtypeStruct((T, D_MODEL), F32),
        compiler_params=pltpu.CompilerParams(dimension_semantics=("arbitrary",),
                                             vmem_limit_bytes=VMEM_LIMIT),
        name="ple",
    )(x1s, routed, p2d, *consts)


def _lane_map(*runs):
    src = np.full((LANES,), -1)
    for lane, dim, n in runs:
        src[lane:lane + n] = np.arange(dim, dim + n)
    return src


_MLA_HALF = MLA_ROPE // 2
_SWA_HALF = SWA_HD // 2
MLA_LANES = _lane_map((0, MLA_NOPE, _MLA_HALF), (_MLA_HALF, 0, LANES // 2 - _MLA_HALF),
                      (LANES // 2, MLA_NOPE + _MLA_HALF, _MLA_HALF),
                      (LANES // 2 + _MLA_HALF, LANES // 2 - _MLA_HALF, MLA_NOPE - LANES // 2 + _MLA_HALF))
MLA_NOPE_LANES = np.where(MLA_LANES < MLA_NOPE, MLA_LANES, -1)
MLA_ROPE_LANES = np.where(MLA_LANES >= MLA_NOPE, MLA_LANES - MLA_NOPE, -1)
SWA_LANES = _lane_map((LANES // 2 - _SWA_HALF, 0, _SWA_HALF), (LANES - _SWA_HALF, _SWA_HALF, _SWA_HALF))


def _spread(w, heads, lane_src):
    k = w.shape[0]
    dim = w.shape[1] // heads
    w = jnp.pad(w.reshape(k, heads, dim), ((0, 0), (0, 0), (0, 1)))
    return w[:, :, np.where(lane_src < 0, dim, lane_src)].reshape(k, heads * LANES)


def _rope_table():
    def inv_freq(dim):
        return 1.0 / (ROPE_THETA ** (jnp.arange(0, dim, 2, dtype=F32) / dim))

    def selector(lane_src, half):
        sel = np.where(lane_src < 0, 0.0, np.where(lane_src < half, -1.0, 1.0))
        return jnp.asarray(sel, F32)

    sel_m = selector(MLA_ROPE_LANES, _MLA_HALF)
    sel_s = selector(SWA_LANES, _SWA_HALF)
    freq_m = _spread(jnp.tile(inv_freq(MLA_ROPE), 2)[None], 1, MLA_ROPE_LANES)[0]
    freq_s = _spread(jnp.tile(inv_freq(SWA_HD), 2)[None], 1, SWA_LANES)[0]
    zero = jnp.zeros((LANES,), F32)
    rows = [freq_m + freq_s, sel_m, jnp.abs(sel_m), sel_s, jnp.abs(sel_s), zero, zero, zero]
    return jnp.stack(rows)


def _layer(x2d, p2d, pos2d, B, S, g_mix, w_in, b_gate, g_cq, w_uq, g_ckv, w_ukv, g_qn_mla, g_kn_mla,
           g_qn_swa, g_kn_swa, sink, w_br_mla, w_br_swa, w_out, g_moe, w_router, router_bias,
           w_exp_gu, w_exp_down, w_sh_gu, w_sh_down, g_ple, w_ple_gate, b_ple, w_ple_proj):
    w_kr = _spread(w_in[:, OFF_CKV:OFF_KR], 1, MLA_ROPE_LANES)
    w_vs = w_in[:, OFF_KS:OFF_VS].reshape(D_MODEL, SWA_KV_HEADS, 1, SWA_HD)
    w_vs = jnp.broadcast_to(w_vs, (D_MODEL, SWA_KV_HEADS, 2, SWA_HD)).reshape(D_MODEL, -1)
    w_all = jnp.concatenate([
        w_in[:, :OFF_CKV], w_kr,
        _spread(w_in[:, OFF_KR:OFF_QS], SWA_HEADS, SWA_LANES),
        _spread(w_in[:, OFF_QS:OFF_KS], SWA_KV_HEADS, SWA_LANES),
        w_vs, w_in[:, OFF_VS:]], axis=1).astype(BF16)
    assert w_all.shape[1] == C_END
    w_uq_p = _spread(w_uq, MLA_HEADS, MLA_LANES).astype(BF16)
    w_ukv3 = w_ukv.reshape(MLA_KV_RANK, MLA_HEADS, MLA_NOPE + MLA_V)
    w_k = _spread(w_ukv3[:, :, :MLA_NOPE].reshape(MLA_KV_RANK, -1), MLA_HEADS, MLA_NOPE_LANES).astype(BF16)
    w_v = w_ukv3[:, :, MLA_NOPE:].reshape(MLA_KV_RANK, -1).astype(BF16)

    qm, km, vm, qw, kw, vw, gates = _pre_attention(
        x2d, pos2d, w_all, w_uq_p, w_k, w_v, g_mix[None], g_cq[None], g_ckv[None],
        _spread(g_qn_mla[None], 1, MLA_LANES), _spread(g_kn_mla[None], 1, MLA_LANES),
        _spread(g_qn_swa[None], 1, SWA_LANES), _spread(g_kn_swa[None], 1, SWA_LANES),
        b_gate[None], _rope_table(), tm=PRE_TILE)

    kmax = (KEY_NORM_MARGIN * MLA_QK ** 0.5) * jnp.max(jnp.abs(g_kn_mla), keepdims=True)
    om, w_gu_bf, w_down_bf = _mla_attention(kmax, qm, km, vm, w_exp_gu, w_exp_down, B, S,
                                            tq=MLA_Q_TILE)
    om = om.reshape(B * S, -1)
    ow = _swa_attention(qw, kw, vw, sink, B, S, tq=SWA_Q_TILE).reshape(B * S, -1)

    x1s, hp, comb_t = _post_attention(
        om, ow, gates, x2d, w_br_mla.astype(BF16), w_br_swa.astype(BF16), w_out.astype(BF16),
        g_moe[None], w_router.T, router_bias[:, None], w_sh_gu.astype(BF16),
        w_sh_down.astype(BF16), tm=ROW_TILE)

    slots, wts, offs, cnts = _moe_plan(comb_t)
    routed = _moe_sparse(offs[:, :, 0], cnts[:, :, 0], slots, wts, hp, w_gu_bf, w_down_bf)

    return _ple(x1s, routed, p2d, g_ple[None], w_ple_gate.astype(BF16), b_ple[None],
                w_ple_proj.astype(BF16), tm=ROW_TILE)


def kernel(x, p, positions, g_mix, w_in, b_gate, g_cq, w_uq, g_ckv, w_ukv, g_qn_mla, g_kn_mla, g_qn_swa, g_kn_swa, sink, w_br_mla, w_br_swa, w_out, g_moe, w_router, router_bias, w_exp_gu, w_exp_down, w_sh_gu, w_sh_down, g_ple, w_ple_gate, b_ple, w_ple_proj):
    B, S, D = x.shape
    x2d = x.reshape(B * S, D)
    pos2d = positions.reshape(B * S, 1)
    for i in range(p.shape[0]):
        x2d = _layer(x2d, p[i].reshape(B * S, -1), pos2d, B, S, g_mix[i], w_in[i], b_gate[i],
                     g_cq[i], w_uq[i], g_ckv[i], w_ukv[i], g_qn_mla[i], g_kn_mla[i], g_qn_swa[i],
                     g_kn_swa[i], sink[i], w_br_mla[i], w_br_swa[i], w_out[i], g_moe[i],
                     w_router[i], router_bias[i], w_exp_gu[i], w_exp_down[i], w_sh_gu[i],
                     w_sh_down[i], g_ple[i], w_ple_gate[i], b_ple[i], w_ple_proj[i])
    return x2d.reshape(B, S, D)
```

```python
import functools

import jax
import jax.numpy as jnp
import numpy as np
from jax import lax
from jax.experimental import pallas as pl
from jax.experimental.pallas import tpu as pltpu

D_MODEL = 1024
PLE_DIM = 256
ROPE_THETA = 10000.0
EPS = 1e-6
NEG_INF = -1e30

MLA_HEADS = 8
MLA_Q_RANK = 384
MLA_KV_RANK = 256
MLA_NOPE = 64
MLA_ROPE = 32
MLA_QK = MLA_NOPE + MLA_ROPE
MLA_V = 64

SWA_HEADS = 8
SWA_KV_HEADS = 2
SWA_GROUP = SWA_HEADS // SWA_KV_HEADS
SWA_HD = 64
WINDOW = 128

OFF_CQ = MLA_Q_RANK
OFF_CKV = OFF_CQ + MLA_KV_RANK
OFF_KR = OFF_CKV + MLA_ROPE
OFF_QS = OFF_KR + SWA_HEADS * SWA_HD
OFF_KS = OFF_QS + SWA_KV_HEADS * SWA_HD
OFF_VS = OFF_KS + SWA_KV_HEADS * SWA_HD
OFF_GA = OFF_VS + D_MODEL

N_EXPERTS = 64
TOP_K = 8
N_GROUPS = 8
TOPK_GROUPS = 4
EXPERTS_PER_GROUP = N_EXPERTS // N_GROUPS
EXPERT_FF = 256
SHARED_FF = 256
ROUTED_SCALE = 2.5

LANES = 128
ROW_WORDS = D_MODEL // 2
ROW_SUB = ROW_WORDS // LANES
MOE_CHUNK = 2048
MOE_TILE = 128
MOE_TILE_BIG = 320
SLOT_ALIGN = 16
COMBINE_BLOCK = 256
COMBINE_UNROLL = 16
EXPERTS_PER_STEP = 4
EXPERT_STEPS = N_EXPERTS // EXPERTS_PER_STEP
PRE_TILE = 256
ROW_TILE = 1024
MLA_Q_TILE = 1024
MLA_KV_TILE = 256
LOG2E = 1.4426950408889634
SOFTMAX_UNDERFLOW_GUARD = 1e-30
KEY_NORM_MARGIN = 1.01
SWA_BLOCK = 128
SWA_PAIRS_PER_KV = (MLA_HEADS // 2) // SWA_KV_HEADS
VMEM_LIMIT = 56 * 1024 * 1024

BF16 = jnp.bfloat16
F32 = jnp.float32

C_CQ = 0
C_CKV = C_CQ + MLA_Q_RANK
C_KR = C_CKV + MLA_KV_RANK
C_QS = C_KR + LANES
C_KS = C_QS + SWA_HEADS * LANES
C_VS = C_KS + SWA_KV_HEADS * LANES
C_GA = C_VS + SWA_KV_HEADS * LANES
C_END = C_GA + 2 * D_MODEL


def _full(shape):
    nd = len(shape)
    return pl.BlockSpec(shape, lambda *_: (0,) * nd)


def _dot(a, b):
    return jnp.dot(a, b, preferred_element_type=F32)


def _dot_nt(a, b, precision=None):
    return lax.dot_general(a, b, (((1,), (1,)), ((), ())), precision=precision,
                           preferred_element_type=F32)


def _rms(v, n):
    return v * lax.rsqrt(jnp.sum(v * v, axis=-1, keepdims=True) * (1.0 / n) + EPS)


def _rope(v, cos, sin):
    return v * cos + pltpu.roll(v, LANES // 2, 1) * sin


def _pre_kernel(x_ref, pos_ref, w_all_ref, w_uq_ref, w_k_ref, w_v_ref, g_mix_ref, g_cq_ref,
                g_ckv_ref, gq_m_ref, gk_m_ref, gq_s_ref, gk_s_ref, b_gate_ref, rope_ref,
                qm_ref, km_ref, vm_ref, qw_ref, kw_ref, vw_ref, gate_ref):
    x = x_ref[...]
    h = (_rms(x, D_MODEL) * g_mix_ref[...]).astype(BF16)

    def proj(lo, hi):
        return _dot(h, w_all_ref[:, lo:hi])

    z_lat = proj(C_CQ, C_QS)
    z_qs = proj(C_QS, C_KS)
    half = (C_END - C_GA) // 2
    z_ga = proj(C_GA, C_GA + half)

    pos = pos_ref[...].astype(F32)
    rope = rope_ref[...]
    ang = pos * rope[0:1, :]
    cos_m1 = jnp.cos(ang) - 1.0
    sin = jnp.sin(ang)
    cos_m, sin_m = 1.0 + cos_m1 * rope[2:3, :], sin * rope[1:2, :]
    cos_s, sin_s = 1.0 + cos_m1 * rope[4:5, :], sin * rope[3:4, :]

    cqn = (_rms(z_lat[:, C_CQ:C_CKV], MLA_Q_RANK) * g_cq_ref[...]).astype(BF16)
    ckvn = (_rms(z_lat[:, C_CKV:C_KR], MLA_KV_RANK) * g_ckv_ref[...]).astype(BF16)
    q = _dot(cqn, w_uq_ref[...])
    gq_m = gq_m_ref[...]
    for hd in range(MLA_HEADS):
        qh = _rms(q[:, hd * LANES:(hd + 1) * LANES], MLA_QK) * gq_m
        qh = _rope(qh, cos_m, sin_m) * (MLA_QK ** -0.5 * LOG2E)
        qm_ref[:, hd * LANES:(hd + 1) * LANES] = qh.astype(BF16)

    kn = _dot(ckvn, w_k_ref[...])
    vm_ref[...] = _dot(ckvn, w_v_ref[...]).astype(BF16)
    z_kv = proj(C_KS, C_GA)

    gq_s = gq_s_ref[...]
    for hd in range(SWA_HEADS):
        qh = _rms(z_qs[:, hd * LANES:(hd + 1) * LANES], SWA_HD) * gq_s
        qh = _rope(qh, cos_s, sin_s) * (SWA_HD ** -0.5 * LOG2E)
        qw_ref[:, hd * LANES:(hd + 1) * LANES] = qh.astype(BF16)

    gk_m = gk_m_ref[...]
    kr = z_lat[:, C_KR:C_QS]
    ss_kr = jnp.sum(kr * kr, axis=-1, keepdims=True)
    kr_rot = _rope(kr * gk_m, cos_m, sin_m)
    for hd in range(MLA_HEADS):
        kh = kn[:, hd * LANES:(hd + 1) * LANES]
        ss = jnp.sum(kh * kh, axis=-1, keepdims=True) + ss_kr
        sc = lax.rsqrt(ss * (1.0 / MLA_QK) + EPS)
        km_ref[:, hd * LANES:(hd + 1) * LANES] = ((kh * gk_m + kr_rot) * sc).astype(BF16)

    z_gb = proj(C_GA + half, C_END)

    gk_s = gk_s_ref[...]
    for hd in range(SWA_KV_HEADS):
        kh = _rms(z_kv[:, hd * LANES:(hd + 1) * LANES], SWA_HD) * gk_s
        kh = _rope(kh, cos_s, sin_s)
        kw_ref[:, hd * LANES:(hd + 1) * LANES] = kh.astype(BF16)
    vw_ref[...] = z_kv[:, C_VS - C_KS:].astype(BF16)

    gate_ref[:, :half] = jax.nn.sigmoid(z_ga + b_gate_ref[:, :half]).astype(BF16)
    gate_ref[:, half:] = jax.nn.sigmoid(z_gb + b_gate_ref[:, half:]).astype(BF16)


def _pre_attention(x2d, pos2d, w_all, w_uq, w_k, w_v, g_mix, g_cq, g_ckv, gq_m, gk_m, gq_s,
                   gk_s, b_gate, rope_tab, tm):
    T = x2d.shape[0]
    row = lambda n: pl.BlockSpec((tm, n), lambda i: (i, 0))
    outs = [(MLA_HEADS * LANES, BF16), (MLA_HEADS * LANES, BF16), (MLA_HEADS * MLA_V, BF16),
            (SWA_HEADS * LANES, BF16), (SWA_KV_HEADS * LANES, BF16),
            (SWA_KV_HEADS * LANES, BF16), (2 * D_MODEL, BF16)]
    consts = [w_all, w_uq, w_k, w_v, g_mix, g_cq, g_ckv, gq_m, gk_m, gq_s, gk_s, b_gate, rope_tab]
    return pl.pallas_call(
        _pre_kernel,
        grid=(T // tm,),
        in_specs=[row(D_MODEL), row(1)] + [_full(c.shape) for c in consts],
        out_specs=[row(n) for n, _ in outs],
        out_shape=[jax.ShapeDtypeStruct((T, n), dt) for n, dt in outs],
        compiler_params=pltpu.CompilerParams(dimension_semantics=("arbitrary",),
                                             vmem_limit_bytes=VMEM_LIMIT),
        name="pre_attention",
    )(x2d, pos2d, *consts)


def _half_masks(dtype):
    lane = lax.broadcasted_iota(jnp.int32, (1, LANES), 1)
    lo = (lane < LANES // 2).astype(dtype)
    return lo, 1 - lo


def _mla_exact(q_ref, k_ref, v_ref, o_ref):
    v = v_ref[...]
    masks = _half_masks(v.dtype)
    acc = None
    for hh in range(2):
        q = q_ref[:, hh * LANES:(hh + 1) * LANES]
        k = k_ref[:, hh * LANES:(hh + 1) * LANES]
        s = _dot_nt(q, k)
        m = jnp.max(s, axis=-1, keepdims=True)
        p = jnp.exp2(s - m)
        l = jnp.sum(p, axis=-1, keepdims=True)
        o = _dot(p.astype(BF16), v * masks[hh]) / l
        acc = o if acc is None else acc + o
    o_ref[...] = acc.astype(o_ref.dtype)


def _attn_kernel(kmax_ref, sink_ref, q_ref, k_ref, v_ref, sq_ref, sk_ref, sv_ref, wa_ref, wb_ref,
                 o_ref, so_ref, wa_out, wb_out, vt_ref):
    @pl.when(pl.program_id(2) == 0)
    def _():
        vt_ref[...] = v_ref[...].astype(F32).T.astype(BF16)

    wa_out[...] = wa_ref[...].astype(BF16)
    wb_out[...] = wb_ref[...].astype(BF16)

    pair = pl.program_id(1)
    part = (pair % SWA_PAIRS_PER_KV) * pl.num_programs(2) + pl.program_id(2)
    _swa_block(sink_ref, sq_ref, sk_ref, sv_ref, so_ref, pair // SWA_PAIRS_PER_KV,
               part * sq_ref.shape[0])

    ones = jnp.ones((8, LANES), BF16)
    outs = []
    lmin = None
    for hh in range(2):
        q = q_ref[:, hh * LANES:(hh + 1) * LANES]
        qf = q.astype(F32)
        q_sq = _dot_nt(ones, (qf * qf).astype(BF16))[0:1, :]
        bound = jnp.sqrt(q_sq) * kmax_ref[0]
        p_t = jnp.exp2(_dot_nt(k_ref[:, hh * LANES:(hh + 1) * LANES], q) - bound)
        l = jnp.sum(p_t, axis=0, keepdims=True)
        o_t = _dot(vt_ref[hh * MLA_V:(hh + 1) * MLA_V, :], p_t.astype(BF16))
        outs.append(o_t / l)
        lm = jnp.min(l)
        lmin = lm if lmin is None else jnp.minimum(lmin, lm)
    o_ref[...] = jnp.concatenate(outs, axis=0).T.astype(o_ref.dtype)

    @pl.when(jnp.logical_not(lmin > SOFTMAX_UNDERFLOW_GUARD))
    def _():
        _mla_exact(q_ref, k_ref, v_ref, o_ref)


def _attention(kmax, sink, qm, km, vm, qw, kw, vw, w_a, w_b, B, S, tq):
    pairs = MLA_HEADS // 2
    q3 = qm.reshape(B, S, MLA_HEADS * LANES)
    k3 = km.reshape(B, S, MLA_HEADS * LANES)
    v3 = vm.reshape(B, S, MLA_HEADS * MLA_V)
    sq3 = qw.reshape(B, S, SWA_HEADS * LANES)
    sk3 = kw.reshape(B, S, SWA_KV_HEADS * LANES)
    sv3 = vw.reshape(B, S, SWA_KV_HEADS * LANES)
    assert pairs == SWA_PAIRS_PER_KV * SWA_KV_HEADS
    s_tq = S // (SWA_PAIRS_PER_KV * (S // tq))
    s_part = lambda b, p, i: (b, (p % SWA_PAIRS_PER_KV) * (S // tq) + i, p // SWA_PAIRS_PER_KV)
    s_head = lambda b, p, i: (b, 0, p // SWA_PAIRS_PER_KV)
    steps = B * pairs * (S // tq)
    wa3 = w_a.reshape(steps, -1, w_a.shape[-1])
    wb3 = w_b.reshape(steps, -1, w_b.shape[-1])
    step = lambda b, p, i: ((b * pairs + p) * (S // tq) + i, 0, 0)
    w_spec = lambda w: pl.BlockSpec((None,) + w.shape[1:], step)
    om, ow, wa_bf, wb_bf = pl.pallas_call(
        _attn_kernel,
        grid=(B, pairs, S // tq),
        in_specs=[pl.BlockSpec(memory_space=pltpu.SMEM),
                  pl.BlockSpec(memory_space=pltpu.SMEM),
                  pl.BlockSpec((None, tq, 2 * LANES), lambda b, p, i: (b, i, p)),
                  pl.BlockSpec((None, S, 2 * LANES), lambda b, p, i: (b, 0, p)),
                  pl.BlockSpec((None, S, LANES), lambda b, p, i: (b, 0, p)),
                  pl.BlockSpec((None, s_tq, SWA_GROUP * LANES), s_part),
                  pl.BlockSpec((None, S, LANES), s_head),
                  pl.BlockSpec((None, S, LANES), s_head),
                  w_spec(wa3), w_spec(wb3)],
        out_specs=[pl.BlockSpec((None, tq, LANES), lambda b, p, i: (b, i, p)),
                   pl.BlockSpec((None, s_tq, SWA_GROUP * SWA_HD), s_part),
                   w_spec(wa3), w_spec(wb3)],
        out_shape=[jax.ShapeDtypeStruct((B, S, MLA_HEADS * MLA_V), BF16),
                   jax.ShapeDtypeStruct((B, S, SWA_HEADS * SWA_HD), BF16),
                   jax.ShapeDtypeStruct(wa3.shape, BF16),
                   jax.ShapeDtypeStruct(wb3.shape, BF16)],
        scratch_shapes=[pltpu.VMEM((LANES, S), BF16)],
        compiler_params=pltpu.CompilerParams(
            dimension_semantics=("arbitrary", "arbitrary", "arbitrary"),
            vmem_limit_bytes=VMEM_LIMIT),
        name="attention",
    )(kmax, sink, q3, k3, v3, sq3, sk3, sv3, wa3, wb3)
    return om, ow, wa_bf.reshape(w_a.shape), wb_bf.reshape(w_b.shape)


def _swa_block(sink_ref, q_ref, k_ref, v_ref, o_ref, hk, q_base):
    tq = q_ref.shape[0]
    S = k_ref.shape[0]
    tk = SWA_BLOCK + 2 * WINDOW
    row = lax.broadcasted_iota(jnp.int32, (SWA_GROUP * SWA_BLOCK, 1), 0)
    qoff = row & (SWA_BLOCK - 1)
    head = row // SWA_BLOCK
    sk = jnp.zeros((SWA_GROUP * SWA_BLOCK, 1), F32)
    for g in range(SWA_GROUP):
        sk = jnp.where(head == g, sink_ref[SWA_GROUP * hk + g] * LOG2E, sk)
    low_half = lax.broadcasted_iota(jnp.int32, (1, LANES), 1) < LANES // 2
    for sub in range(tq // SWA_BLOCK):
        rows = slice(sub * SWA_BLOCK, (sub + 1) * SWA_BLOCK)
        q0 = q_base + sub * SWA_BLOCK
        kstart = pl.multiple_of(jnp.clip(q0 - WINDOW, 0, S - tk), WINDOW)
        k = k_ref[pl.ds(kstart, tk), :]
        v = v_ref[pl.ds(kstart, tk), :]
        q = jnp.concatenate([q_ref[rows, g * LANES:(g + 1) * LANES] for g in range(SWA_GROUP)], axis=0)
        kpos = kstart + lax.broadcasted_iota(jnp.int32, (1, tk), 1)
        valid = jnp.abs(kpos - (q0 + qoff[:SWA_BLOCK])) <= WINDOW
        s = _dot_nt(q, k).reshape(SWA_GROUP, SWA_BLOCK, tk)
        s = jnp.where(valid[None], s, NEG_INF).reshape(SWA_GROUP * SWA_BLOCK, tk)
        m = jnp.maximum(jnp.max(s, axis=-1, keepdims=True), sk)
        e = jnp.exp2(s - m)
        denom = jnp.sum(e, axis=-1, keepdims=True) + jnp.exp2(sk - m)
        o = _dot(e.astype(BF16), v) / denom
        for j in range(SWA_GROUP // 2):
            even = o[(2 * j) * SWA_BLOCK:(2 * j + 1) * SWA_BLOCK, :]
            odd = o[(2 * j + 1) * SWA_BLOCK:(2 * j + 2) * SWA_BLOCK, :]
            o_ref[rows, j * LANES:(j + 1) * LANES] = jnp.where(low_half, even, odd).astype(o_ref.dtype)


def _beats(vj, vi, j_first):
    return (vj >= vi) if j_first else (vj > vi)


def _route(scores, sel):
    G, P = N_GROUPS, EXPERTS_PER_GROUP
    groups = [sel[g * P:(g + 1) * P, :] for g in range(G)]
    row = lax.broadcasted_iota(jnp.int32, (P, 1), 0)
    gscore = []
    for vg in groups:
        m1 = jnp.max(vg, axis=0, keepdims=True)
        first = jnp.min(jnp.where(vg == m1, row, P), axis=0, keepdims=True)
        m2 = jnp.max(jnp.where(row == first, -jnp.inf, vg), axis=0, keepdims=True)
        gscore.append(m1 + m2)
    masked = []
    for g in range(G):
        rank = jnp.zeros_like(gscore[g], dtype=jnp.int32)
        for g2 in range(G):
            if g2 != g:
                rank = rank + _beats(gscore[g2], gscore[g], g2 < g).astype(jnp.int32)
        masked.append(jnp.where(rank < TOPK_GROUPS, groups[g], NEG_INF))
    index = [row + g * P for g in range(G)]
    chosen = [None] * G
    for _ in range(TOP_K):
        best = masked[0]
        for g in range(1, G):
            best = jnp.maximum(best, masked[g])
        best = jnp.max(best, axis=0, keepdims=True)
        first = jnp.where(masked[0] == best, index[0], N_EXPERTS)
        for g in range(1, G):
            first = jnp.minimum(first, jnp.where(masked[g] == best, index[g], N_EXPERTS))
        first = jnp.min(first, axis=0, keepdims=True)
        for g in range(G):
            hit = index[g] == first
            chosen[g] = hit if chosen[g] is None else (chosen[g] | hit)
            masked[g] = jnp.where(hit, -jnp.inf, masked[g])
    picked = [jnp.where(chosen[g], scores[g * P:(g + 1) * P, :], 0.0) for g in range(G)]
    total = picked[0]
    for g in range(1, G):
        total = total + picked[g]
    denom = jnp.sum(total, axis=0, keepdims=True)
    return [pk / denom * ROUTED_SCALE for pk in picked]


def _pack_pair(lo, hi):
    return pltpu.pack_elementwise([lo, hi], packed_dtype=BF16)


def _unpack_pair(word):
    lo = pltpu.unpack_elementwise(word, index=0, packed_dtype=BF16, unpacked_dtype=F32)
    hi = pltpu.unpack_elementwise(word, index=1, packed_dtype=BF16, unpacked_dtype=F32)
    return lo, hi


def _store_rows_dense(ref, words):
    for j in range(ROW_SUB):
        ref[pl.ds(j, words.shape[0], stride=ROW_SUB), :] = words[:, j * LANES:(j + 1) * LANES]


def _load_rows_dense(ref, rows):
    sub = ROW_SUB
    return jnp.concatenate([ref[pl.ds(j, rows, stride=sub), :] for j in range(sub)], axis=1)


def _post_kernel(om_ref, ow_ref, gate_ref, x_ref, wbm_ref, wbw_ref, wout_ref, g_moe_ref, wr_ref,
                 rb_ref, wsgu_ref, wsd_ref, x1_ref, hp_ref, comb_ref):
    am = _dot(om_ref[...], wbm_ref[...])
    aw = _dot(ow_ref[...], wbw_ref[...])
    gates = gate_ref[...].astype(F32)
    merged = gates[:, :D_MODEL] * am + gates[:, D_MODEL:] * aw
    x1 = x_ref[...] + _dot(merged.astype(BF16), wout_ref[...])

    h2 = _rms(x1, D_MODEL) * g_moe_ref[...]
    h2b = h2.astype(BF16)
    _store_rows_dense(hp_ref, _pack_pair(h2[:, :ROW_WORDS], h2[:, ROW_WORDS:]))

    wr = wr_ref[...]
    w_hi = wr.astype(BF16)
    w_lo = (wr - w_hi.astype(F32)).astype(BF16)
    h_lo = (h2 - h2b.astype(F32)).astype(BF16)
    by_hi = _dot_nt(jnp.concatenate([w_hi, w_lo], axis=0), h2b)
    logits = by_hi[:N_EXPERTS] + by_hi[N_EXPERTS:] + _dot_nt(w_hi, h_lo)

    sgu = _dot(h2b, wsgu_ref[...])
    sh = jax.nn.silu(sgu[:, :SHARED_FF]) * sgu[:, SHARED_FF:]
    x1_ref[...] = x1 + _dot(sh.astype(BF16), wsd_ref[...])

    scores = jax.nn.sigmoid(logits)
    comb = _route(scores, scores + rb_ref[...])
    for g in range(N_GROUPS):
        comb_ref[g * EXPERTS_PER_GROUP:(g + 1) * EXPERTS_PER_GROUP, :] = comb[g]


def _post_attention(om, ow, gates, x2d, wbm, wbw, wout, g_moe, wr_t, rbias, wsgu, wsd, tm):
    T = x2d.shape[0]
    row = lambda n: pl.BlockSpec((tm, n), lambda i: (i, 0))
    consts = [wbm, wbw, wout, g_moe, wr_t, rbias, wsgu, wsd]
    return pl.pallas_call(
        _post_kernel,
        grid=(T // tm,),
        in_specs=[row(om.shape[1]), row(ow.shape[1]), row(2 * D_MODEL), row(D_MODEL)]
        + [_full(c.shape) for c in consts],
        out_specs=[row(D_MODEL), pl.BlockSpec((tm * ROW_SUB, LANES), lambda i: (i, 0)),
                   pl.BlockSpec((N_EXPERTS, tm), lambda i: (0, i))],
        out_shape=[jax.ShapeDtypeStruct((T, D_MODEL), F32),
                   jax.ShapeDtypeStruct((T * ROW_SUB, LANES), jnp.uint32),
                   jax.ShapeDtypeStruct((N_EXPERTS, T), F32)],
        compiler_params=pltpu.CompilerParams(dimension_semantics=("arbitrary",),
                                             vmem_limit_bytes=VMEM_LIMIT),
        name="post_attention",
    )(om, ow, gates, x2d, *consts)


SLOT_ROWS = -(-(MOE_CHUNK * TOP_K + N_EXPERTS * (SLOT_ALIGN - 1) + MOE_TILE_BIG) // MOE_TILE) * MOE_TILE
DUMMY_SLOT = SLOT_ROWS - 1
PLAN_BLOCK = 256


def _plan_kernel(comb_ref, slot_ref, w_ref, off_ref, cnt_ref):
    comb = comb_ref[...]
    sel = comb > 0.0
    m = sel.astype(F32)
    mb = m.astype(BF16)
    r_i = lax.broadcasted_iota(jnp.int32, (PLAN_BLOCK, PLAN_BLOCK), 0)
    c_i = lax.broadcasted_iota(jnp.int32, (PLAN_BLOCK, PLAN_BLOCK), 1)
    before = (r_i < c_i).astype(BF16)
    carry = jnp.zeros((N_EXPERTS, 1), F32)
    ranks = []
    for b in range(MOE_CHUNK // PLAN_BLOCK):
        blk = slice(b * PLAN_BLOCK, (b + 1) * PLAN_BLOCK)
        ranks.append(_dot(mb[:, blk], before) + carry)
        carry = carry + jnp.sum(m[:, blk], axis=1, keepdims=True)
    rank = jnp.concatenate(ranks, axis=1)
    cnt = carry
    cnt_pad = jnp.floor((cnt + (SLOT_ALIGN - 1)) * (1.0 / SLOT_ALIGN)) * SLOT_ALIGN
    e_r = lax.broadcasted_iota(jnp.int32, (N_EXPERTS, N_EXPERTS), 0)
    e_c = lax.broadcasted_iota(jnp.int32, (N_EXPERTS, N_EXPERTS), 1)
    below = (e_c < e_r).astype(F32)
    off = jnp.dot(below, jnp.broadcast_to(cnt_pad, (N_EXPERTS, LANES)),
                  precision=lax.Precision.HIGHEST, preferred_element_type=F32)
    slot = off[:, :1] + rank
    kidx = _dot(below.astype(BF16), mb)
    row = lax.broadcasted_iota(jnp.int32, (TOP_K, 1), 0)
    slot_acc = jnp.zeros((TOP_K, MOE_CHUNK), F32)
    w_acc = jnp.zeros((TOP_K, MOE_CHUNK), F32)
    for k in range(TOP_K):
        pick = jnp.where(sel & (kidx == k), 1.0, 0.0)
        found = jnp.sum(pick, axis=0, keepdims=True) > 0.0
        s_k = jnp.where(found, jnp.sum(pick * slot, axis=0, keepdims=True), float(DUMMY_SLOT))
        w_k = jnp.sum(pick * comb, axis=0, keepdims=True)
        slot_acc = jnp.where(row == k, s_k, slot_acc)
        w_acc = jnp.where(row == k, w_k, w_acc)
    slot_ref[...] = slot_acc.astype(jnp.int32) * ROW_SUB
    w_ref[...] = w_acc
    off_ref[...] = off.astype(jnp.int32)
    cnt_ref[...] = jnp.broadcast_to(cnt, (N_EXPERTS, LANES)).astype(jnp.int32)


def _moe_plan(comb_t):
    T = comb_t.shape[1]
    nch = T // MOE_CHUNK
    per_pair = pl.BlockSpec((None, TOP_K, MOE_CHUNK), lambda c: (c, 0, 0))
    per_expert = pl.BlockSpec((None, N_EXPERTS, LANES), lambda c: (c, 0, 0))
    return pl.pallas_call(
        _plan_kernel,
        grid=(nch,),
        in_specs=[pl.BlockSpec((N_EXPERTS, MOE_CHUNK), lambda c: (0, c))],
        out_specs=[per_pair, per_pair, per_expert, per_expert],
        out_shape=[jax.ShapeDtypeStruct((nch, TOP_K, MOE_CHUNK), jnp.int32),
                   jax.ShapeDtypeStruct((nch, TOP_K, MOE_CHUNK), F32),
                   jax.ShapeDtypeStruct((nch, N_EXPERTS, LANES), jnp.int32),
                   jax.ShapeDtypeStruct((nch, N_EXPERTS, LANES), jnp.int32)],
        compiler_params=pltpu.CompilerParams(dimension_semantics=("arbitrary",),
                                             vmem_limit_bytes=VMEM_LIMIT),
        name="moe_plan",
    )(comb_t)


def _slab_at(ref, first):
    return ref.at[pl.ds(pl.multiple_of(first, ROW_SUB), ROW_SUB), :]


def _slab(ref, row):
    return _slab_at(ref, row * ROW_SUB)


def _moe_kernel(off_ref, cnt_ref, slot_hbm, w_hbm, hp_ref, wgu_ref, wd_ref, o_ref,
                buf, clo, chi, *smem_and_sem):
    slot_s = smem_and_sem[:TOP_K]
    w_s = smem_and_sem[TOP_K:2 * TOP_K]
    sem = smem_and_sem[2 * TOP_K]
    c = pl.program_id(0)
    s = pl.program_id(1)

    @pl.when(s == 0)
    def _dispatch():
        copies = [pltpu.make_async_copy(slot_hbm.at[c, k], slot_s[k], sem.at[k]) for k in range(TOP_K)]
        copies += [pltpu.make_async_copy(w_hbm.at[c, k], w_s[k], sem.at[TOP_K + k]) for k in range(TOP_K)]
        for cp in copies:
            cp.start()

        @pl.when(c == 0)
        def _():
            buf[...] = jnp.zeros_like(buf)

        for cp in copies:
            cp.wait()

        def scatter(tb, carry):
            for tt in range(8):
                t = tb * 8 + tt
                slab = _slab(hp_ref, t)[...]
                for k in range(TOP_K):
                    _slab_at(buf, slot_s[k][t])[...] = slab
            return carry

        lax.fori_loop(0, MOE_CHUNK // 8, scatter, 0)

    def expert(ee):
        e = s * EXPERTS_PER_STEP + ee
        n = cnt_ref[c, e]
        off = off_ref[c, e]

        def ffn(start, rows):
            view = buf.at[pl.ds(pl.multiple_of((off + start) * ROW_SUB, SLOT_ALIGN * ROW_SUB),
                                rows * ROW_SUB), :]
            x_lo, x_hi = _unpack_pair(_load_rows_dense(view, rows))
            gu = (_dot(x_lo.astype(BF16), wgu_ref[ee, :ROW_WORDS, :])
                  + _dot(x_hi.astype(BF16), wgu_ref[ee, ROW_WORDS:, :]))
            hid = jax.nn.silu(gu[:, :EXPERT_FF]) * gu[:, EXPERT_FF:]
            y = _dot(hid.astype(BF16), wd_ref[ee])
            mine = lax.broadcasted_iota(jnp.int32, (rows, 1), 0) < (n - start)
            _store_rows_dense(view, _pack_pair(jnp.where(mine, y[:, :ROW_WORDS], x_lo),
                                               jnp.where(mine, y[:, ROW_WORDS:], x_hi)))

        n_big = (n + MOE_TILE_BIG - MOE_TILE - 1) // MOE_TILE_BIG

        def big(r, carry):
            ffn(r * MOE_TILE_BIG, MOE_TILE_BIG)
            return carry

        lax.fori_loop(0, n_big, big, 0)

        @pl.when(n > n_big * MOE_TILE_BIG)
        def _():
            ffn(n_big * MOE_TILE_BIG, MOE_TILE)

    @pl.when(s < EXPERT_STEPS)
    def _experts():
        for ee in range(EXPERTS_PER_STEP):
            expert(ee)

    @pl.when(s >= EXPERT_STEPS)
    def _combine():
        t0 = (s - EXPERT_STEPS) * COMBINE_BLOCK

        def gather(i, carry):
            for tt in range(COMBINE_UNROLL):
                tl = i * COMBINE_UNROLL + tt
                t = t0 + tl
                lo, hi = _unpack_pair(_slab_at(buf, slot_s[0][t])[...])
                acc_lo = w_s[0][t] * lo
                acc_hi = w_s[0][t] * hi
                for k in range(1, TOP_K):
                    lo, hi = _unpack_pair(_slab_at(buf, slot_s[k][t])[...])
                    wk = w_s[k][t]
                    acc_lo = acc_lo + wk * lo
                    acc_hi = acc_hi + wk * hi
                _slab(clo, tl)[...] = acc_lo
                _slab(chi, tl)[...] = acc_hi
            return carry

        lax.fori_loop(0, COMBINE_BLOCK // COMBINE_UNROLL, gather, 0)
        o_ref[:, :ROW_WORDS] = _load_rows_dense(clo, COMBINE_BLOCK)
        o_ref[:, ROW_WORDS:] = _load_rows_dense(chi, COMBINE_BLOCK)


def _moe_sparse(offs, cnts, slots, wts, hp, w_gu, w_d):
    nch = offs.shape[0]
    T = nch * MOE_CHUNK
    blocks = MOE_CHUNK // COMBINE_BLOCK
    expert = lambda c, s, *_: (jnp.minimum(s, EXPERT_STEPS - 1), 0, 0)
    grid_spec = pltpu.PrefetchScalarGridSpec(
        num_scalar_prefetch=2,
        grid=(nch, EXPERT_STEPS + blocks),
        in_specs=[pl.BlockSpec(memory_space=pl.ANY),
                  pl.BlockSpec(memory_space=pl.ANY),
                  pl.BlockSpec((MOE_CHUNK * ROW_SUB, LANES), lambda c, s, *_: (c, 0),
                               pipeline_mode=pl.Buffered(1)),
                  pl.BlockSpec((EXPERTS_PER_STEP, D_MODEL, 2 * EXPERT_FF), expert),
                  pl.BlockSpec((EXPERTS_PER_STEP, EXPERT_FF, D_MODEL), expert)],
        out_specs=pl.BlockSpec(
            (COMBINE_BLOCK, D_MODEL),
            lambda c, s, *_: (c * blocks + jnp.maximum(s - EXPERT_STEPS, 0), 0)),
        scratch_shapes=[pltpu.VMEM((SLOT_ROWS * ROW_SUB, LANES), jnp.uint32),
                        pltpu.VMEM((COMBINE_BLOCK * ROW_SUB, LANES), F32),
                        pltpu.VMEM((COMBINE_BLOCK * ROW_SUB, LANES), F32),
                        *[pltpu.SMEM((MOE_CHUNK,), jnp.int32) for _ in range(TOP_K)],
                        *[pltpu.SMEM((MOE_CHUNK,), F32) for _ in range(TOP_K)],
                        pltpu.SemaphoreType.DMA((2 * TOP_K,))])
    return pl.pallas_call(
        _moe_kernel,
        grid_spec=grid_spec,
        out_shape=jax.ShapeDtypeStruct((T, D_MODEL), F32),
        compiler_params=pltpu.CompilerParams(dimension_semantics=("arbitrary", "arbitrary"),
                                             vmem_limit_bytes=VMEM_LIMIT),
        name="moe_experts",
    )(offs, cnts, slots, wts, hp, w_gu, w_d)


def _ple_kernel(x1_ref, r_ref, p_ref, g_ref, wg_ref, b_ref, wp_ref, o_ref):
    x2 = x1_ref[...] + r_ref[...]
    hn = (_rms(x2, D_MODEL) * g_ref[...]).astype(BF16)
    gate = jax.nn.sigmoid(_dot(hn, wg_ref[...]) + b_ref[...])
    o_ref[...] = x2 + gate * _dot(p_ref[...].astype(BF16), wp_ref[...])


def _ple(x1s, routed, p2d, g_ple, wg, b_ple, wp, tm):
    T = x1s.shape[0]
    row = lambda n: pl.BlockSpec((tm, n), lambda i: (i, 0))
    consts = [g_ple, wg, b_ple, wp]
    return pl.pallas_call(
        _ple_kernel,
        grid=(T // tm,),
        in_specs=[row(D_MODEL), row(D_MODEL), row(PLE_DIM)] + [_full(c.shape) for c in consts],
        out_specs=row(D_MODEL),
        out_shape=jax.ShapeDtypeStruct((T, D_MODEL), F32),
        compiler_params=pltpu.CompilerParams(dimension_semantics=("arbitrary",),
                                             vmem_limit_bytes=VMEM_LIMIT),
        name="ple",
    )(x1s, routed, p2d, *consts)


def _lane_map(*runs):
    src = np.full((LANES,), -1)
    for lane, dim, n in runs:
        src[lane:lane + n] = np.arange(dim, dim + n)
    return src


_MLA_HALF = MLA_ROPE // 2
_SWA_HALF = SWA_HD // 2
MLA_LANES = _lane_map((0, MLA_NOPE, _MLA_HALF), (_MLA_HALF, 0, LANES // 2 - _MLA_HALF),
                      (LANES // 2, MLA_NOPE + _MLA_HALF, _MLA_HALF),
                      (LANES // 2 + _MLA_HALF, LANES // 2 - _MLA_HALF, MLA_NOPE - LANES // 2 + _MLA_HALF))
MLA_NOPE_LANES = np.where(MLA_LANES < MLA_NOPE, MLA_LANES, -1)
MLA_ROPE_LANES = np.where(MLA_LANES >= MLA_NOPE, MLA_LANES - MLA_NOPE, -1)
SWA_LANES = _lane_map((LANES // 2 - _SWA_HALF, 0, _SWA_HALF), (LANES - _SWA_HALF, _SWA_HALF, _SWA_HALF))


def _spread(w, heads, lane_src):
    k = w.shape[0]
    dim = w.shape[1] // heads
    w = jnp.pad(w.reshape(k, heads, dim), ((0, 0), (0, 0), (0, 1)))
    return w[:, :, np.where(lane_src < 0, dim, lane_src)].reshape(k, heads * LANES)


def _rope_table():
    def inv_freq(dim):
        return 1.0 / (ROPE_THETA ** (jnp.arange(0, dim, 2, dtype=F32) / dim))

    def selector(lane_src, half):
        sel = np.where(lane_src < 0, 0.0, np.where(lane_src < half, -1.0, 1.0))
        return jnp.asarray(sel, F32)

    sel_m = selector(MLA_ROPE_LANES, _MLA_HALF)
    sel_s = selector(SWA_LANES, _SWA_HALF)
    freq_m = _spread(jnp.tile(inv_freq(MLA_ROPE), 2)[None], 1, MLA_ROPE_LANES)[0]
    freq_s = _spread(jnp.tile(inv_freq(SWA_HD), 2)[None], 1, SWA_LANES)[0]
    zero = jnp.zeros((LANES,), F32)
    rows = [freq_m + freq_s, sel_m, jnp.abs(sel_m), sel_s, jnp.abs(sel_s), zero, zero, zero]
    return jnp.stack(rows)


def _layer(x2d, p2d, pos2d, B, S, g_mix, w_in, b_gate, g_cq, w_uq, g_ckv, w_ukv, g_qn_mla, g_kn_mla,
           g_qn_swa, g_kn_swa, sink, w_br_mla, w_br_swa, w_out, g_moe, w_router, router_bias,
           w_exp_gu, w_exp_down, w_sh_gu, w_sh_down, g_ple, w_ple_gate, b_ple, w_ple_proj):
    w_kr = _spread(w_in[:, OFF_CKV:OFF_KR], 1, MLA_ROPE_LANES)
    w_vs = w_in[:, OFF_KS:OFF_VS].reshape(D_MODEL, SWA_KV_HEADS, 1, SWA_HD)
    w_vs = jnp.broadcast_to(w_vs, (D_MODEL, SWA_KV_HEADS, 2, SWA_HD)).reshape(D_MODEL, -1)
    w_all = jnp.concatenate([
        w_in[:, :OFF_CKV], w_kr,
        _spread(w_in[:, OFF_KR:OFF_QS], SWA_HEADS, SWA_LANES),
        _spread(w_in[:, OFF_QS:OFF_KS], SWA_KV_HEADS, SWA_LANES),
        w_vs, w_in[:, OFF_VS:]], axis=1).astype(BF16)
    assert w_all.shape[1] == C_END
    w_uq_p = _spread(w_uq, MLA_HEADS, MLA_LANES).astype(BF16)
    w_ukv3 = w_ukv.reshape(MLA_KV_RANK, MLA_HEADS, MLA_NOPE + MLA_V)
    w_k = _spread(w_ukv3[:, :, :MLA_NOPE].reshape(MLA_KV_RANK, -1), MLA_HEADS, MLA_NOPE_LANES).astype(BF16)
    w_v = w_ukv3[:, :, MLA_NOPE:].reshape(MLA_KV_RANK, -1).astype(BF16)

    qm, km, vm, qw, kw, vw, gates = _pre_attention(
        x2d, pos2d, w_all, w_uq_p, w_k, w_v, g_mix[None], g_cq[None], g_ckv[None],
        _spread(g_qn_mla[None], 1, MLA_LANES), _spread(g_kn_mla[None], 1, MLA_LANES),
        _spread(g_qn_swa[None], 1, SWA_LANES), _spread(g_kn_swa[None], 1, SWA_LANES),
        b_gate[None], _rope_table(), tm=PRE_TILE)

    kmax = (KEY_NORM_MARGIN * MLA_QK ** 0.5) * jnp.max(jnp.abs(g_kn_mla), keepdims=True)
    om, ow, w_gu_bf, w_down_bf = _attention(kmax, sink, qm, km, vm, qw, kw, vw, w_exp_gu,
                                            w_exp_down, B, S, tq=MLA_Q_TILE)
    om = om.reshape(B * S, -1)
    ow = ow.reshape(B * S, -1)

    x1s, hp, comb_t = _post_attention(
        om, ow, gates, x2d, w_br_mla.astype(BF16), w_br_swa.astype(BF16), w_out.astype(BF16),
        g_moe[None], w_router.T, router_bias[:, None], w_sh_gu.astype(BF16),
        w_sh_down.astype(BF16), tm=ROW_TILE)

    slots, wts, offs, cnts = _moe_plan(comb_t)
    routed = _moe_sparse(offs[:, :, 0], cnts[:, :, 0], slots, wts, hp, w_gu_bf, w_down_bf)

    return _ple(x1s, routed, p2d, g_ple[None], w_ple_gate.astype(BF16), b_ple[None],
                w_ple_proj.astype(BF16), tm=ROW_TILE)


def kernel(x, p, positions, g_mix, w_in, b_gate, g_cq, w_uq, g_ckv, w_ukv, g_qn_mla, g_kn_mla, g_qn_swa, g_kn_swa, sink, w_br_mla, w_br_swa, w_out, g_moe, w_router, router_bias, w_exp_gu, w_exp_down, w_sh_gu, w_sh_down, g_ple, w_ple_gate, b_ple, w_ple_proj):
    B, S, D = x.shape
    x2d = x.reshape(B * S, D)
    pos2d = positions.reshape(B * S, 1)
    for i in range(p.shape[0]):
        x2d = _layer(x2d, p[i].reshape(B * S, -1), pos2d, B, S, g_mix[i], w_in[i], b_gate[i],
                     g_cq[i], w_uq[i], g_ckv[i], w_ukv[i], g_qn_mla[i], g_kn_mla[i], g_qn_swa[i],
                     g_kn_swa[i], sink[i], w_br_mla[i], w_br_swa[i], w_out[i], g_moe[i],
                     w_router[i], router_bias[i], w_exp_gu[i], w_exp_down[i], w_sh_gu[i],
                     w_sh_down[i], g_ple[i], w_ple_gate[i], b_ple[i], w_ple_proj[i])
    return x2d.reshape(B, S, D)
```

```python
import functools

import jax
import jax.numpy as jnp
import numpy as np
from jax import lax
from jax.experimental import pallas as pl
from jax.experimental.pallas import tpu as pltpu

D_MODEL = 1024
PLE_DIM = 256
ROPE_THETA = 10000.0
EPS = 1e-6
NEG_INF = -1e30

MLA_HEADS = 8
MLA_Q_RANK = 384
MLA_KV_RANK = 256
MLA_NOPE = 64
MLA_ROPE = 32
MLA_QK = MLA_NOPE + MLA_ROPE
MLA_V = 64

SWA_HEADS = 8
SWA_KV_HEADS = 2
SWA_GROUP = SWA_HEADS // SWA_KV_HEADS
SWA_HD = 64
WINDOW = 128

OFF_CQ = MLA_Q_RANK
OFF_CKV = OFF_CQ + MLA_KV_RANK
OFF_KR = OFF_CKV + MLA_ROPE
OFF_QS = OFF_KR + SWA_HEADS * SWA_HD
OFF_KS = OFF_QS + SWA_KV_HEADS * SWA_HD
OFF_VS = OFF_KS + SWA_KV_HEADS * SWA_HD
OFF_GA = OFF_VS + D_MODEL

N_EXPERTS = 64
TOP_K = 8
N_GROUPS = 8
TOPK_GROUPS = 4
EXPERTS_PER_GROUP = N_EXPERTS // N_GROUPS
EXPERT_FF = 256
SHARED_FF = 256
ROUTED_SCALE = 2.5

LANES = 128
ROW_WORDS = D_MODEL // 2
ROW_SUB = ROW_WORDS // LANES
MOE_CHUNK = 2048
MOE_TILE = 128
MOE_TILE_BIG = 288
SLOT_ALIGN = 16
COMBINE_BLOCK = 256
COMBINE_UNROLL = 16
EXPERTS_PER_STEP = 4
EXPERT_STEPS = N_EXPERTS // EXPERTS_PER_STEP
EXPERTS_IN_FLIGHT = 2
PRE_TILE = 256
ROW_TILE = 1024
MLA_Q_TILE = 1024
MLA_KV_TILE = 256
LOG2E = 1.4426950408889634
SOFTMAX_UNDERFLOW_GUARD = 1e-30
KEY_NORM_MARGIN = 1.01
SWA_Q_TILE = 1024
SWA_BLOCK = 128
VMEM_LIMIT = 60 * 1024 * 1024

BF16 = jnp.bfloat16
F32 = jnp.float32

C_CQ = 0
C_CKV = C_CQ + MLA_Q_RANK
C_KR = C_CKV + MLA_KV_RANK
C_QS = C_KR + LANES
C_KS = C_QS + SWA_HEADS * LANES
C_VS = C_KS + SWA_KV_HEADS * LANES
C_GA = C_VS + SWA_KV_HEADS * LANES
C_END = C_GA + 2 * D_MODEL


def _full(shape):
    nd = len(shape)
    return pl.BlockSpec(shape, lambda *_: (0,) * nd)


def _dot(a, b):
    return jnp.dot(a, b, preferred_element_type=F32)


def _dot_nt(a, b, precision=None):
    return lax.dot_general(a, b, (((1,), (1,)), ((), ())), precision=precision,
                           preferred_element_type=F32)


def _rms(v, n):
    return v * lax.rsqrt(jnp.sum(v * v, axis=-1, keepdims=True) * (1.0 / n) + EPS)


def _rope(v, cos, sin):
    return v * cos + pltpu.roll(v, LANES // 2, 1) * sin


def _pre_kernel(x_ref, pos_ref, w_all_ref, w_uq_ref, w_k_ref, w_v_ref, g_mix_ref, g_cq_ref,
                g_ckv_ref, gq_m_ref, gk_m_ref, gq_s_ref, gk_s_ref, b_gate_ref, rope_ref,
                qm_ref, km_ref, vm_ref, qw_ref, kw_ref, vw_ref, gate_ref):
    x = x_ref[...]
    h = (_rms(x, D_MODEL) * g_mix_ref[...]).astype(BF16)

    def proj(lo, hi):
        return _dot(h, w_all_ref[:, lo:hi])

    z_lat = proj(C_CQ, C_QS)
    z_qs = proj(C_QS, C_KS)
    half = (C_END - C_GA) // 2
    z_ga = proj(C_GA, C_GA + half)

    pos = pos_ref[...].astype(F32)
    rope = rope_ref[...]
    ang = pos * rope[0:1, :]
    cos_m1 = jnp.cos(ang) - 1.0
    sin = jnp.sin(ang)
    cos_m, sin_m = 1.0 + cos_m1 * rope[2:3, :], sin * rope[1:2, :]
    cos_s, sin_s = 1.0 + cos_m1 * rope[4:5, :], sin * rope[3:4, :]

    cqn = (_rms(z_lat[:, C_CQ:C_CKV], MLA_Q_RANK) * g_cq_ref[...]).astype(BF16)
    ckvn = (_rms(z_lat[:, C_CKV:C_KR], MLA_KV_RANK) * g_ckv_ref[...]).astype(BF16)
    q = _dot(cqn, w_uq_ref[...])
    gq_m = gq_m_ref[...]
    for hd in range(MLA_HEADS):
        qh = _rms(q[:, hd * LANES:(hd + 1) * LANES], MLA_QK) * gq_m
        qh = _rope(qh, cos_m, sin_m) * (MLA_QK ** -0.5 * LOG2E)
        qm_ref[:, hd * LANES:(hd + 1) * LANES] = qh.astype(BF16)

    kn = _dot(ckvn, w_k_ref[...])
    vm_ref[...] = _dot(ckvn, w_v_ref[...]).astype(BF16)
    z_kv = proj(C_KS, C_GA)

    gq_s = gq_s_ref[...]
    for hd in range(SWA_HEADS):
        qh = _rms(z_qs[:, hd * LANES:(hd + 1) * LANES], SWA_HD) * gq_s
        qh = _rope(qh, cos_s, sin_s) * (SWA_HD ** -0.5 * LOG2E)
        qw_ref[:, hd * LANES:(hd + 1) * LANES] = qh.astype(BF16)

    gk_m = gk_m_ref[...]
    kr = z_lat[:, C_KR:C_QS]
    ss_kr = jnp.sum(kr * kr, axis=-1, keepdims=True)
    kr_rot = _rope(kr * gk_m, cos_m, sin_m)
    for hd in range(MLA_HEADS):
        kh = kn[:, hd * LANES:(hd + 1) * LANES]
        ss = jnp.sum(kh * kh, axis=-1, keepdims=True) + ss_kr
        sc = lax.rsqrt(ss * (1.0 / MLA_QK) + EPS)
        km_ref[:, hd * LANES:(hd + 1) * LANES] = ((kh * gk_m + kr_rot) * sc).astype(BF16)

    z_gb = proj(C_GA + half, C_END)

    gk_s = gk_s_ref[...]
    for hd in range(SWA_KV_HEADS):
        kh = _rms(z_kv[:, hd * LANES:(hd + 1) * LANES], SWA_HD) * gk_s
        kh = _rope(kh, cos_s, sin_s)
        kw_ref[:, hd * LANES:(hd + 1) * LANES] = kh.astype(BF16)
    vw_ref[...] = z_kv[:, C_VS - C_KS:].astype(BF16)

    gate_ref[:, :half] = jax.nn.sigmoid(z_ga + b_gate_ref[:, :half]).astype(BF16)
    gate_ref[:, half:] = jax.nn.sigmoid(z_gb + b_gate_ref[:, half:]).astype(BF16)


def _pre_attention(x2d, pos2d, w_all, w_uq, w_k, w_v, g_mix, g_cq, g_ckv, gq_m, gk_m, gq_s,
                   gk_s, b_gate, rope_tab, tm):
    T = x2d.shape[0]
    row = lambda n: pl.BlockSpec((tm, n), lambda i: (i, 0))
    outs = [(MLA_HEADS * LANES, BF16), (MLA_HEADS * LANES, BF16), (MLA_HEADS * MLA_V, BF16),
            (SWA_HEADS * LANES, BF16), (SWA_KV_HEADS * LANES, BF16),
            (SWA_KV_HEADS * LANES, BF16), (2 * D_MODEL, BF16)]
    consts = [w_all, w_uq, w_k, w_v, g_mix, g_cq, g_ckv, gq_m, gk_m, gq_s, gk_s, b_gate, rope_tab]
    return pl.pallas_call(
        _pre_kernel,
        grid=(T // tm,),
        in_specs=[row(D_MODEL), row(1)] + [_full(c.shape) for c in consts],
        out_specs=[row(n) for n, _ in outs],
        out_shape=[jax.ShapeDtypeStruct((T, n), dt) for n, dt in outs],
        compiler_params=pltpu.CompilerParams(dimension_semantics=("arbitrary",),
                                             vmem_limit_bytes=VMEM_LIMIT),
        name="pre_attention",
    )(x2d, pos2d, *consts)


def _half_masks(dtype):
    lane = lax.broadcasted_iota(jnp.int32, (1, LANES), 1)
    lo = (lane < LANES // 2).astype(dtype)
    return lo, 1 - lo


def _mla_exact(q_ref, k_ref, v_ref, o_ref):
    v = v_ref[...]
    masks = _half_masks(v.dtype)
    acc = None
    for hh in range(2):
        q = q_ref[:, hh * LANES:(hh + 1) * LANES]
        k = k_ref[:, hh * LANES:(hh + 1) * LANES]
        s = _dot_nt(q, k)
        m = jnp.max(s, axis=-1, keepdims=True)
        p = jnp.exp2(s - m)
        l = jnp.sum(p, axis=-1, keepdims=True)
        o = _dot(p.astype(BF16), v * masks[hh]) / l
        acc = o if acc is None else acc + o
    o_ref[...] = acc.astype(o_ref.dtype)


def _mla_kernel(kmax_ref, q_ref, k_ref, v_ref, wa_ref, wb_ref, o_ref, wa_out, wb_out, vt_ref):
    @pl.when(pl.program_id(2) == 0)
    def _():
        vt_ref[...] = v_ref[...].astype(F32).T.astype(BF16)

    wa_out[...] = wa_ref[...].astype(BF16)
    wb_out[...] = wb_ref[...].astype(BF16)

    ones = jnp.ones((8, LANES), BF16)
    outs = []
    lmin = None
    for hh in range(2):
        q = q_ref[:, hh * LANES:(hh + 1) * LANES]
        qf = q.astype(F32)
        q_sq = _dot_nt(ones, (qf * qf).astype(BF16))[0:1, :]
        bound = jnp.sqrt(q_sq) * kmax_ref[0]
        p_t = jnp.exp2(_dot_nt(k_ref[:, hh * LANES:(hh + 1) * LANES], q) - bound)
        l = jnp.sum(p_t, axis=0, keepdims=True)
        o_t = _dot(vt_ref[hh * MLA_V:(hh + 1) * MLA_V, :], p_t.astype(BF16))
        outs.append(o_t / l)
        lm = jnp.min(l)
        lmin = lm if lmin is None else jnp.minimum(lmin, lm)
    o_ref[...] = jnp.concatenate(outs, axis=0).T.astype(o_ref.dtype)

    @pl.when(jnp.logical_not(lmin > SOFTMAX_UNDERFLOW_GUARD))
    def _():
        _mla_exact(q_ref, k_ref, v_ref, o_ref)


def _mla_attention(kmax, qm, km, vm, w_a, w_b, B, S, tq):
    pairs = MLA_HEADS // 2
    q3 = qm.reshape(B, S, MLA_HEADS * LANES)
    k3 = km.reshape(B, S, MLA_HEADS * LANES)
    v3 = vm.reshape(B, S, MLA_HEADS * MLA_V)
    steps = B * pairs * (S // tq)
    wa3 = w_a.reshape(steps, -1, w_a.shape[-1])
    wb3 = w_b.reshape(steps, -1, w_b.shape[-1])
    step = lambda b, p, i: ((b * pairs + p) * (S // tq) + i, 0, 0)
    w_spec = lambda w: pl.BlockSpec((None,) + w.shape[1:], step)
    om, wa_bf, wb_bf = pl.pallas_call(
        _mla_kernel,
        grid=(B, pairs, S // tq),
        in_specs=[pl.BlockSpec(memory_space=pltpu.SMEM),
                  pl.BlockSpec((None, tq, 2 * LANES), lambda b, p, i: (b, i, p)),
                  pl.BlockSpec((None, S, 2 * LANES), lambda b, p, i: (b, 0, p)),
                  pl.BlockSpec((None, S, LANES), lambda b, p, i: (b, 0, p)),
                  w_spec(wa3), w_spec(wb3)],
        out_specs=[pl.BlockSpec((None, tq, LANES), lambda b, p, i: (b, i, p)),
                   w_spec(wa3), w_spec(wb3)],
        out_shape=[jax.ShapeDtypeStruct((B, S, MLA_HEADS * MLA_V), BF16),
                   jax.ShapeDtypeStruct(wa3.shape, BF16),
                   jax.ShapeDtypeStruct(wb3.shape, BF16)],
        scratch_shapes=[pltpu.VMEM((LANES, S), BF16)],
        compiler_params=pltpu.CompilerParams(
            dimension_semantics=("arbitrary", "arbitrary", "arbitrary"),
            vmem_limit_bytes=VMEM_LIMIT),
        name="mla_attention",
    )(kmax, q3, k3, v3, wa3, wb3)
    return om, wa_bf.reshape(w_a.shape), wb_bf.reshape(w_b.shape)


def _swa_kernel(sink_ref, q_ref, k_ref, v_ref, o_ref, *, tq, S):
    hk = pl.program_id(1)
    i = pl.program_id(2)
    tk = SWA_BLOCK + 2 * WINDOW
    row = lax.broadcasted_iota(jnp.int32, (SWA_GROUP * SWA_BLOCK, 1), 0)
    qoff = row & (SWA_BLOCK - 1)
    head = row // SWA_BLOCK
    sk = jnp.zeros((SWA_GROUP * SWA_BLOCK, 1), F32)
    for g in range(SWA_GROUP):
        sk = jnp.where(head == g, sink_ref[SWA_GROUP * hk + g] * LOG2E, sk)
    low_half = lax.broadcasted_iota(jnp.int32, (1, LANES), 1) < LANES // 2
    for sub in range(tq // SWA_BLOCK):
        rows = slice(sub * SWA_BLOCK, (sub + 1) * SWA_BLOCK)
        q0 = i * tq + sub * SWA_BLOCK
        kstart = pl.multiple_of(jnp.clip(q0 - WINDOW, 0, S - tk), WINDOW)
        k = k_ref[pl.ds(kstart, tk), :]
        v = v_ref[pl.ds(kstart, tk), :]
        q = jnp.concatenate([q_ref[rows, g * LANES:(g + 1) * LANES] for g in range(SWA_GROUP)], axis=0)
        kpos = kstart + lax.broadcasted_iota(jnp.int32, (1, tk), 1)
        valid = jnp.abs(kpos - (q0 + qoff[:SWA_BLOCK])) <= WINDOW
        s = _dot_nt(q, k).reshape(SWA_GROUP, SWA_BLOCK, tk)
        s = jnp.where(valid[None], s, NEG_INF).reshape(SWA_GROUP * SWA_BLOCK, tk)
        m = jnp.maximum(jnp.max(s, axis=-1, keepdims=True), sk)
        e = jnp.exp2(s - m)
        denom = jnp.sum(e, axis=-1, keepdims=True) + jnp.exp2(sk - m)
        o = _dot(e.astype(BF16), v) / denom
        for j in range(SWA_GROUP // 2):
            even = o[(2 * j) * SWA_BLOCK:(2 * j + 1) * SWA_BLOCK, :]
            odd = o[(2 * j + 1) * SWA_BLOCK:(2 * j + 2) * SWA_BLOCK, :]
            o_ref[rows, j * LANES:(j + 1) * LANES] = jnp.where(low_half, even, odd).astype(o_ref.dtype)


def _swa_attention(qw, kw, vw, sink, B, S, tq):
    q3 = qw.reshape(B, S, SWA_HEADS * LANES)
    k3 = kw.reshape(B, S, SWA_KV_HEADS * LANES)
    v3 = vw.reshape(B, S, SWA_KV_HEADS * LANES)
    return pl.pallas_call(
        functools.partial(_swa_kernel, tq=tq, S=S),
        grid=(B, SWA_KV_HEADS, S // tq),
        in_specs=[pl.BlockSpec(memory_space=pltpu.SMEM),
                  pl.BlockSpec((None, tq, SWA_GROUP * LANES), lambda b, h, i: (b, i, h)),
                  pl.BlockSpec((None, S, LANES), lambda b, h, i: (b, 0, h)),
                  pl.BlockSpec((None, S, LANES), lambda b, h, i: (b, 0, h))],
        out_specs=pl.BlockSpec((None, tq, SWA_GROUP * SWA_HD), lambda b, h, i: (b, i, h)),
        out_shape=jax.ShapeDtypeStruct((B, S, SWA_HEADS * SWA_HD), BF16),
        compiler_params=pltpu.CompilerParams(
            dimension_semantics=("arbitrary", "arbitrary", "arbitrary"),
            vmem_limit_bytes=VMEM_LIMIT),
        name="swa_attention",
    )(sink, q3, k3, v3)


def _beats(vj, vi, j_first):
    return (vj >= vi) if j_first else (vj > vi)


def _route(scores, sel):
    G, P = N_GROUPS, EXPERTS_PER_GROUP
    groups = [sel[g * P:(g + 1) * P, :] for g in range(G)]
    row = lax.broadcasted_iota(jnp.int32, (P, 1), 0)
    gscore = []
    for vg in groups:
        m1 = jnp.max(vg, axis=0, keepdims=True)
        first = jnp.min(jnp.where(vg == m1, row, P), axis=0, keepdims=True)
        m2 = jnp.max(jnp.where(row == first, -jnp.inf, vg), axis=0, keepdims=True)
        gscore.append(m1 + m2)
    masked = []
    for g in range(G):
        rank = jnp.zeros_like(gscore[g], dtype=jnp.int32)
        for g2 in range(G):
            if g2 != g:
                rank = rank + _beats(gscore[g2], gscore[g], g2 < g).astype(jnp.int32)
        masked.append(jnp.where(rank < TOPK_GROUPS, groups[g], NEG_INF))
    index = [row + g * P for g in range(G)]
    chosen = [None] * G
    for _ in range(TOP_K):
        best = masked[0]
        for g in range(1, G):
            best = jnp.maximum(best, masked[g])
        best = jnp.max(best, axis=0, keepdims=True)
        first = jnp.where(masked[0] == best, index[0], N_EXPERTS)
        for g in range(1, G):
            first = jnp.minimum(first, jnp.where(masked[g] == best, index[g], N_EXPERTS))
        first = jnp.min(first, axis=0, keepdims=True)
        for g in range(G):
            hit = index[g] == first
            chosen[g] = hit if chosen[g] is None else (chosen[g] | hit)
            masked[g] = jnp.where(hit, -jnp.inf, masked[g])
    picked = [jnp.where(chosen[g], scores[g * P:(g + 1) * P, :], 0.0) for g in range(G)]
    total = picked[0]
    for g in range(1, G):
        total = total + picked[g]
    denom = jnp.sum(total, axis=0, keepdims=True)
    return [pk / denom * ROUTED_SCALE for pk in picked]


def _pack_pair(lo, hi):
    return pltpu.pack_elementwise([lo, hi], packed_dtype=BF16)


def _unpack_pair(word):
    lo = pltpu.unpack_elementwise(word, index=0, packed_dtype=BF16, unpacked_dtype=F32)
    hi = pltpu.unpack_elementwise(word, index=1, packed_dtype=BF16, unpacked_dtype=F32)
    return lo, hi


def _store_rows_dense(ref, words):
    for j in range(ROW_SUB):
        ref[pl.ds(j, words.shape[0], stride=ROW_SUB), :] = words[:, j * LANES:(j + 1) * LANES]


def _load_rows_dense(ref, rows):
    sub = ROW_SUB
    return jnp.concatenate([ref[pl.ds(j, rows, stride=sub), :] for j in range(sub)], axis=1)


def _post_kernel(om_ref, ow_ref, gate_ref, x_ref, wbm_ref, wbw_ref, wout_ref, g_moe_ref, wr_ref,
                 rb_ref, wsgu_ref, wsd_ref, x1_ref, hp_ref, comb_ref):
    am = _dot(om_ref[...], wbm_ref[...])
    aw = _dot(ow_ref[...], wbw_ref[...])
    gates = gate_ref[...].astype(F32)
    merged = gates[:, :D_MODEL] * am + gates[:, D_MODEL:] * aw
    x1 = x_ref[...] + _dot(merged.astype(BF16), wout_ref[...])

    h2 = _rms(x1, D_MODEL) * g_moe_ref[...]
    h2b = h2.astype(BF16)
    _store_rows_dense(hp_ref, _pack_pair(h2[:, :ROW_WORDS], h2[:, ROW_WORDS:]))

    wr = wr_ref[...]
    w_hi = wr.astype(BF16)
    w_lo = (wr - w_hi.astype(F32)).astype(BF16)
    h_lo = (h2 - h2b.astype(F32)).astype(BF16)
    by_hi = _dot_nt(jnp.concatenate([w_hi, w_lo], axis=0), h2b)
    logits = by_hi[:N_EXPERTS] + by_hi[N_EXPERTS:] + _dot_nt(w_hi, h_lo)

    sgu = _dot(h2b, wsgu_ref[...])
    sh = jax.nn.silu(sgu[:, :SHARED_FF]) * sgu[:, SHARED_FF:]
    x1_ref[...] = x1 + _dot(sh.astype(BF16), wsd_ref[...])

    scores = jax.nn.sigmoid(logits)
    comb = _route(scores, scores + rb_ref[...])
    for g in range(N_GROUPS):
        comb_ref[g * EXPERTS_PER_GROUP:(g + 1) * EXPERTS_PER_GROUP, :] = comb[g]


def _post_attention(om, ow, gates, x2d, wbm, wbw, wout, g_moe, wr_t, rbias, wsgu, wsd, tm):
    T = x2d.shape[0]
    row = lambda n: pl.BlockSpec((tm, n), lambda i: (i, 0))
    consts = [wbm, wbw, wout, g_moe, wr_t, rbias, wsgu, wsd]
    return pl.pallas_call(
        _post_kernel,
        grid=(T // tm,),
        in_specs=[row(om.shape[1]), row(ow.shape[1]), row(2 * D_MODEL), row(D_MODEL)]
        + [_full(c.shape) for c in consts],
        out_specs=[row(D_MODEL), pl.BlockSpec((tm * ROW_SUB, LANES), lambda i: (i, 0)),
                   pl.BlockSpec((N_EXPERTS, tm), lambda i: (0, i))],
        out_shape=[jax.ShapeDtypeStruct((T, D_MODEL), F32),
                   jax.ShapeDtypeStruct((T * ROW_SUB, LANES), jnp.uint32),
                   jax.ShapeDtypeStruct((N_EXPERTS, T), F32)],
        compiler_params=pltpu.CompilerParams(dimension_semantics=("arbitrary",),
                                             vmem_limit_bytes=VMEM_LIMIT),
        name="post_attention",
    )(om, ow, gates, x2d, *consts)


SLOT_ROWS = -(-(MOE_CHUNK * TOP_K + N_EXPERTS * (SLOT_ALIGN - 1) + MOE_TILE_BIG) // MOE_TILE) * MOE_TILE
DUMMY_SLOT = SLOT_ROWS - 1
PLAN_BLOCK = 256


def _plan_kernel(comb_ref, slot_ref, w_ref, off_ref, cnt_ref):
    comb = comb_ref[...]
    sel = comb > 0.0
    m = sel.astype(F32)
    mb = m.astype(BF16)
    r_i = lax.broadcasted_iota(jnp.int32, (PLAN_BLOCK, PLAN_BLOCK), 0)
    c_i = lax.broadcasted_iota(jnp.int32, (PLAN_BLOCK, PLAN_BLOCK), 1)
    before = (r_i < c_i).astype(BF16)
    carry = jnp.zeros((N_EXPERTS, 1), F32)
    ranks = []
    for b in range(MOE_CHUNK // PLAN_BLOCK):
        blk = slice(b * PLAN_BLOCK, (b + 1) * PLAN_BLOCK)
        ranks.append(_dot(mb[:, blk], before) + carry)
        carry = carry + jnp.sum(m[:, blk], axis=1, keepdims=True)
    rank = jnp.concatenate(ranks, axis=1)
    cnt = carry
    cnt_pad = jnp.floor((cnt + (SLOT_ALIGN - 1)) * (1.0 / SLOT_ALIGN)) * SLOT_ALIGN
    e_r = lax.broadcasted_iota(jnp.int32, (N_EXPERTS, N_EXPERTS), 0)
    e_c = lax.broadcasted_iota(jnp.int32, (N_EXPERTS, N_EXPERTS), 1)
    below = (e_c < e_r).astype(F32)
    off = jnp.dot(below, jnp.broadcast_to(cnt_pad, (N_EXPERTS, LANES)),
                  precision=lax.Precision.HIGHEST, preferred_element_type=F32)
    slot = off[:, :1] + rank
    kidx = _dot(below.astype(BF16), mb)
    row = lax.broadcasted_iota(jnp.int32, (TOP_K, 1), 0)
    slot_acc = jnp.zeros((TOP_K, MOE_CHUNK), F32)
    w_acc = jnp.zeros((TOP_K, MOE_CHUNK), F32)
    for k in range(TOP_K):
        pick = jnp.where(sel & (kidx == k), 1.0, 0.0)
        found = jnp.sum(pick, axis=0, keepdims=True) > 0.0
        s_k = jnp.where(found, jnp.sum(pick * slot, axis=0, keepdims=True), float(DUMMY_SLOT))
        w_k = jnp.sum(pick * comb, axis=0, keepdims=True)
        slot_acc = jnp.where(row == k, s_k, slot_acc)
        w_acc = jnp.where(row == k, w_k, w_acc)
    slot_ref[...] = slot_acc.astype(jnp.int32) * ROW_SUB
    w_ref[...] = w_acc
    off_ref[...] = off.astype(jnp.int32)
    cnt_ref[...] = jnp.broadcast_to(cnt, (N_EXPERTS, LANES)).astype(jnp.int32)


def _moe_plan(comb_t):
    T = comb_t.shape[1]
    nch = T // MOE_CHUNK
    per_pair = pl.BlockSpec((None, TOP_K, MOE_CHUNK), lambda c: (c, 0, 0))
    per_expert = pl.BlockSpec((None, N_EXPERTS, LANES), lambda c: (c, 0, 0))
    return pl.pallas_call(
        _plan_kernel,
        grid=(nch,),
        in_specs=[pl.BlockSpec((N_EXPERTS, MOE_CHUNK), lambda c: (0, c))],
        out_specs=[per_pair, per_pair, per_expert, per_expert],
        out_shape=[jax.ShapeDtypeStruct((nch, TOP_K, MOE_CHUNK), jnp.int32),
                   jax.ShapeDtypeStruct((nch, TOP_K, MOE_CHUNK), F32),
                   jax.ShapeDtypeStruct((nch, N_EXPERTS, LANES), jnp.int32),
                   jax.ShapeDtypeStruct((nch, N_EXPERTS, LANES), jnp.int32)],
        compiler_params=pltpu.CompilerParams(dimension_semantics=("arbitrary",),
                                             vmem_limit_bytes=VMEM_LIMIT),
        name="moe_plan",
    )(comb_t)


def _slab_at(ref, first):
    return ref.at[pl.ds(pl.multiple_of(first, ROW_SUB), ROW_SUB), :]


def _slab(ref, row):
    return _slab_at(ref, row * ROW_SUB)


def _moe_kernel(off_ref, cnt_ref, slot_hbm, w_hbm, hp_ref, wgu_ref, wd_ref, o_ref,
                buf, clo, chi, *smem_and_sem):
    slot_s = smem_and_sem[:TOP_K]
    w_s = smem_and_sem[TOP_K:2 * TOP_K]
    sem = smem_and_sem[2 * TOP_K]
    c = pl.program_id(0)
    s = pl.program_id(1)

    @pl.when(s == 0)
    def _dispatch():
        copies = [pltpu.make_async_copy(slot_hbm.at[c, k], slot_s[k], sem.at[k]) for k in range(TOP_K)]
        copies += [pltpu.make_async_copy(w_hbm.at[c, k], w_s[k], sem.at[TOP_K + k]) for k in range(TOP_K)]
        for cp in copies:
            cp.start()

        @pl.when(c == 0)
        def _():
            buf[...] = jnp.zeros_like(buf)

        for cp in copies:
            cp.wait()

        def scatter(tb, carry):
            for tt in range(8):
                t = tb * 8 + tt
                slab = _slab(hp_ref, t)[...]
                for k in range(TOP_K):
                    _slab_at(buf, slot_s[k][t])[...] = slab
            return carry

        lax.fori_loop(0, MOE_CHUNK // 8, scatter, 0)

    def tile_load(ee, start, rows):
        e = s * EXPERTS_PER_STEP + ee
        first = (off_ref[c, e] + start) * ROW_SUB
        view = buf.at[pl.ds(pl.multiple_of(first, SLOT_ALIGN * ROW_SUB), rows * ROW_SUB), :]
        x_lo, x_hi = _unpack_pair(_load_rows_dense(view, rows))
        return view, x_lo, x_hi, cnt_ref[c, e] - start

    def tile_ffn(ee, x_lo, x_hi):
        gu = (_dot(x_lo.astype(BF16), wgu_ref[ee, :ROW_WORDS, :])
              + _dot(x_hi.astype(BF16), wgu_ref[ee, ROW_WORDS:, :]))
        hid = jax.nn.silu(gu[:, :EXPERT_FF]) * gu[:, EXPERT_FF:]
        return _dot(hid.astype(BF16), wd_ref[ee])

    def tile_store(view, x_lo, x_hi, left, y):
        mine = lax.broadcasted_iota(jnp.int32, (y.shape[0], 1), 0) < left
        _store_rows_dense(view, _pack_pair(jnp.where(mine, y[:, :ROW_WORDS], x_lo),
                                           jnp.where(mine, y[:, ROW_WORDS:], x_hi)))

    def tiles(jobs):
        loaded = [tile_load(ee, start, rows) for ee, start, rows in jobs]
        ys = [tile_ffn(ee, x_lo, x_hi) for (ee, _, _), (_, x_lo, x_hi, _) in zip(jobs, loaded)]
        for (view, x_lo, x_hi, left), y in zip(loaded, ys):
            tile_store(view, x_lo, x_hi, left, y)

    def more_tiles(ee):
        n = cnt_ref[c, s * EXPERTS_PER_STEP + ee]
        n_big = jnp.maximum((n + MOE_TILE_BIG - MOE_TILE - 1) // MOE_TILE_BIG, 1)

        def big(r, carry):
            tiles([(ee, r * MOE_TILE_BIG, MOE_TILE_BIG)])
            return carry

        lax.fori_loop(1, n_big, big, 0)

        @pl.when(n > n_big * MOE_TILE_BIG)
        def _():
            tiles([(ee, n_big * MOE_TILE_BIG, MOE_TILE)])

    @pl.when(s < EXPERT_STEPS)
    def _experts():
        for first in range(0, EXPERTS_PER_STEP, EXPERTS_IN_FLIGHT):
            tiles([(ee, 0, MOE_TILE_BIG) for ee in range(first, first + EXPERTS_IN_FLIGHT)])
        for ee in range(EXPERTS_PER_STEP):
            more_tiles(ee)

    @pl.when(s >= EXPERT_STEPS)
    def _combine():
        t0 = (s - EXPERT_STEPS) * COMBINE_BLOCK

        def gather(i, carry):
            for tt in range(COMBINE_UNROLL):
                tl = i * COMBINE_UNROLL + tt
                t = t0 + tl
                lo, hi = _unpack_pair(_slab_at(buf, slot_s[0][t])[...])
                acc_lo = w_s[0][t] * lo
                acc_hi = w_s[0][t] * hi
                for k in range(1, TOP_K):
                    lo, hi = _unpack_pair(_slab_at(buf, slot_s[k][t])[...])
                    wk = w_s[k][t]
                    acc_lo = acc_lo + wk * lo
                    acc_hi = acc_hi + wk * hi
                _slab(clo, tl)[...] = acc_lo
                _slab(chi, tl)[...] = acc_hi
            return carry

        lax.fori_loop(0, COMBINE_BLOCK // COMBINE_UNROLL, gather, 0)
        o_ref[:, :ROW_WORDS] = _load_rows_dense(clo, COMBINE_BLOCK)
        o_ref[:, ROW_WORDS:] = _load_rows_dense(chi, COMBINE_BLOCK)


def _moe_sparse(offs, cnts, slots, wts, hp, w_gu, w_d):
    nch = offs.shape[0]
    T = nch * MOE_CHUNK
    blocks = MOE_CHUNK // COMBINE_BLOCK
    expert = lambda c, s, *_: (jnp.minimum(s, EXPERT_STEPS - 1), 0, 0)
    grid_spec = pltpu.PrefetchScalarGridSpec(
        num_scalar_prefetch=2,
        grid=(nch, EXPERT_STEPS + blocks),
        in_specs=[pl.BlockSpec(memory_space=pl.ANY),
                  pl.BlockSpec(memory_space=pl.ANY),
                  pl.BlockSpec((MOE_CHUNK * ROW_SUB, LANES), lambda c, s, *_: (c, 0),
                               pipeline_mode=pl.Buffered(1)),
                  pl.BlockSpec((EXPERTS_PER_STEP, D_MODEL, 2 * EXPERT_FF), expert),
                  pl.BlockSpec((EXPERTS_PER_STEP, EXPERT_FF, D_MODEL), expert)],
        out_specs=pl.BlockSpec(
            (COMBINE_BLOCK, D_MODEL),
            lambda c, s, *_: (c * blocks + jnp.maximum(s - EXPERT_STEPS, 0), 0)),
        scratch_shapes=[pltpu.VMEM((SLOT_ROWS * ROW_SUB, LANES), jnp.uint32),
                        pltpu.VMEM((COMBINE_BLOCK * ROW_SUB, LANES), F32),
                        pltpu.VMEM((COMBINE_BLOCK * ROW_SUB, LANES), F32),
                        *[pltpu.SMEM((MOE_CHUNK,), jnp.int32) for _ in range(TOP_K)],
                        *[pltpu.SMEM((MOE_CHUNK,), F32) for _ in range(TOP_K)],
                        pltpu.SemaphoreType.DMA((2 * TOP_K,))])
    return pl.pallas_call(
        _moe_kernel,
        grid_spec=grid_spec,
        out_shape=jax.ShapeDtypeStruct((T, D_MODEL), F32),
        compiler_params=pltpu.CompilerParams(dimension_semantics=("arbitrary", "arbitrary"),
                                             vmem_limit_bytes=VMEM_LIMIT),
        name="moe_experts",
    )(offs, cnts, slots, wts, hp, w_gu, w_d)


def _ple_kernel(x1_ref, r_ref, p_ref, g_ref, wg_ref, b_ref, wp_ref, o_ref):
    x2 = x1_ref[...] + r_ref[...]
    hn = (_rms(x2, D_MODEL) * g_ref[...]).astype(BF16)
    gate = jax.nn.sigmoid(_dot(hn, wg_ref[...]) + b_ref[...])
    o_ref[...] = x2 + gate * _dot(p_ref[...].astype(BF16), wp_ref[...])


def _ple(x1s, routed, p2d, g_ple, wg, b_ple, wp, tm):
    T = x1s.shape[0]
    row = lambda n: pl.BlockSpec((tm, n), lambda i: (i, 0))
    consts = [g_ple, wg, b_ple, wp]
    return pl.pallas_call(
        _ple_kernel,
        grid=(T // tm,),
        in_specs=[row(D_MODEL), row(D_MODEL), row(PLE_DIM)] + [_full(c.shape) for c in consts],
        out_specs=row(D_MODEL),
        out_shape=jax.ShapeDtypeStruct((T, D_MODEL), F32),
        compiler_params=pltpu.CompilerParams(dimension_semantics=("arbitrary",),
                                             vmem_limit_bytes=VMEM_LIMIT),
        name="ple",
    )(x1s, routed, p2d, *consts)


def _lane_map(*runs):
    src = np.full((LANES,), -1)
    for lane, dim, n in runs:
        src[lane:lane + n] = np.arange(dim, dim + n)
    return src


_MLA_HALF = MLA_ROPE // 2
_SWA_HALF = SWA_HD // 2
MLA_LANES = _lane_map((0, MLA_NOPE, _MLA_HALF), (_MLA_HALF, 0, LANES // 2 - _MLA_HALF),
                      (LANES // 2, MLA_NOPE + _MLA_HALF, _MLA_HALF),
                      (LANES // 2 + _MLA_HALF, LANES // 2 - _MLA_HALF, MLA_NOPE - LANES // 2 + _MLA_HALF))
MLA_NOPE_LANES = np.where(MLA_LANES < MLA_NOPE, MLA_LANES, -1)
MLA_ROPE_LANES = np.where(MLA_LANES >= MLA_NOPE, MLA_LANES - MLA_NOPE, -1)
SWA_LANES = _lane_map((LANES // 2 - _SWA_HALF, 0, _SWA_HALF), (LANES - _SWA_HALF, _SWA_HALF, _SWA_HALF))


def _spread(w, heads, lane_src):
    k = w.shape[0]
    dim = w.shape[1] // heads
    w = jnp.pad(w.reshape(k, heads, dim), ((0, 0), (0, 0), (0, 1)))
    return w[:, :, np.where(lane_src < 0, dim, lane_src)].reshape(k, heads * LANES)


def _rope_table():
    def inv_freq(dim):
        return 1.0 / (ROPE_THETA ** (jnp.arange(0, dim, 2, dtype=F32) / dim))

    def selector(lane_src, half):
        sel = np.where(lane_src < 0, 0.0, np.where(lane_src < half, -1.0, 1.0))
        return jnp.asarray(sel, F32)

    sel_m = selector(MLA_ROPE_LANES, _MLA_HALF)
    sel_s = selector(SWA_LANES, _SWA_HALF)
    freq_m = _spread(jnp.tile(inv_freq(MLA_ROPE), 2)[None], 1, MLA_ROPE_LANES)[0]
    freq_s = _spread(jnp.tile(inv_freq(SWA_HD), 2)[None], 1, SWA_LANES)[0]
    zero = jnp.zeros((LANES,), F32)
    rows = [freq_m + freq_s, sel_m, jnp.abs(sel_m), sel_s, jnp.abs(sel_s), zero, zero, zero]
    return jnp.stack(rows)


def _layer(x2d, p2d, pos2d, B, S, g_mix, w_in, b_gate, g_cq, w_uq, g_ckv, w_ukv, g_qn_mla, g_kn_mla,
           g_qn_swa, g_kn_swa, sink, w_br_mla, w_br_swa, w_out, g_moe, w_router, router_bias,
           w_exp_gu, w_exp_down, w_sh_gu, w_sh_down, g_ple, w_ple_gate, b_ple, w_ple_proj):
    w_kr = _spread(w_in[:, OFF_CKV:OFF_KR], 1, MLA_ROPE_LANES)
    w_vs = w_in[:, OFF_KS:OFF_VS].reshape(D_MODEL, SWA_KV_HEADS, 1, SWA_HD)
    w_vs = jnp.broadcast_to(w_vs, (D_MODEL, SWA_KV_HEADS, 2, SWA_HD)).reshape(D_MODEL, -1)
    w_all = jnp.concatenate([
        w_in[:, :OFF_CKV], w_kr,
        _spread(w_in[:, OFF_KR:OFF_QS], SWA_HEADS, SWA_LANES),
        _spread(w_in[:, OFF_QS:OFF_KS], SWA_KV_HEADS, SWA_LANES),
        w_vs, w_in[:, OFF_VS:]], axis=1).astype(BF16)
    assert w_all.shape[1] == C_END
    w_uq_p = _spread(w_uq, MLA_HEADS, MLA_LANES).astype(BF16)
    w_ukv3 = w_ukv.reshape(MLA_KV_RANK, MLA_HEADS, MLA_NOPE + MLA_V)
    w_k = _spread(w_ukv3[:, :, :MLA_NOPE].reshape(MLA_KV_RANK, -1), MLA_HEADS, MLA_NOPE_LANES).astype(BF16)
    w_v = w_ukv3[:, :, MLA_NOPE:].reshape(MLA_KV_RANK, -1).astype(BF16)

    qm, km, vm, qw, kw, vw, gates = _pre_attention(
        x2d, pos2d, w_all, w_uq_p, w_k, w_v, g_mix[None], g_cq[None], g_ckv[None],
        _spread(g_qn_mla[None], 1, MLA_LANES), _spread(g_kn_mla[None], 1, MLA_LANES),
        _spread(g_qn_swa[None], 1, SWA_LANES), _spread(g_kn_swa[None], 1, SWA_LANES),
        b_gate[None], _rope_table(), tm=PRE_TILE)

    kmax = (KEY_NORM_MARGIN * MLA_QK ** 0.5) * jnp.max(jnp.abs(g_kn_mla), keepdims=True)
    om, w_gu_bf, w_down_bf = _mla_attention(kmax, qm, km, vm, w_exp_gu, w_exp_down, B, S,
                                            tq=MLA_Q_TILE)
    om = om.reshape(B * S, -1)
    ow = _swa_attention(qw, kw, vw, sink, B, S, tq=SWA_Q_TILE).reshape(B * S, -1)

    x1s, hp, comb_t = _post_attention(
        om, ow, gates, x2d, w_br_mla.astype(BF16), w_br_swa.astype(BF16), w_out.astype(BF16),
        g_moe[None], w_router.T, router_bias[:, None], w_sh_gu.astype(BF16),
        w_sh_down.astype(BF16), tm=ROW_TILE)

    slots, wts, offs, cnts = _moe_plan(comb_t)
    routed = _moe_sparse(offs[:, :, 0], cnts[:, :, 0], slots, wts, hp, w_gu_bf, w_down_bf)

    return _ple(x1s, routed, p2d, g_ple[None], w_ple_gate.astype(BF16), b_ple[None],
                w_ple_proj.astype(BF16), tm=ROW_TILE)


def kernel(x, p, positions, g_mix, w_in, b_gate, g_cq, w_uq, g_ckv, w_ukv, g_qn_mla, g_kn_mla, g_qn_swa, g_kn_swa, sink, w_br_mla, w_br_swa, w_out, g_moe, w_router, router_bias, w_exp_gu, w_exp_down, w_sh_gu, w_sh_down, g_ple, w_ple_gate, b_ple, w_ple_proj):
    B, S, D = x.shape
    x2d = x.reshape(B * S, D)
    pos2d = positions.reshape(B * S, 1)
    for i in range(p.shape[0]):
        x2d = _layer(x2d, p[i].reshape(B * S, -1), pos2d, B, S, g_mix[i], w_in[i], b_gate[i],
                     g_cq[i], w_uq[i], g_ckv[i], w_ukv[i], g_qn_mla[i], g_kn_mla[i], g_qn_swa[i],
                     g_kn_swa[i], sink[i], w_br_mla[i], w_br_swa[i], w_out[i], g_moe[i],
                     w_router[i], router_bias[i], w_exp_gu[i], w_exp_down[i], w_sh_gu[i],
                     w_sh_down[i], g_ple[i], w_ple_gate[i], b_ple[i], w_ple_proj[i])
    return x2d.reshape(B, S, D)
```

```python
import functools

import jax
import jax.numpy as jnp
import numpy as np
from jax import lax
from jax.experimental import pallas as pl
from jax.experimental.pallas import tpu as pltpu

D_MODEL = 1024
PLE_DIM = 256
ROPE_THETA = 10000.0
EPS = 1e-6
NEG_INF = -1e30

MLA_HEADS = 8
MLA_Q_RANK = 384
MLA_KV_RANK = 256
MLA_NOPE = 64
MLA_ROPE = 32
MLA_QK = MLA_NOPE + MLA_ROPE
MLA_V = 64

SWA_HEADS = 8
SWA_KV_HEADS = 2
SWA_GROUP = SWA_HEADS // SWA_KV_HEADS
SWA_HD = 64
WINDOW = 128

OFF_CQ = MLA_Q_RANK
OFF_CKV = OFF_CQ + MLA_KV_RANK
OFF_KR = OFF_CKV + MLA_ROPE
OFF_QS = OFF_KR + SWA_HEADS * SWA_HD
OFF_KS = OFF_QS + SWA_KV_HEADS * SWA_HD
OFF_VS = OFF_KS + SWA_KV_HEADS * SWA_HD

N_EXPERTS = 64
TOP_K = 8
N_GROUPS = 8
TOPK_GROUPS = 4
EXPERTS_PER_GROUP = N_EXPERTS // N_GROUPS
EXPERT_FF = 256
SHARED_FF = 256
ROUTED_SCALE = 2.5

LANES = 128
ROW_WORDS = D_MODEL // 2
ROW_SUB = ROW_WORDS // LANES
MOE_CHUNK = 2048
MOE_TILE = 128
MOE_TILE_BIG = 320
SLOT_ALIGN = 16
COMBINE_BLOCK = 256
COMBINE_UNROLL = 16
EXPERTS_PER_STEP = 4
EXPERT_STEPS = N_EXPERTS // EXPERTS_PER_STEP
EXPERTS_IN_FLIGHT = 2
PRE_TILE = 256
ROW_TILE = 1024
MLA_Q_TILE = 1024
LOG2E = 1.4426950408889634
SOFTMAX_UNDERFLOW_GUARD = 1e-30
KEY_NORM_MARGIN = 1.01
SWA_Q_TILE = 1024
SWA_BLOCK = 128
VMEM_LIMIT = 60 * 1024 * 1024

BF16 = jnp.bfloat16
F32 = jnp.float32

C_CQ = 0
C_CKV = C_CQ + MLA_Q_RANK
C_KR = C_CKV + MLA_KV_RANK
C_QS = C_KR + LANES
C_KS = C_QS + SWA_HEADS * LANES
C_VS = C_KS + SWA_KV_HEADS * LANES
C_GA = C_VS + SWA_KV_HEADS * LANES
C_END = C_GA + 2 * D_MODEL


def _full(shape):
    nd = len(shape)
    return pl.BlockSpec(shape, lambda *_: (0,) * nd)


def _dot(a, b):
    return jnp.dot(a, b, preferred_element_type=F32)


def _dot_nt(a, b, precision=None):
    return lax.dot_general(a, b, (((1,), (1,)), ((), ())), precision=precision,
                           preferred_element_type=F32)


def _rms(v, n):
    return v * lax.rsqrt(jnp.sum(v * v, axis=-1, keepdims=True) * (1.0 / n) + EPS)


def _rope(v, cos, sin):
    return v * cos + pltpu.roll(v, LANES // 2, 1) * sin


def _pre_kernel(x_ref, pos_ref, w_all_ref, w_uq_ref, w_k_ref, w_v_ref, g_mix_ref, g_cq_ref,
                g_ckv_ref, gq_m_ref, gk_m_ref, gq_s_ref, gk_s_ref, b_gate_ref, rope_ref,
                qm_ref, km_ref, vm_ref, qw_ref, kw_ref, vw_ref, gate_ref):
    x = x_ref[...]
    h = (_rms(x, D_MODEL) * g_mix_ref[...]).astype(BF16)

    def proj(lo, hi):
        return _dot(h, w_all_ref[:, lo:hi])

    z_lat = proj(C_CQ, C_QS)
    z_qs = proj(C_QS, C_KS)
    half = (C_END - C_GA) // 2
    z_ga = proj(C_GA, C_GA + half)

    pos = pos_ref[...].astype(F32)
    rope = rope_ref[...]
    ang = pos * rope[0:1, :]
    cos_m1 = jnp.cos(ang) - 1.0
    sin = jnp.sin(ang)
    cos_m, sin_m = 1.0 + cos_m1 * rope[2:3, :], sin * rope[1:2, :]
    cos_s, sin_s = 1.0 + cos_m1 * rope[4:5, :], sin * rope[3:4, :]

    cqn = (_rms(z_lat[:, C_CQ:C_CKV], MLA_Q_RANK) * g_cq_ref[...]).astype(BF16)
    ckvn = (_rms(z_lat[:, C_CKV:C_KR], MLA_KV_RANK) * g_ckv_ref[...]).astype(BF16)
    q = _dot(cqn, w_uq_ref[...])
    gq_m = gq_m_ref[...]
    for hd in range(MLA_HEADS):
        qh = _rms(q[:, hd * LANES:(hd + 1) * LANES], MLA_QK) * gq_m
        qh = _rope(qh, cos_m, sin_m) * (MLA_QK ** -0.5 * LOG2E)
        qm_ref[:, hd * LANES:(hd + 1) * LANES] = qh.astype(BF16)

    kn = _dot(ckvn, w_k_ref[...])
    vm_ref[...] = _dot(ckvn, w_v_ref[...]).astype(BF16)
    z_kv = proj(C_KS, C_GA)

    gq_s = gq_s_ref[...]
    for hd in range(SWA_HEADS):
        qh = _rms(z_qs[:, hd * LANES:(hd + 1) * LANES], SWA_HD) * gq_s
        qh = _rope(qh, cos_s, sin_s) * (SWA_HD ** -0.5 * LOG2E)
        qw_ref[:, hd * LANES:(hd + 1) * LANES] = qh.astype(BF16)

    gk_m = gk_m_ref[...]
    kr = z_lat[:, C_KR:C_QS]
    ss_kr = jnp.sum(kr * kr, axis=-1, keepdims=True)
    kr_rot = _rope(kr * gk_m, cos_m, sin_m)
    for hd in range(MLA_HEADS):
        kh = kn[:, hd * LANES:(hd + 1) * LANES]
        ss = jnp.sum(kh * kh, axis=-1, keepdims=True) + ss_kr
        sc = lax.rsqrt(ss * (1.0 / MLA_QK) + EPS)
        km_ref[:, hd * LANES:(hd + 1) * LANES] = ((kh * gk_m + kr_rot) * sc).astype(BF16)

    z_gb = proj(C_GA + half, C_END)

    gk_s = gk_s_ref[...]
    for hd in range(SWA_KV_HEADS):
        kh = _rms(z_kv[:, hd * LANES:(hd + 1) * LANES], SWA_HD) * gk_s
        kh = _rope(kh, cos_s, sin_s)
        kw_ref[:, hd * LANES:(hd + 1) * LANES] = kh.astype(BF16)
    vw_ref[...] = z_kv[:, C_VS - C_KS:].astype(BF16)

    gate_ref[:, :half] = jax.nn.sigmoid(z_ga + b_gate_ref[:, :half]).astype(BF16)
    gate_ref[:, half:] = jax.nn.sigmoid(z_gb + b_gate_ref[:, half:]).astype(BF16)


def _pre_attention(x2d, pos2d, w_all, w_uq, w_k, w_v, g_mix, g_cq, g_ckv, gq_m, gk_m, gq_s,
                   gk_s, b_gate, rope_tab, tm):
    T = x2d.shape[0]
    row = lambda n: pl.BlockSpec((tm, n), lambda i: (i, 0))
    outs = [(MLA_HEADS * LANES, BF16), (MLA_HEADS * LANES, BF16), (MLA_HEADS * MLA_V, BF16),
            (SWA_HEADS * LANES, BF16), (SWA_KV_HEADS * LANES, BF16),
            (SWA_KV_HEADS * LANES, BF16), (2 * D_MODEL, BF16)]
    consts = [w_all, w_uq, w_k, w_v, g_mix, g_cq, g_ckv, gq_m, gk_m, gq_s, gk_s, b_gate, rope_tab]
    return pl.pallas_call(
        _pre_kernel,
        grid=(T // tm,),
        in_specs=[row(D_MODEL), row(1)] + [_full(c.shape) for c in consts],
        out_specs=[row(n) for n, _ in outs],
        out_shape=[jax.ShapeDtypeStruct((T, n), dt) for n, dt in outs],
        compiler_params=pltpu.CompilerParams(dimension_semantics=("arbitrary",),
                                             vmem_limit_bytes=VMEM_LIMIT),
        name="pre_attention",
    )(x2d, pos2d, *consts)


def _half_masks(dtype):
    lane = lax.broadcasted_iota(jnp.int32, (1, LANES), 1)
    lo = (lane < LANES // 2).astype(dtype)
    return lo, 1 - lo


def _mla_exact(q_ref, k_ref, v_ref, o_ref):
    v = v_ref[...]
    masks = _half_masks(v.dtype)
    acc = None
    for hh in range(2):
        q = q_ref[:, hh * LANES:(hh + 1) * LANES]
        k = k_ref[:, hh * LANES:(hh + 1) * LANES]
        s = _dot_nt(q, k)
        m = jnp.max(s, axis=-1, keepdims=True)
        p = jnp.exp2(s - m)
        l = jnp.sum(p, axis=-1, keepdims=True)
        o = _dot(p.astype(BF16), v * masks[hh]) / l
        acc = o if acc is None else acc + o
    o_ref[...] = acc.astype(o_ref.dtype)


def _mla_kernel(kmax_ref, q_ref, k_ref, v_ref, wa_ref, wb_ref, o_ref, wa_out, wb_out, vt_ref):
    @pl.when(pl.program_id(2) == 0)
    def _():
        vt_ref[...] = v_ref[...].astype(F32).T.astype(BF16)

    wa_out[...] = wa_ref[...].astype(BF16)
    wb_out[...] = wb_ref[...].astype(BF16)

    ones = jnp.ones((8, LANES), BF16)
    outs = []
    lmin = None
    for hh in range(2):
        q = q_ref[:, hh * LANES:(hh + 1) * LANES]
        qf = q.astype(F32)
        q_sq = _dot_nt(ones, (qf * qf).astype(BF16))[0:1, :]
        bound = jnp.sqrt(q_sq) * kmax_ref[0]
        p_t = jnp.exp2(_dot_nt(k_ref[:, hh * LANES:(hh + 1) * LANES], q) - bound)
        l = jnp.sum(p_t, axis=0, keepdims=True)
        o_t = _dot(vt_ref[hh * MLA_V:(hh + 1) * MLA_V, :], p_t.astype(BF16))
        outs.append(o_t / l)
        lm = jnp.min(l)
        lmin = lm if lmin is None else jnp.minimum(lmin, lm)
    o_ref[...] = jnp.concatenate(outs, axis=0).T.astype(o_ref.dtype)

    @pl.when(jnp.logical_not(lmin > SOFTMAX_UNDERFLOW_GUARD))
    def _():
        _mla_exact(q_ref, k_ref, v_ref, o_ref)


def _mla_attention(kmax, qm, km, vm, w_a, w_b, B, S, tq):
    pairs = MLA_HEADS // 2
    q3 = qm.reshape(B, S, MLA_HEADS * LANES)
    k3 = km.reshape(B, S, MLA_HEADS * LANES)
    v3 = vm.reshape(B, S, MLA_HEADS * MLA_V)
    steps = B * pairs * (S // tq)
    wa3 = w_a.reshape(steps, -1, w_a.shape[-1])
    wb3 = w_b.reshape(steps, -1, w_b.shape[-1])
    step = lambda b, p, i: ((b * pairs + p) * (S // tq) + i, 0, 0)
    w_spec = lambda w: pl.BlockSpec((None,) + w.shape[1:], step)
    om, wa_bf, wb_bf = pl.pallas_call(
        _mla_kernel,
        grid=(B, pairs, S // tq),
        in_specs=[pl.BlockSpec(memory_space=pltpu.SMEM),
                  pl.BlockSpec((None, tq, 2 * LANES), lambda b, p, i: (b, i, p)),
                  pl.BlockSpec((None, S, 2 * LANES), lambda b, p, i: (b, 0, p)),
                  pl.BlockSpec((None, S, LANES), lambda b, p, i: (b, 0, p)),
                  w_spec(wa3), w_spec(wb3)],
        out_specs=[pl.BlockSpec((None, tq, LANES), lambda b, p, i: (b, i, p)),
                   w_spec(wa3), w_spec(wb3)],
        out_shape=[jax.ShapeDtypeStruct((B, S, MLA_HEADS * MLA_V), BF16),
                   jax.ShapeDtypeStruct(wa3.shape, BF16),
                   jax.ShapeDtypeStruct(wb3.shape, BF16)],
        scratch_shapes=[pltpu.VMEM((LANES, S), BF16)],
        compiler_params=pltpu.CompilerParams(
            dimension_semantics=("arbitrary", "arbitrary", "arbitrary"),
            vmem_limit_bytes=VMEM_LIMIT),
        name="mla_attention",
    )(kmax, q3, k3, v3, wa3, wb3)
    return om, wa_bf.reshape(w_a.shape), wb_bf.reshape(w_b.shape)


def _swa_kernel(sink_ref, q_ref, k_ref, v_ref, o_ref, *, tq, S):
    hk = pl.program_id(1)
    i = pl.program_id(2)
    tk = SWA_BLOCK + 2 * WINDOW
    row = lax.broadcasted_iota(jnp.int32, (SWA_GROUP * SWA_BLOCK, 1), 0)
    qoff = row & (SWA_BLOCK - 1)
    head = row // SWA_BLOCK
    sk = jnp.zeros((SWA_GROUP * SWA_BLOCK, 1), F32)
    for g in range(SWA_GROUP):
        sk = jnp.where(head == g, sink_ref[SWA_GROUP * hk + g] * LOG2E, sk)
    low_half = lax.broadcasted_iota(jnp.int32, (1, LANES), 1) < LANES // 2
    for sub in range(tq // SWA_BLOCK):
        rows = slice(sub * SWA_BLOCK, (sub + 1) * SWA_BLOCK)
        q0 = i * tq + sub * SWA_BLOCK
        kstart = pl.multiple_of(jnp.clip(q0 - WINDOW, 0, S - tk), WINDOW)
        k = k_ref[pl.ds(kstart, tk), :]
        v = v_ref[pl.ds(kstart, tk), :]
        q = jnp.concatenate([q_ref[rows, g * LANES:(g + 1) * LANES] for g in range(SWA_GROUP)], axis=0)
        kpos = kstart + lax.broadcasted_iota(jnp.int32, (1, tk), 1)
        valid = jnp.abs(kpos - (q0 + qoff[:SWA_BLOCK])) <= WINDOW
        s = _dot_nt(q, k).reshape(SWA_GROUP, SWA_BLOCK, tk)
        s = jnp.where(valid[None], s, NEG_INF).reshape(SWA_GROUP * SWA_BLOCK, tk)
        m = jnp.maximum(jnp.max(s, axis=-1, keepdims=True), sk)
        e = jnp.exp2(s - m)
        denom = jnp.sum(e, axis=-1, keepdims=True) + jnp.exp2(sk - m)
        o = _dot(e.astype(BF16), v) / denom
        for j in range(SWA_GROUP // 2):
            even = o[(2 * j) * SWA_BLOCK:(2 * j + 1) * SWA_BLOCK, :]
            odd = o[(2 * j + 1) * SWA_BLOCK:(2 * j + 2) * SWA_BLOCK, :]
            o_ref[rows, j * LANES:(j + 1) * LANES] = jnp.where(low_half, even, odd).astype(o_ref.dtype)


def _swa_attention(qw, kw, vw, sink, B, S, tq):
    q3 = qw.reshape(B, S, SWA_HEADS * LANES)
    k3 = kw.reshape(B, S, SWA_KV_HEADS * LANES)
    v3 = vw.reshape(B, S, SWA_KV_HEADS * LANES)
    return pl.pallas_call(
        functools.partial(_swa_kernel, tq=tq, S=S),
        grid=(B, SWA_KV_HEADS, S // tq),
        in_specs=[pl.BlockSpec(memory_space=pltpu.SMEM),
                  pl.BlockSpec((None, tq, SWA_GROUP * LANES), lambda b, h, i: (b, i, h)),
                  pl.BlockSpec((None, S, LANES), lambda b, h, i: (b, 0, h)),
                  pl.BlockSpec((None, S, LANES), lambda b, h, i: (b, 0, h))],
        out_specs=pl.BlockSpec((None, tq, SWA_GROUP * SWA_HD), lambda b, h, i: (b, i, h)),
        out_shape=jax.ShapeDtypeStruct((B, S, SWA_HEADS * SWA_HD), BF16),
        compiler_params=pltpu.CompilerParams(
            dimension_semantics=("arbitrary", "arbitrary", "arbitrary"),
            vmem_limit_bytes=VMEM_LIMIT),
        name="swa_attention",
    )(sink, q3, k3, v3)


def _beats(vj, vi, j_first):
    return (vj >= vi) if j_first else (vj > vi)


def _route(scores, sel):
    G, P = N_GROUPS, EXPERTS_PER_GROUP
    groups = [sel[g * P:(g + 1) * P, :] for g in range(G)]
    row = lax.broadcasted_iota(jnp.int32, (P, 1), 0)
    gscore = []
    for vg in groups:
        m1 = jnp.max(vg, axis=0, keepdims=True)
        first = jnp.min(jnp.where(vg == m1, row, P), axis=0, keepdims=True)
        m2 = jnp.max(jnp.where(row == first, -jnp.inf, vg), axis=0, keepdims=True)
        gscore.append(m1 + m2)
    masked = []
    for g in range(G):
        rank = jnp.zeros_like(gscore[g], dtype=jnp.int32)
        for g2 in range(G):
            if g2 != g:
                rank = rank + _beats(gscore[g2], gscore[g], g2 < g).astype(jnp.int32)
        masked.append(jnp.where(rank < TOPK_GROUPS, groups[g], NEG_INF))
    index = [row + g * P for g in range(G)]
    chosen = [None] * G
    for _ in range(TOP_K):
        best = masked[0]
        for g in range(1, G):
            best = jnp.maximum(best, masked[g])
        best = jnp.max(best, axis=0, keepdims=True)
        first = jnp.where(masked[0] == best, index[0], N_EXPERTS)
        for g in range(1, G):
            first = jnp.minimum(first, jnp.where(masked[g] == best, index[g], N_EXPERTS))
        first = jnp.min(first, axis=0, keepdims=True)
        for g in range(G):
            hit = index[g] == first
            chosen[g] = hit if chosen[g] is None else (chosen[g] | hit)
            masked[g] = jnp.where(hit, -jnp.inf, masked[g])
    picked = [jnp.where(chosen[g], scores[g * P:(g + 1) * P, :], 0.0) for g in range(G)]
    total = picked[0]
    for g in range(1, G):
        total = total + picked[g]
    denom = jnp.sum(total, axis=0, keepdims=True)
    return [pk / denom * ROUTED_SCALE for pk in picked]


def _pack_pair(lo, hi):
    return pltpu.pack_elementwise([lo, hi], packed_dtype=BF16)


def _unpack_pair(word):
    lo = pltpu.unpack_elementwise(word, index=0, packed_dtype=BF16, unpacked_dtype=F32)
    hi = pltpu.unpack_elementwise(word, index=1, packed_dtype=BF16, unpacked_dtype=F32)
    return lo, hi


def _store_rows_dense(ref, words):
    for j in range(ROW_SUB):
        ref[pl.ds(j, words.shape[0], stride=ROW_SUB), :] = words[:, j * LANES:(j + 1) * LANES]


def _load_rows_dense(ref, rows):
    sub = ROW_SUB
    return jnp.concatenate([ref[pl.ds(j, rows, stride=sub), :] for j in range(sub)], axis=1)


def _post_kernel(om_ref, ow_ref, gate_ref, x_ref, wbm_ref, wbw_ref, wout_ref, g_moe_ref, wr_ref,
                 rb_ref, wsgu_ref, wsd_ref, x1_ref, hp_ref, comb_ref):
    am = _dot(om_ref[...], wbm_ref[...])
    aw = _dot(ow_ref[...], wbw_ref[...])
    gates = gate_ref[...].astype(F32)
    merged = gates[:, :D_MODEL] * am + gates[:, D_MODEL:] * aw
    x1 = x_ref[...] + _dot(merged.astype(BF16), wout_ref[...])

    h2 = _rms(x1, D_MODEL) * g_moe_ref[...]
    h2b = h2.astype(BF16)
    _store_rows_dense(hp_ref, _pack_pair(h2[:, :ROW_WORDS], h2[:, ROW_WORDS:]))

    wr = wr_ref[...]
    w_hi = wr.astype(BF16)
    w_lo = (wr - w_hi.astype(F32)).astype(BF16)
    h_lo = (h2 - h2b.astype(F32)).astype(BF16)
    by_hi = _dot_nt(jnp.concatenate([w_hi, w_lo], axis=0), h2b)
    logits = by_hi[:N_EXPERTS] + by_hi[N_EXPERTS:] + _dot_nt(w_hi, h_lo)

    sgu = _dot(h2b, wsgu_ref[...])
    sh = jax.nn.silu(sgu[:, :SHARED_FF]) * sgu[:, SHARED_FF:]
    x1_ref[...] = x1 + _dot(sh.astype(BF16), wsd_ref[...])

    scores = jax.nn.sigmoid(logits)
    comb = _route(scores, scores + rb_ref[...])
    for g in range(N_GROUPS):
        comb_ref[g * EXPERTS_PER_GROUP:(g + 1) * EXPERTS_PER_GROUP, :] = comb[g]


def _post_attention(om, ow, gates, x2d, wbm, wbw, wout, g_moe, wr_t, rbias, wsgu, wsd, tm):
    T = x2d.shape[0]
    row = lambda n: pl.BlockSpec((tm, n), lambda i: (i, 0))
    consts = [wbm, wbw, wout, g_moe, wr_t, rbias, wsgu, wsd]
    return pl.pallas_call(
        _post_kernel,
        grid=(T // tm,),
        in_specs=[row(om.shape[1]), row(ow.shape[1]), row(2 * D_MODEL), row(D_MODEL)]
        + [_full(c.shape) for c in consts],
        out_specs=[row(D_MODEL), pl.BlockSpec((tm * ROW_SUB, LANES), lambda i: (i, 0)),
                   pl.BlockSpec((N_EXPERTS, tm), lambda i: (0, i))],
        out_shape=[jax.ShapeDtypeStruct((T, D_MODEL), F32),
                   jax.ShapeDtypeStruct((T * ROW_SUB, LANES), jnp.uint32),
                   jax.ShapeDtypeStruct((N_EXPERTS, T), F32)],
        compiler_params=pltpu.CompilerParams(dimension_semantics=("arbitrary",),
                                             vmem_limit_bytes=VMEM_LIMIT),
        name="post_attention",
    )(om, ow, gates, x2d, *consts)


SLOT_ROWS = -(-(MOE_CHUNK * TOP_K + N_EXPERTS * (SLOT_ALIGN - 1) + MOE_TILE_BIG) // MOE_TILE) * MOE_TILE
DUMMY_SLOT = SLOT_ROWS - 1
PLAN_BLOCK = 256


def _plan_kernel(comb_ref, slot_ref, w_ref, off_ref, cnt_ref):
    comb = comb_ref[...]
    sel = comb > 0.0
    m = sel.astype(F32)
    mb = m.astype(BF16)
    r_i = lax.broadcasted_iota(jnp.int32, (PLAN_BLOCK, PLAN_BLOCK), 0)
    c_i = lax.broadcasted_iota(jnp.int32, (PLAN_BLOCK, PLAN_BLOCK), 1)
    before = (r_i < c_i).astype(BF16)
    carry = jnp.zeros((N_EXPERTS, 1), F32)
    ranks = []
    for b in range(MOE_CHUNK // PLAN_BLOCK):
        blk = slice(b * PLAN_BLOCK, (b + 1) * PLAN_BLOCK)
        ranks.append(_dot(mb[:, blk], before) + carry)
        carry = carry + jnp.sum(m[:, blk], axis=1, keepdims=True)
    rank = jnp.concatenate(ranks, axis=1)
    cnt = carry
    cnt_pad = jnp.floor((cnt + (SLOT_ALIGN - 1)) * (1.0 / SLOT_ALIGN)) * SLOT_ALIGN
    e_r = lax.broadcasted_iota(jnp.int32, (N_EXPERTS, N_EXPERTS), 0)
    e_c = lax.broadcasted_iota(jnp.int32, (N_EXPERTS, N_EXPERTS), 1)
    below = (e_c < e_r).astype(F32)
    off = jnp.dot(below, jnp.broadcast_to(cnt_pad, (N_EXPERTS, LANES)),
                  precision=lax.Precision.HIGHEST, preferred_element_type=F32)
    slot = off[:, :1] + rank
    kidx = _dot(below.astype(BF16), mb)
    row = lax.broadcasted_iota(jnp.int32, (TOP_K, 1), 0)
    slot_acc = jnp.zeros((TOP_K, MOE_CHUNK), F32)
    w_acc = jnp.zeros((TOP_K, MOE_CHUNK), F32)
    for k in range(TOP_K):
        pick = jnp.where(sel & (kidx == k), 1.0, 0.0)
        found = jnp.sum(pick, axis=0, keepdims=True) > 0.0
        s_k = jnp.where(found, jnp.sum(pick * slot, axis=0, keepdims=True), float(DUMMY_SLOT))
        w_k = jnp.sum(pick * comb, axis=0, keepdims=True)
        slot_acc = jnp.where(row == k, s_k, slot_acc)
        w_acc = jnp.where(row == k, w_k, w_acc)
    slot_ref[...] = slot_acc.astype(jnp.int32) * ROW_SUB
    w_ref[...] = w_acc
    off_ref[...] = off.astype(jnp.int32)
    cnt_ref[...] = jnp.broadcast_to(cnt, (N_EXPERTS, LANES)).astype(jnp.int32)


def _moe_plan(comb_t):
    T = comb_t.shape[1]
    nch = T // MOE_CHUNK
    per_pair = pl.BlockSpec((None, TOP_K, MOE_CHUNK), lambda c: (c, 0, 0))
    per_expert = pl.BlockSpec((None, N_EXPERTS, LANES), lambda c: (c, 0, 0))
    return pl.pallas_call(
        _plan_kernel,
        grid=(nch,),
        in_specs=[pl.BlockSpec((N_EXPERTS, MOE_CHUNK), lambda c: (0, c))],
        out_specs=[per_pair, per_pair, per_expert, per_expert],
        out_shape=[jax.ShapeDtypeStruct((nch, TOP_K, MOE_CHUNK), jnp.int32),
                   jax.ShapeDtypeStruct((nch, TOP_K, MOE_CHUNK), F32),
                   jax.ShapeDtypeStruct((nch, N_EXPERTS, LANES), jnp.int32),
                   jax.ShapeDtypeStruct((nch, N_EXPERTS, LANES), jnp.int32)],
        compiler_params=pltpu.CompilerParams(dimension_semantics=("arbitrary",),
                                             vmem_limit_bytes=VMEM_LIMIT),
        name="moe_plan",
    )(comb_t)


def _slab_at(ref, first):
    return ref.at[pl.ds(pl.multiple_of(first, ROW_SUB), ROW_SUB), :]


def _slab(ref, row):
    return _slab_at(ref, row * ROW_SUB)


def _moe_kernel(off_ref, cnt_ref, slot_hbm, w_hbm, hp_ref, wgu_ref, wd_ref, res_ref, o_ref,
                buf, clo, chi, *smem_and_sem):
    slot_s = smem_and_sem[:TOP_K]
    w_s = smem_and_sem[TOP_K:2 * TOP_K]
    sem = smem_and_sem[2 * TOP_K]
    c = pl.program_id(0)
    s = pl.program_id(1)

    @pl.when(s == 0)
    def _dispatch():
        copies = [pltpu.make_async_copy(slot_hbm.at[c, k], slot_s[k], sem.at[k]) for k in range(TOP_K)]
        copies += [pltpu.make_async_copy(w_hbm.at[c, k], w_s[k], sem.at[TOP_K + k]) for k in range(TOP_K)]
        for cp in copies:
            cp.start()

        @pl.when(c == 0)
        def _():
            buf[...] = jnp.zeros_like(buf)

        for cp in copies:
            cp.wait()

        def scatter(tb, carry):
            for tt in range(8):
                t = tb * 8 + tt
                slab = _slab(hp_ref, t)[...]
                for k in range(TOP_K):
                    _slab_at(buf, slot_s[k][t])[...] = slab
            return carry

        lax.fori_loop(0, MOE_CHUNK // 8, scatter, 0)

    def tile_load(ee, start, rows):
        e = s * EXPERTS_PER_STEP + ee
        first = (off_ref[c, e] + start) * ROW_SUB
        view = buf.at[pl.ds(pl.multiple_of(first, SLOT_ALIGN * ROW_SUB), rows * ROW_SUB), :]
        x_lo, x_hi = _unpack_pair(_load_rows_dense(view, rows))
        return view, x_lo, x_hi, cnt_ref[c, e] - start

    def tile_ffn(ee, x_lo, x_hi):
        gu = (_dot(x_lo.astype(BF16), wgu_ref[ee, :ROW_WORDS, :])
              + _dot(x_hi.astype(BF16), wgu_ref[ee, ROW_WORDS:, :]))
        hid = jax.nn.silu(gu[:, :EXPERT_FF]) * gu[:, EXPERT_FF:]
        return _dot(hid.astype(BF16), wd_ref[ee])

    def tile_store(view, x_lo, x_hi, left, y):
        mine = lax.broadcasted_iota(jnp.int32, (y.shape[0], 1), 0) < left
        _store_rows_dense(view, _pack_pair(jnp.where(mine, y[:, :ROW_WORDS], x_lo),
                                           jnp.where(mine, y[:, ROW_WORDS:], x_hi)))

    def tiles(jobs):
        loaded = [tile_load(ee, start, rows) for ee, start, rows in jobs]
        ys = [tile_ffn(ee, x_lo, x_hi) for (ee, _, _), (_, x_lo, x_hi, _) in zip(jobs, loaded)]
        for (view, x_lo, x_hi, left), y in zip(loaded, ys):
            tile_store(view, x_lo, x_hi, left, y)

    def more_tiles(ee):
        n = cnt_ref[c, s * EXPERTS_PER_STEP + ee]
        n_big = jnp.maximum((n + MOE_TILE_BIG - MOE_TILE - 1) // MOE_TILE_BIG, 1)

        def big(r, carry):
            tiles([(ee, r * MOE_TILE_BIG, MOE_TILE_BIG)])
            return carry

        lax.fori_loop(1, n_big, big, 0)

        @pl.when(n > n_big * MOE_TILE_BIG)
        def _():
            tiles([(ee, n_big * MOE_TILE_BIG, MOE_TILE)])

    @pl.when(s < EXPERT_STEPS)
    def _experts():
        for first in range(0, EXPERTS_PER_STEP, EXPERTS_IN_FLIGHT):
            tiles([(ee, 0, MOE_TILE_BIG) for ee in range(first, first + EXPERTS_IN_FLIGHT)])
        for ee in range(EXPERTS_PER_STEP):
            more_tiles(ee)

    @pl.when(s >= EXPERT_STEPS)
    def _combine():
        t0 = (s - EXPERT_STEPS) * COMBINE_BLOCK

        def gather(i, carry):
            for tt in range(COMBINE_UNROLL):
                tl = i * COMBINE_UNROLL + tt
                t = t0 + tl
                lo, hi = _unpack_pair(_slab_at(buf, slot_s[0][t])[...])
                acc_lo = w_s[0][t] * lo
                acc_hi = w_s[0][t] * hi
                for k in range(1, TOP_K):
                    lo, hi = _unpack_pair(_slab_at(buf, slot_s[k][t])[...])
                    wk = w_s[k][t]
                    acc_lo = acc_lo + wk * lo
                    acc_hi = acc_hi + wk * hi
                _slab(clo, tl)[...] = acc_lo
                _slab(chi, tl)[...] = acc_hi
            return carry

        lax.fori_loop(0, COMBINE_BLOCK // COMBINE_UNROLL, gather, 0)
        o_ref[:, :ROW_WORDS] = res_ref[:, :ROW_WORDS] + _load_rows_dense(clo, COMBINE_BLOCK)
        o_ref[:, ROW_WORDS:] = res_ref[:, ROW_WORDS:] + _load_rows_dense(chi, COMBINE_BLOCK)


def _moe_sparse(offs, cnts, slots, wts, hp, w_gu, w_d, resid):
    nch = offs.shape[0]
    T = nch * MOE_CHUNK
    blocks = MOE_CHUNK // COMBINE_BLOCK
    expert = lambda c, s, *_: (jnp.minimum(s, EXPERT_STEPS - 1), 0, 0)
    token_block = pl.BlockSpec(
        (COMBINE_BLOCK, D_MODEL),
        lambda c, s, *_: (c * blocks + jnp.maximum(s - EXPERT_STEPS, 0), 0))
    grid_spec = pltpu.PrefetchScalarGridSpec(
        num_scalar_prefetch=2,
        grid=(nch, EXPERT_STEPS + blocks),
        in_specs=[pl.BlockSpec(memory_space=pl.ANY),
                  pl.BlockSpec(memory_space=pl.ANY),
                  pl.BlockSpec((MOE_CHUNK * ROW_SUB, LANES), lambda c, s, *_: (c, 0),
                               pipeline_mode=pl.Buffered(1)),
                  pl.BlockSpec((EXPERTS_PER_STEP, D_MODEL, 2 * EXPERT_FF), expert),
                  pl.BlockSpec((EXPERTS_PER_STEP, EXPERT_FF, D_MODEL), expert),
                  token_block],
        out_specs=token_block,
        scratch_shapes=[pltpu.VMEM((SLOT_ROWS * ROW_SUB, LANES), jnp.uint32),
                        pltpu.VMEM((COMBINE_BLOCK * ROW_SUB, LANES), F32),
                        pltpu.VMEM((COMBINE_BLOCK * ROW_SUB, LANES), F32),
                        *[pltpu.SMEM((MOE_CHUNK,), jnp.int32) for _ in range(TOP_K)],
                        *[pltpu.SMEM((MOE_CHUNK,), F32) for _ in range(TOP_K)],
                        pltpu.SemaphoreType.DMA((2 * TOP_K,))])
    return pl.pallas_call(
        _moe_kernel,
        grid_spec=grid_spec,
        out_shape=jax.ShapeDtypeStruct((T, D_MODEL), F32),
        compiler_params=pltpu.CompilerParams(dimension_semantics=("arbitrary", "arbitrary"),
                                             vmem_limit_bytes=VMEM_LIMIT),
        name="moe_experts",
    )(offs, cnts, slots, wts, hp, w_gu, w_d, resid)


def _ple_kernel(x2_ref, p_ref, g_ref, wg_ref, b_ref, wp_ref, o_ref):
    x2 = x2_ref[...]
    hn = (_rms(x2, D_MODEL) * g_ref[...]).astype(BF16)
    gate = jax.nn.sigmoid(_dot(hn, wg_ref[...]) + b_ref[...])
    o_ref[...] = x2 + gate * _dot(p_ref[...].astype(BF16), wp_ref[...])


def _ple(x2, p2d, g_ple, wg, b_ple, wp, tm):
    T = x2.shape[0]
    row = lambda n: pl.BlockSpec((tm, n), lambda i: (i, 0))
    consts = [g_ple, wg, b_ple, wp]
    return pl.pallas_call(
        _ple_kernel,
        grid=(T // tm,),
        in_specs=[row(D_MODEL), row(PLE_DIM)] + [_full(c.shape) for c in consts],
        out_specs=row(D_MODEL),
        out_shape=jax.ShapeDtypeStruct((T, D_MODEL), F32),
        compiler_params=pltpu.CompilerParams(dimension_semantics=("arbitrary",),
                                             vmem_limit_bytes=VMEM_LIMIT),
        name="ple",
    )(x2, p2d, *consts)


def _lane_map(*runs):
    src = np.full((LANES,), -1)
    for lane, dim, n in runs:
        src[lane:lane + n] = np.arange(dim, dim + n)
    return src


_MLA_HALF = MLA_ROPE // 2
_SWA_HALF = SWA_HD // 2
MLA_LANES = _lane_map((0, MLA_NOPE, _MLA_HALF), (_MLA_HALF, 0, LANES // 2 - _MLA_HALF),
                      (LANES // 2, MLA_NOPE + _MLA_HALF, _MLA_HALF),
                      (LANES // 2 + _MLA_HALF, LANES // 2 - _MLA_HALF, MLA_NOPE - LANES // 2 + _MLA_HALF))
MLA_NOPE_LANES = np.where(MLA_LANES < MLA_NOPE, MLA_LANES, -1)
MLA_ROPE_LANES = np.where(MLA_LANES >= MLA_NOPE, MLA_LANES - MLA_NOPE, -1)
SWA_LANES = _lane_map((LANES // 2 - _SWA_HALF, 0, _SWA_HALF), (LANES - _SWA_HALF, _SWA_HALF, _SWA_HALF))


def _spread(w, heads, lane_src):
    k = w.shape[0]
    dim = w.shape[1] // heads
    w = jnp.pad(w.reshape(k, heads, dim), ((0, 0), (0, 0), (0, 1)))
    return w[:, :, np.where(lane_src < 0, dim, lane_src)].reshape(k, heads * LANES)


def _rope_table():
    def inv_freq(dim):
        return 1.0 / (ROPE_THETA ** (jnp.arange(0, dim, 2, dtype=F32) / dim))

    def selector(lane_src, half):
        sel = np.where(lane_src < 0, 0.0, np.where(lane_src < half, -1.0, 1.0))
        return jnp.asarray(sel, F32)

    sel_m = selector(MLA_ROPE_LANES, _MLA_HALF)
    sel_s = selector(SWA_LANES, _SWA_HALF)
    freq_m = _spread(jnp.tile(inv_freq(MLA_ROPE), 2)[None], 1, MLA_ROPE_LANES)[0]
    freq_s = _spread(jnp.tile(inv_freq(SWA_HD), 2)[None], 1, SWA_LANES)[0]
    zero = jnp.zeros((LANES,), F32)
    rows = [freq_m + freq_s, sel_m, jnp.abs(sel_m), sel_s, jnp.abs(sel_s), zero, zero, zero]
    return jnp.stack(rows)


def _layer(x2d, p2d, pos2d, B, S, g_mix, w_in, b_gate, g_cq, w_uq, g_ckv, w_ukv, g_qn_mla, g_kn_mla,
           g_qn_swa, g_kn_swa, sink, w_br_mla, w_br_swa, w_out, g_moe, w_router, router_bias,
           w_exp_gu, w_exp_down, w_sh_gu, w_sh_down, g_ple, w_ple_gate, b_ple, w_ple_proj):
    w_kr = _spread(w_in[:, OFF_CKV:OFF_KR], 1, MLA_ROPE_LANES)
    w_vs = w_in[:, OFF_KS:OFF_VS].reshape(D_MODEL, SWA_KV_HEADS, 1, SWA_HD)
    w_vs = jnp.broadcast_to(w_vs, (D_MODEL, SWA_KV_HEADS, 2, SWA_HD)).reshape(D_MODEL, -1)
    w_all = jnp.concatenate([
        w_in[:, :OFF_CKV], w_kr,
        _spread(w_in[:, OFF_KR:OFF_QS], SWA_HEADS, SWA_LANES),
        _spread(w_in[:, OFF_QS:OFF_KS], SWA_KV_HEADS, SWA_LANES),
        w_vs, w_in[:, OFF_VS:]], axis=1).astype(BF16)
    assert w_all.shape[1] == C_END
    w_uq_p = _spread(w_uq, MLA_HEADS, MLA_LANES).astype(BF16)
    w_ukv3 = w_ukv.reshape(MLA_KV_RANK, MLA_HEADS, MLA_NOPE + MLA_V)
    w_k = _spread(w_ukv3[:, :, :MLA_NOPE].reshape(MLA_KV_RANK, -1), MLA_HEADS, MLA_NOPE_LANES).astype(BF16)
    w_v = w_ukv3[:, :, MLA_NOPE:].reshape(MLA_KV_RANK, -1).astype(BF16)

    qm, km, vm, qw, kw, vw, gates = _pre_attention(
        x2d, pos2d, w_all, w_uq_p, w_k, w_v, g_mix[None], g_cq[None], g_ckv[None],
        _spread(g_qn_mla[None], 1, MLA_LANES), _spread(g_kn_mla[None], 1, MLA_LANES),
        _spread(g_qn_swa[None], 1, SWA_LANES), _spread(g_kn_swa[None], 1, SWA_LANES),
        b_gate[None], _rope_table(), tm=PRE_TILE)

    kmax = (KEY_NORM_MARGIN * MLA_QK ** 0.5) * jnp.max(jnp.abs(g_kn_mla), keepdims=True)
    om, w_gu_bf, w_down_bf = _mla_attention(kmax, qm, km, vm, w_exp_gu, w_exp_down, B, S,
                                            tq=MLA_Q_TILE)
    om = om.reshape(B * S, -1)
    ow = _swa_attention(qw, kw, vw, sink, B, S, tq=SWA_Q_TILE).reshape(B * S, -1)

    x1s, hp, comb_t = _post_attention(
        om, ow, gates, x2d, w_br_mla.astype(BF16), w_br_swa.astype(BF16), w_out.astype(BF16),
        g_moe[None], w_router.T, router_bias[:, None], w_sh_gu.astype(BF16),
        w_sh_down.astype(BF16), tm=ROW_TILE)

    slots, wts, offs, cnts = _moe_plan(comb_t)
    x2 = _moe_sparse(offs[:, :, 0], cnts[:, :, 0], slots, wts, hp, w_gu_bf, w_down_bf, x1s)

    return _ple(x2, p2d, g_ple[None], w_ple_gate.astype(BF16), b_ple[None],
                w_ple_proj.astype(BF16), tm=ROW_TILE)


def kernel(x, p, positions, g_mix, w_in, b_gate, g_cq, w_uq, g_ckv, w_ukv, g_qn_mla, g_kn_mla, g_qn_swa, g_kn_swa, sink, w_br_mla, w_br_swa, w_out, g_moe, w_router, router_bias, w_exp_gu, w_exp_down, w_sh_gu, w_sh_down, g_ple, w_ple_gate, b_ple, w_ple_proj):
    B, S, D = x.shape
    x2d = x.reshape(B * S, D)
    pos2d = positions.reshape(B * S, 1)
    for i in range(p.shape[0]):
        x2d = _layer(x2d, p[i].reshape(B * S, -1), pos2d, B, S, g_mix[i], w_in[i], b_gate[i],
                     g_cq[i], w_uq[i], g_ckv[i], w_ukv[i], g_qn_mla[i], g_kn_mla[i], g_qn_swa[i],
                     g_kn_swa[i], sink[i], w_br_mla[i], w_br_swa[i], w_out[i], g_moe[i],
                     w_router[i], router_bias[i], w_exp_gu[i], w_exp_down[i], w_sh_gu[i],
                     w_sh_down[i], g_ple[i], w_ple_gate[i], b_ple[i], w_ple_proj[i])
    return x2d.reshape(B, S, D)
```

```python
import functools

import jax
import jax.numpy as jnp
import numpy as np
from jax import lax
from jax.experimental import pallas as pl
from jax.experimental.pallas import tpu as pltpu

D_MODEL = 1024
PLE_DIM = 256
ROPE_THETA = 10000.0
EPS = 1e-6
NEG_INF = -1e30

MLA_HEADS = 8
MLA_Q_RANK = 384
MLA_KV_RANK = 256
MLA_NOPE = 64
MLA_ROPE = 32
MLA_QK = MLA_NOPE + MLA_ROPE
MLA_V = 64

SWA_HEADS = 8
SWA_KV_HEADS = 2
SWA_GROUP = SWA_HEADS // SWA_KV_HEADS
SWA_HD = 64
WINDOW = 128

OFF_CQ = MLA_Q_RANK
OFF_CKV = OFF_CQ + MLA_KV_RANK
OFF_KR = OFF_CKV + MLA_ROPE
OFF_QS = OFF_KR + SWA_HEADS * SWA_HD
OFF_KS = OFF_QS + SWA_KV_HEADS * SWA_HD
OFF_VS = OFF_KS + SWA_KV_HEADS * SWA_HD

N_EXPERTS = 64
TOP_K = 8
N_GROUPS = 8
TOPK_GROUPS = 4
EXPERTS_PER_GROUP = N_EXPERTS // N_GROUPS
EXPERT_FF = 256
SHARED_FF = 256
ROUTED_SCALE = 2.5

LANES = 128
ROW_WORDS = D_MODEL // 2
ROW_SUB = ROW_WORDS // LANES
MOE_CHUNK = 2048
MOE_TILE = 128
MOE_TILE_BIG = 320
SLOT_ALIGN = 16
COMBINE_BLOCK = 256
COMBINE_UNROLL = 16
EXPERTS_PER_STEP = 4
EXPERT_STEPS = N_EXPERTS // EXPERTS_PER_STEP
EXPERTS_IN_FLIGHT = 2
PRE_TILE = 256
ROW_TILE = 1024
MLA_Q_TILE = 1024
LOG2E = 1.4426950408889634
SOFTMAX_UNDERFLOW_GUARD = 1e-30
KEY_NORM_MARGIN = 1.01
SWA_Q_TILE = 1024
SWA_BLOCK = 128
VMEM_LIMIT = 60 * 1024 * 1024

BF16 = jnp.bfloat16
F32 = jnp.float32

C_CQ = 0
C_CKV = C_CQ + MLA_Q_RANK
C_KR = C_CKV + MLA_KV_RANK
C_QS = C_KR + LANES
C_KS = C_QS + SWA_HEADS * LANES
C_VS = C_KS + SWA_KV_HEADS * LANES
C_GA = C_VS + SWA_KV_HEADS * LANES
C_END = C_GA + 2 * D_MODEL


def _full(shape):
    nd = len(shape)
    return pl.BlockSpec(shape, lambda *_: (0,) * nd)


def _dot(a, b):
    return jnp.dot(a, b, preferred_element_type=F32)


def _dot_nt(a, b, precision=None):
    return lax.dot_general(a, b, (((1,), (1,)), ((), ())), precision=precision,
                           preferred_element_type=F32)


def _rms(v, n):
    return v * lax.rsqrt(jnp.sum(v * v, axis=-1, keepdims=True) * (1.0 / n) + EPS)


def _rope(v, cos, sin):
    return v * cos + pltpu.roll(v, LANES // 2, 1) * sin


def _pre_kernel(x_ref, pos_ref, w_all_ref, w_uq_ref, w_k_ref, w_v_ref, g_mix_ref, g_cq_ref,
                g_ckv_ref, gq_m_ref, gk_m_ref, gq_s_ref, gk_s_ref, b_gate_ref, rope_ref,
                qm_ref, km_ref, vm_ref, qw_ref, kw_ref, vw_ref, gate_ref):
    x = x_ref[...]
    h = (_rms(x, D_MODEL) * g_mix_ref[...]).astype(BF16)

    def proj(lo, hi):
        return _dot(h, w_all_ref[:, lo:hi])

    z_lat = proj(C_CQ, C_QS)
    z_qs = proj(C_QS, C_KS)
    half = (C_END - C_GA) // 2
    z_ga = proj(C_GA, C_GA + half)

    pos = pos_ref[...].astype(F32)
    rope = rope_ref[...]
    ang = pos * rope[0:1, :]
    cos_m1 = jnp.cos(ang) - 1.0
    sin = jnp.sin(ang)
    cos_m, sin_m = 1.0 + cos_m1 * rope[2:3, :], sin * rope[1:2, :]
    cos_s, sin_s = 1.0 + cos_m1 * rope[4:5, :], sin * rope[3:4, :]

    cqn = (_rms(z_lat[:, C_CQ:C_CKV], MLA_Q_RANK) * g_cq_ref[...]).astype(BF16)
    ckvn = (_rms(z_lat[:, C_CKV:C_KR], MLA_KV_RANK) * g_ckv_ref[...]).astype(BF16)
    q = _dot(cqn, w_uq_ref[...])
    gq_m = gq_m_ref[...]
    for hd in range(MLA_HEADS):
        qh = _rms(q[:, hd * LANES:(hd + 1) * LANES], MLA_QK) * gq_m
        qh = _rope(qh, cos_m, sin_m) * (MLA_QK ** -0.5 * LOG2E)
        qm_ref[:, hd * LANES:(hd + 1) * LANES] = qh.astype(BF16)

    kn = _dot(ckvn, w_k_ref[...])
    vm_ref[...] = _dot(ckvn, w_v_ref[...]).astype(BF16)
    z_kv = proj(C_KS, C_GA)

    gq_s = gq_s_ref[...]
    for hd in range(SWA_HEADS):
        qh = _rms(z_qs[:, hd * LANES:(hd + 1) * LANES], SWA_HD) * gq_s
        qh = _rope(qh, cos_s, sin_s) * (SWA_HD ** -0.5 * LOG2E)
        qw_ref[:, hd * LANES:(hd + 1) * LANES] = qh.astype(BF16)

    gk_m = gk_m_ref[...]
    kr = z_lat[:, C_KR:C_QS]
    ss_kr = jnp.sum(kr * kr, axis=-1, keepdims=True)
    kr_rot = _rope(kr * gk_m, cos_m, sin_m)
    for hd in range(MLA_HEADS):
        kh = kn[:, hd * LANES:(hd + 1) * LANES]
        ss = jnp.sum(kh * kh, axis=-1, keepdims=True) + ss_kr
        sc = lax.rsqrt(ss * (1.0 / MLA_QK) + EPS)
        km_ref[:, hd * LANES:(hd + 1) * LANES] = ((kh * gk_m + kr_rot) * sc).astype(BF16)

    z_gb = proj(C_GA + half, C_END)

    gk_s = gk_s_ref[...]
    for hd in range(SWA_KV_HEADS):
        kh = _rms(z_kv[:, hd * LANES:(hd + 1) * LANES], SWA_HD) * gk_s
        kh = _rope(kh, cos_s, sin_s)
        kw_ref[:, hd * LANES:(hd + 1) * LANES] = kh.astype(BF16)
    vw_ref[...] = z_kv[:, C_VS - C_KS:].astype(BF16)

    gate_ref[:, :half] = jax.nn.sigmoid(z_ga + b_gate_ref[:, :half]).astype(BF16)
    gate_ref[:, half:] = jax.nn.sigmoid(z_gb + b_gate_ref[:, half:]).astype(BF16)


def _pre_attention(x2d, pos2d, w_all, w_uq, w_k, w_v, g_mix, g_cq, g_ckv, gq_m, gk_m, gq_s,
                   gk_s, b_gate, rope_tab, tm):
    T = x2d.shape[0]
    row = lambda n: pl.BlockSpec((tm, n), lambda i: (i, 0))
    outs = [(MLA_HEADS * LANES, BF16), (MLA_HEADS * LANES, BF16), (MLA_HEADS * MLA_V, BF16),
            (SWA_HEADS * LANES, BF16), (SWA_KV_HEADS * LANES, BF16),
            (SWA_KV_HEADS * LANES, BF16), (2 * D_MODEL, BF16)]
    consts = [w_all, w_uq, w_k, w_v, g_mix, g_cq, g_ckv, gq_m, gk_m, gq_s, gk_s, b_gate, rope_tab]
    return pl.pallas_call(
        _pre_kernel,
        grid=(T // tm,),
        in_specs=[row(D_MODEL), row(1)] + [_full(c.shape) for c in consts],
        out_specs=[row(n) for n, _ in outs],
        out_shape=[jax.ShapeDtypeStruct((T, n), dt) for n, dt in outs],
        compiler_params=pltpu.CompilerParams(dimension_semantics=("arbitrary",),
                                             vmem_limit_bytes=VMEM_LIMIT),
        name="pre_attention",
    )(x2d, pos2d, *consts)


def _half_masks(dtype):
    lane = lax.broadcasted_iota(jnp.int32, (1, LANES), 1)
    lo = (lane < LANES // 2).astype(dtype)
    return lo, 1 - lo


def _mla_exact(q_ref, k_ref, v_ref, o_ref):
    v = v_ref[...]
    masks = _half_masks(v.dtype)
    acc = None
    for hh in range(2):
        q = q_ref[:, hh * LANES:(hh + 1) * LANES]
        k = k_ref[:, hh * LANES:(hh + 1) * LANES]
        s = _dot_nt(q, k)
        m = jnp.max(s, axis=-1, keepdims=True)
        p = jnp.exp2(s - m)
        l = jnp.sum(p, axis=-1, keepdims=True)
        o = _dot(p.astype(BF16), v * masks[hh]) / l
        acc = o if acc is None else acc + o
    o_ref[...] = acc.astype(o_ref.dtype)


def _mla_kernel(kmax_ref, q_ref, k_ref, v_ref, wa_ref, wb_ref, o_ref, wa_out, wb_out, vt_ref):
    @pl.when(pl.program_id(2) == 0)
    def _():
        vt_ref[...] = v_ref[...].astype(F32).T.astype(BF16)

    wa_out[...] = wa_ref[...].astype(BF16)
    wb_out[...] = wb_ref[...].astype(BF16)

    ones = jnp.ones((8, LANES), BF16)
    outs = []
    lmin = None
    for hh in range(2):
        q = q_ref[:, hh * LANES:(hh + 1) * LANES]
        qf = q.astype(F32)
        q_sq = _dot_nt(ones, (qf * qf).astype(BF16))[0:1, :]
        bound = jnp.sqrt(q_sq) * kmax_ref[0]
        p_t = jnp.exp2(_dot_nt(k_ref[:, hh * LANES:(hh + 1) * LANES], q) - bound)
        l = jnp.sum(p_t, axis=0, keepdims=True)
        o_t = _dot(vt_ref[hh * MLA_V:(hh + 1) * MLA_V, :], p_t.astype(BF16))
        outs.append(o_t / l)
        lm = jnp.min(l)
        lmin = lm if lmin is None else jnp.minimum(lmin, lm)
    o_ref[...] = jnp.concatenate(outs, axis=0).T.astype(o_ref.dtype)

    @pl.when(jnp.logical_not(lmin > SOFTMAX_UNDERFLOW_GUARD))
    def _():
        _mla_exact(q_ref, k_ref, v_ref, o_ref)


def _mla_attention(kmax, qm, km, vm, w_a, w_b, B, S, tq):
    pairs = MLA_HEADS // 2
    q3 = qm.reshape(B, S, MLA_HEADS * LANES)
    k3 = km.reshape(B, S, MLA_HEADS * LANES)
    v3 = vm.reshape(B, S, MLA_HEADS * MLA_V)
    steps = B * pairs * (S // tq)
    wa3 = w_a.reshape(steps, -1, w_a.shape[-1])
    wb3 = w_b.reshape(steps, -1, w_b.shape[-1])
    step = lambda b, p, i: ((b * pairs + p) * (S // tq) + i, 0, 0)
    w_spec = lambda w: pl.BlockSpec((None,) + w.shape[1:], step)
    om, wa_bf, wb_bf = pl.pallas_call(
        _mla_kernel,
        grid=(B, pairs, S // tq),
        in_specs=[pl.BlockSpec(memory_space=pltpu.SMEM),
                  pl.BlockSpec((None, tq, 2 * LANES), lambda b, p, i: (b, i, p)),
                  pl.BlockSpec((None, S, 2 * LANES), lambda b, p, i: (b, 0, p)),
                  pl.BlockSpec((None, S, LANES), lambda b, p, i: (b, 0, p)),
                  w_spec(wa3), w_spec(wb3)],
        out_specs=[pl.BlockSpec((None, tq, LANES), lambda b, p, i: (b, i, p)),
                   w_spec(wa3), w_spec(wb3)],
        out_shape=[jax.ShapeDtypeStruct((B, S, MLA_HEADS * MLA_V), BF16),
                   jax.ShapeDtypeStruct(wa3.shape, BF16),
                   jax.ShapeDtypeStruct(wb3.shape, BF16)],
        scratch_shapes=[pltpu.VMEM((LANES, S), BF16)],
        compiler_params=pltpu.CompilerParams(
            dimension_semantics=("arbitrary", "arbitrary", "arbitrary"),
            vmem_limit_bytes=VMEM_LIMIT),
        name="mla_attention",
    )(kmax, q3, k3, v3, wa3, wb3)
    return om, wa_bf.reshape(w_a.shape), wb_bf.reshape(w_b.shape)


def _swa_kernel(sink_ref, q_ref, k_ref, v_ref, o_ref, *, tq, S):
    hk = pl.program_id(1)
    i = pl.program_id(2)
    tk = SWA_BLOCK + 2 * WINDOW
    row = lax.broadcasted_iota(jnp.int32, (SWA_GROUP * SWA_BLOCK, 1), 0)
    qoff = row & (SWA_BLOCK - 1)
    head = row // SWA_BLOCK
    sk = jnp.zeros((SWA_GROUP * SWA_BLOCK, 1), F32)
    for g in range(SWA_GROUP):
        sk = jnp.where(head == g, sink_ref[SWA_GROUP * hk + g] * LOG2E, sk)
    low_half = lax.broadcasted_iota(jnp.int32, (1, LANES), 1) < LANES // 2
    for sub in range(tq // SWA_BLOCK):
        rows = slice(sub * SWA_BLOCK, (sub + 1) * SWA_BLOCK)
        q0 = i * tq + sub * SWA_BLOCK
        kstart = pl.multiple_of(jnp.clip(q0 - WINDOW, 0, S - tk), WINDOW)
        k = k_ref[pl.ds(kstart, tk), :]
        v = v_ref[pl.ds(kstart, tk), :]
        q = jnp.concatenate([q_ref[rows, g * LANES:(g + 1) * LANES] for g in range(SWA_GROUP)], axis=0)
        kpos = kstart + lax.broadcasted_iota(jnp.int32, (1, tk), 1)
        valid = jnp.abs(kpos - (q0 + qoff[:SWA_BLOCK])) <= WINDOW
        s = _dot_nt(q, k).reshape(SWA_GROUP, SWA_BLOCK, tk)
        s = jnp.where(valid[None], s, NEG_INF).reshape(SWA_GROUP * SWA_BLOCK, tk)
        m = jnp.maximum(jnp.max(s, axis=-1, keepdims=True), sk)
        e = jnp.exp2(s - m)
        denom = jnp.sum(e, axis=-1, keepdims=True) + jnp.exp2(sk - m)
        o = _dot(e.astype(BF16), v) / denom
        for j in range(SWA_GROUP // 2):
            even = o[(2 * j) * SWA_BLOCK:(2 * j + 1) * SWA_BLOCK, :]
            odd = o[(2 * j + 1) * SWA_BLOCK:(2 * j + 2) * SWA_BLOCK, :]
            o_ref[rows, j * LANES:(j + 1) * LANES] = jnp.where(low_half, even, odd).astype(o_ref.dtype)


def _swa_attention(qw, kw, vw, sink, B, S, tq):
    q3 = qw.reshape(B, S, SWA_HEADS * LANES)
    k3 = kw.reshape(B, S, SWA_KV_HEADS * LANES)
    v3 = vw.reshape(B, S, SWA_KV_HEADS * LANES)
    return pl.pallas_call(
        functools.partial(_swa_kernel, tq=tq, S=S),
        grid=(B, SWA_KV_HEADS, S // tq),
        in_specs=[pl.BlockSpec(memory_space=pltpu.SMEM),
                  pl.BlockSpec((None, tq, SWA_GROUP * LANES), lambda b, h, i: (b, i, h)),
                  pl.BlockSpec((None, S, LANES), lambda b, h, i: (b, 0, h)),
                  pl.BlockSpec((None, S, LANES), lambda b, h, i: (b, 0, h))],
        out_specs=pl.BlockSpec((None, tq, SWA_GROUP * SWA_HD), lambda b, h, i: (b, i, h)),
        out_shape=jax.ShapeDtypeStruct((B, S, SWA_HEADS * SWA_HD), BF16),
        compiler_params=pltpu.CompilerParams(
            dimension_semantics=("arbitrary", "arbitrary", "arbitrary"),
            vmem_limit_bytes=VMEM_LIMIT),
        name="swa_attention",
    )(sink, q3, k3, v3)


def _beats(vj, vi, j_first):
    return (vj >= vi) if j_first else (vj > vi)


def _route(scores, sel):
    G, P = N_GROUPS, EXPERTS_PER_GROUP
    groups = [sel[g * P:(g + 1) * P, :] for g in range(G)]
    row = lax.broadcasted_iota(jnp.int32, (P, 1), 0)
    gscore = []
    for vg in groups:
        m1 = jnp.max(vg, axis=0, keepdims=True)
        first = jnp.min(jnp.where(vg == m1, row, P), axis=0, keepdims=True)
        m2 = jnp.max(jnp.where(row == first, -jnp.inf, vg), axis=0, keepdims=True)
        gscore.append(m1 + m2)
    masked = []
    for g in range(G):
        rank = jnp.zeros_like(gscore[g], dtype=jnp.int32)
        for g2 in range(G):
            if g2 != g:
                rank = rank + _beats(gscore[g2], gscore[g], g2 < g).astype(jnp.int32)
        masked.append(jnp.where(rank < TOPK_GROUPS, groups[g], NEG_INF))
    index = [row + g * P for g in range(G)]
    chosen = [None] * G
    for _ in range(TOP_K):
        best = masked[0]
        for g in range(1, G):
            best = jnp.maximum(best, masked[g])
        best = jnp.max(best, axis=0, keepdims=True)
        first = jnp.where(masked[0] == best, index[0], N_EXPERTS)
        for g in range(1, G):
            first = jnp.minimum(first, jnp.where(masked[g] == best, index[g], N_EXPERTS))
        first = jnp.min(first, axis=0, keepdims=True)
        for g in range(G):
            hit = index[g] == first
            chosen[g] = hit if chosen[g] is None else (chosen[g] | hit)
            masked[g] = jnp.where(hit, -jnp.inf, masked[g])
    picked = [jnp.where(chosen[g], scores[g * P:(g + 1) * P, :], 0.0) for g in range(G)]
    total = picked[0]
    for g in range(1, G):
        total = total + picked[g]
    denom = jnp.sum(total, axis=0, keepdims=True)
    return [pk / denom * ROUTED_SCALE for pk in picked]


def _pack_pair(lo, hi):
    return pltpu.pack_elementwise([lo, hi], packed_dtype=BF16)


def _unpack_pair(word):
    lo = pltpu.unpack_elementwise(word, index=0, packed_dtype=BF16, unpacked_dtype=F32)
    hi = pltpu.unpack_elementwise(word, index=1, packed_dtype=BF16, unpacked_dtype=F32)
    return lo, hi


def _store_rows_dense(ref, words):
    for j in range(ROW_SUB):
        ref[pl.ds(j, words.shape[0], stride=ROW_SUB), :] = words[:, j * LANES:(j + 1) * LANES]


def _load_rows_dense(ref, rows):
    sub = ROW_SUB
    return jnp.concatenate([ref[pl.ds(j, rows, stride=sub), :] for j in range(sub)], axis=1)


def _post_kernel(om_ref, ow_ref, gate_ref, x_ref, wbm_ref, wbw_ref, wout_ref, g_moe_ref, wr_ref,
                 rb_ref, wsgu_ref, wsd_ref, x1_ref, hp_ref, comb_ref):
    am = _dot(om_ref[...], wbm_ref[...])
    aw = _dot(ow_ref[...], wbw_ref[...])
    gates = gate_ref[...].astype(F32)
    merged = gates[:, :D_MODEL] * am + gates[:, D_MODEL:] * aw
    x1 = x_ref[...] + _dot(merged.astype(BF16), wout_ref[...])

    h2 = _rms(x1, D_MODEL) * g_moe_ref[...]
    h2b = h2.astype(BF16)
    _store_rows_dense(hp_ref, _pack_pair(h2[:, :ROW_WORDS], h2[:, ROW_WORDS:]))

    wr = wr_ref[...]
    w_hi = wr.astype(BF16)
    w_lo = (wr - w_hi.astype(F32)).astype(BF16)
    h_lo = (h2 - h2b.astype(F32)).astype(BF16)
    by_hi = _dot_nt(jnp.concatenate([w_hi, w_lo], axis=0), h2b)
    logits = by_hi[:N_EXPERTS] + by_hi[N_EXPERTS:] + _dot_nt(w_hi, h_lo)

    sgu = _dot(h2b, wsgu_ref[...])
    sh = jax.nn.silu(sgu[:, :SHARED_FF]) * sgu[:, SHARED_FF:]
    x1_ref[...] = x1 + _dot(sh.astype(BF16), wsd_ref[...])

    scores = jax.nn.sigmoid(logits)
    comb = _route(scores, scores + rb_ref[...])
    for g in range(N_GROUPS):
        comb_ref[g * EXPERTS_PER_GROUP:(g + 1) * EXPERTS_PER_GROUP, :] = comb[g]


def _post_attention(om, ow, gates, x2d, wbm, wbw, wout, g_moe, wr_t, rbias, wsgu, wsd, tm):
    T = x2d.shape[0]
    row = lambda n: pl.BlockSpec((tm, n), lambda i: (i, 0))
    consts = [wbm, wbw, wout, g_moe, wr_t, rbias, wsgu, wsd]
    return pl.pallas_call(
        _post_kernel,
        grid=(T // tm,),
        in_specs=[row(om.shape[1]), row(ow.shape[1]), row(2 * D_MODEL), row(D_MODEL)]
        + [_full(c.shape) for c in consts],
        out_specs=[row(D_MODEL), pl.BlockSpec((tm * ROW_SUB, LANES), lambda i: (i, 0)),
                   pl.BlockSpec((N_EXPERTS, tm), lambda i: (0, i))],
        out_shape=[jax.ShapeDtypeStruct((T, D_MODEL), F32),
                   jax.ShapeDtypeStruct((T * ROW_SUB, LANES), jnp.uint32),
                   jax.ShapeDtypeStruct((N_EXPERTS, T), F32)],
        compiler_params=pltpu.CompilerParams(dimension_semantics=("arbitrary",),
                                             vmem_limit_bytes=VMEM_LIMIT),
        name="post_attention",
    )(om, ow, gates, x2d, *consts)


SLOT_ROWS = -(-(MOE_CHUNK * TOP_K + N_EXPERTS * (SLOT_ALIGN - 1) + MOE_TILE_BIG) // MOE_TILE) * MOE_TILE
DUMMY_SLOT = SLOT_ROWS - 1
PLAN_BLOCK = 256


def _plan_kernel(comb_ref, slot_ref, w_ref, off_ref, cnt_ref):
    comb = comb_ref[...]
    sel = comb > 0.0
    m = sel.astype(F32)
    mb = m.astype(BF16)
    r_i = lax.broadcasted_iota(jnp.int32, (PLAN_BLOCK, PLAN_BLOCK), 0)
    c_i = lax.broadcasted_iota(jnp.int32, (PLAN_BLOCK, PLAN_BLOCK), 1)
    before = (r_i < c_i).astype(BF16)
    carry = jnp.zeros((N_EXPERTS, 1), F32)
    ranks = []
    for b in range(MOE_CHUNK // PLAN_BLOCK):
        blk = slice(b * PLAN_BLOCK, (b + 1) * PLAN_BLOCK)
        ranks.append(_dot(mb[:, blk], before) + carry)
        carry = carry + jnp.sum(m[:, blk], axis=1, keepdims=True)
    rank = jnp.concatenate(ranks, axis=1)
    cnt = carry
    cnt_pad = jnp.floor((cnt + (SLOT_ALIGN - 1)) * (1.0 / SLOT_ALIGN)) * SLOT_ALIGN
    e_r = lax.broadcasted_iota(jnp.int32, (N_EXPERTS, N_EXPERTS), 0)
    e_c = lax.broadcasted_iota(jnp.int32, (N_EXPERTS, N_EXPERTS), 1)
    below = (e_c < e_r).astype(F32)
    off = jnp.dot(below, jnp.broadcast_to(cnt_pad, (N_EXPERTS, LANES)),
                  precision=lax.Precision.HIGHEST, preferred_element_type=F32)
    slot = off[:, :1] + rank
    kidx = _dot(below.astype(BF16), mb)
    row = lax.broadcasted_iota(jnp.int32, (TOP_K, 1), 0)
    slot_acc = jnp.zeros((TOP_K, MOE_CHUNK), F32)
    w_acc = jnp.zeros((TOP_K, MOE_CHUNK), F32)
    for k in range(TOP_K):
        pick = jnp.where(sel & (kidx == k), 1.0, 0.0)
        found = jnp.sum(pick, axis=0, keepdims=True) > 0.0
        s_k = jnp.where(found, jnp.sum(pick * slot, axis=0, keepdims=True), float(DUMMY_SLOT))
        w_k = jnp.sum(pick * comb, axis=0, keepdims=True)
        slot_acc = jnp.where(row == k, s_k, slot_acc)
        w_acc = jnp.where(row == k, w_k, w_acc)
    slot_ref[...] = slot_acc.astype(jnp.int32) * ROW_SUB
    w_ref[...] = w_acc
    off_ref[...] = off.astype(jnp.int32)
    cnt_ref[...] = jnp.broadcast_to(cnt, (N_EXPERTS, LANES)).astype(jnp.int32)


def _moe_plan(comb_t):
    T = comb_t.shape[1]
    nch = T // MOE_CHUNK
    per_pair = pl.BlockSpec((None, TOP_K, MOE_CHUNK), lambda c: (c, 0, 0))
    per_expert = pl.BlockSpec((None, N_EXPERTS, LANES), lambda c: (c, 0, 0))
    return pl.pallas_call(
        _plan_kernel,
        grid=(nch,),
        in_specs=[pl.BlockSpec((N_EXPERTS, MOE_CHUNK), lambda c: (0, c))],
        out_specs=[per_pair, per_pair, per_expert, per_expert],
        out_shape=[jax.ShapeDtypeStruct((nch, TOP_K, MOE_CHUNK), jnp.int32),
                   jax.ShapeDtypeStruct((nch, TOP_K, MOE_CHUNK), F32),
                   jax.ShapeDtypeStruct((nch, N_EXPERTS, LANES), jnp.int32),
                   jax.ShapeDtypeStruct((nch, N_EXPERTS, LANES), jnp.int32)],
        compiler_params=pltpu.CompilerParams(dimension_semantics=("arbitrary",),
                                             vmem_limit_bytes=VMEM_LIMIT),
        name="moe_plan",
    )(comb_t)


def _slab_at(ref, first):
    return ref.at[pl.ds(pl.multiple_of(first, ROW_SUB), ROW_SUB), :]


def _slab(ref, row):
    return _slab_at(ref, row * ROW_SUB)


def _moe_kernel(off_ref, cnt_ref, slot_hbm, w_hbm, hp_ref, wgu_ref, wd_ref, o_ref,
                buf, clo, chi, *smem_and_sem):
    slot_s = smem_and_sem[:TOP_K]
    w_s = smem_and_sem[TOP_K:2 * TOP_K]
    sem = smem_and_sem[2 * TOP_K]
    c = pl.program_id(0)
    s = pl.program_id(1)

    @pl.when(s == 0)
    def _dispatch():
        copies = [pltpu.make_async_copy(slot_hbm.at[c, k], slot_s[k], sem.at[k]) for k in range(TOP_K)]
        copies += [pltpu.make_async_copy(w_hbm.at[c, k], w_s[k], sem.at[TOP_K + k]) for k in range(TOP_K)]
        for cp in copies:
            cp.start()

        @pl.when(c == 0)
        def _():
            buf[...] = jnp.zeros_like(buf)

        for cp in copies:
            cp.wait()

        def scatter(tb, carry):
            for tt in range(8):
                t = tb * 8 + tt
                slab = _slab(hp_ref, t)[...]
                for k in range(TOP_K):
                    _slab_at(buf, slot_s[k][t])[...] = slab
            return carry

        lax.fori_loop(0, MOE_CHUNK // 8, scatter, 0)

    def tile_load(ee, start, rows):
        e = s * EXPERTS_PER_STEP + ee
        first = (off_ref[c, e] + start) * ROW_SUB
        view = buf.at[pl.ds(pl.multiple_of(first, SLOT_ALIGN * ROW_SUB), rows * ROW_SUB), :]
        x_lo, x_hi = _unpack_pair(_load_rows_dense(view, rows))
        return view, x_lo, x_hi, cnt_ref[c, e] - start

    def tile_ffn(ee, x_lo, x_hi):
        gu = (_dot(x_lo.astype(BF16), wgu_ref[ee, :ROW_WORDS, :])
              + _dot(x_hi.astype(BF16), wgu_ref[ee, ROW_WORDS:, :]))
        hid = jax.nn.silu(gu[:, :EXPERT_FF]) * gu[:, EXPERT_FF:]
        return _dot(hid.astype(BF16), wd_ref[ee])

    def tile_store(view, x_lo, x_hi, left, y):
        mine = lax.broadcasted_iota(jnp.int32, (y.shape[0], 1), 0) < left
        _store_rows_dense(view, _pack_pair(jnp.where(mine, y[:, :ROW_WORDS], x_lo),
                                           jnp.where(mine, y[:, ROW_WORDS:], x_hi)))

    def tiles(jobs):
        loaded = [tile_load(ee, start, rows) for ee, start, rows in jobs]
        ys = [tile_ffn(ee, x_lo, x_hi) for (ee, _, _), (_, x_lo, x_hi, _) in zip(jobs, loaded)]
        for (view, x_lo, x_hi, left), y in zip(loaded, ys):
            tile_store(view, x_lo, x_hi, left, y)

    def more_tiles(ee):
        n = cnt_ref[c, s * EXPERTS_PER_STEP + ee]
        n_big = jnp.maximum((n + MOE_TILE_BIG - MOE_TILE - 1) // MOE_TILE_BIG, 1)

        def big(r, carry):
            tiles([(ee, r * MOE_TILE_BIG, MOE_TILE_BIG)])
            return carry

        lax.fori_loop(1, n_big, big, 0)

        @pl.when(n > n_big * MOE_TILE_BIG)
        def _():
            tiles([(ee, n_big * MOE_TILE_BIG, MOE_TILE)])

    @pl.when(s < EXPERT_STEPS)
    def _experts():
        for first in range(0, EXPERTS_PER_STEP, EXPERTS_IN_FLIGHT):
            tiles([(ee, 0, MOE_TILE_BIG) for ee in range(first, first + EXPERTS_IN_FLIGHT)])
        for ee in range(EXPERTS_PER_STEP):
            more_tiles(ee)

    @pl.when(s >= EXPERT_STEPS)
    def _combine():
        t0 = (s - EXPERT_STEPS) * COMBINE_BLOCK

        def gather(i, carry):
            for tt in range(COMBINE_UNROLL):
                tl = i * COMBINE_UNROLL + tt
                t = t0 + tl
                lo, hi = _unpack_pair(_slab_at(buf, slot_s[0][t])[...])
                acc_lo = w_s[0][t] * lo
                acc_hi = w_s[0][t] * hi
                for k in range(1, TOP_K):
                    lo, hi = _unpack_pair(_slab_at(buf, slot_s[k][t])[...])
                    wk = w_s[k][t]
                    acc_lo = acc_lo + wk * lo
                    acc_hi = acc_hi + wk * hi
                _slab(clo, tl)[...] = acc_lo
                _slab(chi, tl)[...] = acc_hi
            return carry

        lax.fori_loop(0, COMBINE_BLOCK // COMBINE_UNROLL, gather, 0)
        o_ref[:, :ROW_WORDS] = _load_rows_dense(clo, COMBINE_BLOCK)
        o_ref[:, ROW_WORDS:] = _load_rows_dense(chi, COMBINE_BLOCK)


def _moe_sparse(offs, cnts, slots, wts, hp, w_gu, w_d):
    nch = offs.shape[0]
    T = nch * MOE_CHUNK
    blocks = MOE_CHUNK // COMBINE_BLOCK
    expert = lambda c, s, *_: (jnp.minimum(s, EXPERT_STEPS - 1), 0, 0)
    grid_spec = pltpu.PrefetchScalarGridSpec(
        num_scalar_prefetch=2,
        grid=(nch, EXPERT_STEPS + blocks),
        in_specs=[pl.BlockSpec(memory_space=pl.ANY),
                  pl.BlockSpec(memory_space=pl.ANY),
                  pl.BlockSpec((MOE_CHUNK * ROW_SUB, LANES), lambda c, s, *_: (c, 0),
                               pipeline_mode=pl.Buffered(1)),
                  pl.BlockSpec((EXPERTS_PER_STEP, D_MODEL, 2 * EXPERT_FF), expert),
                  pl.BlockSpec((EXPERTS_PER_STEP, EXPERT_FF, D_MODEL), expert)],
        out_specs=pl.BlockSpec(
            (COMBINE_BLOCK, D_MODEL),
            lambda c, s, *_: (c * blocks + jnp.maximum(s - EXPERT_STEPS, 0), 0)),
        scratch_shapes=[pltpu.VMEM((SLOT_ROWS * ROW_SUB, LANES), jnp.uint32),
                        pltpu.VMEM((COMBINE_BLOCK * ROW_SUB, LANES), F32),
                        pltpu.VMEM((COMBINE_BLOCK * ROW_SUB, LANES), F32),
                        *[pltpu.SMEM((MOE_CHUNK,), jnp.int32) for _ in range(TOP_K)],
                        *[pltpu.SMEM((MOE_CHUNK,), F32) for _ in range(TOP_K)],
                        pltpu.SemaphoreType.DMA((2 * TOP_K,))])
    return pl.pallas_call(
        _moe_kernel,
        grid_spec=grid_spec,
        out_shape=jax.ShapeDtypeStruct((T, D_MODEL), F32),
        compiler_params=pltpu.CompilerParams(dimension_semantics=("arbitrary", "arbitrary"),
                                             vmem_limit_bytes=VMEM_LIMIT),
        name="moe_experts",
    )(offs, cnts, slots, wts, hp, w_gu, w_d)


def _ple_kernel(x1_ref, r_ref, p_ref, g_ref, wg_ref, b_ref, wp_ref, o_ref):
    x2 = x1_ref[...] + r_ref[...]
    hn = (_rms(x2, D_MODEL) * g_ref[...]).astype(BF16)
    gate = jax.nn.sigmoid(_dot(hn, wg_ref[...]) + b_ref[...])
    o_ref[...] = x2 + gate * _dot(p_ref[...].astype(BF16), wp_ref[...])


def _ple(x1s, routed, p2d, g_ple, wg, b_ple, wp, tm):
    T = x1s.shape[0]
    row = lambda n: pl.BlockSpec((tm, n), lambda i: (i, 0))
    consts = [g_ple, wg, b_ple, wp]
    return pl.pallas_call(
        _ple_kernel,
        grid=(T // tm,),
        in_specs=[row(D_MODEL), row(D_MODEL), row(PLE_DIM)] + [_full(c.shape) for c in consts],
        out_specs=row(D_MODEL),
        out_shape=jax.ShapeDtypeStruct((T, D_MODEL), F32),
        compiler_params=pltpu.CompilerParams(dimension_semantics=("arbitrary",),
                                             vmem_limit_bytes=VMEM_LIMIT),
        name="ple",
    )(x1s, routed, p2d, *consts)


def _lane_map(*runs):
    src = np.full((LANES,), -1)
    for lane, dim, n in runs:
        src[lane:lane + n] = np.arange(dim, dim + n)
    return src


_MLA_HALF = MLA_ROPE // 2
_SWA_HALF = SWA_HD // 2
MLA_LANES = _lane_map((0, MLA_NOPE, _MLA_HALF), (_MLA_HALF, 0, LANES // 2 - _MLA_HALF),
                      (LANES // 2, MLA_NOPE + _MLA_HALF, _MLA_HALF),
                      (LANES // 2 + _MLA_HALF, LANES // 2 - _MLA_HALF, MLA_NOPE - LANES // 2 + _MLA_HALF))
MLA_NOPE_LANES = np.where(MLA_LANES < MLA_NOPE, MLA_LANES, -1)
MLA_ROPE_LANES = np.where(MLA_LANES >= MLA_NOPE, MLA_LANES - MLA_NOPE, -1)
SWA_LANES = _lane_map((LANES // 2 - _SWA_HALF, 0, _SWA_HALF), (LANES - _SWA_HALF, _SWA_HALF, _SWA_HALF))


def _spread(w, heads, lane_src):
    k = w.shape[0]
    dim = w.shape[1] // heads
    w = w.reshape(k, heads, dim)
    pieces, lane = [], 0
    while lane < LANES:
        end = lane + 1
        while end < LANES and ((lane_src[end] < 0) == (lane_src[lane] < 0)) and (
                lane_src[lane] < 0 or lane_src[end] == lane_src[end - 1] + 1):
            end += 1
        if lane_src[lane] < 0:
            pieces.append(jnp.zeros((k, heads, end - lane), w.dtype))
        else:
            pieces.append(w[:, :, int(lane_src[lane]):int(lane_src[lane]) + end - lane])
        lane = end
    return jnp.concatenate(pieces, axis=-1).reshape(k, heads * LANES)


def _rope_table():
    def inv_freq(dim):
        return 1.0 / (ROPE_THETA ** (jnp.arange(0, dim, 2, dtype=F32) / dim))

    def selector(lane_src, half):
        sel = np.where(lane_src < 0, 0.0, np.where(lane_src < half, -1.0, 1.0))
        return jnp.asarray(sel, F32)

    sel_m = selector(MLA_ROPE_LANES, _MLA_HALF)
    sel_s = selector(SWA_LANES, _SWA_HALF)
    freq_m = _spread(jnp.tile(inv_freq(MLA_ROPE), 2)[None], 1, MLA_ROPE_LANES)[0]
    freq_s = _spread(jnp.tile(inv_freq(SWA_HD), 2)[None], 1, SWA_LANES)[0]
    zero = jnp.zeros((LANES,), F32)
    rows = [freq_m + freq_s, sel_m, jnp.abs(sel_m), sel_s, jnp.abs(sel_s), zero, zero, zero]
    return jnp.stack(rows)


def _layer(x2d, p2d, pos2d, B, S, g_mix, w_in, b_gate, g_cq, w_uq, g_ckv, w_ukv, g_qn_mla, g_kn_mla,
           g_qn_swa, g_kn_swa, sink, w_br_mla, w_br_swa, w_out, g_moe, w_router, router_bias,
           w_exp_gu, w_exp_down, w_sh_gu, w_sh_down, g_ple, w_ple_gate, b_ple, w_ple_proj):
    w_kr = _spread(w_in[:, OFF_CKV:OFF_KR], 1, MLA_ROPE_LANES)
    w_vs = w_in[:, OFF_KS:OFF_VS].reshape(D_MODEL, SWA_KV_HEADS, 1, SWA_HD)
    w_vs = jnp.broadcast_to(w_vs, (D_MODEL, SWA_KV_HEADS, 2, SWA_HD)).reshape(D_MODEL, -1)
    w_all = jnp.concatenate([
        w_in[:, :OFF_CKV], w_kr,
        _spread(w_in[:, OFF_KR:OFF_QS], SWA_HEADS, SWA_LANES),
        _spread(w_in[:, OFF_QS:OFF_KS], SWA_KV_HEADS, SWA_LANES),
        w_vs, w_in[:, OFF_VS:]], axis=1).astype(BF16)
    assert w_all.shape[1] == C_END
    w_uq_p = _spread(w_uq, MLA_HEADS, MLA_LANES).astype(BF16)
    w_ukv3 = w_ukv.reshape(MLA_KV_RANK, MLA_HEADS, MLA_NOPE + MLA_V)
    w_k = _spread(w_ukv3[:, :, :MLA_NOPE].reshape(MLA_KV_RANK, -1), MLA_HEADS, MLA_NOPE_LANES).astype(BF16)
    w_v = w_ukv3[:, :, MLA_NOPE:].reshape(MLA_KV_RANK, -1).astype(BF16)

    qm, km, vm, qw, kw, vw, gates = _pre_attention(
        x2d, pos2d, w_all, w_uq_p, w_k, w_v, g_mix[None], g_cq[None], g_ckv[None],
        _spread(g_qn_mla[None], 1, MLA_LANES), _spread(g_kn_mla[None], 1, MLA_LANES),
        _spread(g_qn_swa[None], 1, SWA_LANES), _spread(g_kn_swa[None], 1, SWA_LANES),
        b_gate[None], _rope_table(), tm=PRE_TILE)

    kmax = (KEY_NORM_MARGIN * MLA_QK ** 0.5) * jnp.max(jnp.abs(g_kn_mla), keepdims=True)
    om, w_gu_bf, w_down_bf = _mla_attention(kmax, qm, km, vm, w_exp_gu, w_exp_down, B, S,
                                            tq=MLA_Q_TILE)
    om = om.reshape(B * S, -1)
    ow = _swa_attention(qw, kw, vw, sink, B, S, tq=SWA_Q_TILE).reshape(B * S, -1)

    x1s, hp, comb_t = _post_attention(
        om, ow, gates, x2d, w_br_mla.astype(BF16), w_br_swa.astype(BF16), w_out.astype(BF16),
        g_moe[None], w_router.T, router_bias[:, None], w_sh_gu.astype(BF16),
        w_sh_down.astype(BF16), tm=ROW_TILE)

    slots, wts, offs, cnts = _moe_plan(comb_t)
    routed = _moe_sparse(offs[:, :, 0], cnts[:, :, 0], slots, wts, hp, w_gu_bf, w_down_bf)

    return _ple(x1s, routed, p2d, g_ple[None], w_ple_gate.astype(BF16), b_ple[None],
                w_ple_proj.astype(BF16), tm=ROW_TILE)


def kernel(x, p, positions, g_mix, w_in, b_gate, g_cq, w_uq, g_ckv, w_ukv, g_qn_mla, g_kn_mla, g_qn_swa, g_kn_swa, sink, w_br_mla, w_br_swa, w_out, g_moe, w_router, router_bias, w_exp_gu, w_exp_down, w_sh_gu, w_sh_down, g_ple, w_ple_gate, b_ple, w_ple_proj):
    B, S, D = x.shape
    x2d = x.reshape(B * S, D)
    pos2d = positions.reshape(B * S, 1)
    for i in range(p.shape[0]):
        x2d = _layer(x2d, p[i].reshape(B * S, -1), pos2d, B, S, g_mix[i], w_in[i], b_gate[i],
                     g_cq[i], w_uq[i], g_ckv[i], w_ukv[i], g_qn_mla[i], g_kn_mla[i], g_qn_swa[i],
                     g_kn_swa[i], sink[i], w_br_mla[i], w_br_swa[i], w_out[i], g_moe[i],
                     w_router[i], router_bias[i], w_exp_gu[i], w_exp_down[i], w_sh_gu[i],
                     w_sh_down[i], g_ple[i], w_ple_gate[i], b_ple[i], w_ple_proj[i])
    return x2d.reshape(B, S, D)
```

```python
import functools

import jax
import jax.numpy as jnp
import numpy as np
from jax import lax
from jax.experimental import pallas as pl
from jax.experimental.pallas import tpu as pltpu

D_MODEL = 1024
PLE_DIM = 256
ROPE_THETA = 10000.0
EPS = 1e-6
NEG_INF = -1e30

MLA_HEADS = 8
MLA_Q_RANK = 384
MLA_KV_RANK = 256
MLA_NOPE = 64
MLA_ROPE = 32
MLA_QK = MLA_NOPE + MLA_ROPE
MLA_V = 64

SWA_HEADS = 8
SWA_KV_HEADS = 2
SWA_GROUP = SWA_HEADS // SWA_KV_HEADS
SWA_HD = 64
WINDOW = 128

OFF_CQ = MLA_Q_RANK
OFF_CKV = OFF_CQ + MLA_KV_RANK
OFF_KR = OFF_CKV + MLA_ROPE
OFF_QS = OFF_KR + SWA_HEADS * SWA_HD
OFF_KS = OFF_QS + SWA_KV_HEADS * SWA_HD
OFF_VS = OFF_KS + SWA_KV_HEADS * SWA_HD

N_EXPERTS = 64
TOP_K = 8
N_GROUPS = 8
TOPK_GROUPS = 4
EXPERTS_PER_GROUP = N_EXPERTS // N_GROUPS
EXPERT_FF = 256
SHARED_FF = 256
ROUTED_SCALE = 2.5

LANES = 128
ROW_WORDS = D_MODEL // 2
ROW_SUB = ROW_WORDS // LANES
MOE_CHUNK = 2048
MOE_TILE = 128
MOE_TILE_BIG = 320
SLOT_ALIGN = 16
COMBINE_BLOCK = 256
COMBINE_UNROLL = 16
EXPERTS_PER_STEP = 4
EXPERT_STEPS = N_EXPERTS // EXPERTS_PER_STEP
EXPERTS_IN_FLIGHT = 2
PRE_TILE = 256
ROW_TILE = 1024
MLA_Q_TILE = 1024
LOG2E = 1.4426950408889634
SOFTMAX_UNDERFLOW_GUARD = 1e-30
KEY_NORM_MARGIN = 1.01
SWA_Q_TILE = 1024
SWA_BLOCK = 128
VMEM_LIMIT = 60 * 1024 * 1024

BF16 = jnp.bfloat16
F32 = jnp.float32

C_CQ = 0
C_CKV = C_CQ + MLA_Q_RANK
C_KR = C_CKV + MLA_KV_RANK
C_QS = C_KR + LANES
C_KS = C_QS + SWA_HEADS * LANES
C_VS = C_KS + SWA_KV_HEADS * LANES
C_GA = C_VS + SWA_KV_HEADS * LANES
C_END = C_GA + 2 * D_MODEL


def _full(shape):
    nd = len(shape)
    return pl.BlockSpec(shape, lambda *_: (0,) * nd)


def _dot(a, b):
    return jnp.dot(a, b, preferred_element_type=F32)


def _dot_nt(a, b, precision=None):
    return lax.dot_general(a, b, (((1,), (1,)), ((), ())), precision=precision,
                           preferred_element_type=F32)


def _rms(v, n):
    return v * lax.rsqrt(jnp.sum(v * v, axis=-1, keepdims=True) * (1.0 / n) + EPS)


def _rope(v, cos, sin):
    return v * cos + pltpu.roll(v, LANES // 2, 1) * sin


def _pre_kernel(x_ref, pos_ref, w_all_ref, w_uq_ref, w_k_ref, w_v_ref, g_mix_ref, g_cq_ref,
                g_ckv_ref, gq_m_ref, gk_m_ref, gq_s_ref, gk_s_ref, b_gate_ref, rope_ref,
                qm_ref, km_ref, vm_ref, qw_ref, kw_ref, vw_ref, gate_ref):
    x = x_ref[...]
    h = (_rms(x, D_MODEL) * g_mix_ref[...]).astype(BF16)

    def proj(lo, hi):
        return _dot(h, w_all_ref[:, lo:hi])

    z_lat = proj(C_CQ, C_QS)
    z_qs = proj(C_QS, C_KS)
    half = (C_END - C_GA) // 2
    z_ga = proj(C_GA, C_GA + half)

    pos = pos_ref[...].astype(F32)
    rope = rope_ref[...]
    ang = pos * rope[0:1, :]
    cos_m1 = jnp.cos(ang) - 1.0
    sin = jnp.sin(ang)
    cos_m, sin_m = 1.0 + cos_m1 * rope[2:3, :], sin * rope[1:2, :]
    cos_s, sin_s = 1.0 + cos_m1 * rope[4:5, :], sin * rope[3:4, :]
    scale_m, scale_s = MLA_QK ** -0.5 * LOG2E, SWA_HD ** -0.5 * LOG2E
    cos_mq, sin_mq = (cos_m * scale_m).astype(BF16), (sin_m * scale_m).astype(BF16)
    cos_sq, sin_sq = (cos_s * scale_s).astype(BF16), (sin_s * scale_s).astype(BF16)

    cqn = (_rms(z_lat[:, C_CQ:C_CKV], MLA_Q_RANK) * g_cq_ref[...]).astype(BF16)
    ckvn = (_rms(z_lat[:, C_CKV:C_KR], MLA_KV_RANK) * g_ckv_ref[...]).astype(BF16)
    q = _dot(cqn, w_uq_ref[...])
    gq_m = gq_m_ref[...]
    for hd in range(MLA_HEADS):
        qh = _rms(q[:, hd * LANES:(hd + 1) * LANES], MLA_QK) * gq_m
        qm_ref[:, hd * LANES:(hd + 1) * LANES] = _rope(qh.astype(BF16), cos_mq, sin_mq)

    kn = _dot(ckvn, w_k_ref[...])
    vm_ref[...] = _dot(ckvn, w_v_ref[...]).astype(BF16)
    z_kv = proj(C_KS, C_GA)

    gq_s = gq_s_ref[...]
    for hd in range(SWA_HEADS):
        qh = _rms(z_qs[:, hd * LANES:(hd + 1) * LANES], SWA_HD) * gq_s
        qw_ref[:, hd * LANES:(hd + 1) * LANES] = _rope(qh.astype(BF16), cos_sq, sin_sq)

    gk_m = gk_m_ref[...]
    kr = z_lat[:, C_KR:C_QS]
    ss_kr = jnp.sum(kr * kr, axis=-1, keepdims=True)
    kr_rot = _rope(kr * gk_m, cos_m, sin_m)
    for hd in range(MLA_HEADS):
        kh = kn[:, hd * LANES:(hd + 1) * LANES]
        ss = jnp.sum(kh * kh, axis=-1, keepdims=True) + ss_kr
        sc = lax.rsqrt(ss * (1.0 / MLA_QK) + EPS)
        km_ref[:, hd * LANES:(hd + 1) * LANES] = ((kh * gk_m + kr_rot) * sc).astype(BF16)

    z_gb = proj(C_GA + half, C_END)

    gk_s = gk_s_ref[...]
    for hd in range(SWA_KV_HEADS):
        kh = _rms(z_kv[:, hd * LANES:(hd + 1) * LANES], SWA_HD) * gk_s
        kh = _rope(kh, cos_s, sin_s)
        kw_ref[:, hd * LANES:(hd + 1) * LANES] = kh.astype(BF16)
    vw_ref[...] = z_kv[:, C_VS - C_KS:].astype(BF16)

    gate_ref[:, :half] = jax.nn.sigmoid(z_ga + b_gate_ref[:, :half]).astype(BF16)
    gate_ref[:, half:] = jax.nn.sigmoid(z_gb + b_gate_ref[:, half:]).astype(BF16)


def _pre_attention(x2d, pos2d, w_all, w_uq, w_k, w_v, g_mix, g_cq, g_ckv, gq_m, gk_m, gq_s,
                   gk_s, b_gate, rope_tab, tm):
    T = x2d.shape[0]
    row = lambda n: pl.BlockSpec((tm, n), lambda i: (i, 0))
    outs = [(MLA_HEADS * LANES, BF16), (MLA_HEADS * LANES, BF16), (MLA_HEADS * MLA_V, BF16),
            (SWA_HEADS * LANES, BF16), (SWA_KV_HEADS * LANES, BF16),
            (SWA_KV_HEADS * LANES, BF16), (2 * D_MODEL, BF16)]
    consts = [w_all, w_uq, w_k, w_v, g_mix, g_cq, g_ckv, gq_m, gk_m, gq_s, gk_s, b_gate, rope_tab]
    return pl.pallas_call(
        _pre_kernel,
        grid=(T // tm,),
        in_specs=[row(D_MODEL), row(1)] + [_full(c.shape) for c in consts],
        out_specs=[row(n) for n, _ in outs],
        out_shape=[jax.ShapeDtypeStruct((T, n), dt) for n, dt in outs],
        compiler_params=pltpu.CompilerParams(dimension_semantics=("arbitrary",),
                                             vmem_limit_bytes=VMEM_LIMIT),
        name="pre_attention",
    )(x2d, pos2d, *consts)


def _half_masks(dtype):
    lane = lax.broadcasted_iota(jnp.int32, (1, LANES), 1)
    lo = (lane < LANES // 2).astype(dtype)
    return lo, 1 - lo


def _mla_exact(q_ref, k_ref, v_ref, o_ref):
    v = v_ref[...]
    masks = _half_masks(v.dtype)
    acc = None
    for hh in range(2):
        q = q_ref[:, hh * LANES:(hh + 1) * LANES]
        k = k_ref[:, hh * LANES:(hh + 1) * LANES]
        s = _dot_nt(q, k)
        m = jnp.max(s, axis=-1, keepdims=True)
        p = jnp.exp2(s - m)
        l = jnp.sum(p, axis=-1, keepdims=True)
        o = _dot(p.astype(BF16), v * masks[hh]) / l
        acc = o if acc is None else acc + o
    o_ref[...] = acc.astype(o_ref.dtype)


def _mla_kernel(kmax_ref, q_ref, k_ref, v_ref, wa_ref, wb_ref, o_ref, wa_out, wb_out, vt_ref):
    @pl.when(pl.program_id(2) == 0)
    def _():
        vt_ref[...] = v_ref[...].astype(F32).T.astype(BF16)

    wa_out[...] = wa_ref[...].astype(BF16)
    wb_out[...] = wb_ref[...].astype(BF16)

    ones = jnp.ones((8, LANES), BF16)
    outs = []
    lmin = None
    for hh in range(2):
        q = q_ref[:, hh * LANES:(hh + 1) * LANES]
        qf = q.astype(F32)
        q_sq = _dot_nt(ones, (qf * qf).astype(BF16))[0:1, :]
        bound = jnp.sqrt(q_sq) * kmax_ref[0]
        p_t = jnp.exp2(_dot_nt(k_ref[:, hh * LANES:(hh + 1) * LANES], q) - bound)
        l = jnp.sum(p_t, axis=0, keepdims=True)
        o_t = _dot(vt_ref[hh * MLA_V:(hh + 1) * MLA_V, :], p_t.astype(BF16))
        outs.append(o_t / l)
        lm = jnp.min(l)
        lmin = lm if lmin is None else jnp.minimum(lmin, lm)
    o_ref[...] = jnp.concatenate(outs, axis=0).T.astype(o_ref.dtype)

    @pl.when(jnp.logical_not(lmin > SOFTMAX_UNDERFLOW_GUARD))
    def _():
        _mla_exact(q_ref, k_ref, v_ref, o_ref)


def _mla_attention(kmax, qm, km, vm, w_a, w_b, B, S, tq):
    pairs = MLA_HEADS // 2
    q3 = qm.reshape(B, S, MLA_HEADS * LANES)
    k3 = km.reshape(B, S, MLA_HEADS * LANES)
    v3 = vm.reshape(B, S, MLA_HEADS * MLA_V)
    steps = B * pairs * (S // tq)
    wa3 = w_a.reshape(steps, -1, w_a.shape[-1])
    wb3 = w_b.reshape(steps, -1, w_b.shape[-1])
    step = lambda b, p, i: ((b * pairs + p) * (S // tq) + i, 0, 0)
    w_spec = lambda w: pl.BlockSpec((None,) + w.shape[1:], step)
    om, wa_bf, wb_bf = pl.pallas_call(
        _mla_kernel,
        grid=(B, pairs, S // tq),
        in_specs=[pl.BlockSpec(memory_space=pltpu.SMEM),
                  pl.BlockSpec((None, tq, 2 * LANES), lambda b, p, i: (b, i, p)),
                  pl.BlockSpec((None, S, 2 * LANES), lambda b, p, i: (b, 0, p)),
                  pl.BlockSpec((None, S, LANES), lambda b, p, i: (b, 0, p)),
                  w_spec(wa3), w_spec(wb3)],
        out_specs=[pl.BlockSpec((None, tq, LANES), lambda b, p, i: (b, i, p)),
                   w_spec(wa3), w_spec(wb3)],
        out_shape=[jax.ShapeDtypeStruct((B, S, MLA_HEADS * MLA_V), BF16),
                   jax.ShapeDtypeStruct(wa3.shape, BF16),
                   jax.ShapeDtypeStruct(wb3.shape, BF16)],
        scratch_shapes=[pltpu.VMEM((LANES, S), BF16)],
        compiler_params=pltpu.CompilerParams(
            dimension_semantics=("arbitrary", "arbitrary", "arbitrary"),
            vmem_limit_bytes=VMEM_LIMIT),
        name="mla_attention",
    )(kmax, q3, k3, v3, wa3, wb3)
    return om, wa_bf.reshape(w_a.shape), wb_bf.reshape(w_b.shape)


def _swa_kernel(sink_ref, q_ref, k_ref, v_ref, o_ref, *, tq, S):
    hk = pl.program_id(1)
    i = pl.program_id(2)
    tk = SWA_BLOCK + 2 * WINDOW
    row = lax.broadcasted_iota(jnp.int32, (SWA_GROUP * SWA_BLOCK, 1), 0)
    qoff = row & (SWA_BLOCK - 1)
    head = row // SWA_BLOCK
    sk = jnp.zeros((SWA_GROUP * SWA_BLOCK, 1), F32)
    for g in range(SWA_GROUP):
        sk = jnp.where(head == g, sink_ref[SWA_GROUP * hk + g] * LOG2E, sk)
    low_half = lax.broadcasted_iota(jnp.int32, (1, LANES), 1) < LANES // 2
    for sub in range(tq // SWA_BLOCK):
        rows = slice(sub * SWA_BLOCK, (sub + 1) * SWA_BLOCK)
        q0 = i * tq + sub * SWA_BLOCK
        kstart = pl.multiple_of(jnp.clip(q0 - WINDOW, 0, S - tk), WINDOW)
        k = k_ref[pl.ds(kstart, tk), :]
        v = v_ref[pl.ds(kstart, tk), :]
        q = jnp.concatenate([q_ref[rows, g * LANES:(g + 1) * LANES] for g in range(SWA_GROUP)], axis=0)
        kpos = kstart + lax.broadcasted_iota(jnp.int32, (1, tk), 1)
        valid = jnp.abs(kpos - (q0 + qoff[:SWA_BLOCK])) <= WINDOW
        s = _dot_nt(q, k).reshape(SWA_GROUP, SWA_BLOCK, tk)
        s = jnp.where(valid[None], s, NEG_INF).reshape(SWA_GROUP * SWA_BLOCK, tk)
        m = jnp.maximum(jnp.max(s, axis=-1, keepdims=True), sk)
        e = jnp.exp2(s - m)
        denom = jnp.sum(e, axis=-1, keepdims=True) + jnp.exp2(sk - m)
        o = _dot(e.astype(BF16), v) / denom
        for j in range(SWA_GROUP // 2):
            even = o[(2 * j) * SWA_BLOCK:(2 * j + 1) * SWA_BLOCK, :]
            odd = o[(2 * j + 1) * SWA_BLOCK:(2 * j + 2) * SWA_BLOCK, :]
            o_ref[rows, j * LANES:(j + 1) * LANES] = jnp.where(low_half, even, odd).astype(o_ref.dtype)


def _swa_attention(qw, kw, vw, sink, B, S, tq):
    q3 = qw.reshape(B, S, SWA_HEADS * LANES)
    k3 = kw.reshape(B, S, SWA_KV_HEADS * LANES)
    v3 = vw.reshape(B, S, SWA_KV_HEADS * LANES)
    return pl.pallas_call(
        functools.partial(_swa_kernel, tq=tq, S=S),
        grid=(B, SWA_KV_HEADS, S // tq),
        in_specs=[pl.BlockSpec(memory_space=pltpu.SMEM),
                  pl.BlockSpec((None, tq, SWA_GROUP * LANES), lambda b, h, i: (b, i, h)),
                  pl.BlockSpec((None, S, LANES), lambda b, h, i: (b, 0, h)),
                  pl.BlockSpec((None, S, LANES), lambda b, h, i: (b, 0, h))],
        out_specs=pl.BlockSpec((None, tq, SWA_GROUP * SWA_HD), lambda b, h, i: (b, i, h)),
        out_shape=jax.ShapeDtypeStruct((B, S, SWA_HEADS * SWA_HD), BF16),
        compiler_params=pltpu.CompilerParams(
            dimension_semantics=("arbitrary", "arbitrary", "arbitrary"),
            vmem_limit_bytes=VMEM_LIMIT),
        name="swa_attention",
    )(sink, q3, k3, v3)


def _beats(vj, vi, j_first):
    return (vj >= vi) if j_first else (vj > vi)


def _route(scores, sel):
    G, P = N_GROUPS, EXPERTS_PER_GROUP
    groups = [sel[g * P:(g + 1) * P, :] for g in range(G)]
    row = lax.broadcasted_iota(jnp.int32, (P, 1), 0)
    gscore = []
    for vg in groups:
        m1 = jnp.max(vg, axis=0, keepdims=True)
        first = jnp.min(jnp.where(vg == m1, row, P), axis=0, keepdims=True)
        m2 = jnp.max(jnp.where(row == first, -jnp.inf, vg), axis=0, keepdims=True)
        gscore.append(m1 + m2)
    masked = []
    for g in range(G):
        rank = jnp.zeros_like(gscore[g], dtype=jnp.int32)
        for g2 in range(G):
            if g2 != g:
                rank = rank + _beats(gscore[g2], gscore[g], g2 < g).astype(jnp.int32)
        masked.append(jnp.where(rank < TOPK_GROUPS, groups[g], NEG_INF))
    index = [row + g * P for g in range(G)]
    chosen = [None] * G
    for _ in range(TOP_K):
        best = masked[0]
        for g in range(1, G):
            best = jnp.maximum(best, masked[g])
        best = jnp.max(best, axis=0, keepdims=True)
        first = jnp.where(masked[0] == best, index[0], N_EXPERTS)
        for g in range(1, G):
            first = jnp.minimum(first, jnp.where(masked[g] == best, index[g], N_EXPERTS))
        first = jnp.min(first, axis=0, keepdims=True)
        for g in range(G):
            hit = index[g] == first
            chosen[g] = hit if chosen[g] is None else (chosen[g] | hit)
            masked[g] = jnp.where(hit, -jnp.inf, masked[g])
    picked = [jnp.where(chosen[g], scores[g * P:(g + 1) * P, :], 0.0) for g in range(G)]
    total = picked[0]
    for g in range(1, G):
        total = total + picked[g]
    denom = jnp.sum(total, axis=0, keepdims=True)
    return [pk / denom * ROUTED_SCALE for pk in picked]


def _pack_pair(lo, hi):
    return pltpu.pack_elementwise([lo, hi], packed_dtype=BF16)


def _unpack_pair(word):
    lo = pltpu.unpack_elementwise(word, index=0, packed_dtype=BF16, unpacked_dtype=F32)
    hi = pltpu.unpack_elementwise(word, index=1, packed_dtype=BF16, unpacked_dtype=F32)
    return lo, hi


def _store_rows_dense(ref, words):
    for j in range(ROW_SUB):
        ref[pl.ds(j, words.shape[0], stride=ROW_SUB), :] = words[:, j * LANES:(j + 1) * LANES]


def _load_rows_dense(ref, rows):
    sub = ROW_SUB
    return jnp.concatenate([ref[pl.ds(j, rows, stride=sub), :] for j in range(sub)], axis=1)


def _post_kernel(om_ref, ow_ref, gate_ref, x_ref, wbm_ref, wbw_ref, wout_ref, g_moe_ref, wr_ref,
                 rb_ref, wsgu_ref, wsd_ref, x1_ref, hp_ref, comb_ref):
    am = _dot(om_ref[...], wbm_ref[...])
    aw = _dot(ow_ref[...], wbw_ref[...])
    gates = gate_ref[...].astype(F32)
    merged = gates[:, :D_MODEL] * am + gates[:, D_MODEL:] * aw
    x1 = x_ref[...] + _dot(merged.astype(BF16), wout_ref[...])

    h2 = _rms(x1, D_MODEL) * g_moe_ref[...]
    h2b = h2.astype(BF16)
    _store_rows_dense(hp_ref, _pack_pair(h2[:, :ROW_WORDS], h2[:, ROW_WORDS:]))

    wr = wr_ref[...]
    w_hi = wr.astype(BF16)
    w_lo = (wr - w_hi.astype(F32)).astype(BF16)
    h_lo = (h2 - h2b.astype(F32)).astype(BF16)
    by_hi = _dot_nt(jnp.concatenate([w_hi, w_lo], axis=0), h2b)
    logits = by_hi[:N_EXPERTS] + by_hi[N_EXPERTS:] + _dot_nt(w_hi, h_lo)

    sgu = _dot(h2b, wsgu_ref[...])
    sh = jax.nn.silu(sgu[:, :SHARED_FF]) * sgu[:, SHARED_FF:]
    x1_ref[...] = x1 + _dot(sh.astype(BF16), wsd_ref[...])

    scores = jax.nn.sigmoid(logits)
    comb = _route(scores, scores + rb_ref[...])
    for g in range(N_GROUPS):
        comb_ref[g * EXPERTS_PER_GROUP:(g + 1) * EXPERTS_PER_GROUP, :] = comb[g]


def _post_attention(om, ow, gates, x2d, wbm, wbw, wout, g_moe, wr_t, rbias, wsgu, wsd, tm):
    T = x2d.shape[0]
    row = lambda n: pl.BlockSpec((tm, n), lambda i: (i, 0))
    consts = [wbm, wbw, wout, g_moe, wr_t, rbias, wsgu, wsd]
    return pl.pallas_call(
        _post_kernel,
        grid=(T // tm,),
        in_specs=[row(om.shape[1]), row(ow.shape[1]), row(2 * D_MODEL), row(D_MODEL)]
        + [_full(c.shape) for c in consts],
        out_specs=[row(D_MODEL), pl.BlockSpec((tm * ROW_SUB, LANES), lambda i: (i, 0)),
                   pl.BlockSpec((N_EXPERTS, tm), lambda i: (0, i))],
        out_shape=[jax.ShapeDtypeStruct((T, D_MODEL), F32),
                   jax.ShapeDtypeStruct((T * ROW_SUB, LANES), jnp.uint32),
                   jax.ShapeDtypeStruct((N_EXPERTS, T), F32)],
        compiler_params=pltpu.CompilerParams(dimension_semantics=("arbitrary",),
                                             vmem_limit_bytes=VMEM_LIMIT),
        name="post_attention",
    )(om, ow, gates, x2d, *consts)


SLOT_ROWS = -(-(MOE_CHUNK * TOP_K + N_EXPERTS * (SLOT_ALIGN - 1) + MOE_TILE_BIG) // MOE_TILE) * MOE_TILE
DUMMY_SLOT = SLOT_ROWS - 1
PLAN_BLOCK = 256


def _plan_kernel(comb_ref, slot_ref, w_ref, off_ref, cnt_ref):
    comb = comb_ref[...]
    sel = comb > 0.0
    m = sel.astype(F32)
    mb = m.astype(BF16)
    r_i = lax.broadcasted_iota(jnp.int32, (PLAN_BLOCK, PLAN_BLOCK), 0)
    c_i = lax.broadcasted_iota(jnp.int32, (PLAN_BLOCK, PLAN_BLOCK), 1)
    before = (r_i < c_i).astype(BF16)
    carry = jnp.zeros((N_EXPERTS, 1), F32)
    ranks = []
    for b in range(MOE_CHUNK // PLAN_BLOCK):
        blk = slice(b * PLAN_BLOCK, (b + 1) * PLAN_BLOCK)
        ranks.append(_dot(mb[:, blk], before) + carry)
        carry = carry + jnp.sum(m[:, blk], axis=1, keepdims=True)
    rank = jnp.concatenate(ranks, axis=1)
    cnt = carry
    cnt_pad = jnp.floor((cnt + (SLOT_ALIGN - 1)) * (1.0 / SLOT_ALIGN)) * SLOT_ALIGN
    e_r = lax.broadcasted_iota(jnp.int32, (N_EXPERTS, N_EXPERTS), 0)
    e_c = lax.broadcasted_iota(jnp.int32, (N_EXPERTS, N_EXPERTS), 1)
    below = (e_c < e_r).astype(F32)
    off = jnp.dot(below, jnp.broadcast_to(cnt_pad, (N_EXPERTS, LANES)),
                  precision=lax.Precision.HIGHEST, preferred_element_type=F32)
    slot = off[:, :1] + rank
    kidx = _dot(below.astype(BF16), mb)
    row = lax.broadcasted_iota(jnp.int32, (TOP_K, 1), 0)
    slot_acc = jnp.zeros((TOP_K, MOE_CHUNK), F32)
    w_acc = jnp.zeros((TOP_K, MOE_CHUNK), F32)
    for k in range(TOP_K):
        pick = jnp.where(sel & (kidx == k), 1.0, 0.0)
        found = jnp.sum(pick, axis=0, keepdims=True) > 0.0
        s_k = jnp.where(found, jnp.sum(pick * slot, axis=0, keepdims=True), float(DUMMY_SLOT))
        w_k = jnp.sum(pick * comb, axis=0, keepdims=True)
        slot_acc = jnp.where(row == k, s_k, slot_acc)
        w_acc = jnp.where(row == k, w_k, w_acc)
    slot_ref[...] = slot_acc.astype(jnp.int32) * ROW_SUB
    w_ref[...] = w_acc
    off_ref[...] = off.astype(jnp.int32)
    cnt_ref[...] = jnp.broadcast_to(cnt, (N_EXPERTS, LANES)).astype(jnp.int32)


def _moe_plan(comb_t):
    T = comb_t.shape[1]
    nch = T // MOE_CHUNK
    per_pair = pl.BlockSpec((None, TOP_K, MOE_CHUNK), lambda c: (c, 0, 0))
    per_expert = pl.BlockSpec((None, N_EXPERTS, LANES), lambda c: (c, 0, 0))
    return pl.pallas_call(
        _plan_kernel,
        grid=(nch,),
        in_specs=[pl.BlockSpec((N_EXPERTS, MOE_CHUNK), lambda c: (0, c))],
        out_specs=[per_pair, per_pair, per_expert, per_expert],
        out_shape=[jax.ShapeDtypeStruct((nch, TOP_K, MOE_CHUNK), jnp.int32),
                   jax.ShapeDtypeStruct((nch, TOP_K, MOE_CHUNK), F32),
                   jax.ShapeDtypeStruct((nch, N_EXPERTS, LANES), jnp.int32),
                   jax.ShapeDtypeStruct((nch, N_EXPERTS, LANES), jnp.int32)],
        compiler_params=pltpu.CompilerParams(dimension_semantics=("arbitrary",),
                                             vmem_limit_bytes=VMEM_LIMIT),
        name="moe_plan",
    )(comb_t)


def _slab_at(ref, first):
    return ref.at[pl.ds(pl.multiple_of(first, ROW_SUB), ROW_SUB), :]


def _slab(ref, row):
    return _slab_at(ref, row * ROW_SUB)


def _moe_kernel(off_ref, cnt_ref, slot_hbm, w_hbm, hp_ref, wgu_ref, wd_ref, o_ref,
                buf, clo, chi, *smem_and_sem):
    slot_s = smem_and_sem[:TOP_K]
    w_s = smem_and_sem[TOP_K:2 * TOP_K]
    sem = smem_and_sem[2 * TOP_K]
    c = pl.program_id(0)
    s = pl.program_id(1)

    @pl.when(s == 0)
    def _dispatch():
        copies = [pltpu.make_async_copy(slot_hbm.at[c, k], slot_s[k], sem.at[k]) for k in range(TOP_K)]
        copies += [pltpu.make_async_copy(w_hbm.at[c, k], w_s[k], sem.at[TOP_K + k]) for k in range(TOP_K)]
        for cp in copies:
            cp.start()

        @pl.when(c == 0)
        def _():
            buf[...] = jnp.zeros_like(buf)

        for cp in copies:
            cp.wait()

        def scatter(tb, carry):
            for tt in range(8):
                t = tb * 8 + tt
                slab = _slab(hp_ref, t)[...]
                for k in range(TOP_K):
                    _slab_at(buf, slot_s[k][t])[...] = slab
            return carry

        lax.fori_loop(0, MOE_CHUNK // 8, scatter, 0)

    def tile_load(ee, start, rows):
        e = s * EXPERTS_PER_STEP + ee
        first = (off_ref[c, e] + start) * ROW_SUB
        view = buf.at[pl.ds(pl.multiple_of(first, SLOT_ALIGN * ROW_SUB), rows * ROW_SUB), :]
        x_lo, x_hi = _unpack_pair(_load_rows_dense(view, rows))
        return view, x_lo, x_hi, cnt_ref[c, e] - start

    def tile_ffn(ee, x_lo, x_hi):
        gu = (_dot(x_lo.astype(BF16), wgu_ref[ee, :ROW_WORDS, :])
              + _dot(x_hi.astype(BF16), wgu_ref[ee, ROW_WORDS:, :]))
        hid = jax.nn.silu(gu[:, :EXPERT_FF]) * gu[:, EXPERT_FF:]
        return _dot(hid.astype(BF16), wd_ref[ee])

    def tile_store(view, x_lo, x_hi, left, y):
        mine = lax.broadcasted_iota(jnp.int32, (y.shape[0], 1), 0) < left
        _store_rows_dense(view, _pack_pair(jnp.where(mine, y[:, :ROW_WORDS], x_lo),
                                           jnp.where(mine, y[:, ROW_WORDS:], x_hi)))

    def tiles(jobs):
        loaded = [tile_load(ee, start, rows) for ee, start, rows in jobs]
        ys = [tile_ffn(ee, x_lo, x_hi) for (ee, _, _), (_, x_lo, x_hi, _) in zip(jobs, loaded)]
        for (view, x_lo, x_hi, left), y in zip(loaded, ys):
            tile_store(view, x_lo, x_hi, left, y)

    def more_tiles(ee):
        n = cnt_ref[c, s * EXPERTS_PER_STEP + ee]
        n_big = jnp.maximum((n + MOE_TILE_BIG - MOE_TILE - 1) // MOE_TILE_BIG, 1)

        def big(r, carry):
            tiles([(ee, r * MOE_TILE_BIG, MOE_TILE_BIG)])
            return carry

        lax.fori_loop(1, n_big, big, 0)

        @pl.when(n > n_big * MOE_TILE_BIG)
        def _():
            tiles([(ee, n_big * MOE_TILE_BIG, MOE_TILE)])

    @pl.when(s < EXPERT_STEPS)
    def _experts():
        for first in range(0, EXPERTS_PER_STEP, EXPERTS_IN_FLIGHT):
            tiles([(ee, 0, MOE_TILE_BIG) for ee in range(first, first + EXPERTS_IN_FLIGHT)])
        for ee in range(EXPERTS_PER_STEP):
            more_tiles(ee)

    @pl.when(s >= EXPERT_STEPS)
    def _combine():
        t0 = (s - EXPERT_STEPS) * COMBINE_BLOCK

        def gather(i, carry):
            for tt in range(COMBINE_UNROLL):
                tl = i * COMBINE_UNROLL + tt
                t = t0 + tl
                lo, hi = _unpack_pair(_slab_at(buf, slot_s[0][t])[...])
                acc_lo = w_s[0][t] * lo
                acc_hi = w_s[0][t] * hi
                for k in range(1, TOP_K):
                    lo, hi = _unpack_pair(_slab_at(buf, slot_s[k][t])[...])
                    wk = w_s[k][t]
                    acc_lo = acc_lo + wk * lo
                    acc_hi = acc_hi + wk * hi
                _slab(clo, tl)[...] = acc_lo
                _slab(chi, tl)[...] = acc_hi
            return carry

        lax.fori_loop(0, COMBINE_BLOCK // COMBINE_UNROLL, gather, 0)
        o_ref[:, :ROW_WORDS] = _load_rows_dense(clo, COMBINE_BLOCK)
        o_ref[:, ROW_WORDS:] = _load_rows_dense(chi, COMBINE_BLOCK)


def _moe_sparse(offs, cnts, slots, wts, hp, w_gu, w_d):
    nch = offs.shape[0]
    T = nch * MOE_CHUNK
    blocks = MOE_CHUNK // COMBINE_BLOCK
    expert = lambda c, s, *_: (jnp.minimum(s, EXPERT_STEPS - 1), 0, 0)
    grid_spec = pltpu.PrefetchScalarGridSpec(
        num_scalar_prefetch=2,
        grid=(nch, EXPERT_STEPS + blocks),
        in_specs=[pl.BlockSpec(memory_space=pl.ANY),
                  pl.BlockSpec(memory_space=pl.ANY),
                  pl.BlockSpec((MOE_CHUNK * ROW_SUB, LANES), lambda c, s, *_: (c, 0),
                               pipeline_mode=pl.Buffered(1)),
                  pl.BlockSpec((EXPERTS_PER_STEP, D_MODEL, 2 * EXPERT_FF), expert),
                  pl.BlockSpec((EXPERTS_PER_STEP, EXPERT_FF, D_MODEL), expert)],
        out_specs=pl.BlockSpec(
            (COMBINE_BLOCK, D_MODEL),
            lambda c, s, *_: (c * blocks + jnp.maximum(s - EXPERT_STEPS, 0), 0)),
        scratch_shapes=[pltpu.VMEM((SLOT_ROWS * ROW_SUB, LANES), jnp.uint32),
                        pltpu.VMEM((COMBINE_BLOCK * ROW_SUB, LANES), F32),
                        pltpu.VMEM((COMBINE_BLOCK * ROW_SUB, LANES), F32),
                        *[pltpu.SMEM((MOE_CHUNK,), jnp.int32) for _ in range(TOP_K)],
                        *[pltpu.SMEM((MOE_CHUNK,), F32) for _ in range(TOP_K)],
                        pltpu.SemaphoreType.DMA((2 * TOP_K,))])
    return pl.pallas_call(
        _moe_kernel,
        grid_spec=grid_spec,
        out_shape=jax.ShapeDtypeStruct((T, D_MODEL), F32),
        compiler_params=pltpu.CompilerParams(dimension_semantics=("arbitrary", "arbitrary"),
                                             vmem_limit_bytes=VMEM_LIMIT),
        name="moe_experts",
    )(offs, cnts, slots, wts, hp, w_gu, w_d)


def _ple_kernel(x1_ref, r_ref, p_ref, g_ref, wg_ref, b_ref, wp_ref, o_ref):
    x2 = x1_ref[...] + r_ref[...]
    hn = (_rms(x2, D_MODEL) * g_ref[...]).astype(BF16)
    gate = jax.nn.sigmoid(_dot(hn, wg_ref[...]) + b_ref[...])
    o_ref[...] = x2 + gate * _dot(p_ref[...].astype(BF16), wp_ref[...])


def _ple(x1s, routed, p2d, g_ple, wg, b_ple, wp, tm):
    T = x1s.shape[0]
    row = lambda n: pl.BlockSpec((tm, n), lambda i: (i, 0))
    consts = [g_ple, wg, b_ple, wp]
    return pl.pallas_call(
        _ple_kernel,
        grid=(T // tm,),
        in_specs=[row(D_MODEL), row(D_MODEL), row(PLE_DIM)] + [_full(c.shape) for c in consts],
        out_specs=row(D_MODEL),
        out_shape=jax.ShapeDtypeStruct((T, D_MODEL), F32),
        compiler_params=pltpu.CompilerParams(dimension_semantics=("arbitrary",),
                                             vmem_limit_bytes=VMEM_LIMIT),
        name="ple",
    )(x1s, routed, p2d, *consts)


def _lane_map(*runs):
    src = np.full((LANES,), -1)
    for lane, dim, n in runs:
        src[lane:lane + n] = np.arange(dim, dim + n)
    return src


_MLA_HALF = MLA_ROPE // 2
_SWA_HALF = SWA_HD // 2
MLA_LANES = _lane_map((0, MLA_NOPE, _MLA_HALF), (_MLA_HALF, 0, LANES // 2 - _MLA_HALF),
                      (LANES // 2, MLA_NOPE + _MLA_HALF, _MLA_HALF),
                      (LANES // 2 + _MLA_HALF, LANES // 2 - _MLA_HALF, MLA_NOPE - LANES // 2 + _MLA_HALF))
MLA_NOPE_LANES = np.where(MLA_LANES < MLA_NOPE, MLA_LANES, -1)
MLA_ROPE_LANES = np.where(MLA_LANES >= MLA_NOPE, MLA_LANES - MLA_NOPE, -1)
SWA_LANES = _lane_map((LANES // 2 - _SWA_HALF, 0, _SWA_HALF), (LANES - _SWA_HALF, _SWA_HALF, _SWA_HALF))


def _spread(w, heads, lane_src):
    k = w.shape[0]
    dim = w.shape[1] // heads
    w = w.reshape(k, heads, dim)
    pieces, lane = [], 0
    while lane < LANES:
        end = lane + 1
        while end < LANES and ((lane_src[end] < 0) == (lane_src[lane] < 0)) and (
                lane_src[lane] < 0 or lane_src[end] == lane_src[end - 1] + 1):
            end += 1
        if lane_src[lane] < 0:
            pieces.append(jnp.zeros((k, heads, end - lane), w.dtype))
        else:
            pieces.append(w[:, :, int(lane_src[lane]):int(lane_src[lane]) + end - lane])
        lane = end
    return jnp.concatenate(pieces, axis=-1).reshape(k, heads * LANES)


def _rope_table():
    def inv_freq(dim):
        return 1.0 / (ROPE_THETA ** (jnp.arange(0, dim, 2, dtype=F32) / dim))

    def selector(lane_src, half):
        sel = np.where(lane_src < 0, 0.0, np.where(lane_src < half, -1.0, 1.0))
        return jnp.asarray(sel, F32)

    sel_m = selector(MLA_ROPE_LANES, _MLA_HALF)
    sel_s = selector(SWA_LANES, _SWA_HALF)
    freq_m = _spread(jnp.tile(inv_freq(MLA_ROPE), 2)[None], 1, MLA_ROPE_LANES)[0]
    freq_s = _spread(jnp.tile(inv_freq(SWA_HD), 2)[None], 1, SWA_LANES)[0]
    zero = jnp.zeros((LANES,), F32)
    rows = [freq_m + freq_s, sel_m, jnp.abs(sel_m), sel_s, jnp.abs(sel_s), zero, zero, zero]
    return jnp.stack(rows)


def _layer(x2d, p2d, pos2d, B, S, g_mix, w_in, b_gate, g_cq, w_uq, g_ckv, w_ukv, g_qn_mla, g_kn_mla,
           g_qn_swa, g_kn_swa, sink, w_br_mla, w_br_swa, w_out, g_moe, w_router, router_bias,
           w_exp_gu, w_exp_down, w_sh_gu, w_sh_down, g_ple, w_ple_gate, b_ple, w_ple_proj):
    w_kr = _spread(w_in[:, OFF_CKV:OFF_KR], 1, MLA_ROPE_LANES)
    w_vs = w_in[:, OFF_KS:OFF_VS].reshape(D_MODEL, SWA_KV_HEADS, 1, SWA_HD)
    w_vs = jnp.broadcast_to(w_vs, (D_MODEL, SWA_KV_HEADS, 2, SWA_HD)).reshape(D_MODEL, -1)
    w_all = jnp.concatenate([
        w_in[:, :OFF_CKV], w_kr,
        _spread(w_in[:, OFF_KR:OFF_QS], SWA_HEADS, SWA_LANES),
        _spread(w_in[:, OFF_QS:OFF_KS], SWA_KV_HEADS, SWA_LANES),
        w_vs, w_in[:, OFF_VS:]], axis=1).astype(BF16)
    assert w_all.shape[1] == C_END
    w_uq_p = _spread(w_uq, MLA_HEADS, MLA_LANES).astype(BF16)
    w_ukv3 = w_ukv.reshape(MLA_KV_RANK, MLA_HEADS, MLA_NOPE + MLA_V)
    w_k = _spread(w_ukv3[:, :, :MLA_NOPE].reshape(MLA_KV_RANK, -1), MLA_HEADS, MLA_NOPE_LANES).astype(BF16)
    w_v = w_ukv3[:, :, MLA_NOPE:].reshape(MLA_KV_RANK, -1).astype(BF16)

    qm, km, vm, qw, kw, vw, gates = _pre_attention(
        x2d, pos2d, w_all, w_uq_p, w_k, w_v, g_mix[None], g_cq[None], g_ckv[None],
        _spread(g_qn_mla[None], 1, MLA_LANES), _spread(g_kn_mla[None], 1, MLA_LANES),
        _spread(g_qn_swa[None], 1, SWA_LANES), _spread(g_kn_swa[None], 1, SWA_LANES),
        b_gate[None], _rope_table(), tm=PRE_TILE)

    kmax = (KEY_NORM_MARGIN * MLA_QK ** 0.5) * jnp.max(jnp.abs(g_kn_mla), keepdims=True)
    om, w_gu_bf, w_down_bf = _mla_attention(kmax, qm, km, vm, w_exp_gu, w_exp_down, B, S,
                                            tq=MLA_Q_TILE)
    om = om.reshape(B * S, -1)
    ow = _swa_attention(qw, kw, vw, sink, B, S, tq=SWA_Q_TILE).reshape(B * S, -1)

    x1s, hp, comb_t = _post_attention(
        om, ow, gates, x2d, w_br_mla.astype(BF16), w_br_swa.astype(BF16), w_out.astype(BF16),
        g_moe[None], w_router.T, router_bias[:, None], w_sh_gu.astype(BF16),
        w_sh_down.astype(BF16), tm=ROW_TILE)

    slots, wts, offs, cnts = _moe_plan(comb_t)
    routed = _moe_sparse(offs[:, :, 0], cnts[:, :, 0], slots, wts, hp, w_gu_bf, w_down_bf)

    return _ple(x1s, routed, p2d, g_ple[None], w_ple_gate.astype(BF16), b_ple[None],
                w_ple_proj.astype(BF16), tm=ROW_TILE)


def kernel(x, p, positions, g_mix, w_in, b_gate, g_cq, w_uq, g_ckv, w_ukv, g_qn_mla, g_kn_mla, g_qn_swa, g_kn_swa, sink, w_br_mla, w_br_swa, w_out, g_moe, w_router, router_bias, w_exp_gu, w_exp_down, w_sh_gu, w_sh_down, g_ple, w_ple_gate, b_ple, w_ple_proj):
    B, S, D = x.shape
    x2d = x.reshape(B * S, D)
    pos2d = positions.reshape(B * S, 1)
    for i in range(p.shape[0]):
        x2d = _layer(x2d, p[i].reshape(B * S, -1), pos2d, B, S, g_mix[i], w_in[i], b_gate[i],
                     g_cq[i], w_uq[i], g_ckv[i], w_ukv[i], g_qn_mla[i], g_kn_mla[i], g_qn_swa[i],
                     g_kn_swa[i], sink[i], w_br_mla[i], w_br_swa[i], w_out[i], g_moe[i],
                     w_router[i], router_bias[i], w_exp_gu[i], w_exp_down[i], w_sh_gu[i],
                     w_sh_down[i], g_ple[i], w_ple_gate[i], b_ple[i], w_ple_proj[i])
    return x2d.reshape(B, S, D)
```
